```python
import math
import jax
import jax.numpy as jnp
from jax import lax
import numpy as np

D_MODEL = 2048
BATCH = 4
SEQ = 2048
DEPTH = 4

D_MIX = D_MODEL
N_GROUPS = 4
GROUP_W = D_MIX // N_GROUPS
HEAD_DIM = 128
N_HEADS = GROUP_W // HEAD_DIM
DIFF_QK_DIM = HEAD_DIM // 2
MOBA_BLOCK = 256
MOBA_TOPK = 3
MOBA_Q_CHUNK = 64
GLA_DK = HEAD_DIM // 2
GLA_DV = HEAD_DIM
GLA_RANK = 16
GLA_TAU = 16.0
GLA_CHUNK = 64
Q_BLOCK = 128
ROPE_THETA = 500000.0
ROPE_FRACTION = 4
RMS_EPS = 1e-6
FOX_F_BIAS_CENTER = 3.0

SEGMENTS = (
    ("fox_q", GROUP_W), ("fox_k", GROUP_W), ("fox_v", GROUP_W),
    ("fox_f", N_HEADS), ("fox_g", GROUP_W),
    ("diff_q", GROUP_W), ("diff_k", GROUP_W), ("diff_v", GROUP_W), ("diff_g", GROUP_W),
    ("moba_q", GROUP_W), ("moba_k", GROUP_W), ("moba_v", GROUP_W), ("moba_g", GROUP_W),
    ("gla_q", N_HEADS * GLA_DK), ("gla_k", N_HEADS * GLA_DK), ("gla_v", N_HEADS * GLA_DV),
    ("gla_a", GLA_RANK), ("gla_g", GROUP_W),
)
D_IN = sum(w for _, w in SEGMENTS)

kernel_name = "hymba_fox_diff_moba_gla_block"


def rms_norm(x, g):
    xf = x.astype(jnp.float32)
    y = xf * lax.rsqrt(jnp.mean(xf * xf, axis=-1, keepdims=True) + RMS_EPS)
    return (y * g.astype(jnp.float32)).astype(x.dtype)


def split_columns(proj):
    parts, off = {}, 0
    for name, width in SEGMENTS:
        parts[name] = proj[..., off:off + width]
        off += width
    return parts


def to_heads(t, d):
    b, s, _ = t.shape
    return t.reshape(b, s, -1, d).transpose(0, 2, 1, 3)


def from_heads(t):
    b, h, s, d = t.shape
    return t.transpose(0, 2, 1, 3).reshape(b, s, h * d)


def diff_pair(t):
    b, s, _ = t.shape
    t = t.reshape(b, s, N_HEADS, 2, DIFF_QK_DIM).transpose(3, 0, 2, 1, 4)
    return t[0], t[1]


def query_blocks(t, block):
    b, h, s, d = t.shape
    return t.reshape(b, h, s // block, block, d).transpose(2, 0, 1, 3, 4)


def merge_query_blocks(o):
    n, b, h, blk, d = o.shape
    return o.transpose(1, 2, 0, 3, 4).reshape(b, h, n * blk, d)


def rope_table(seq, rot_dim):
    inv_freq = ROPE_THETA ** (-jnp.arange(0, rot_dim, 2, dtype=jnp.float32) / rot_dim)
    ang = jnp.arange(seq, dtype=jnp.float32)[:, None] * inv_freq[None, :]
    return jnp.cos(ang), jnp.sin(ang)


def partial_rope(t, cos, sin):
    half = cos.shape[-1]
    rot = 2 * half
    t1, t2, rest = t[..., :half], t[..., half:rot], t[..., rot:]
    c = cos.astype(t.dtype)
    s = sin.astype(t.dtype)
    return jnp.concatenate([t1 * c - t2 * s, t1 * s + t2 * c, rest], axis=-1)


def forgetting_attention(q, k, v, log_f):
    b, h, s_len, d = q.shape
    cum = jnp.cumsum(log_f.astype(jnp.float32), axis=-1)
    n_blk = s_len // Q_BLOCK
    cum_blocks = cum.reshape(b, h, n_blk, Q_BLOCK).transpose(2, 0, 1, 3)
    k_pos = jnp.arange(s_len)
    scale = d ** -0.5

    def block(args):
        i, q_i, c_i = args
        logits = jnp.einsum("bhqd,bhkd->bhqk", q_i, k, preferred_element_type=jnp.float32) * scale
        logits = logits + c_i[..., :, None] - cum[:, :, None, :]
        q_pos = i * Q_BLOCK + jnp.arange(Q_BLOCK)
        logits = jnp.where(k_pos[None, :] <= q_pos[:, None], logits, -jnp.inf)
        p = jax.nn.softmax(logits, axis=-1)
        return jnp.einsum("bhqk,bhkd->bhqd", p.astype(v.dtype), v)

    out = lax.map(block, (jnp.arange(n_blk), query_blocks(q, Q_BLOCK), cum_blocks))
    return merge_query_blocks(out)


def differential_attention(q1, q2, k1, k2, v, lam):
    s_len, d = q1.shape[2], q1.shape[3]
    n_blk = s_len // Q_BLOCK
    k_pos = jnp.arange(s_len)
    scale = d ** -0.5

    def block(args):
        i, q1_i, q2_i = args
        q_pos = i * Q_BLOCK + jnp.arange(Q_BLOCK)
        causal = k_pos[None, :] <= q_pos[:, None]

        def probs(q_i, k_full):
            logits = jnp.einsum("bhqd,bhkd->bhqk", q_i, k_full, preferred_element_type=jnp.float32) * scale
            return jax.nn.softmax(jnp.where(causal, logits, -jnp.inf), axis=-1)

        p = probs(q1_i, k1) - lam * probs(q2_i, k2)
        return jnp.einsum("bhqk,bhkd->bhqd", p.astype(v.dtype), v)

    out = lax.map(block, (jnp.arange(n_blk), query_blocks(q1, Q_BLOCK), query_blocks(q2, Q_BLOCK)))
    return merge_query_blocks(out)


def moba_attention(q, k, v):
    b, h, s_len, d = q.shape
    n_blk = -(-s_len // MOBA_BLOCK)
    pad = n_blk * MOBA_BLOCK - s_len
    widths = ((0, 0), (0, 0), (0, pad), (0, 0))
    qp, kp, vp = (jnp.pad(t, widths) for t in (q, k, v))
    k_blocks = kp.reshape(b, h, n_blk, MOBA_BLOCK, d)
    v_blocks = vp.reshape(b, h, n_blk, MOBA_BLOCK, d)
    k_mean = jnp.mean(k_blocks.astype(jnp.float32), axis=3)
    n_sel = min(MOBA_TOPK, n_blk)
    scale = d ** -0.5
    gather = jax.vmap(jax.vmap(lambda blocks, idx: blocks[idx]))

    def chunk(args):
        c, q_c = args
        q_pos = c * MOBA_Q_CHUNK + jnp.arange(MOBA_Q_CHUNK)
        own = (c * MOBA_Q_CHUNK) // MOBA_BLOCK
        gate = jnp.einsum("bhqd,bhnd->bhqn", q_c.astype(jnp.float32), k_mean)
        gate = jnp.where(jnp.arange(n_blk) < own, gate, -jnp.inf)
        _, top_idx = lax.top_k(gate, n_sel)
        sel_ok = jnp.arange(n_sel) < own
        k_sel = gather(k_blocks, top_idx)
        v_sel = gather(v_blocks, top_idx)
        s_sel = jnp.einsum("bhqd,bhqjld->bhqjl", q_c, k_sel, preferred_element_type=jnp.float32) * scale
        s_sel = jnp.where(sel_ok[:, None], s_sel, -jnp.inf).reshape(b, h, MOBA_Q_CHUNK, n_sel * MOBA_BLOCK)
        k_own = lax.dynamic_slice_in_dim(kp, own * MOBA_BLOCK, MOBA_BLOCK, axis=2)
        v_own = lax.dynamic_slice_in_dim(vp, own * MOBA_BLOCK, MOBA_BLOCK, axis=2)
        s_own = jnp.einsum("bhqd,bhld->bhql", q_c, k_own, preferred_element_type=jnp.float32) * scale
        own_pos = own * MOBA_BLOCK + jnp.arange(MOBA_BLOCK)
        s_own = jnp.where(own_pos[None, :] <= q_pos[:, None], s_own, -jnp.inf)
        p = jax.nn.softmax(jnp.concatenate([s_sel, s_own], axis=-1), axis=-1).astype(v.dtype)
        p_sel = p[..., :n_sel * MOBA_BLOCK].reshape(b, h, MOBA_Q_CHUNK, n_sel, MOBA_BLOCK)
        p_own = p[..., n_sel * MOBA_BLOCK:]
        return (jnp.einsum("bhqjl,bhqjld->bhqd", p_sel, v_sel)
                + jnp.einsum("bhql,bhld->bhqd", p_own, v_own))

    n_chunk = (n_blk * MOBA_BLOCK) // MOBA_Q_CHUNK
    out = lax.map(chunk, (jnp.arange(n_chunk), query_blocks(qp, MOBA_Q_CHUNK)))
    return merge_query_blocks(out)[:, :, :s_len]


def gla_chunked(q, k, v, log_a):
    b, h, s_len, dk = q.shape
    dv = v.shape[-1]
    qf = q.astype(jnp.float32) * dk ** -0.5
    kf = k.astype(jnp.float32)
    vf = v.astype(jnp.float32)
    gf = log_a.astype(jnp.float32)
    qc, kc, vc, gc = (query_blocks(t, GLA_CHUNK) for t in (qf, kf, vf, gf))
    causal = jnp.tril(jnp.ones((GLA_CHUNK, GLA_CHUNK), dtype=bool))[..., None]

    def step(state, inp):
        q_i, k_i, v_i, g_i = inp
        cum = jnp.cumsum(g_i, axis=2)
        rel = jnp.where(causal, cum[:, :, :, None, :] - cum[:, :, None, :, :], -jnp.inf)
        scores = jnp.einsum("bhtd,bhsd,bhtsd->bhts", q_i, k_i, jnp.exp(rel))
        o = (jnp.einsum("bhts,bhsv->bhtv", scores, v_i)
             + jnp.einsum("bhtd,bhdv->bhtv", q_i * jnp.exp(cum), state))
        last = cum[:, :, -1:, :]
        state = (jnp.exp(last[:, :, 0, :])[..., None] * state
                 + jnp.einsum("bhsd,bhsv->bhdv", k_i * jnp.exp(last - cum), v_i))
        return state, o

    state0 = jnp.zeros((b, h, dk, dv), jnp.float32)
    _, out = lax.scan(step, state0, (qc, kc, vc, gc))
    return merge_query_blocks(out).astype(v.dtype)


def hybrid_layer(x, layer_idx, norm_g, w_in, fox_fb, diff_lam, diff_norm_g,
                 gla_wa2, gla_ba, gla_norm_g, w_out, rope_diff, rope_moba):
    h = rms_norm(x, norm_g)
    p = split_columns(jnp.einsum("bsd,de->bse", h, w_in))

    log_f = jax.nn.log_sigmoid((p["fox_f"] + fox_fb).astype(jnp.float32)).transpose(0, 2, 1)
    fox = forgetting_attention(to_heads(p["fox_q"], HEAD_DIM), to_heads(p["fox_k"], HEAD_DIM),
                               to_heads(p["fox_v"], HEAD_DIM), log_f)
    fox = from_heads(fox) * jax.nn.silu(p["fox_g"])

    lambda_init = 0.8 - 0.6 * math.exp(-0.3 * layer_idx)
    lf = diff_lam.astype(jnp.float32)
    lam = jnp.exp(jnp.sum(lf[0] * lf[1])) - jnp.exp(jnp.sum(lf[2] * lf[3])) + lambda_init
    q1, q2 = diff_pair(p["diff_q"])
    k1, k2 = diff_pair(p["diff_k"])
    q1, q2, k1, k2 = (partial_rope(t, *rope_diff) for t in (q1, q2, k1, k2))
    diff = differential_attention(q1, q2, k1, k2, to_heads(p["diff_v"], HEAD_DIM), lam)
    diff = rms_norm(diff, diff_norm_g) * (1.0 - lambda_init)
    diff = from_heads(diff) * jax.nn.silu(p["diff_g"])

    mq = partial_rope(to_heads(p["moba_q"], HEAD_DIM), *rope_moba)
    mk = partial_rope(to_heads(p["moba_k"], HEAD_DIM), *rope_moba)
    moba = moba_attention(mq, mk, to_heads(p["moba_v"], HEAD_DIM))
    moba = from_heads(moba) * jax.nn.silu(p["moba_g"])

    a_logit = jnp.einsum("bsr,re->bse", p["gla_a"], gla_wa2) + gla_ba
    log_a = jax.nn.log_sigmoid(a_logit.astype(jnp.float32)) / GLA_TAU
    gla = gla_chunked(to_heads(p["gla_q"], GLA_DK), to_heads(p["gla_k"], GLA_DK),
                      to_heads(p["gla_v"], GLA_DV), to_heads(log_a, GLA_DK))
    gla = from_heads(rms_norm(gla, gla_norm_g)) * jax.nn.silu(p["gla_g"])

    mixed = jnp.concatenate([fox, diff, moba, gla], axis=-1)
    return x + jnp.einsum("bse,ed->bsd", mixed, w_out)


def setup_inputs(seed: int = 0) -> dict:
    key = jax.random.key(seed)
    ks = jax.random.split(key, 11)
    f32 = jnp.float32
    nrm = jax.random.normal
    x = nrm(ks[0], (BATCH, SEQ, D_MODEL), f32)
    norm_g = 1.0 + 0.02 * nrm(ks[1], (DEPTH, D_MODEL), f32)
    w_in = nrm(ks[2], (DEPTH, D_MODEL, D_IN), f32) * D_MODEL ** -0.5
    fox_fb = FOX_F_BIAS_CENTER + 0.1 * nrm(ks[3], (DEPTH, N_HEADS), f32)
    diff_lam = 0.1 * nrm(ks[4], (DEPTH, 4, DIFF_QK_DIM), f32)
    diff_norm_g = 1.0 + 0.02 * nrm(ks[5], (DEPTH, HEAD_DIM), f32)
    gla_wa2 = nrm(ks[6], (DEPTH, GLA_RANK, N_HEADS * GLA_DK), f32) * GLA_RANK ** -0.5
    gla_ba = 0.1 * nrm(ks[7], (DEPTH, N_HEADS * GLA_DK), f32)
    gla_norm_g = 1.0 + 0.02 * nrm(ks[8], (DEPTH, GLA_DV), f32)
    w_out = nrm(ks[9], (DEPTH, D_MIX, D_MODEL), f32) * D_MIX ** -0.5
    final_norm_g = 1.0 + 0.02 * nrm(ks[10], (D_MODEL,), f32)
    return {"x": x, "norm_g": norm_g, "w_in": w_in, "fox_fb": fox_fb, "diff_lam": diff_lam,
            "diff_norm_g": diff_norm_g, "gla_wa2": gla_wa2, "gla_ba": gla_ba,
            "gla_norm_g": gla_norm_g, "w_out": w_out, "final_norm_g": final_norm_g}


def reference(x, norm_g, w_in, fox_fb, diff_lam, diff_norm_g, gla_wa2, gla_ba,
              gla_norm_g, w_out, final_norm_g):
    s_len = x.shape[1]
    rope_diff = rope_table(s_len, DIFF_QK_DIM // ROPE_FRACTION)
    rope_moba = rope_table(s_len, HEAD_DIM // ROPE_FRACTION)
    for l in range(DEPTH):
        x = hybrid_layer(x, l, norm_g[l], w_in[l], fox_fb[l], diff_lam[l], diff_norm_g[l],
                         gla_wa2[l], gla_ba[l], gla_norm_g[l], w_out[l], rope_diff, rope_moba)
    return rms_norm(x, final_norm_g)
```

```python
import functools
import math

import jax
import jax.numpy as jnp
from jax import lax
from jax.experimental import pallas as pl
from jax.experimental.pallas import tpu as pltpu

F32 = jnp.float32
BF16 = jnp.bfloat16

LANES = 128
HEAD_DIM = 128
N_HEADS = 4
GROUP_W = N_HEADS * HEAD_DIM
DIFF_QK_DIM = HEAD_DIM // 2
MOBA_BLOCK = 256
MOBA_TOPK = 3
GLA_DK = HEAD_DIM // 2
GLA_RANK = 16
GLA_TAU = 16.0
GLA_CHUNK = 64
ROPE_THETA = 500000.0
ROPE_FRACTION = 4
RMS_EPS = 1e-6
MASKED_LOGIT = -1e30

_SEGMENTS = (
    ("fox_q", GROUP_W), ("fox_k", GROUP_W), ("fox_v", GROUP_W),
    ("fox_f", N_HEADS), ("fox_g", GROUP_W),
    ("diff_q", GROUP_W), ("diff_k", GROUP_W), ("diff_v", GROUP_W), ("diff_g", GROUP_W),
    ("moba_q", GROUP_W), ("moba_k", GROUP_W), ("moba_v", GROUP_W), ("moba_g", GROUP_W),
    ("gla_q", N_HEADS * GLA_DK), ("gla_k", N_HEADS * GLA_DK), ("gla_v", GROUP_W),
    ("gla_a", GLA_RANK), ("gla_g", GROUP_W),
)
_SEG_OFF = {}
_off = 0
for _name, _w in _SEGMENTS:
    _SEG_OFF[_name] = (_off, _w)
    _off += _w

_SLABS = ("fox_q", "fox_k", "fox_v", "fox_g", "diff_q", "diff_k", "diff_v", "diff_g",
          "moba_q", "moba_k", "moba_v", "moba_g", "gla_qk", "gla_v", "gla_g")
_SLAB = {name: i for i, name in enumerate(_SLABS)}
N_MAIN = len(_SLABS) * GROUP_W

Q_TILE = 256
K_TILE = 256
CUM_BLOCK = 128


def _silu(x):
    return x * (1.0 / (1.0 + jnp.exp(-x)))


def _log_sigmoid(x):
    return jnp.minimum(x, 0.0) - jnp.log1p(jnp.exp(-jnp.abs(x)))


def _nt_dot(a, b):
    return lax.dot_general(a, b, (((1,), (1,)), ((), ())), preferred_element_type=F32)


def _inproj_kernel(x_ref, g_ref, w_ref, ws_ref, o_ref, small_ref, h_scr):
    @pl.when(pl.program_id(1) == 0)
    def _():
        x = x_ref[...]
        y = x * lax.rsqrt(jnp.mean(x * x, axis=-1, keepdims=True) + RMS_EPS) * g_ref[...]
        h = y.astype(BF16)
        h_scr[...] = h
        small_ref[...] = jnp.dot(h, ws_ref[...], preferred_element_type=F32)

    acc = jnp.dot(h_scr[...], w_ref[...], preferred_element_type=F32)
    for c in range(o_ref.shape[0]):
        o_ref[c] = acc[:, c * LANES:(c + 1) * LANES].astype(BF16)


def _inproj(x2, norm_g, w_main, w_small, *, tm, tn):
    t, d = x2.shape
    n_main = w_main.shape[1]
    return pl.pallas_call(
        _inproj_kernel,
        grid=(t // tm, n_main // tn),
        in_specs=[
            pl.BlockSpec((tm, d), lambda i, j: (i, 0)),
            pl.BlockSpec((1, d), lambda i, j: (0, 0)),
            pl.BlockSpec((d, tn), lambda i, j: (0, j)),
            pl.BlockSpec((d, LANES), lambda i, j: (0, 0)),
        ],
        out_specs=[
            pl.BlockSpec((tn // LANES, tm, LANES), lambda i, j: (j, i, 0)),
            pl.BlockSpec((tm, LANES), lambda i, j: (i, 0)),
        ],
        out_shape=[
            jax.ShapeDtypeStruct((n_main // LANES, t, LANES), BF16),
            jax.ShapeDtypeStruct((t, LANES), F32),
        ],
        scratch_shapes=[pltpu.VMEM((tm, d), BF16)],
        compiler_params=pltpu.CompilerParams(
            dimension_semantics=("arbitrary", "arbitrary"),
            vmem_limit_bytes=56 * 1024 * 1024),
        name="inproj",
    )(x2, norm_g.reshape(1, d), w_main, w_small)


def _gates_kernel(small_ref, fb_ref, wa_ref, ba_ref, fcol_ref, frow_ref, p_ref):
    s_len = small_ref.shape[0]
    small = small_ref[...]
    log_f = _log_sigmoid(small + fb_ref[...])
    a_logit = jnp.dot(small.astype(BF16), wa_ref[...], preferred_element_type=F32) + ba_ref[...]
    log_a = _log_sigmoid(a_logit) * (1.0 / GLA_TAU)
    both = jnp.concatenate([log_f, log_a], axis=-1)

    row = lax.broadcasted_iota(jnp.int32, (CUM_BLOCK, CUM_BLOCK), 0)
    col = lax.broadcasted_iota(jnp.int32, (CUM_BLOCK, CUM_BLOCK), 1)
    tri = (col <= row).astype(F32)
    carry = jnp.zeros((1, both.shape[1]), F32)
    blocks_per_tile = K_TILE // CUM_BLOCK
    f_t = []
    for c in range(s_len // CUM_BLOCK):
        blk = both[c * CUM_BLOCK:(c + 1) * CUM_BLOCK]
        cum = jnp.dot(tri, blk, precision=lax.Precision.HIGHEST,
                      preferred_element_type=F32) + carry
        carry = cum[CUM_BLOCK - 1:CUM_BLOCK]
        rows = slice(c * CUM_BLOCK, (c + 1) * CUM_BLOCK)
        fcol_ref[rows, :] = cum[:, :LANES]
        for h in range(N_HEADS):
            p_ref[h, rows, :] = cum[:, (h + 1) * LANES:(h + 2) * LANES]
        f_t.append(cum[:, :LANES].T[:8])
        if len(f_t) == blocks_per_tile:
            frow_ref[c // blocks_per_tile] = jnp.concatenate(f_t, axis=-1)
            f_t = []


def _gates(small, fb_pad, wa_pad, ba_dup, *, batch, s_len):
    t = small.shape[0]
    nk = s_len // K_TILE
    return pl.pallas_call(
        _gates_kernel,
        grid=(batch,),
        in_specs=[
            pl.BlockSpec((s_len, LANES), lambda b: (b, 0)),
            pl.BlockSpec((1, LANES), lambda b: (0, 0)),
            pl.BlockSpec((LANES, GROUP_W), lambda b: (0, 0)),
            pl.BlockSpec((1, GROUP_W), lambda b: (0, 0)),
        ],
        out_specs=[
            pl.BlockSpec((s_len, LANES), lambda b: (b, 0)),
            pl.BlockSpec((None, nk, 8, K_TILE), lambda b: (b, 0, 0, 0)),
            pl.BlockSpec((N_HEADS, s_len, LANES), lambda b: (0, b, 0)),
        ],
        out_shape=[
            jax.ShapeDtypeStruct((t, LANES), F32),
            jax.ShapeDtypeStruct((batch, nk, 8, K_TILE), F32),
            jax.ShapeDtypeStruct((N_HEADS, t, LANES), F32),
        ],
        compiler_params=pltpu.CompilerParams(dimension_semantics=("arbitrary",)),
        name="gates",
    )(small, fb_pad, wa_pad, ba_dup)


def _causal_mask(tq, tk):
    row = lax.broadcasted_iota(jnp.int32, (tq, tk), 0)
    col = lax.broadcasted_iota(jnp.int32, (tq, tk), 1)
    return col <= row


def _softmax_update(carry, s, v):
    m, l, acc = carry
    m_new = jnp.maximum(m, jnp.max(s, axis=-1, keepdims=True))
    alpha = jnp.exp(m - m_new)
    p = jnp.exp(s - m_new)
    l = alpha * l + jnp.sum(p, axis=-1, keepdims=True)
    acc = alpha * acc + jnp.dot(p.astype(BF16), v, preferred_element_type=F32)
    return m_new, l, acc


def _softmax_init(tq):
    return (jnp.full((tq, 1), -jnp.inf, F32), jnp.zeros((tq, 1), F32),
            jnp.zeros((tq, HEAD_DIM), F32))


def _rope(x, cos_t, sin_up, sin_dn, half):
    return (x * cos_t + pltpu.roll(x, LANES - half, 1) * sin_up
            + pltpu.roll(x, half, 1) * sin_dn)


def _fox_kernel(q_ref, k_ref, v_ref, g_ref, fcol_ref, frow_ref, o_ref):
    i = pl.program_id(1)
    tq = q_ref.shape[1]
    scale = HEAD_DIM ** -0.5
    fcol = fcol_ref[...]
    qs = [q_ref[h] for h in range(N_HEADS)]
    fqs = [fcol[:, h:h + 1] for h in range(N_HEADS)]

    def logits(h, j):
        k = k_ref[h, pl.ds(pl.multiple_of(j * K_TILE, K_TILE), K_TILE), :]
        fk = frow_ref[j][h:h + 1, :]
        return _nt_dot(qs[h], k) * scale + (fqs[h] - fk)

    def values(h, j):
        return v_ref[h, pl.ds(pl.multiple_of(j * K_TILE, K_TILE), K_TILE), :]

    def body(j, carries):
        return tuple(_softmax_update(carries[h], logits(h, j), values(h, j))
                     for h in range(N_HEADS))

    carries = lax.fori_loop(0, i, body, tuple(_softmax_init(tq) for _ in range(N_HEADS)))
    mask = _causal_mask(tq, K_TILE)
    for h in range(N_HEADS):
        s = jnp.where(mask, logits(h, i), -jnp.inf)
        _, l, acc = _softmax_update(carries[h], s, values(h, i))
        o_ref[h] = (acc / l * _silu(g_ref[h].astype(F32))).astype(BF16)


def _fox(proj, fcol, frow, *, batch, s_len):
    t = proj.shape[1]
    nq = s_len // Q_TILE
    nk = s_len // K_TILE
    qspec = lambda slab: pl.BlockSpec((N_HEADS, Q_TILE, LANES), lambda b, i: (slab, b * nq + i, 0))
    kspec = lambda slab: pl.BlockSpec((N_HEADS, s_len, LANES), lambda b, i: (slab, b, 0))
    return pl.pallas_call(
        _fox_kernel,
        grid=(batch, nq),
        in_specs=[
            qspec(_SLAB["fox_q"]), kspec(_SLAB["fox_k"]), kspec(_SLAB["fox_v"]),
            qspec(_SLAB["fox_g"]),
            pl.BlockSpec((Q_TILE, LANES), lambda b, i: (b * nq + i, 0)),
            pl.BlockSpec((None, nk, 8, K_TILE), lambda b, i: (b, 0, 0, 0)),
        ],
        out_specs=pl.BlockSpec((N_HEADS, Q_TILE, LANES), lambda b, i: (0, b * nq + i, 0)),
        out_shape=jax.ShapeDtypeStruct((N_HEADS, t, LANES), BF16),
        compiler_params=pltpu.CompilerParams(dimension_semantics=("arbitrary", "arbitrary")),
        name="fox",
    )(proj, proj, proj, proj, fcol, frow)


def _diff_kernel(lam_init, q_ref, k_ref, v_ref, g_ref, lam_ref, ng_ref,
                 cq_ref, uq_ref, dq_ref, ck_ref, uk_ref, dk_ref, o_ref, kr_scr):
    i = pl.program_id(1)
    tq = q_ref.shape[1]
    half = DIFF_QK_DIM // ROPE_FRACTION // 2
    scale = DIFF_QK_DIM ** -0.5

    @pl.when(i == 0)
    def _():
        for h in range(N_HEADS):
            kr_scr[h] = _rope(k_ref[h].astype(F32), ck_ref[...], uk_ref[...], dk_ref[...],
                              half).astype(BF16)

    lf = lam_ref[...]
    lam = (jnp.exp(jnp.sum(lf[0:1] * lf[1:2], axis=-1, keepdims=True))
           - jnp.exp(jnp.sum(lf[2:3] * lf[3:4], axis=-1, keepdims=True)) + lam_init)

    lane = lax.broadcasted_iota(jnp.int32, (tq, LANES), 1)
    first = lane < DIFF_QK_DIM
    q1s, q2s = [], []
    for h in range(N_HEADS):
        q = _rope(q_ref[h].astype(F32), cq_ref[...], uq_ref[...], dq_ref[...], half)
        q1s.append(jnp.where(first, q, 0.0).astype(BF16))
        q2s.append(jnp.where(first, 0.0, q).astype(BF16))

    def tiles(h, j):
        rows = pl.ds(pl.multiple_of(j * K_TILE, K_TILE), K_TILE)
        k = kr_scr[h, rows, :]
        return _nt_dot(q1s[h], k) * scale, _nt_dot(q2s[h], k) * scale, v_ref[h, rows, :]

    def body(j, carries):
        out = []
        for h in range(N_HEADS):
            s1, s2, v = tiles(h, j)
            out.append((_softmax_update(carries[h][0], s1, v),
                        _softmax_update(carries[h][1], s2, v)))
        return tuple(out)

    init = tuple((_softmax_init(tq), _softmax_init(tq)) for _ in range(N_HEADS))
    carries = lax.fori_loop(0, i, body, init)
    mask = _causal_mask(tq, K_TILE)
    for h in range(N_HEADS):
        s1, s2, v = tiles(h, i)
        _, l1, a1 = _softmax_update(carries[h][0], jnp.where(mask, s1, -jnp.inf), v)
        _, l2, a2 = _softmax_update(carries[h][1], jnp.where(mask, s2, -jnp.inf), v)
        o = a1 / l1 - lam * (a2 / l2)
        o = o * lax.rsqrt(jnp.mean(o * o, axis=-1, keepdims=True) + RMS_EPS) * ng_ref[...]
        o = o * (1.0 - lam_init)
        o_ref[h] = (o * _silu(g_ref[h].astype(F32))).astype(BF16)


def _diff(proj, diff_lam, diff_norm_g, tables, lam_init, *, batch, s_len):
    t = proj.shape[1]
    nq = s_len // Q_TILE
    qspec = lambda slab: pl.BlockSpec((N_HEADS, Q_TILE, LANES), lambda b, i: (slab, b * nq + i, 0))
    kspec = lambda slab: pl.BlockSpec((N_HEADS, s_len, LANES), lambda b, i: (slab, b, 0))
    tq_spec = pl.BlockSpec((Q_TILE, LANES), lambda b, i: (i, 0))
    tk_spec = pl.BlockSpec((s_len, LANES), lambda b, i: (0, 0))
    return pl.pallas_call(
        functools.partial(_diff_kernel, lam_init),
        grid=(batch, nq),
        in_specs=[
            qspec(_SLAB["diff_q"]), kspec(_SLAB["diff_k"]), kspec(_SLAB["diff_v"]),
            qspec(_SLAB["diff_g"]),
            pl.BlockSpec((4, DIFF_QK_DIM), lambda b, i: (0, 0)),
            pl.BlockSpec((1, HEAD_DIM), lambda b, i: (0, 0)),
            tq_spec, tq_spec, tq_spec, tk_spec, tk_spec, tk_spec,
        ],
        out_specs=pl.BlockSpec((N_HEADS, Q_TILE, LANES), lambda b, i: (0, b * nq + i, 0)),
        out_shape=jax.ShapeDtypeStruct((N_HEADS, t, LANES), BF16),
        scratch_shapes=[pltpu.VMEM((N_HEADS, s_len, LANES), BF16)],
        compiler_params=pltpu.CompilerParams(dimension_semantics=("arbitrary", "arbitrary")),
        name="diff",
    )(proj, proj, proj, proj, diff_lam, diff_norm_g.reshape(1, HEAD_DIM), *tables, *tables)


def _moba_kernel(q_ref, k_ref, v_ref, g_ref, cq_ref, uq_ref, dq_ref, ck_ref, uk_ref, dk_ref,
                 o_ref, kr_scr, kmean_scr):
    i = pl.program_id(1)
    tq = q_ref.shape[1]
    half = HEAD_DIM // ROPE_FRACTION // 2
    scale = HEAD_DIM ** -0.5
    n_blk = k_ref.shape[1] // MOBA_BLOCK

    @pl.when(i == 0)
    def _():
        kmean_scr[...] = jnp.zeros(kmean_scr.shape, F32)
        for h in range(N_HEADS):
            kr = _rope(k_ref[h].astype(F32), ck_ref[...], uk_ref[...], dk_ref[...], half)
            kr_scr[h] = kr.astype(BF16)
            for n in range(n_blk):
                blk = kr[n * MOBA_BLOCK:(n + 1) * MOBA_BLOCK]
                kmean_scr[h, n:n + 1, :] = jnp.sum(blk, axis=0, keepdims=True) * (1.0 / MOBA_BLOCK)

    lane = lax.broadcasted_iota(jnp.int32, (tq, LANES), 1).astype(F32)
    past = lane < i.astype(F32)
    qa = []
    for h in range(N_HEADS):
        q = _rope(q_ref[h].astype(F32), cq_ref[...], uq_ref[...], dq_ref[...], half).astype(BF16)
        gate = _nt_dot(q, kmean_scr[h].astype(BF16))
        gate = jnp.where(past, gate, -jnp.inf)
        open_ = past
        sel = jnp.zeros((tq, LANES), jnp.bool_)
        for _ in range(MOBA_TOPK):
            top = jnp.max(gate, axis=-1, keepdims=True)
            idx = jnp.min(jnp.where(gate == top, lane, float(LANES)), axis=-1, keepdims=True)
            pick = (lane == idx) & open_
            sel = sel | pick
            open_ = open_ & jnp.logical_not(pick)
            gate = jnp.where(pick, -jnp.inf, gate)
        bias = jnp.where(sel, 0.0, MASKED_LOGIT).astype(BF16)
        qa.append((q, jnp.concatenate([q, bias], axis=-1)))

    def kv(h, j):
        rows = pl.ds(pl.multiple_of(j * K_TILE, K_TILE), K_TILE)
        return kr_scr[h, rows, :], v_ref[h, rows, :]

    mask = _causal_mask(tq, K_TILE)
    carries = []
    for h in range(N_HEADS):
        k, v = kv(h, i)
        s = jnp.where(mask, _nt_dot(qa[h][0], k) * scale, -jnp.inf)
        carries.append(_softmax_update(_softmax_init(tq), s, v))

    klane = lax.broadcasted_iota(jnp.int32, (K_TILE, LANES), 1)

    def body(j, carries):
        onehot = jnp.where(klane == j, 1.0, 0.0).astype(BF16)
        out = []
        for h in range(N_HEADS):
            k, v = kv(h, j)
            s = _nt_dot(qa[h][1], jnp.concatenate([k, onehot], axis=-1)) * scale
            out.append(_softmax_update(carries[h], s, v))
        return tuple(out)

    carries = lax.fori_loop(0, i, body, tuple(carries))
    for h in range(N_HEADS):
        _, l, acc = carries[h]
        o_ref[h] = (acc / l * _silu(g_ref[h].astype(F32))).astype(BF16)


def _moba(proj, tables, *, batch, s_len):
    t = proj.shape[1]
    nq = s_len // Q_TILE
    qspec = lambda slab: pl.BlockSpec((N_HEADS, Q_TILE, LANES), lambda b, i: (slab, b * nq + i, 0))
    kspec = lambda slab: pl.BlockSpec((N_HEADS, s_len, LANES), lambda b, i: (slab, b, 0))
    tq_spec = pl.BlockSpec((Q_TILE, LANES), lambda b, i: (i, 0))
    tk_spec = pl.BlockSpec((s_len, LANES), lambda b, i: (0, 0))
    return pl.pallas_call(
        _moba_kernel,
        grid=(batch, nq),
        in_specs=[
            qspec(_SLAB["moba_q"]), kspec(_SLAB["moba_k"]), kspec(_SLAB["moba_v"]),
            qspec(_SLAB["moba_g"]),
            tq_spec, tq_spec, tq_spec, tk_spec, tk_spec, tk_spec,
        ],
        out_specs=pl.BlockSpec((N_HEADS, Q_TILE, LANES), lambda b, i: (0, b * nq + i, 0)),
        out_shape=jax.ShapeDtypeStruct((N_HEADS, t, LANES), BF16),
        scratch_shapes=[pltpu.VMEM((N_HEADS, s_len, LANES), BF16),
                        pltpu.VMEM((N_HEADS, LANES, LANES), F32)],
        compiler_params=pltpu.CompilerParams(dimension_semantics=("arbitrary", "arbitrary")),
        name="moba",
    )(proj, proj, proj, proj, *tables, *tables)


def _gla_kernel(x_ref, v_ref, g_ref, p_ref, ng_ref, o_ref):
    s_len = x_ref.shape[1]
    c_len = GLA_CHUNK
    scale = GLA_DK ** -0.5
    lane = lax.broadcasted_iota(jnp.int32, (c_len, LANES), 1)
    qhalf = lane < GLA_DK
    qhalf_row = lax.broadcasted_iota(jnp.int32, (1, LANES), 1) < GLA_DK
    causal = _causal_mask(c_len, c_len)

    def body(c, carries):
        rows = pl.ds(pl.multiple_of(c * c_len, c_len), c_len)
        out = []
        for h in range(N_HEADS):
            state_t, base = carries[h]
            x = x_ref[h, rows, :].astype(F32)
            v = v_ref[h, rows, :]
            p = p_ref[h, rows, :]
            cum = p - base
            xt = x * jnp.exp(jnp.where(qhalf, cum, -cum))
            last = cum[c_len - 1:c_len]
            qz = jnp.where(qhalf, xt, 0.0).astype(BF16)
            kr = pltpu.roll(xt, GLA_DK, 1)
            scores = jnp.where(causal, _nt_dot(qz, kr.astype(BF16)) * scale, 0.0)
            o = jnp.dot(scores.astype(BF16), v, preferred_element_type=F32)
            o = o + _nt_dot(qz, state_t.astype(BF16)) * scale
            decay = jnp.where(qhalf_row, jnp.exp(last), 0.0)
            khat = (jnp.where(qhalf, kr, 0.0) * decay).astype(BF16)
            upd = lax.dot_general(v, khat, (((0,), (0,)), ((), ())),
                                  preferred_element_type=F32)
            state_t = state_t * decay + upd
            y = o * lax.rsqrt(jnp.mean(o * o, axis=-1, keepdims=True) + RMS_EPS) * ng_ref[...]
            o_ref[h, rows, :] = (y * _silu(g_ref[h, rows, :].astype(F32))).astype(BF16)
            out.append((state_t, p[c_len - 1:c_len]))
        return tuple(out)

    init = tuple((jnp.zeros((HEAD_DIM, LANES), F32), jnp.zeros((1, LANES), F32))
                 for _ in range(N_HEADS))
    lax.fori_loop(0, s_len // c_len, body, init)


def _gla(proj, p_cum, gla_norm_g, *, batch, s_len):
    t = proj.shape[1]
    spec = lambda slab: pl.BlockSpec((N_HEADS, s_len, LANES), lambda b: (slab, b, 0))
    return pl.pallas_call(
        _gla_kernel,
        grid=(batch,),
        in_specs=[
            spec(_SLAB["gla_qk"]), spec(_SLAB["gla_v"]), spec(_SLAB["gla_g"]),
            pl.BlockSpec((N_HEADS, s_len, LANES), lambda b: (0, b, 0)),
            pl.BlockSpec((1, HEAD_DIM), lambda b: (0, 0)),
        ],
        out_specs=pl.BlockSpec((N_HEADS, s_len, LANES), lambda b: (0, b, 0)),
        out_shape=jax.ShapeDtypeStruct((N_HEADS, t, LANES), BF16),
        compiler_params=pltpu.CompilerParams(dimension_semantics=("arbitrary",)),
        name="gla",
    )(proj, proj, proj, p_cum, gla_norm_g.reshape(1, HEAD_DIM))


def _outproj_kernel(final, x_ref, a_ref, b_ref, c_ref, d_ref, w_ref, fg_ref, o_ref):
    parts = [r[h] for r in (a_ref, b_ref, c_ref, d_ref) for h in range(N_HEADS)]
    mixed = jnp.concatenate(parts, axis=-1)
    y = x_ref[...] + jnp.dot(mixed, w_ref[...], preferred_element_type=F32)
    if final:
        y = y * lax.rsqrt(jnp.mean(y * y, axis=-1, keepdims=True) + RMS_EPS) * fg_ref[...]
    o_ref[...] = y


def _outproj(x2, mixers, w_out, final_g, *, final, tm):
    t, d = x2.shape
    mspec = pl.BlockSpec((N_HEADS, tm, LANES), lambda i: (0, i, 0))
    return pl.pallas_call(
        functools.partial(_outproj_kernel, final),
        grid=(t // tm,),
        in_specs=[
            pl.BlockSpec((tm, d), lambda i: (i, 0)),
            mspec, mspec, mspec, mspec,
            pl.BlockSpec(w_out.shape, lambda i: (0, 0)),
            pl.BlockSpec((1, d), lambda i: (0, 0)),
        ],
        out_specs=pl.BlockSpec((tm, d), lambda i: (i, 0)),
        out_shape=jax.ShapeDtypeStruct((t, d), F32),
        compiler_params=pltpu.CompilerParams(
            dimension_semantics=("arbitrary",), vmem_limit_bytes=56 * 1024 * 1024),
        name="outproj",
    )(x2, *mixers, w_out, final_g.reshape(1, d))


def _seg(w, name):
    off, width = _SEG_OFF[name]
    return w[..., off:off + width]


def _relayout_w_in(w_in):
    cols = [_seg(w_in, n) for n in _SLABS[:12]]
    gq, gk = _seg(w_in, "gla_q"), _seg(w_in, "gla_k")
    for h in range(N_HEADS):
        cols += [gq[..., h * GLA_DK:(h + 1) * GLA_DK], gk[..., h * GLA_DK:(h + 1) * GLA_DK]]
    cols += [_seg(w_in, "gla_v"), _seg(w_in, "gla_g")]
    main = jnp.concatenate(cols, axis=-1).astype(BF16)
    f, a = _seg(w_in, "fox_f"), _seg(w_in, "gla_a")
    pad = jnp.zeros(w_in.shape[:-1] + (LANES - N_HEADS - GLA_RANK,), w_in.dtype)
    small = jnp.concatenate([f, a, pad], axis=-1).astype(BF16)
    return main, small


def _dup_halves(a):
    parts = []
    for h in range(N_HEADS):
        blk = a[..., h * GLA_DK:(h + 1) * GLA_DK]
        parts += [blk, blk]
    return jnp.concatenate(parts, axis=-1)


def _rope_tables(s_len, comp_dim, rot_dim):
    half = rot_dim // 2
    inv_freq = ROPE_THETA ** (-jnp.arange(0, rot_dim, 2, dtype=F32) / rot_dim)
    ang = jnp.arange(s_len, dtype=F32)[:, None] * inv_freq[None, :]
    cos, sin = jnp.cos(ang), jnp.sin(ang)
    zeros = jnp.zeros((s_len, comp_dim - rot_dim), F32)
    ones = jnp.ones((s_len, comp_dim - rot_dim), F32)
    reps = LANES // comp_dim
    cos_t = jnp.tile(jnp.concatenate([cos, cos, ones], axis=-1), (1, reps))
    sin_up = jnp.tile(jnp.concatenate([-sin, jnp.zeros_like(sin), zeros], axis=-1), (1, reps))
    sin_dn = jnp.tile(jnp.concatenate([jnp.zeros_like(sin), sin, zeros], axis=-1), (1, reps))
    return cos_t, sin_up, sin_dn


def kernel(x, norm_g, w_in, fox_fb, diff_lam, diff_norm_g, gla_wa2, gla_ba, gla_norm_g, w_out,
           final_norm_g):
    batch, s_len, d_model = x.shape
    depth = w_in.shape[0]
    assert s_len % Q_TILE == 0 and Q_TILE == K_TILE == MOBA_BLOCK
    t = batch * s_len
    tm_in = min(512, t)
    tn_in = 1536
    tm_out = min(512, t)

    w_main, w_small = _relayout_w_in(w_in)
    w_out_b = w_out.astype(BF16)
    fb_pad = jnp.pad(fox_fb, ((0, 0), (0, LANES - N_HEADS)))[:, None, :]
    wa_pad = jnp.pad(_dup_halves(gla_wa2), ((0, 0), (N_HEADS, LANES - N_HEADS - GLA_RANK), (0, 0))
                     ).astype(BF16)
    ba_dup = _dup_halves(gla_ba)[:, None, :]
    rope_diff = _rope_tables(s_len, DIFF_QK_DIM, DIFF_QK_DIM // ROPE_FRACTION)
    rope_moba = _rope_tables(s_len, HEAD_DIM, HEAD_DIM // ROPE_FRACTION)

    x2 = x.reshape(t, d_model)
    for l in range(depth):
        proj, small = _inproj(x2, norm_g[l], w_main[l], w_small[l], tm=tm_in, tn=tn_in)
        fcol, frow, p_cum = _gates(small, fb_pad[l], wa_pad[l], ba_dup[l], batch=batch, s_len=s_len)
        lam_init = 0.8 - 0.6 * math.exp(-0.3 * l)
        mixers = (
            _fox(proj, fcol, frow, batch=batch, s_len=s_len),
            _diff(proj, diff_lam[l], diff_norm_g[l], rope_diff, lam_init, batch=batch, s_len=s_len),
            _moba(proj, rope_moba, batch=batch, s_len=s_len),
            _gla(proj, p_cum, gla_norm_g[l], batch=batch, s_len=s_len),
        )
        x2 = _outproj(x2, mixers, w_out_b[l], final_norm_g, final=(l == depth - 1), tm=tm_out)
    return x2.reshape(batch, s_len, d_model)
```

```python
import functools
import math

import jax
import jax.numpy as jnp
from jax import lax
from jax.experimental import pallas as pl
from jax.experimental.pallas import tpu as pltpu

F32 = jnp.float32
BF16 = jnp.bfloat16

LANES = 128
HEAD_DIM = 128
N_HEADS = 4
GROUP_W = N_HEADS * HEAD_DIM
DIFF_QK_DIM = HEAD_DIM // 2
MOBA_BLOCK = 256
MOBA_TOPK = 3
GLA_DK = HEAD_DIM // 2
GLA_RANK = 16
GLA_TAU = 16.0
GLA_CHUNK = 64
ROPE_THETA = 500000.0
ROPE_FRACTION = 4
RMS_EPS = 1e-6
MASKED_LOGIT = -1e30

_SEGMENTS = (
    ("fox_q", GROUP_W), ("fox_k", GROUP_W), ("fox_v", GROUP_W),
    ("fox_f", N_HEADS), ("fox_g", GROUP_W),
    ("diff_q", GROUP_W), ("diff_k", GROUP_W), ("diff_v", GROUP_W), ("diff_g", GROUP_W),
    ("moba_q", GROUP_W), ("moba_k", GROUP_W), ("moba_v", GROUP_W), ("moba_g", GROUP_W),
    ("gla_q", N_HEADS * GLA_DK), ("gla_k", N_HEADS * GLA_DK), ("gla_v", GROUP_W),
    ("gla_a", GLA_RANK), ("gla_g", GROUP_W),
)
_SEG_OFF = {}
_off = 0
for _name, _w in _SEGMENTS:
    _SEG_OFF[_name] = (_off, _w)
    _off += _w

_SLABS = ("fox_q", "fox_k", "fox_v", "fox_g", "diff_q", "diff_k", "diff_v", "diff_g",
          "moba_q", "moba_k", "moba_v", "moba_g", "gla_qk", "gla_v", "gla_g")
_SLAB = {name: i for i, name in enumerate(_SLABS)}
N_MAIN = len(_SLABS) * GROUP_W

Q_TILE = 256
K_TILE = 256
CUM_BLOCK = 128
GATE_ROWS = 16


def _silu(x):
    return x * (1.0 / (1.0 + jnp.exp(-x)))


def _log_sigmoid(x):
    return jnp.minimum(x, 0.0) - jnp.log1p(jnp.exp(-jnp.abs(x)))


def _nt_dot(a, b):
    return lax.dot_general(a, b, (((1,), (1,)), ((), ())), preferred_element_type=F32)


def _inproj_kernel(x_ref, g_ref, w_ref, ws_ref, o_ref, small_ref, h_scr):
    @pl.when(pl.program_id(1) == 0)
    def _():
        x = x_ref[...]
        y = x * lax.rsqrt(jnp.mean(x * x, axis=-1, keepdims=True) + RMS_EPS) * g_ref[...]
        h = y.astype(BF16)
        h_scr[...] = h
        small_ref[...] = jnp.dot(h, ws_ref[...], preferred_element_type=F32)

    acc = jnp.dot(h_scr[...], w_ref[...], preferred_element_type=F32)
    for c in range(o_ref.shape[0]):
        o_ref[c] = acc[:, c * LANES:(c + 1) * LANES].astype(BF16)


def _inproj(x2, norm_g, w_main, w_small, *, tm, tn):
    t, d = x2.shape
    n_main = w_main.shape[1]
    return pl.pallas_call(
        _inproj_kernel,
        grid=(t // tm, n_main // tn),
        in_specs=[
            pl.BlockSpec((tm, d), lambda i, j: (i, 0)),
            pl.BlockSpec((1, d), lambda i, j: (0, 0)),
            pl.BlockSpec((d, tn), lambda i, j: (0, j)),
            pl.BlockSpec((d, LANES), lambda i, j: (0, 0)),
        ],
        out_specs=[
            pl.BlockSpec((tn // LANES, tm, LANES), lambda i, j: (j, i, 0)),
            pl.BlockSpec((tm, LANES), lambda i, j: (i, 0)),
        ],
        out_shape=[
            jax.ShapeDtypeStruct((n_main // LANES, t, LANES), BF16),
            jax.ShapeDtypeStruct((t, LANES), F32),
        ],
        scratch_shapes=[pltpu.VMEM((tm, d), BF16)],
        compiler_params=pltpu.CompilerParams(
            dimension_semantics=("arbitrary", "arbitrary"),
            vmem_limit_bytes=56 * 1024 * 1024),
        name="inproj",
    )(x2, norm_g.reshape(1, d), w_main, w_small)


def _gates_kernel(small_ref, fb_ref, wa_ref, ba_ref, fk_ref, fq_ref, p_ref):
    s_len = small_ref.shape[0]
    small = small_ref[...]
    log_f = _log_sigmoid(small + fb_ref[...])
    a_logit = jnp.dot(small.astype(BF16), wa_ref[...], preferred_element_type=F32) + ba_ref[...]
    log_a = _log_sigmoid(a_logit) * (1.0 / GLA_TAU)
    both = jnp.concatenate([log_f, log_a], axis=-1)

    row = lax.broadcasted_iota(jnp.int32, (CUM_BLOCK, CUM_BLOCK), 0)
    col = lax.broadcasted_iota(jnp.int32, (CUM_BLOCK, CUM_BLOCK), 1)
    tri = (col <= row).astype(F32)
    carry = jnp.zeros((1, both.shape[1]), F32)
    for c in range(s_len // CUM_BLOCK):
        blk = both[c * CUM_BLOCK:(c + 1) * CUM_BLOCK]
        cum = jnp.dot(tri, blk, precision=lax.Precision.HIGHEST,
                      preferred_element_type=F32) + carry
        carry = cum[CUM_BLOCK - 1:CUM_BLOCK]
        rows = slice(c * CUM_BLOCK, (c + 1) * CUM_BLOCK)
        f_cum = cum[:, :LANES]
        fq_ref[:, rows] = f_cum.T[:GATE_ROWS]
        for h in range(N_HEADS):
            fk_ref[h, rows, :] = jnp.broadcast_to(f_cum[:, h:h + 1], (CUM_BLOCK, LANES))
            p_ref[h, rows, :] = cum[:, (h + 1) * LANES:(h + 2) * LANES]


def _gates(small, fb_pad, wa_pad, ba_dup, *, batch, s_len):
    t = small.shape[0]
    return pl.pallas_call(
        _gates_kernel,
        grid=(batch,),
        in_specs=[
            pl.BlockSpec((s_len, LANES), lambda b: (b, 0)),
            pl.BlockSpec((1, LANES), lambda b: (0, 0)),
            pl.BlockSpec((LANES, GROUP_W), lambda b: (0, 0)),
            pl.BlockSpec((1, GROUP_W), lambda b: (0, 0)),
        ],
        out_specs=[
            pl.BlockSpec((N_HEADS, s_len, LANES), lambda b: (0, b, 0)),
            pl.BlockSpec((None, GATE_ROWS, s_len), lambda b: (b, 0, 0)),
            pl.BlockSpec((N_HEADS, s_len, LANES), lambda b: (0, b, 0)),
        ],
        out_shape=[
            jax.ShapeDtypeStruct((N_HEADS, t, LANES), F32),
            jax.ShapeDtypeStruct((batch, GATE_ROWS, s_len), F32),
            jax.ShapeDtypeStruct((N_HEADS, t, LANES), F32),
        ],
        compiler_params=pltpu.CompilerParams(dimension_semantics=("arbitrary",)),
        name="gates",
    )(small, fb_pad, wa_pad, ba_dup)


def _causal_mask_t(tk, width, tq):
    key = lax.broadcasted_iota(jnp.int32, (tk, width), 0)
    qry = lax.broadcasted_iota(jnp.int32, (tk, width), 1)
    if width != tq:
        qry = qry & (tq - 1)
    return key <= qry


def _transpose_values(v_ref, vt_scr):
    for h in range(v_ref.shape[0]):
        for j in range(v_ref.shape[1] // K_TILE):
            blk = v_ref[h, j * K_TILE:(j + 1) * K_TILE, :].astype(F32)
            vt_scr[h, j] = blk.T.astype(BF16)


def _online_softmax_t(last, logits_fn, vt_fn, tq, z_scr, acc_scr):
    n, _, width = acc_scr.shape

    def update(j, slot, ms, ls, masked):
        stats = []
        for s in range(n):
            z = z_scr[slot, s]
            if masked:
                z = jnp.where(_causal_mask_t(K_TILE, width, tq), z, -jnp.inf)
            m_new = jnp.maximum(ms[s], jnp.max(z, axis=0, keepdims=True))
            alpha = jnp.exp(ms[s] - m_new)
            p = jnp.exp(z - m_new)
            stats.append((m_new, alpha * ls[s] + jnp.sum(p, axis=0, keepdims=True), alpha,
                          p.astype(BF16)))
        for s in range(n):
            pv = jnp.dot(vt_fn(s, j), stats[s][3], preferred_element_type=F32)
            acc_scr[s] = stats[s][2] * acc_scr[s] + pv
        return tuple(st[0] for st in stats), tuple(st[1] for st in stats)

    for s in range(n):
        z_scr[0, s] = logits_fn(s, jnp.int32(0))
        acc_scr[s] = jnp.zeros(acc_scr.shape[1:], F32)

    def body(j, state):
        ms, ls = state
        slot = lax.rem(j, 2)
        z_next = [logits_fn(s, j + 1) for s in range(n)]
        ms, ls = update(j, slot, ms, ls, False)
        for s in range(n):
            z_scr[1 - slot, s] = z_next[s]
        return ms, ls

    init = (tuple(jnp.full((1, width), -jnp.inf, F32) for _ in range(n)),
            tuple(jnp.zeros((1, width), F32) for _ in range(n)))
    ms, ls = lax.fori_loop(0, last, body, init)
    _, ls = update(last, lax.rem(last, 2), ms, ls, True)
    return ls


def _rope(x, cos_t, sin_up, sin_dn, half):
    return (x * cos_t + pltpu.roll(x, LANES - half, 1) * sin_up
            + pltpu.roll(x, half, 1) * sin_dn)


def _key_rows(j):
    return pl.ds(pl.multiple_of(j * K_TILE, K_TILE), K_TILE)


def _mixer_specs(s_len, nq):
    qspec = lambda slab: pl.BlockSpec((N_HEADS, Q_TILE, LANES), lambda b, i: (slab, b * nq + i, 0))
    kspec = lambda slab: pl.BlockSpec((N_HEADS, s_len, LANES), lambda b, i: (slab, b, 0))
    ospec = pl.BlockSpec((N_HEADS, Q_TILE, LANES), lambda b, i: (0, b * nq + i, 0))
    return qspec, kspec, ospec


def _fox_kernel(q_ref, k_ref, v_ref, g_ref, fk_ref, fq_ref, o_ref, vt_scr, z_scr, acc_scr):
    i = pl.program_id(1)
    tq = q_ref.shape[1]
    scale = HEAD_DIM ** -0.5

    @pl.when(i == 0)
    def _():
        _transpose_values(v_ref, vt_scr)

    fq_all = fq_ref[...]
    qs = [q_ref[h] for h in range(N_HEADS)]
    fqs = [fq_all[h:h + 1, :] for h in range(N_HEADS)]

    def logits(h, j):
        fk = fk_ref[h, _key_rows(j), :]
        fk = jnp.concatenate([fk] * (tq // LANES), axis=-1)
        return _nt_dot(k_ref[h, _key_rows(j), :], qs[h]) * scale + (fqs[h] - fk)

    ls = _online_softmax_t(i, logits, lambda h, j: vt_scr[h, j], tq, z_scr, acc_scr)
    for h in range(N_HEADS):
        o = (acc_scr[h] / ls[h]).T
        o_ref[h] = (o * _silu(g_ref[h].astype(F32))).astype(BF16)


def _fox(proj, fk_rep, fq_rows, *, batch, s_len):
    t = proj.shape[1]
    nq = s_len // Q_TILE
    nk = s_len // K_TILE
    qspec, kspec, ospec = _mixer_specs(s_len, nq)
    return pl.pallas_call(
        _fox_kernel,
        grid=(batch, nq),
        in_specs=[
            qspec(_SLAB["fox_q"]), kspec(_SLAB["fox_k"]), kspec(_SLAB["fox_v"]),
            qspec(_SLAB["fox_g"]),
            pl.BlockSpec((N_HEADS, s_len, LANES), lambda b, i: (0, b, 0)),
            pl.BlockSpec((None, GATE_ROWS, Q_TILE), lambda b, i: (b, 0, i)),
        ],
        out_specs=ospec,
        out_shape=jax.ShapeDtypeStruct((N_HEADS, t, LANES), BF16),
        scratch_shapes=[pltpu.VMEM((N_HEADS, nk, HEAD_DIM, K_TILE), BF16),
                        pltpu.VMEM((2, N_HEADS, K_TILE, Q_TILE), F32),
                        pltpu.VMEM((N_HEADS, HEAD_DIM, Q_TILE), F32)],
        compiler_params=pltpu.CompilerParams(dimension_semantics=("arbitrary", "arbitrary")),
        name="fox",
    )(proj, proj, proj, proj, fk_rep, fq_rows)


def _diff_kernel(lam_init, q_ref, k_ref, v_ref, g_ref, lam_ref, ng_ref,
                 cq_ref, uq_ref, dq_ref, ck_ref, uk_ref, dk_ref, o_ref,
                 kr_scr, vt_scr, z_scr, acc_scr):
    i = pl.program_id(1)
    tq = q_ref.shape[1]
    half = DIFF_QK_DIM // ROPE_FRACTION // 2
    scale = DIFF_QK_DIM ** -0.5

    @pl.when(i == 0)
    def _():
        _transpose_values(v_ref, vt_scr)
        for h in range(N_HEADS):
            kr_scr[h] = _rope(k_ref[h].astype(F32), ck_ref[...], uk_ref[...], dk_ref[...],
                              half).astype(BF16)

    lf = lam_ref[...]
    lam = (jnp.exp(jnp.sum(lf[0:1] * lf[1:2], axis=-1, keepdims=True))
           - jnp.exp(jnp.sum(lf[2:3] * lf[3:4], axis=-1, keepdims=True)) + lam_init)

    first = lax.broadcasted_iota(jnp.int32, (tq, LANES), 1) < DIFF_QK_DIM
    q12 = []
    for h in range(N_HEADS):
        q = _rope(q_ref[h].astype(F32), cq_ref[...], uq_ref[...], dq_ref[...], half)
        q12.append(jnp.concatenate([jnp.where(first, q, 0.0),
                                    jnp.where(first, 0.0, q)], axis=0).astype(BF16))

    def logits(h, j):
        return _nt_dot(kr_scr[h, _key_rows(j), :], q12[h]) * scale

    ls = _online_softmax_t(i, logits, lambda h, j: vt_scr[h, j], tq, z_scr, acc_scr)
    for h in range(N_HEADS):
        a = acc_scr[h] / ls[h]
        o = (a[:, :tq] - lam * a[:, tq:]).T
        o = o * lax.rsqrt(jnp.mean(o * o, axis=-1, keepdims=True) + RMS_EPS) * ng_ref[...]
        o = o * (1.0 - lam_init)
        o_ref[h] = (o * _silu(g_ref[h].astype(F32))).astype(BF16)


def _diff(proj, diff_lam, diff_norm_g, tables, lam_init, *, batch, s_len):
    t = proj.shape[1]
    nq = s_len // Q_TILE
    nk = s_len // K_TILE
    qspec, kspec, ospec = _mixer_specs(s_len, nq)
    tq_spec = pl.BlockSpec((Q_TILE, LANES), lambda b, i: (i, 0))
    tk_spec = pl.BlockSpec((s_len, LANES), lambda b, i: (0, 0))
    return pl.pallas_call(
        functools.partial(_diff_kernel, lam_init),
        grid=(batch, nq),
        in_specs=[
            qspec(_SLAB["diff_q"]), kspec(_SLAB["diff_k"]), kspec(_SLAB["diff_v"]),
            qspec(_SLAB["diff_g"]),
            pl.BlockSpec((4, DIFF_QK_DIM), lambda b, i: (0, 0)),
            pl.BlockSpec((1, HEAD_DIM), lambda b, i: (0, 0)),
            tq_spec, tq_spec, tq_spec, tk_spec, tk_spec, tk_spec,
        ],
        out_specs=ospec,
        out_shape=jax.ShapeDtypeStruct((N_HEADS, t, LANES), BF16),
        scratch_shapes=[pltpu.VMEM((N_HEADS, s_len, LANES), BF16),
                        pltpu.VMEM((N_HEADS, nk, HEAD_DIM, K_TILE), BF16),
                        pltpu.VMEM((2, N_HEADS, K_TILE, 2 * Q_TILE), F32),
                        pltpu.VMEM((N_HEADS, HEAD_DIM, 2 * Q_TILE), F32)],
        compiler_params=pltpu.CompilerParams(dimension_semantics=("arbitrary", "arbitrary")),
        name="diff",
    )(proj, proj, proj, proj, diff_lam, diff_norm_g.reshape(1, HEAD_DIM), *tables, *tables)


def _moba_kernel(q_ref, k_ref, v_ref, g_ref, cq_ref, uq_ref, dq_ref, ck_ref, uk_ref, dk_ref,
                 o_ref, kr_scr, kmean_scr, vt_scr, z_scr, acc_scr):
    i = pl.program_id(1)
    tq = q_ref.shape[1]
    half = HEAD_DIM // ROPE_FRACTION // 2
    scale = HEAD_DIM ** -0.5
    n_blk = k_ref.shape[1] // MOBA_BLOCK

    @pl.when(i == 0)
    def _():
        _transpose_values(v_ref, vt_scr)
        kmean_scr[...] = jnp.zeros(kmean_scr.shape, F32)
        for h in range(N_HEADS):
            kr = _rope(k_ref[h].astype(F32), ck_ref[...], uk_ref[...], dk_ref[...], half)
            kr_scr[h] = kr.astype(BF16)
            for n in range(n_blk):
                blk = kr[n * MOBA_BLOCK:(n + 1) * MOBA_BLOCK]
                kmean_scr[h, n:n + 1, :] = jnp.sum(blk, axis=0, keepdims=True) * (1.0 / MOBA_BLOCK)

    blk_id = lax.broadcasted_iota(jnp.int32, (GATE_ROWS, tq), 0).astype(F32)
    past = blk_id < i.astype(F32)
    qs, biases = [], []
    for h in range(N_HEADS):
        q = _rope(q_ref[h].astype(F32), cq_ref[...], uq_ref[...], dq_ref[...], half).astype(BF16)
        gate = _nt_dot(kmean_scr[h].astype(BF16), q)
        gate = jnp.where(past, gate, -jnp.inf)
        open_ = past
        sel = jnp.zeros((GATE_ROWS, tq), jnp.bool_)
        for _ in range(MOBA_TOPK):
            top = jnp.max(gate, axis=0, keepdims=True)
            idx = jnp.min(jnp.where(gate == top, blk_id, float(GATE_ROWS)), axis=0, keepdims=True)
            pick = (blk_id == idx) & open_
            sel = sel | pick
            open_ = open_ & jnp.logical_not(pick)
            gate = jnp.where(pick, -jnp.inf, gate)
        qs.append(q)
        biases.append(jnp.where(sel, 0.0, MASKED_LOGIT))

    def logits(h, j):
        bias = jnp.sum(jnp.where(blk_id == j.astype(F32), biases[h], 0.0), axis=0, keepdims=True)
        bias = jnp.where(j == i, 0.0, bias)
        return _nt_dot(kr_scr[h, _key_rows(j), :], qs[h]) * scale + bias

    ls = _online_softmax_t(i, logits, lambda h, j: vt_scr[h, j], tq, z_scr, acc_scr)
    for h in range(N_HEADS):
        o = (acc_scr[h] / ls[h]).T
        o_ref[h] = (o * _silu(g_ref[h].astype(F32))).astype(BF16)


def _moba(proj, tables, *, batch, s_len):
    t = proj.shape[1]
    nq = s_len // Q_TILE
    nk = s_len // K_TILE
    qspec, kspec, ospec = _mixer_specs(s_len, nq)
    tq_spec = pl.BlockSpec((Q_TILE, LANES), lambda b, i: (i, 0))
    tk_spec = pl.BlockSpec((s_len, LANES), lambda b, i: (0, 0))
    return pl.pallas_call(
        _moba_kernel,
        grid=(batch, nq),
        in_specs=[
            qspec(_SLAB["moba_q"]), kspec(_SLAB["moba_k"]), kspec(_SLAB["moba_v"]),
            qspec(_SLAB["moba_g"]),
            tq_spec, tq_spec, tq_spec, tk_spec, tk_spec, tk_spec,
        ],
        out_specs=ospec,
        out_shape=jax.ShapeDtypeStruct((N_HEADS, t, LANES), BF16),
        scratch_shapes=[pltpu.VMEM((N_HEADS, s_len, LANES), BF16),
                        pltpu.VMEM((N_HEADS, GATE_ROWS, LANES), F32),
                        pltpu.VMEM((N_HEADS, nk, HEAD_DIM, K_TILE), BF16),
                        pltpu.VMEM((2, N_HEADS, K_TILE, Q_TILE), F32),
                        pltpu.VMEM((N_HEADS, HEAD_DIM, Q_TILE), F32)],
        compiler_params=pltpu.CompilerParams(dimension_semantics=("arbitrary", "arbitrary")),
        name="moba",
    )(proj, proj, proj, proj, *tables, *tables)


def _gla_kernel(x_ref, v_ref, g_ref, p_ref, ng_ref, o_ref):
    s_len = x_ref.shape[1]
    c_len = GLA_CHUNK
    scale = GLA_DK ** -0.5
    lane = lax.broadcasted_iota(jnp.int32, (c_len, LANES), 1)
    qhalf = lane < GLA_DK
    qhalf_row = lax.broadcasted_iota(jnp.int32, (1, LANES), 1) < GLA_DK
    row = lax.broadcasted_iota(jnp.int32, (c_len, c_len), 0)
    col = lax.broadcasted_iota(jnp.int32, (c_len, c_len), 1)
    causal = col <= row

    def body(c, carries):
        rows = pl.ds(pl.multiple_of(c * c_len, c_len), c_len)
        out = []
        for h in range(N_HEADS):
            state_t, base = carries[h]
            x = x_ref[h, rows, :].astype(F32)
            v = v_ref[h, rows, :]
            p = p_ref[h, rows, :]
            cum = p - base
            xt = x * jnp.exp(jnp.where(qhalf, cum, -cum))
            last = cum[c_len - 1:c_len]
            qz = jnp.where(qhalf, xt, 0.0).astype(BF16)
            kr = pltpu.roll(xt, GLA_DK, 1)
            scores = jnp.where(causal, _nt_dot(qz, kr.astype(BF16)) * scale, 0.0)
            o = jnp.dot(scores.astype(BF16), v, preferred_element_type=F32)
            o = o + _nt_dot(qz, state_t.astype(BF16)) * scale
            decay = jnp.where(qhalf_row, jnp.exp(last), 0.0)
            khat = (jnp.where(qhalf, kr, 0.0) * decay).astype(BF16)
            upd = lax.dot_general(v, khat, (((0,), (0,)), ((), ())),
                                  preferred_element_type=F32)
            state_t = state_t * decay + upd
            y = o * lax.rsqrt(jnp.mean(o * o, axis=-1, keepdims=True) + RMS_EPS) * ng_ref[...]
            o_ref[h, rows, :] = (y * _silu(g_ref[h, rows, :].astype(F32))).astype(BF16)
            out.append((state_t, p[c_len - 1:c_len]))
        return tuple(out)

    init = tuple((jnp.zeros((HEAD_DIM, LANES), F32), jnp.zeros((1, LANES), F32))
                 for _ in range(N_HEADS))
    lax.fori_loop(0, s_len // c_len, body, init)


def _gla(proj, p_cum, gla_norm_g, *, batch, s_len):
    t = proj.shape[1]
    spec = lambda slab: pl.BlockSpec((N_HEADS, s_len, LANES), lambda b: (slab, b, 0))
    return pl.pallas_call(
        _gla_kernel,
        grid=(batch,),
        in_specs=[
            spec(_SLAB["gla_qk"]), spec(_SLAB["gla_v"]), spec(_SLAB["gla_g"]),
            pl.BlockSpec((N_HEADS, s_len, LANES), lambda b: (0, b, 0)),
            pl.BlockSpec((1, HEAD_DIM), lambda b: (0, 0)),
        ],
        out_specs=pl.BlockSpec((N_HEADS, s_len, LANES), lambda b: (0, b, 0)),
        out_shape=jax.ShapeDtypeStruct((N_HEADS, t, LANES), BF16),
        compiler_params=pltpu.CompilerParams(dimension_semantics=("arbitrary",)),
        name="gla",
    )(proj, proj, proj, p_cum, gla_norm_g.reshape(1, HEAD_DIM))


def _outproj_kernel(final, x_ref, a_ref, b_ref, c_ref, d_ref, w_ref, fg_ref, o_ref):
    parts = [r[h] for r in (a_ref, b_ref, c_ref, d_ref) for h in range(N_HEADS)]
    mixed = jnp.concatenate(parts, axis=-1)
    y = x_ref[...] + jnp.dot(mixed, w_ref[...], preferred_element_type=F32)
    if final:
        y = y * lax.rsqrt(jnp.mean(y * y, axis=-1, keepdims=True) + RMS_EPS) * fg_ref[...]
    o_ref[...] = y


def _outproj(x2, mixers, w_out, final_g, *, final, tm):
    t, d = x2.shape
    mspec = pl.BlockSpec((N_HEADS, tm, LANES), lambda i: (0, i, 0))
    return pl.pallas_call(
        functools.partial(_outproj_kernel, final),
        grid=(t // tm,),
        in_specs=[
            pl.BlockSpec((tm, d), lambda i: (i, 0)),
            mspec, mspec, mspec, mspec,
            pl.BlockSpec(w_out.shape, lambda i: (0, 0)),
            pl.BlockSpec((1, d), lambda i: (0, 0)),
        ],
        out_specs=pl.BlockSpec((tm, d), lambda i: (i, 0)),
        out_shape=jax.ShapeDtypeStruct((t, d), F32),
        compiler_params=pltpu.CompilerParams(
            dimension_semantics=("arbitrary",), vmem_limit_bytes=56 * 1024 * 1024),
        name="outproj",
    )(x2, *mixers, w_out, final_g.reshape(1, d))


def _seg(w, name):
    off, width = _SEG_OFF[name]
    return w[..., off:off + width]


def _relayout_w_in(w_in):
    cols = [_seg(w_in, n) for n in _SLABS[:12]]
    gq, gk = _seg(w_in, "gla_q"), _seg(w_in, "gla_k")
    for h in range(N_HEADS):
        cols += [gq[..., h * GLA_DK:(h + 1) * GLA_DK], gk[..., h * GLA_DK:(h + 1) * GLA_DK]]
    cols += [_seg(w_in, "gla_v"), _seg(w_in, "gla_g")]
    main = jnp.concatenate(cols, axis=-1).astype(BF16)
    f, a = _seg(w_in, "fox_f"), _seg(w_in, "gla_a")
    pad = jnp.zeros(w_in.shape[:-1] + (LANES - N_HEADS - GLA_RANK,), w_in.dtype)
    small = jnp.concatenate([f, a, pad], axis=-1).astype(BF16)
    return main, small


def _dup_halves(a):
    parts = []
    for h in range(N_HEADS):
        blk = a[..., h * GLA_DK:(h + 1) * GLA_DK]
        parts += [blk, blk]
    return jnp.concatenate(parts, axis=-1)


def _rope_tables(s_len, comp_dim, rot_dim):
    half = rot_dim // 2
    inv_freq = ROPE_THETA ** (-jnp.arange(0, rot_dim, 2, dtype=F32) / rot_dim)
    ang = jnp.arange(s_len, dtype=F32)[:, None] * inv_freq[None, :]
    cos, sin = jnp.cos(ang), jnp.sin(ang)
    zeros = jnp.zeros((s_len, comp_dim - rot_dim), F32)
    ones = jnp.ones((s_len, comp_dim - rot_dim), F32)
    reps = LANES // comp_dim
    cos_t = jnp.tile(jnp.concatenate([cos, cos, ones], axis=-1), (1, reps))
    sin_up = jnp.tile(jnp.concatenate([-sin, jnp.zeros_like(sin), zeros], axis=-1), (1, reps))
    sin_dn = jnp.tile(jnp.concatenate([jnp.zeros_like(sin), sin, zeros], axis=-1), (1, reps))
    return cos_t, sin_up, sin_dn


def kernel(x, norm_g, w_in, fox_fb, diff_lam, diff_norm_g, gla_wa2, gla_ba, gla_norm_g, w_out,
           final_norm_g):
    batch, s_len, d_model = x.shape
    depth = w_in.shape[0]
    assert s_len % Q_TILE == 0 and Q_TILE == K_TILE == MOBA_BLOCK
    assert s_len // MOBA_BLOCK <= GATE_ROWS
    t = batch * s_len
    tm_in = min(512, t)
    tn_in = 1536
    tm_out = min(512, t)

    w_main, w_small = _relayout_w_in(w_in)
    w_out_b = w_out.astype(BF16)
    fb_pad = jnp.pad(fox_fb, ((0, 0), (0, LANES - N_HEADS)))[:, None, :]
    wa_pad = jnp.pad(_dup_halves(gla_wa2), ((0, 0), (N_HEADS, LANES - N_HEADS - GLA_RANK), (0, 0))
                     ).astype(BF16)
    ba_dup = _dup_halves(gla_ba)[:, None, :]
    rope_diff = _rope_tables(s_len, DIFF_QK_DIM, DIFF_QK_DIM // ROPE_FRACTION)
    rope_moba = _rope_tables(s_len, HEAD_DIM, HEAD_DIM // ROPE_FRACTION)

    x2 = x.reshape(t, d_model)
    for l in range(depth):
        proj, small = _inproj(x2, norm_g[l], w_main[l], w_small[l], tm=tm_in, tn=tn_in)
        fk_rep, fq_rows, p_cum = _gates(small, fb_pad[l], wa_pad[l], ba_dup[l],
                                        batch=batch, s_len=s_len)
        lam_init = 0.8 - 0.6 * math.exp(-0.3 * l)
        mixers = (
            _fox(proj, fk_rep, fq_rows, batch=batch, s_len=s_len),
            _diff(proj, diff_lam[l], diff_norm_g[l], rope_diff, lam_init, batch=batch, s_len=s_len),
            _moba(proj, rope_moba, batch=batch, s_len=s_len),
            _gla(proj, p_cum, gla_norm_g[l], batch=batch, s_len=s_len),
        )
        x2 = _outproj(x2, mixers, w_out_b[l], final_norm_g, final=(l == depth - 1), tm=tm_out)
    return x2.reshape(batch, s_len, d_model)
```

```python
import functools
import math

import jax
import jax.numpy as jnp
from jax import lax
from jax.experimental import pallas as pl
from jax.experimental.pallas import tpu as pltpu

F32 = jnp.float32
BF16 = jnp.bfloat16

LANES = 128
HEAD_DIM = 128
N_HEADS = 4
GROUP_W = N_HEADS * HEAD_DIM
DIFF_QK_DIM = HEAD_DIM // 2
MOBA_BLOCK = 256
MOBA_TOPK = 3
GLA_DK = HEAD_DIM // 2
GLA_RANK = 16
GLA_TAU = 16.0
GLA_CHUNK = 64
ROPE_THETA = 500000.0
ROPE_FRACTION = 4
RMS_EPS = 1e-6
MASKED_LOGIT = -1e30

_SEGMENTS = (
    ("fox_q", GROUP_W), ("fox_k", GROUP_W), ("fox_v", GROUP_W),
    ("fox_f", N_HEADS), ("fox_g", GROUP_W),
    ("diff_q", GROUP_W), ("diff_k", GROUP_W), ("diff_v", GROUP_W), ("diff_g", GROUP_W),
    ("moba_q", GROUP_W), ("moba_k", GROUP_W), ("moba_v", GROUP_W), ("moba_g", GROUP_W),
    ("gla_q", N_HEADS * GLA_DK), ("gla_k", N_HEADS * GLA_DK), ("gla_v", GROUP_W),
    ("gla_a", GLA_RANK), ("gla_g", GROUP_W),
)
_SEG_OFF = {}
_off = 0
for _name, _w in _SEGMENTS:
    _SEG_OFF[_name] = (_off, _w)
    _off += _w

_SLABS = ("fox_q", "fox_k", "fox_v", "fox_g", "diff_q", "diff_k", "diff_v", "diff_g",
          "moba_q", "moba_k", "moba_v", "moba_g", "gla_qk", "gla_v", "gla_g")
_SLAB = {name: i for i, name in enumerate(_SLABS)}
N_MAIN = len(_SLABS) * GROUP_W
_MAIN_RUNS = ((0, _SEG_OFF["fox_f"][0]),
              (_SEG_OFF["fox_g"][0], _SEG_OFF["gla_a"][0]),
              (_SEG_OFF["gla_g"][0], _off))
assert sum(b - a for a, b in _MAIN_RUNS) == N_MAIN

Q_TILE = 256
K_TILE = 256
CUM_BLOCK = 128
GATE_ROWS = 16
GLA_GROUP = 4


def _silu(x):
    return x * (1.0 / (1.0 + jnp.exp(-x)))


def _log_sigmoid(x):
    return jnp.minimum(x, 0.0) - jnp.log1p(jnp.exp(-jnp.abs(x)))


def _nt_dot(a, b):
    return lax.dot_general(a, b, (((1,), (1,)), ((), ())), preferred_element_type=F32)


def _inproj_kernel(x_ref, g_ref, w_ref, ws_ref, o_ref, small_ref, h_scr):
    @pl.when(pl.program_id(1) == 0)
    def _():
        x = x_ref[...]
        y = x * lax.rsqrt(jnp.mean(x * x, axis=-1, keepdims=True) + RMS_EPS) * g_ref[...]
        h = y.astype(BF16)
        h_scr[...] = h
        small_ref[...] = jnp.dot(h, ws_ref[...], preferred_element_type=F32)

    acc = jnp.dot(h_scr[...], w_ref[...], preferred_element_type=F32)
    for c in range(o_ref.shape[0]):
        o_ref[c] = acc[:, c * LANES:(c + 1) * LANES].astype(BF16)


def _inproj(x2, norm_g, w_main, w_small, layer, *, tm, tn):
    t, d = x2.shape
    n_main = w_main.shape[2]
    return pl.pallas_call(
        _inproj_kernel,
        grid=(t // tm, n_main // tn),
        in_specs=[
            pl.BlockSpec((tm, d), lambda i, j: (i, 0)),
            pl.BlockSpec((1, d), lambda i, j: (0, 0)),
            pl.BlockSpec((None, d, tn), lambda i, j: (layer, 0, j)),
            pl.BlockSpec((None, d, LANES), lambda i, j: (layer, 0, 0)),
        ],
        out_specs=[
            pl.BlockSpec((tn // LANES, tm, LANES), lambda i, j: (j, i, 0)),
            pl.BlockSpec((tm, LANES), lambda i, j: (i, 0)),
        ],
        out_shape=[
            jax.ShapeDtypeStruct((n_main // LANES, t, LANES), BF16),
            jax.ShapeDtypeStruct((t, LANES), F32),
        ],
        scratch_shapes=[pltpu.VMEM((tm, d), BF16)],
        compiler_params=pltpu.CompilerParams(
            dimension_semantics=("arbitrary", "arbitrary"),
            vmem_limit_bytes=56 * 1024 * 1024),
        name="inproj",
    )(x2, norm_g.reshape(1, d), w_main, w_small)


def _gates_kernel(small_ref, fb_ref, wa_ref, ba_ref, fk_ref, fq_ref, p_ref):
    s_len = small_ref.shape[0]
    small = small_ref[...]
    log_f = _log_sigmoid(small + fb_ref[...])
    a_logit = jnp.dot(small.astype(BF16), wa_ref[...], preferred_element_type=F32) + ba_ref[...]
    log_a = _log_sigmoid(a_logit) * (1.0 / GLA_TAU)
    both = jnp.concatenate([log_f, log_a], axis=-1)

    row = lax.broadcasted_iota(jnp.int32, (CUM_BLOCK, CUM_BLOCK), 0)
    col = lax.broadcasted_iota(jnp.int32, (CUM_BLOCK, CUM_BLOCK), 1)
    tri = (col <= row).astype(F32)
    carry = jnp.zeros((1, both.shape[1]), F32)
    for c in range(s_len // CUM_BLOCK):
        blk = both[c * CUM_BLOCK:(c + 1) * CUM_BLOCK]
        cum = jnp.dot(tri, blk, precision=lax.Precision.HIGHEST,
                      preferred_element_type=F32) + carry
        carry = cum[CUM_BLOCK - 1:CUM_BLOCK]
        rows = slice(c * CUM_BLOCK, (c + 1) * CUM_BLOCK)
        f_cum = cum[:, :LANES]
        fq_ref[:, rows] = f_cum.T[:GATE_ROWS]
        for h in range(N_HEADS):
            fk_ref[h, rows, :] = jnp.broadcast_to(f_cum[:, h:h + 1], (CUM_BLOCK, LANES))
            p_ref[h, rows, :] = cum[:, (h + 1) * LANES:(h + 2) * LANES]


def _gates(small, fb_pad, wa_pad, ba_dup, *, batch, s_len):
    t = small.shape[0]
    return pl.pallas_call(
        _gates_kernel,
        grid=(batch,),
        in_specs=[
            pl.BlockSpec((s_len, LANES), lambda b: (b, 0)),
            pl.BlockSpec((1, LANES), lambda b: (0, 0)),
            pl.BlockSpec((LANES, GROUP_W), lambda b: (0, 0)),
            pl.BlockSpec((1, GROUP_W), lambda b: (0, 0)),
        ],
        out_specs=[
            pl.BlockSpec((N_HEADS, s_len, LANES), lambda b: (0, b, 0)),
            pl.BlockSpec((None, GATE_ROWS, s_len), lambda b: (b, 0, 0)),
            pl.BlockSpec((N_HEADS, s_len, LANES), lambda b: (0, b, 0)),
        ],
        out_shape=[
            jax.ShapeDtypeStruct((N_HEADS, t, LANES), F32),
            jax.ShapeDtypeStruct((batch, GATE_ROWS, s_len), F32),
            jax.ShapeDtypeStruct((N_HEADS, t, LANES), F32),
        ],
        compiler_params=pltpu.CompilerParams(dimension_semantics=("arbitrary",)),
        name="gates",
    )(small, fb_pad, wa_pad, ba_dup)


def _causal_mask_t(tk, width, tq):
    key = lax.broadcasted_iota(jnp.int32, (tk, width), 0)
    qry = lax.broadcasted_iota(jnp.int32, (tk, width), 1)
    if width != tq:
        qry = qry & (tq - 1)
    return key <= qry


def _transpose_values(v_ref, vt_scr):
    for h in range(v_ref.shape[0]):
        for j in range(v_ref.shape[1] // K_TILE):
            blk = v_ref[h, j * K_TILE:(j + 1) * K_TILE, :].astype(F32)
            vt_scr[h, j] = blk.T.astype(BF16)


def _online_softmax_t(last, logits_fn, vt_fn, tq, z_scr, acc_scr):
    n, _, width = acc_scr.shape

    def update(j, slot, ms, ls, masked):
        stats = []
        for s in range(n):
            z = z_scr[slot, s]
            if masked:
                z = jnp.where(_causal_mask_t(K_TILE, width, tq), z, -jnp.inf)
            m_new = jnp.maximum(ms[s], jnp.max(z, axis=0, keepdims=True))
            alpha = jnp.exp(ms[s] - m_new)
            p = jnp.exp(z - m_new)
            stats.append((m_new, alpha * ls[s] + jnp.sum(p, axis=0, keepdims=True), alpha,
                          p.astype(BF16)))
        for s in range(n):
            pv = jnp.dot(vt_fn(s, j), stats[s][3], preferred_element_type=F32)
            acc_scr[s] = stats[s][2] * acc_scr[s] + pv
        return tuple(st[0] for st in stats), tuple(st[1] for st in stats)

    for s in range(n):
        z_scr[0, s] = logits_fn(s, jnp.int32(0))
        acc_scr[s] = jnp.zeros(acc_scr.shape[1:], F32)

    def body(j, state):
        ms, ls = state
        slot = lax.rem(j, 2)
        z_next = [logits_fn(s, j + 1) for s in range(n)]
        ms, ls = update(j, slot, ms, ls, False)
        for s in range(n):
            z_scr[1 - slot, s] = z_next[s]
        return ms, ls

    init = (tuple(jnp.full((1, width), -jnp.inf, F32) for _ in range(n)),
            tuple(jnp.zeros((1, width), F32) for _ in range(n)))
    ms, ls = lax.fori_loop(0, last, body, init)
    _, ls = update(last, lax.rem(last, 2), ms, ls, True)
    return ls


def _rope(x, cos_t, sin_up, sin_dn, half):
    return (x * cos_t + pltpu.roll(x, LANES - half, 1) * sin_up
            + pltpu.roll(x, half, 1) * sin_dn)


def _key_rows(j):
    return pl.ds(pl.multiple_of(j * K_TILE, K_TILE), K_TILE)


def _mixer_specs(s_len, nq):
    qspec = lambda slab: pl.BlockSpec((N_HEADS, Q_TILE, LANES), lambda b, i: (slab, b * nq + i, 0))
    kspec = lambda slab: pl.BlockSpec((N_HEADS, s_len, LANES), lambda b, i: (slab, b, 0))
    ospec = pl.BlockSpec((N_HEADS, Q_TILE, LANES), lambda b, i: (0, b * nq + i, 0))
    return qspec, kspec, ospec


def _fox_kernel(q_ref, k_ref, v_ref, g_ref, fk_ref, fq_ref, o_ref, vt_scr, z_scr, acc_scr):
    i = pl.program_id(1)
    tq = q_ref.shape[1]
    scale = HEAD_DIM ** -0.5

    @pl.when(i == 0)
    def _():
        _transpose_values(v_ref, vt_scr)

    fq_all = fq_ref[...]
    qs = [q_ref[h] for h in range(N_HEADS)]
    fqs = [fq_all[h:h + 1, :] for h in range(N_HEADS)]

    def logits(h, j):
        fk = fk_ref[h, _key_rows(j), :]
        fk = jnp.concatenate([fk] * (tq // LANES), axis=-1)
        return _nt_dot(k_ref[h, _key_rows(j), :], qs[h]) * scale + (fqs[h] - fk)

    ls = _online_softmax_t(i, logits, lambda h, j: vt_scr[h, j], tq, z_scr, acc_scr)
    for h in range(N_HEADS):
        o = (acc_scr[h] / ls[h]).T
        o_ref[h] = (o * _silu(g_ref[h].astype(F32))).astype(BF16)


def _fox(proj, fk_rep, fq_rows, *, batch, s_len):
    t = proj.shape[1]
    nq = s_len // Q_TILE
    nk = s_len // K_TILE
    qspec, kspec, ospec = _mixer_specs(s_len, nq)
    return pl.pallas_call(
        _fox_kernel,
        grid=(batch, nq),
        in_specs=[
            qspec(_SLAB["fox_q"]), kspec(_SLAB["fox_k"]), kspec(_SLAB["fox_v"]),
            qspec(_SLAB["fox_g"]),
            pl.BlockSpec((N_HEADS, s_len, LANES), lambda b, i: (0, b, 0)),
            pl.BlockSpec((None, GATE_ROWS, Q_TILE), lambda b, i: (b, 0, i)),
        ],
        out_specs=ospec,
        out_shape=jax.ShapeDtypeStruct((N_HEADS, t, LANES), BF16),
        scratch_shapes=[pltpu.VMEM((N_HEADS, nk, HEAD_DIM, K_TILE), BF16),
                        pltpu.VMEM((2, N_HEADS, K_TILE, Q_TILE), F32),
                        pltpu.VMEM((N_HEADS, HEAD_DIM, Q_TILE), F32)],
        compiler_params=pltpu.CompilerParams(dimension_semantics=("arbitrary", "arbitrary")),
        name="fox",
    )(proj, proj, proj, proj, fk_rep, fq_rows)


def _diff_kernel(lam_init, q_ref, k_ref, v_ref, g_ref, lam_ref, ng_ref,
                 cq_ref, uq_ref, dq_ref, ck_ref, uk_ref, dk_ref, o_ref,
                 kr_scr, vt_scr, z_scr, acc_scr):
    i = pl.program_id(1)
    tq = q_ref.shape[1]
    half = DIFF_QK_DIM // ROPE_FRACTION // 2
    scale = DIFF_QK_DIM ** -0.5

    @pl.when(i == 0)
    def _():
        _transpose_values(v_ref, vt_scr)
        for h in range(N_HEADS):
            kr_scr[h] = _rope(k_ref[h].astype(F32), ck_ref[...], uk_ref[...], dk_ref[...],
                              half).astype(BF16)

    lf = lam_ref[...]
    lam = (jnp.exp(jnp.sum(lf[0:1] * lf[1:2], axis=-1, keepdims=True))
           - jnp.exp(jnp.sum(lf[2:3] * lf[3:4], axis=-1, keepdims=True)) + lam_init)

    first = lax.broadcasted_iota(jnp.int32, (tq, LANES), 1) < DIFF_QK_DIM
    q12 = []
    for h in range(N_HEADS):
        q = _rope(q_ref[h].astype(F32), cq_ref[...], uq_ref[...], dq_ref[...], half)
        q12.append(jnp.concatenate([jnp.where(first, q, 0.0),
                                    jnp.where(first, 0.0, q)], axis=0).astype(BF16))

    def logits(h, j):
        return _nt_dot(kr_scr[h, _key_rows(j), :], q12[h]) * scale

    ls = _online_softmax_t(i, logits, lambda h, j: vt_scr[h, j], tq, z_scr, acc_scr)
    for h in range(N_HEADS):
        a = acc_scr[h] / ls[h]
        o = (a[:, :tq] - lam * a[:, tq:]).T
        o = o * lax.rsqrt(jnp.mean(o * o, axis=-1, keepdims=True) + RMS_EPS) * ng_ref[...]
        o = o * (1.0 - lam_init)
        o_ref[h] = (o * _silu(g_ref[h].astype(F32))).astype(BF16)


def _diff(proj, diff_lam, diff_norm_g, tables, lam_init, *, batch, s_len):
    t = proj.shape[1]
    nq = s_len // Q_TILE
    nk = s_len // K_TILE
    qspec, kspec, ospec = _mixer_specs(s_len, nq)
    tq_spec = pl.BlockSpec((Q_TILE, LANES), lambda b, i: (i, 0))
    tk_spec = pl.BlockSpec((s_len, LANES), lambda b, i: (0, 0))
    return pl.pallas_call(
        functools.partial(_diff_kernel, lam_init),
        grid=(batch, nq),
        in_specs=[
            qspec(_SLAB["diff_q"]), kspec(_SLAB["diff_k"]), kspec(_SLAB["diff_v"]),
            qspec(_SLAB["diff_g"]),
            pl.BlockSpec((4, DIFF_QK_DIM), lambda b, i: (0, 0)),
            pl.BlockSpec((1, HEAD_DIM), lambda b, i: (0, 0)),
            tq_spec, tq_spec, tq_spec, tk_spec, tk_spec, tk_spec,
        ],
        out_specs=ospec,
        out_shape=jax.ShapeDtypeStruct((N_HEADS, t, LANES), BF16),
        scratch_shapes=[pltpu.VMEM((N_HEADS, s_len, LANES), BF16),
                        pltpu.VMEM((N_HEADS, nk, HEAD_DIM, K_TILE), BF16),
                        pltpu.VMEM((2, N_HEADS, K_TILE, 2 * Q_TILE), F32),
                        pltpu.VMEM((N_HEADS, HEAD_DIM, 2 * Q_TILE), F32)],
        compiler_params=pltpu.CompilerParams(dimension_semantics=("arbitrary", "arbitrary")),
        name="diff",
    )(proj, proj, proj, proj, diff_lam, diff_norm_g.reshape(1, HEAD_DIM), *tables, *tables)


def _moba_kernel(q_ref, k_ref, v_ref, g_ref, cq_ref, uq_ref, dq_ref, ck_ref, uk_ref, dk_ref,
                 o_ref, kr_scr, kmean_scr, vt_scr, z_scr, acc_scr):
    i = pl.program_id(1)
    tq = q_ref.shape[1]
    half = HEAD_DIM // ROPE_FRACTION // 2
    scale = HEAD_DIM ** -0.5
    n_blk = k_ref.shape[1] // MOBA_BLOCK

    @pl.when(i == 0)
    def _():
        _transpose_values(v_ref, vt_scr)
        kmean_scr[...] = jnp.zeros(kmean_scr.shape, F32)
        for h in range(N_HEADS):
            kr = _rope(k_ref[h].astype(F32), ck_ref[...], uk_ref[...], dk_ref[...], half)
            kr_scr[h] = kr.astype(BF16)
            for n in range(n_blk):
                blk = kr[n * MOBA_BLOCK:(n + 1) * MOBA_BLOCK]
                kmean_scr[h, n:n + 1, :] = jnp.sum(blk, axis=0, keepdims=True) * (1.0 / MOBA_BLOCK)

    blk_id = lax.broadcasted_iota(jnp.int32, (GATE_ROWS, tq), 0).astype(F32)
    past = blk_id < i.astype(F32)
    qs, biases = [], []
    for h in range(N_HEADS):
        q = _rope(q_ref[h].astype(F32), cq_ref[...], uq_ref[...], dq_ref[...], half).astype(BF16)
        gate = _nt_dot(kmean_scr[h].astype(BF16), q)
        gate = jnp.where(past, gate, -jnp.inf)
        open_ = past
        sel = jnp.zeros((GATE_ROWS, tq), jnp.bool_)
        for _ in range(MOBA_TOPK):
            top = jnp.max(gate, axis=0, keepdims=True)
            idx = jnp.min(jnp.where(gate == top, blk_id, float(GATE_ROWS)), axis=0, keepdims=True)
            pick = (blk_id == idx) & open_
            sel = sel | pick
            open_ = open_ & jnp.logical_not(pick)
            gate = jnp.where(pick, -jnp.inf, gate)
        qs.append(q)
        biases.append(jnp.where(sel, 0.0, MASKED_LOGIT))

    def logits(h, j):
        bias = jnp.sum(jnp.where(blk_id == j.astype(F32), biases[h], 0.0), axis=0, keepdims=True)
        bias = jnp.where(j == i, 0.0, bias)
        return _nt_dot(kr_scr[h, _key_rows(j), :], qs[h]) * scale + bias

    ls = _online_softmax_t(i, logits, lambda h, j: vt_scr[h, j], tq, z_scr, acc_scr)
    for h in range(N_HEADS):
        o = (acc_scr[h] / ls[h]).T
        o_ref[h] = (o * _silu(g_ref[h].astype(F32))).astype(BF16)


def _moba(proj, tables, *, batch, s_len):
    t = proj.shape[1]
    nq = s_len // Q_TILE
    nk = s_len // K_TILE
    qspec, kspec, ospec = _mixer_specs(s_len, nq)
    tq_spec = pl.BlockSpec((Q_TILE, LANES), lambda b, i: (i, 0))
    tk_spec = pl.BlockSpec((s_len, LANES), lambda b, i: (0, 0))
    return pl.pallas_call(
        _moba_kernel,
        grid=(batch, nq),
        in_specs=[
            qspec(_SLAB["moba_q"]), kspec(_SLAB["moba_k"]), kspec(_SLAB["moba_v"]),
            qspec(_SLAB["moba_g"]),
            tq_spec, tq_spec, tq_spec, tk_spec, tk_spec, tk_spec,
        ],
        out_specs=ospec,
        out_shape=jax.ShapeDtypeStruct((N_HEADS, t, LANES), BF16),
        scratch_shapes=[pltpu.VMEM((N_HEADS, s_len, LANES), BF16),
                        pltpu.VMEM((N_HEADS, GATE_ROWS, LANES), F32),
                        pltpu.VMEM((N_HEADS, nk, HEAD_DIM, K_TILE), BF16),
                        pltpu.VMEM((2, N_HEADS, K_TILE, Q_TILE), F32),
                        pltpu.VMEM((N_HEADS, HEAD_DIM, Q_TILE), F32)],
        compiler_params=pltpu.CompilerParams(dimension_semantics=("arbitrary", "arbitrary")),
        name="moba",
    )(proj, proj, proj, proj, *tables, *tables)


def _gla_kernel(x_ref, v_ref, g_ref, p_ref, ng_ref, o_ref):
    s_len = x_ref.shape[1]
    c_len = GLA_CHUNK
    scale = GLA_DK ** -0.5
    lane = lax.broadcasted_iota(jnp.int32, (c_len, LANES), 1)
    qhalf = lane < GLA_DK
    qhalf_row = lax.broadcasted_iota(jnp.int32, (1, LANES), 1) < GLA_DK
    row = lax.broadcasted_iota(jnp.int32, (c_len, c_len), 0)
    col = lax.broadcasted_iota(jnp.int32, (c_len, c_len), 1)
    causal = col <= row

    group = GLA_GROUP
    pairs = [(h, c) for h in range(N_HEADS) for c in range(group)]

    def body(g, carries):
        prep = {}
        for h, c in pairs:
            rows = pl.ds(pl.multiple_of(g * (group * c_len), group * c_len) + c * c_len, c_len)
            qcb = x_ref[h // 2, rows, :].astype(F32)
            kcb = x_ref[2 + h // 2, rows, :].astype(F32)
            if h % 2 == 0:
                x = jnp.where(qhalf, qcb, pltpu.roll(kcb, GLA_DK, 1))
            else:
                x = jnp.where(qhalf, pltpu.roll(qcb, GLA_DK, 1), kcb)
            p = p_ref[h, rows, :]
            base = carries[h][1] if c == 0 else prep[h, c - 1]["p_last"]
            cum = p - base
            xt = x * jnp.exp(jnp.where(qhalf, cum, -cum))
            kr = pltpu.roll(xt, GLA_DK, 1)
            decay = jnp.where(qhalf_row, jnp.exp(cum[c_len - 1:c_len]), 0.0)
            prep[h, c] = dict(
                rows=rows, qz=jnp.where(qhalf, xt, 0.0).astype(BF16), kr=kr.astype(BF16),
                khat=(jnp.where(qhalf, kr, 0.0) * decay).astype(BF16),
                decay=decay, v=v_ref[h, rows, :], p_last=p[c_len - 1:c_len])
        scores, upd = {}, {}
        for h, c in pairs:
            d = prep[h, c]
            scores[h, c] = _nt_dot(d["qz"], d["kr"])
            upd[h, c] = lax.dot_general(d["v"], d["khat"], (((0,), (0,)), ((), ())),
                                        preferred_element_type=F32)
        states = {}
        for h in range(N_HEADS):
            states[h, 0] = carries[h][0]
            for c in range(group):
                states[h, c + 1] = states[h, c] * prep[h, c]["decay"] + upd[h, c]
        outs = {}
        for h, c in pairs:
            sc = jnp.where(causal, scores[h, c] * scale, 0.0).astype(BF16)
            outs[h, c] = (jnp.dot(sc, prep[h, c]["v"], preferred_element_type=F32),
                          _nt_dot(prep[h, c]["qz"], states[h, c].astype(BF16)))
        for h, c in pairs:
            o = outs[h, c][0] + outs[h, c][1] * scale
            y = o * lax.rsqrt(jnp.mean(o * o, axis=-1, keepdims=True) + RMS_EPS) * ng_ref[...]
            rows = prep[h, c]["rows"]
            o_ref[h, rows, :] = (y * _silu(g_ref[h, rows, :].astype(F32))).astype(BF16)
        return tuple((states[h, group], prep[h, group - 1]["p_last"]) for h in range(N_HEADS))

    init = tuple((jnp.zeros((HEAD_DIM, LANES), F32), jnp.zeros((1, LANES), F32))
                 for _ in range(N_HEADS))
    lax.fori_loop(0, s_len // (group * c_len), body, init)


def _gla(proj, p_cum, gla_norm_g, *, batch, s_len):
    t = proj.shape[1]
    spec = lambda slab: pl.BlockSpec((N_HEADS, s_len, LANES), lambda b: (slab, b, 0))
    return pl.pallas_call(
        _gla_kernel,
        grid=(batch,),
        in_specs=[
            spec(_SLAB["gla_qk"]), spec(_SLAB["gla_v"]), spec(_SLAB["gla_g"]),
            pl.BlockSpec((N_HEADS, s_len, LANES), lambda b: (0, b, 0)),
            pl.BlockSpec((1, HEAD_DIM), lambda b: (0, 0)),
        ],
        out_specs=pl.BlockSpec((N_HEADS, s_len, LANES), lambda b: (0, b, 0)),
        out_shape=jax.ShapeDtypeStruct((N_HEADS, t, LANES), BF16),
        compiler_params=pltpu.CompilerParams(dimension_semantics=("arbitrary",)),
        name="gla",
    )(proj, proj, proj, p_cum, gla_norm_g.reshape(1, HEAD_DIM))


def _outproj_kernel(final, x_ref, a_ref, b_ref, c_ref, d_ref, w_ref, fg_ref, o_ref):
    parts = [r[h] for r in (a_ref, b_ref, c_ref, d_ref) for h in range(N_HEADS)]
    mixed = jnp.concatenate(parts, axis=-1)
    y = x_ref[...] + jnp.dot(mixed, w_ref[...], preferred_element_type=F32)
    if final:
        y = y * lax.rsqrt(jnp.mean(y * y, axis=-1, keepdims=True) + RMS_EPS) * fg_ref[...]
    o_ref[...] = y


def _outproj(x2, mixers, w_out, layer, final_g, *, final, tm):
    t, d = x2.shape
    mspec = pl.BlockSpec((N_HEADS, tm, LANES), lambda i: (0, i, 0))
    return pl.pallas_call(
        functools.partial(_outproj_kernel, final),
        grid=(t // tm,),
        in_specs=[
            pl.BlockSpec((tm, d), lambda i: (i, 0)),
            mspec, mspec, mspec, mspec,
            pl.BlockSpec((None,) + w_out.shape[1:], lambda i: (layer, 0, 0)),
            pl.BlockSpec((1, d), lambda i: (0, 0)),
        ],
        out_specs=pl.BlockSpec((tm, d), lambda i: (i, 0)),
        out_shape=jax.ShapeDtypeStruct((t, d), F32),
        compiler_params=pltpu.CompilerParams(
            dimension_semantics=("arbitrary",), vmem_limit_bytes=56 * 1024 * 1024),
        name="outproj",
    )(x2, *mixers, w_out, final_g.reshape(1, d))


def _seg(w, name):
    off, width = _SEG_OFF[name]
    return w[..., off:off + width]


def _relayout_kernel(w_ref, o_ref):
    w = w_ref[...]
    pieces = [w[:, a:b] for a, b in _MAIN_RUNS]
    o_ref[...] = jnp.concatenate(pieces, axis=-1).astype(BF16)


def _relayout_w_in(w_in, *, rows):
    depth, d, d_in = w_in.shape
    main = pl.pallas_call(
        _relayout_kernel,
        grid=(depth, d // rows),
        in_specs=[pl.BlockSpec((None, rows, d_in), lambda l, r: (l, r, 0))],
        out_specs=pl.BlockSpec((None, rows, N_MAIN), lambda l, r: (l, r, 0)),
        out_shape=jax.ShapeDtypeStruct((depth, d, N_MAIN), BF16),
        compiler_params=pltpu.CompilerParams(dimension_semantics=("arbitrary", "arbitrary")),
        name="relayout",
    )(w_in)
    f, a = _seg(w_in, "fox_f"), _seg(w_in, "gla_a")
    pad = jnp.zeros(w_in.shape[:-1] + (LANES - N_HEADS - GLA_RANK,), w_in.dtype)
    small = jnp.concatenate([f, a, pad], axis=-1).astype(BF16)
    return main, small


def _dup_halves(a):
    parts = []
    for h in range(N_HEADS):
        blk = a[..., h * GLA_DK:(h + 1) * GLA_DK]
        parts += [blk, blk]
    return jnp.concatenate(parts, axis=-1)


def _rope_tables(s_len, comp_dim, rot_dim):
    half = rot_dim // 2
    inv_freq = ROPE_THETA ** (-jnp.arange(0, rot_dim, 2, dtype=F32) / rot_dim)
    ang = jnp.arange(s_len, dtype=F32)[:, None] * inv_freq[None, :]
    cos, sin = jnp.cos(ang), jnp.sin(ang)
    zeros = jnp.zeros((s_len, comp_dim - rot_dim), F32)
    ones = jnp.ones((s_len, comp_dim - rot_dim), F32)
    reps = LANES // comp_dim
    cos_t = jnp.tile(jnp.concatenate([cos, cos, ones], axis=-1), (1, reps))
    sin_up = jnp.tile(jnp.concatenate([-sin, jnp.zeros_like(sin), zeros], axis=-1), (1, reps))
    sin_dn = jnp.tile(jnp.concatenate([jnp.zeros_like(sin), sin, zeros], axis=-1), (1, reps))
    return cos_t, sin_up, sin_dn


def kernel(x, norm_g, w_in, fox_fb, diff_lam, diff_norm_g, gla_wa2, gla_ba, gla_norm_g, w_out,
           final_norm_g):
    batch, s_len, d_model = x.shape
    depth = w_in.shape[0]
    assert s_len % Q_TILE == 0 and Q_TILE == K_TILE == MOBA_BLOCK
    assert s_len // MOBA_BLOCK <= GATE_ROWS
    t = batch * s_len
    tm_in = min(512, t)
    tn_in = 1536
    tm_out = min(512, t)

    w_main, w_small = _relayout_w_in(w_in, rows=min(256, d_model))
    w_out_b = w_out.astype(BF16)
    fb_pad = jnp.pad(fox_fb, ((0, 0), (0, LANES - N_HEADS)))[:, None, :]
    wa_pad = jnp.pad(_dup_halves(gla_wa2), ((0, 0), (N_HEADS, LANES - N_HEADS - GLA_RANK), (0, 0))
                     ).astype(BF16)
    ba_dup = _dup_halves(gla_ba)[:, None, :]
    rope_diff = _rope_tables(s_len, DIFF_QK_DIM, DIFF_QK_DIM // ROPE_FRACTION)
    rope_moba = _rope_tables(s_len, HEAD_DIM, HEAD_DIM // ROPE_FRACTION)

    x2 = x.reshape(t, d_model)
    for l in range(depth):
        proj, small = _inproj(x2, norm_g[l], w_main, w_small, l, tm=tm_in, tn=tn_in)
        fk_rep, fq_rows, p_cum = _gates(small, fb_pad[l], wa_pad[l], ba_dup[l],
                                        batch=batch, s_len=s_len)
        lam_init = 0.8 - 0.6 * math.exp(-0.3 * l)
        mixers = (
            _fox(proj, fk_rep, fq_rows, batch=batch, s_len=s_len),
            _diff(proj, diff_lam[l], diff_norm_g[l], rope_diff, lam_init, batch=batch, s_len=s_len),
            _moba(proj, rope_moba, batch=batch, s_len=s_len),
            _gla(proj, p_cum, gla_norm_g[l], batch=batch, s_len=s_len),
        )
        x2 = _outproj(x2, mixers, w_out_b, l, final_norm_g, final=(l == depth - 1), tm=tm_out)
    return x2.reshape(batch, s_len, d_model)
```

```python
import functools
import math

import jax
import jax.numpy as jnp
from jax import lax
from jax.experimental import pallas as pl
from jax.experimental.pallas import tpu as pltpu

F32 = jnp.float32
BF16 = jnp.bfloat16

LANES = 128
HEAD_DIM = 128
N_HEADS = 4
GROUP_W = N_HEADS * HEAD_DIM
DIFF_QK_DIM = HEAD_DIM // 2
MOBA_BLOCK = 256
MOBA_TOPK = 3
GLA_DK = HEAD_DIM // 2
GLA_RANK = 16
GLA_TAU = 16.0
GLA_CHUNK = 64
ROPE_THETA = 500000.0
ROPE_FRACTION = 4
RMS_EPS = 1e-6
MASKED_LOGIT = -1e30

_SEGMENTS = (
    ("fox_q", GROUP_W), ("fox_k", GROUP_W), ("fox_v", GROUP_W),
    ("fox_f", N_HEADS), ("fox_g", GROUP_W),
    ("diff_q", GROUP_W), ("diff_k", GROUP_W), ("diff_v", GROUP_W), ("diff_g", GROUP_W),
    ("moba_q", GROUP_W), ("moba_k", GROUP_W), ("moba_v", GROUP_W), ("moba_g", GROUP_W),
    ("gla_q", N_HEADS * GLA_DK), ("gla_k", N_HEADS * GLA_DK), ("gla_v", GROUP_W),
    ("gla_a", GLA_RANK), ("gla_g", GROUP_W),
)
_SEG_OFF = {}
_off = 0
for _name, _w in _SEGMENTS:
    _SEG_OFF[_name] = (_off, _w)
    _off += _w

_SLABS = ("fox_q", "fox_k", "fox_v", "fox_g", "diff_q", "diff_k", "diff_v", "diff_g",
          "moba_q", "moba_k", "moba_v", "moba_g", "gla_qk", "gla_v", "gla_g")
_SLAB = {name: i for i, name in enumerate(_SLABS)}
N_MAIN = len(_SLABS) * GROUP_W
_MAIN_RUNS = ((0, _SEG_OFF["fox_f"][0]),
              (_SEG_OFF["fox_g"][0], _SEG_OFF["gla_a"][0]),
              (_SEG_OFF["gla_g"][0], _off))
assert sum(b - a for a, b in _MAIN_RUNS) == N_MAIN

Q_TILE = 256
K_TILE = 256
CUM_BLOCK = 128
GATE_ROWS = 16
GLA_GROUP = 4


def _silu(x):
    return x * (1.0 / (1.0 + jnp.exp(-x)))


def _log_sigmoid(x):
    return jnp.minimum(x, 0.0) - jnp.log1p(jnp.exp(-jnp.abs(x)))


def _nt_dot(a, b):
    return lax.dot_general(a, b, (((1,), (1,)), ((), ())), preferred_element_type=F32)


def _inproj_kernel(x_ref, g_ref, w_ref, ws_ref, o_ref, small_ref, h_scr):
    @pl.when(pl.program_id(1) == 0)
    def _():
        x = x_ref[...]
        y = x * lax.rsqrt(jnp.mean(x * x, axis=-1, keepdims=True) + RMS_EPS) * g_ref[...]
        h = y.astype(BF16)
        h_scr[...] = h
        small_ref[...] = jnp.dot(h, ws_ref[...], preferred_element_type=F32)

    acc = jnp.dot(h_scr[...], w_ref[...], preferred_element_type=F32)
    for c in range(o_ref.shape[0]):
        o_ref[c] = acc[:, c * LANES:(c + 1) * LANES].astype(BF16)


def _inproj(x2, norm_g, w_main, w_small, layer, *, tm, tn):
    t, d = x2.shape
    n_main = w_main.shape[2]
    return pl.pallas_call(
        _inproj_kernel,
        grid=(t // tm, n_main // tn),
        in_specs=[
            pl.BlockSpec((tm, d), lambda i, j: (i, 0)),
            pl.BlockSpec((1, d), lambda i, j: (0, 0)),
            pl.BlockSpec((None, d, tn), lambda i, j: (layer, 0, j)),
            pl.BlockSpec((None, d, LANES), lambda i, j: (layer, 0, 0)),
        ],
        out_specs=[
            pl.BlockSpec((tn // LANES, tm, LANES), lambda i, j: (j, i, 0)),
            pl.BlockSpec((tm, LANES), lambda i, j: (i, 0)),
        ],
        out_shape=[
            jax.ShapeDtypeStruct((n_main // LANES, t, LANES), BF16),
            jax.ShapeDtypeStruct((t, LANES), F32),
        ],
        scratch_shapes=[pltpu.VMEM((tm, d), BF16)],
        compiler_params=pltpu.CompilerParams(
            dimension_semantics=("arbitrary", "arbitrary"),
            vmem_limit_bytes=56 * 1024 * 1024),
        name="inproj",
    )(x2, norm_g.reshape(1, d), w_main, w_small)


def _gates_kernel(small_ref, fb_ref, wa_ref, ba_ref, fk_ref, fq_ref, p_ref):
    s_len = small_ref.shape[0]
    small = small_ref[...]
    log_f = _log_sigmoid(small + fb_ref[...])
    a_logit = jnp.dot(small.astype(BF16), wa_ref[...], preferred_element_type=F32) + ba_ref[...]
    log_a = _log_sigmoid(a_logit) * (1.0 / GLA_TAU)
    both = jnp.concatenate([log_f, log_a], axis=-1)

    row = lax.broadcasted_iota(jnp.int32, (CUM_BLOCK, CUM_BLOCK), 0)
    col = lax.broadcasted_iota(jnp.int32, (CUM_BLOCK, CUM_BLOCK), 1)
    tri = (col <= row).astype(F32)
    carry = jnp.zeros((1, both.shape[1]), F32)
    for c in range(s_len // CUM_BLOCK):
        blk = both[c * CUM_BLOCK:(c + 1) * CUM_BLOCK]
        cum = jnp.dot(tri, blk, precision=lax.Precision.HIGHEST,
                      preferred_element_type=F32) + carry
        carry = cum[CUM_BLOCK - 1:CUM_BLOCK]
        rows = slice(c * CUM_BLOCK, (c + 1) * CUM_BLOCK)
        f_cum = cum[:, :LANES]
        fq_ref[:, rows] = f_cum.T[:GATE_ROWS]
        for h in range(N_HEADS):
            fk_ref[h, rows, :] = jnp.broadcast_to(f_cum[:, h:h + 1], (CUM_BLOCK, LANES))
            p_ref[h, rows, :] = cum[:, (h + 1) * LANES:(h + 2) * LANES]


def _gates(small, fb_pad, wa_pad, ba_dup, *, batch, s_len):
    t = small.shape[0]
    return pl.pallas_call(
        _gates_kernel,
        grid=(batch,),
        in_specs=[
            pl.BlockSpec((s_len, LANES), lambda b: (b, 0)),
            pl.BlockSpec((1, LANES), lambda b: (0, 0)),
            pl.BlockSpec((LANES, GROUP_W), lambda b: (0, 0)),
            pl.BlockSpec((1, GROUP_W), lambda b: (0, 0)),
        ],
        out_specs=[
            pl.BlockSpec((N_HEADS, s_len, LANES), lambda b: (0, b, 0)),
            pl.BlockSpec((None, GATE_ROWS, s_len), lambda b: (b, 0, 0)),
            pl.BlockSpec((N_HEADS, s_len, LANES), lambda b: (0, b, 0)),
        ],
        out_shape=[
            jax.ShapeDtypeStruct((N_HEADS, t, LANES), F32),
            jax.ShapeDtypeStruct((batch, GATE_ROWS, s_len), F32),
            jax.ShapeDtypeStruct((N_HEADS, t, LANES), F32),
        ],
        compiler_params=pltpu.CompilerParams(dimension_semantics=("arbitrary",)),
        name="gates",
    )(small, fb_pad, wa_pad, ba_dup)


def _causal_mask_t(tk, width, tq):
    key = lax.broadcasted_iota(jnp.int32, (tk, width), 0)
    qry = lax.broadcasted_iota(jnp.int32, (tk, width), 1)
    if width != tq:
        qry = qry & (tq - 1)
    return key <= qry


def _transpose_values(v_ref, vt_scr):
    for h in range(v_ref.shape[0]):
        for j in range(v_ref.shape[1] // K_TILE):
            blk = v_ref[h, j * K_TILE:(j + 1) * K_TILE, :].astype(F32)
            vt_scr[h, j] = blk.T.astype(BF16)


def _col_reduce(z, op):
    rows = z.shape[0]
    while rows > 8:
        rows //= 2
        z = op(z[:rows], z[rows:])
    if op is jnp.add:
        return jnp.sum(z, axis=0, keepdims=True)
    return jnp.max(z, axis=0, keepdims=True)


def _online_softmax_t(last, logits_fn, vt_fn, tq, z_scr, acc_scr):
    n, _, width = acc_scr.shape

    def update(j, slot, ms, ls, tile_max, masked):
        stats = []
        for s in range(n):
            z = z_scr[slot, s]
            if masked:
                z = jnp.where(_causal_mask_t(K_TILE, width, tq), z, -jnp.inf)
                m_new = jnp.maximum(ms[s], _col_reduce(z, jnp.maximum))
            else:
                m_new = jnp.maximum(ms[s], tile_max[s])
            alpha = jnp.exp(ms[s] - m_new)
            p = jnp.exp(z - m_new)
            stats.append((m_new, alpha * ls[s] + _col_reduce(p, jnp.add), alpha, p.astype(BF16)))
        for s in range(n):
            pv = jnp.dot(vt_fn(s, j), stats[s][3], preferred_element_type=F32)
            acc_scr[s] = stats[s][2] * acc_scr[s] + pv
        return tuple(st[0] for st in stats), tuple(st[1] for st in stats)

    first = [logits_fn(s, jnp.int32(0)) for s in range(n)]
    for s in range(n):
        z_scr[0, s] = first[s]
        acc_scr[s] = jnp.zeros(acc_scr.shape[1:], F32)

    def body(j, state):
        ms, ls, tile_max = state
        slot = lax.rem(j, 2)
        z_next = [logits_fn(s, j + 1) for s in range(n)]
        ms, ls = update(j, slot, ms, ls, tile_max, False)
        for s in range(n):
            z_scr[1 - slot, s] = z_next[s]
        return ms, ls, tuple(_col_reduce(z, jnp.maximum) for z in z_next)

    init = (tuple(jnp.full((1, width), -jnp.inf, F32) for _ in range(n)),
            tuple(jnp.zeros((1, width), F32) for _ in range(n)),
            tuple(_col_reduce(z, jnp.maximum) for z in first))
    ms, ls, _ = lax.fori_loop(0, last, body, init)
    _, ls = update(last, lax.rem(last, 2), ms, ls, None, True)
    return ls


def _rope(x, cos_t, sin_up, sin_dn, half):
    return (x * cos_t + pltpu.roll(x, LANES - half, 1) * sin_up
            + pltpu.roll(x, half, 1) * sin_dn)


def _key_rows(j):
    return pl.ds(pl.multiple_of(j * K_TILE, K_TILE), K_TILE)


def _mixer_specs(s_len, nq):
    qspec = lambda slab: pl.BlockSpec((N_HEADS, Q_TILE, LANES), lambda b, i: (slab, b * nq + i, 0))
    kspec = lambda slab: pl.BlockSpec((N_HEADS, s_len, LANES), lambda b, i: (slab, b, 0))
    ospec = pl.BlockSpec((N_HEADS, Q_TILE, LANES), lambda b, i: (0, b * nq + i, 0))
    return qspec, kspec, ospec


def _fox_kernel(q_ref, k_ref, v_ref, g_ref, fk_ref, fq_ref, o_ref, vt_scr, z_scr, acc_scr):
    i = pl.program_id(1)
    tq = q_ref.shape[1]
    scale = HEAD_DIM ** -0.5

    @pl.when(i == 0)
    def _():
        _transpose_values(v_ref, vt_scr)

    fq_all = fq_ref[...]
    qs = [q_ref[h] for h in range(N_HEADS)]
    fqs = [fq_all[h:h + 1, :] for h in range(N_HEADS)]

    def logits(h, j):
        fk = fk_ref[h, _key_rows(j), :]
        fk = jnp.concatenate([fk] * (tq // LANES), axis=-1)
        return _nt_dot(k_ref[h, _key_rows(j), :], qs[h]) * scale + (fqs[h] - fk)

    ls = _online_softmax_t(i, logits, lambda h, j: vt_scr[h, j], tq, z_scr, acc_scr)
    for h in range(N_HEADS):
        o = (acc_scr[h] / ls[h]).T
        o_ref[h] = (o * _silu(g_ref[h].astype(F32))).astype(BF16)


def _fox(proj, fk_rep, fq_rows, *, batch, s_len):
    t = proj.shape[1]
    nq = s_len // Q_TILE
    nk = s_len // K_TILE
    qspec, kspec, ospec = _mixer_specs(s_len, nq)
    return pl.pallas_call(
        _fox_kernel,
        grid=(batch, nq),
        in_specs=[
            qspec(_SLAB["fox_q"]), kspec(_SLAB["fox_k"]), kspec(_SLAB["fox_v"]),
            qspec(_SLAB["fox_g"]),
            pl.BlockSpec((N_HEADS, s_len, LANES), lambda b, i: (0, b, 0)),
            pl.BlockSpec((None, GATE_ROWS, Q_TILE), lambda b, i: (b, 0, i)),
        ],
        out_specs=ospec,
        out_shape=jax.ShapeDtypeStruct((N_HEADS, t, LANES), BF16),
        scratch_shapes=[pltpu.VMEM((N_HEADS, nk, HEAD_DIM, K_TILE), BF16),
                        pltpu.VMEM((2, N_HEADS, K_TILE, Q_TILE), F32),
                        pltpu.VMEM((N_HEADS, HEAD_DIM, Q_TILE), F32)],
        compiler_params=pltpu.CompilerParams(dimension_semantics=("arbitrary", "arbitrary")),
        name="fox",
    )(proj, proj, proj, proj, fk_rep, fq_rows)


def _diff_kernel(lam_init, q_ref, k_ref, v_ref, g_ref, lam_ref, ng_ref,
                 cq_ref, uq_ref, dq_ref, ck_ref, uk_ref, dk_ref, o_ref,
                 kr_scr, vt_scr, z_scr, acc_scr):
    i = pl.program_id(1)
    tq = q_ref.shape[1]
    half = DIFF_QK_DIM // ROPE_FRACTION // 2
    scale = DIFF_QK_DIM ** -0.5

    @pl.when(i == 0)
    def _():
        _transpose_values(v_ref, vt_scr)
        for h in range(N_HEADS):
            kr_scr[h] = _rope(k_ref[h].astype(F32), ck_ref[...], uk_ref[...], dk_ref[...],
                              half).astype(BF16)

    lf = lam_ref[...]
    lam = (jnp.exp(jnp.sum(lf[0:1] * lf[1:2], axis=-1, keepdims=True))
           - jnp.exp(jnp.sum(lf[2:3] * lf[3:4], axis=-1, keepdims=True)) + lam_init)

    first = lax.broadcasted_iota(jnp.int32, (tq, LANES), 1) < DIFF_QK_DIM
    q12 = []
    for h in range(N_HEADS):
        q = _rope(q_ref[h].astype(F32), cq_ref[...], uq_ref[...], dq_ref[...], half)
        q12.append(jnp.concatenate([jnp.where(first, q, 0.0),
                                    jnp.where(first, 0.0, q)], axis=0).astype(BF16))

    def logits(h, j):
        return _nt_dot(kr_scr[h, _key_rows(j), :], q12[h]) * scale

    ls = _online_softmax_t(i, logits, lambda h, j: vt_scr[h, j], tq, z_scr, acc_scr)
    for h in range(N_HEADS):
        a = acc_scr[h] / ls[h]
        o = (a[:, :tq] - lam * a[:, tq:]).T
        o = o * lax.rsqrt(jnp.mean(o * o, axis=-1, keepdims=True) + RMS_EPS) * ng_ref[...]
        o = o * (1.0 - lam_init)
        o_ref[h] = (o * _silu(g_ref[h].astype(F32))).astype(BF16)


def _diff(proj, diff_lam, diff_norm_g, tables, lam_init, *, batch, s_len):
    t = proj.shape[1]
    nq = s_len // Q_TILE
    nk = s_len // K_TILE
    qspec, kspec, ospec = _mixer_specs(s_len, nq)
    tq_spec = pl.BlockSpec((Q_TILE, LANES), lambda b, i: (i, 0))
    tk_spec = pl.BlockSpec((s_len, LANES), lambda b, i: (0, 0))
    return pl.pallas_call(
        functools.partial(_diff_kernel, lam_init),
        grid=(batch, nq),
        in_specs=[
            qspec(_SLAB["diff_q"]), kspec(_SLAB["diff_k"]), kspec(_SLAB["diff_v"]),
            qspec(_SLAB["diff_g"]),
            pl.BlockSpec((4, DIFF_QK_DIM), lambda b, i: (0, 0)),
            pl.BlockSpec((1, HEAD_DIM), lambda b, i: (0, 0)),
            tq_spec, tq_spec, tq_spec, tk_spec, tk_spec, tk_spec,
        ],
        out_specs=ospec,
        out_shape=jax.ShapeDtypeStruct((N_HEADS, t, LANES), BF16),
        scratch_shapes=[pltpu.VMEM((N_HEADS, s_len, LANES), BF16),
                        pltpu.VMEM((N_HEADS, nk, HEAD_DIM, K_TILE), BF16),
                        pltpu.VMEM((2, N_HEADS, K_TILE, 2 * Q_TILE), F32),
                        pltpu.VMEM((N_HEADS, HEAD_DIM, 2 * Q_TILE), F32)],
        compiler_params=pltpu.CompilerParams(dimension_semantics=("arbitrary", "arbitrary")),
        name="diff",
    )(proj, proj, proj, proj, diff_lam, diff_norm_g.reshape(1, HEAD_DIM), *tables, *tables)


def _moba_kernel(q_ref, k_ref, v_ref, g_ref, cq_ref, uq_ref, dq_ref, ck_ref, uk_ref, dk_ref,
                 o_ref, kr_scr, kmean_scr, vt_scr, z_scr, acc_scr):
    i = pl.program_id(1)
    tq = q_ref.shape[1]
    half = HEAD_DIM // ROPE_FRACTION // 2
    scale = HEAD_DIM ** -0.5
    n_blk = k_ref.shape[1] // MOBA_BLOCK

    @pl.when(i == 0)
    def _():
        _transpose_values(v_ref, vt_scr)
        kmean_scr[...] = jnp.zeros(kmean_scr.shape, F32)
        for h in range(N_HEADS):
            kr = _rope(k_ref[h].astype(F32), ck_ref[...], uk_ref[...], dk_ref[...], half)
            kr_scr[h] = kr.astype(BF16)
            for n in range(n_blk):
                blk = kr[n * MOBA_BLOCK:(n + 1) * MOBA_BLOCK]
                kmean_scr[h, n:n + 1, :] = jnp.sum(blk, axis=0, keepdims=True) * (1.0 / MOBA_BLOCK)

    blk_id = lax.broadcasted_iota(jnp.int32, (GATE_ROWS, tq), 0).astype(F32)
    past = blk_id < i.astype(F32)
    qs, biases = [], []
    for h in range(N_HEADS):
        q = _rope(q_ref[h].astype(F32), cq_ref[...], uq_ref[...], dq_ref[...], half).astype(BF16)
        gate = _nt_dot(kmean_scr[h].astype(BF16), q)
        gate = jnp.where(past, gate, -jnp.inf)
        open_ = past
        sel = jnp.zeros((GATE_ROWS, tq), jnp.bool_)
        for _ in range(MOBA_TOPK):
            top = jnp.max(gate, axis=0, keepdims=True)
            idx = jnp.min(jnp.where(gate == top, blk_id, float(GATE_ROWS)), axis=0, keepdims=True)
            pick = (blk_id == idx) & open_
            sel = sel | pick
            open_ = open_ & jnp.logical_not(pick)
            gate = jnp.where(pick, -jnp.inf, gate)
        qs.append(q)
        biases.append(jnp.where(sel, 0.0, MASKED_LOGIT))

    def logits(h, j):
        bias = jnp.sum(jnp.where(blk_id == j.astype(F32), biases[h], 0.0), axis=0, keepdims=True)
        bias = jnp.where(j == i, 0.0, bias)
        return _nt_dot(kr_scr[h, _key_rows(j), :], qs[h]) * scale + bias

    ls = _online_softmax_t(i, logits, lambda h, j: vt_scr[h, j], tq, z_scr, acc_scr)
    for h in range(N_HEADS):
        o = (acc_scr[h] / ls[h]).T
        o_ref[h] = (o * _silu(g_ref[h].astype(F32))).astype(BF16)


def _moba(proj, tables, *, batch, s_len):
    t = proj.shape[1]
    nq = s_len // Q_TILE
    nk = s_len // K_TILE
    qspec, kspec, ospec = _mixer_specs(s_len, nq)
    tq_spec = pl.BlockSpec((Q_TILE, LANES), lambda b, i: (i, 0))
    tk_spec = pl.BlockSpec((s_len, LANES), lambda b, i: (0, 0))
    return pl.pallas_call(
        _moba_kernel,
        grid=(batch, nq),
        in_specs=[
            qspec(_SLAB["moba_q"]), kspec(_SLAB["moba_k"]), kspec(_SLAB["moba_v"]),
            qspec(_SLAB["moba_g"]),
            tq_spec, tq_spec, tq_spec, tk_spec, tk_spec, tk_spec,
        ],
        out_specs=ospec,
        out_shape=jax.ShapeDtypeStruct((N_HEADS, t, LANES), BF16),
        scratch_shapes=[pltpu.VMEM((N_HEADS, s_len, LANES), BF16),
                        pltpu.VMEM((N_HEADS, GATE_ROWS, LANES), F32),
                        pltpu.VMEM((N_HEADS, nk, HEAD_DIM, K_TILE), BF16),
                        pltpu.VMEM((2, N_HEADS, K_TILE, Q_TILE), F32),
                        pltpu.VMEM((N_HEADS, HEAD_DIM, Q_TILE), F32)],
        compiler_params=pltpu.CompilerParams(dimension_semantics=("arbitrary", "arbitrary")),
        name="moba",
    )(proj, proj, proj, proj, *tables, *tables)


def _gla_kernel(x_ref, v_ref, g_ref, p_ref, ng_ref, o_ref):
    s_len = x_ref.shape[1]
    c_len = GLA_CHUNK
    scale = GLA_DK ** -0.5
    lane = lax.broadcasted_iota(jnp.int32, (c_len, LANES), 1)
    qhalf = lane < GLA_DK
    qhalf_row = lax.broadcasted_iota(jnp.int32, (1, LANES), 1) < GLA_DK
    row = lax.broadcasted_iota(jnp.int32, (c_len, c_len), 0)
    col = lax.broadcasted_iota(jnp.int32, (c_len, c_len), 1)
    causal = col <= row

    group = GLA_GROUP
    pairs = [(h, c) for h in range(N_HEADS) for c in range(group)]

    def body(g, carries):
        prep = {}
        for h, c in pairs:
            rows = pl.ds(pl.multiple_of(g * (group * c_len), group * c_len) + c * c_len, c_len)
            qcb = x_ref[h // 2, rows, :].astype(F32)
            kcb = x_ref[2 + h // 2, rows, :].astype(F32)
            if h % 2 == 0:
                x = jnp.where(qhalf, qcb, pltpu.roll(kcb, GLA_DK, 1))
            else:
                x = jnp.where(qhalf, pltpu.roll(qcb, GLA_DK, 1), kcb)
            p = p_ref[h, rows, :]
            base = carries[h][1] if c == 0 else prep[h, c - 1]["p_last"]
            cum = p - base
            xt = x * jnp.exp(jnp.where(qhalf, cum, -cum))
            kr = pltpu.roll(xt, GLA_DK, 1)
            decay = jnp.where(qhalf_row, jnp.exp(cum[c_len - 1:c_len]), 0.0)
            prep[h, c] = dict(
                rows=rows, qz=jnp.where(qhalf, xt, 0.0).astype(BF16), kr=kr.astype(BF16),
                khat=(jnp.where(qhalf, kr, 0.0) * decay).astype(BF16),
                decay=decay, v=v_ref[h, rows, :], p_last=p[c_len - 1:c_len])
        scores, upd = {}, {}
        for h, c in pairs:
            d = prep[h, c]
            scores[h, c] = _nt_dot(d["qz"], d["kr"])
            upd[h, c] = lax.dot_general(d["v"], d["khat"], (((0,), (0,)), ((), ())),
                                        preferred_element_type=F32)
        states = {}
        for h in range(N_HEADS):
            states[h, 0] = carries[h][0]
            for c in range(group):
                states[h, c + 1] = states[h, c] * prep[h, c]["decay"] + upd[h, c]
        outs = {}
        for h, c in pairs:
            sc = jnp.where(causal, scores[h, c] * scale, 0.0).astype(BF16)
            outs[h, c] = (jnp.dot(sc, prep[h, c]["v"], preferred_element_type=F32),
                          _nt_dot(prep[h, c]["qz"], states[h, c].astype(BF16)))
        for h, c in pairs:
            o = outs[h, c][0] + outs[h, c][1] * scale
            y = o * lax.rsqrt(jnp.mean(o * o, axis=-1, keepdims=True) + RMS_EPS) * ng_ref[...]
            rows = prep[h, c]["rows"]
            o_ref[h, rows, :] = (y * _silu(g_ref[h, rows, :].astype(F32))).astype(BF16)
        return tuple((states[h, group], prep[h, group - 1]["p_last"]) for h in range(N_HEADS))

    init = tuple((jnp.zeros((HEAD_DIM, LANES), F32), jnp.zeros((1, LANES), F32))
                 for _ in range(N_HEADS))
    lax.fori_loop(0, s_len // (group * c_len), body, init)


def _gla(proj, p_cum, gla_norm_g, *, batch, s_len):
    t = proj.shape[1]
    spec = lambda slab: pl.BlockSpec((N_HEADS, s_len, LANES), lambda b: (slab, b, 0))
    return pl.pallas_call(
        _gla_kernel,
        grid=(batch,),
        in_specs=[
            spec(_SLAB["gla_qk"]), spec(_SLAB["gla_v"]), spec(_SLAB["gla_g"]),
            pl.BlockSpec((N_HEADS, s_len, LANES), lambda b: (0, b, 0)),
            pl.BlockSpec((1, HEAD_DIM), lambda b: (0, 0)),
        ],
        out_specs=pl.BlockSpec((N_HEADS, s_len, LANES), lambda b: (0, b, 0)),
        out_shape=jax.ShapeDtypeStruct((N_HEADS, t, LANES), BF16),
        compiler_params=pltpu.CompilerParams(dimension_semantics=("arbitrary",)),
        name="gla",
    )(proj, proj, proj, p_cum, gla_norm_g.reshape(1, HEAD_DIM))


def _outproj_kernel(final, x_ref, a_ref, b_ref, c_ref, d_ref, w_ref, fg_ref, o_ref):
    parts = [r[h] for r in (a_ref, b_ref, c_ref, d_ref) for h in range(N_HEADS)]
    mixed = jnp.concatenate(parts, axis=-1)
    y = x_ref[...] + jnp.dot(mixed, w_ref[...], preferred_element_type=F32)
    if final:
        y = y * lax.rsqrt(jnp.mean(y * y, axis=-1, keepdims=True) + RMS_EPS) * fg_ref[...]
    o_ref[...] = y


def _outproj(x2, mixers, w_out, layer, final_g, *, final, tm):
    t, d = x2.shape
    mspec = pl.BlockSpec((N_HEADS, tm, LANES), lambda i: (0, i, 0))
    return pl.pallas_call(
        functools.partial(_outproj_kernel, final),
        grid=(t // tm,),
        in_specs=[
            pl.BlockSpec((tm, d), lambda i: (i, 0)),
            mspec, mspec, mspec, mspec,
            pl.BlockSpec((None,) + w_out.shape[1:], lambda i: (layer, 0, 0)),
            pl.BlockSpec((1, d), lambda i: (0, 0)),
        ],
        out_specs=pl.BlockSpec((tm, d), lambda i: (i, 0)),
        out_shape=jax.ShapeDtypeStruct((t, d), F32),
        compiler_params=pltpu.CompilerParams(
            dimension_semantics=("arbitrary",), vmem_limit_bytes=56 * 1024 * 1024),
        name="outproj",
    )(x2, *mixers, w_out, final_g.reshape(1, d))


def _seg(w, name):
    off, width = _SEG_OFF[name]
    return w[..., off:off + width]


def _relayout_kernel(w_ref, o_ref):
    w = w_ref[...]
    pieces = [w[:, a:b] for a, b in _MAIN_RUNS]
    o_ref[...] = jnp.concatenate(pieces, axis=-1).astype(BF16)


def _relayout_w_in(w_in, *, rows):
    depth, d, d_in = w_in.shape
    main = pl.pallas_call(
        _relayout_kernel,
        grid=(depth, d // rows),
        in_specs=[pl.BlockSpec((None, rows, d_in), lambda l, r: (l, r, 0))],
        out_specs=pl.BlockSpec((None, rows, N_MAIN), lambda l, r: (l, r, 0)),
        out_shape=jax.ShapeDtypeStruct((depth, d, N_MAIN), BF16),
        compiler_params=pltpu.CompilerParams(dimension_semantics=("arbitrary", "arbitrary")),
        name="relayout",
    )(w_in)
    f, a = _seg(w_in, "fox_f"), _seg(w_in, "gla_a")
    pad = jnp.zeros(w_in.shape[:-1] + (LANES - N_HEADS - GLA_RANK,), w_in.dtype)
    small = jnp.concatenate([f, a, pad], axis=-1).astype(BF16)
    return main, small


def _dup_halves(a):
    parts = []
    for h in range(N_HEADS):
        blk = a[..., h * GLA_DK:(h + 1) * GLA_DK]
        parts += [blk, blk]
    return jnp.concatenate(parts, axis=-1)


def _rope_tables(s_len, comp_dim, rot_dim):
    half = rot_dim // 2
    inv_freq = ROPE_THETA ** (-jnp.arange(0, rot_dim, 2, dtype=F32) / rot_dim)
    ang = jnp.arange(s_len, dtype=F32)[:, None] * inv_freq[None, :]
    cos, sin = jnp.cos(ang), jnp.sin(ang)
    zeros = jnp.zeros((s_len, comp_dim - rot_dim), F32)
    ones = jnp.ones((s_len, comp_dim - rot_dim), F32)
    reps = LANES // comp_dim
    cos_t = jnp.tile(jnp.concatenate([cos, cos, ones], axis=-1), (1, reps))
    sin_up = jnp.tile(jnp.concatenate([-sin, jnp.zeros_like(sin), zeros], axis=-1), (1, reps))
    sin_dn = jnp.tile(jnp.concatenate([jnp.zeros_like(sin), sin, zeros], axis=-1), (1, reps))
    return cos_t, sin_up, sin_dn


def kernel(x, norm_g, w_in, fox_fb, diff_lam, diff_norm_g, gla_wa2, gla_ba, gla_norm_g, w_out,
           final_norm_g):
    batch, s_len, d_model = x.shape
    depth = w_in.shape[0]
    assert s_len % Q_TILE == 0 and Q_TILE == K_TILE == MOBA_BLOCK
    assert s_len // MOBA_BLOCK <= GATE_ROWS
    t = batch * s_len
    tm_in = min(512, t)
    tn_in = 1536
    tm_out = min(512, t)

    w_main, w_small = _relayout_w_in(w_in, rows=min(256, d_model))
    w_out_b = w_out.astype(BF16)
    fb_pad = jnp.pad(fox_fb, ((0, 0), (0, LANES - N_HEADS)))[:, None, :]
    wa_pad = jnp.pad(_dup_halves(gla_wa2), ((0, 0), (N_HEADS, LANES - N_HEADS - GLA_RANK), (0, 0))
                     ).astype(BF16)
    ba_dup = _dup_halves(gla_ba)[:, None, :]
    rope_diff = _rope_tables(s_len, DIFF_QK_DIM, DIFF_QK_DIM // ROPE_FRACTION)
    rope_moba = _rope_tables(s_len, HEAD_DIM, HEAD_DIM // ROPE_FRACTION)

    x2 = x.reshape(t, d_model)
    for l in range(depth):
        proj, small = _inproj(x2, norm_g[l], w_main, w_small, l, tm=tm_in, tn=tn_in)
        fk_rep, fq_rows, p_cum = _gates(small, fb_pad[l], wa_pad[l], ba_dup[l],
                                        batch=batch, s_len=s_len)
        lam_init = 0.8 - 0.6 * math.exp(-0.3 * l)
        mixers = (
            _fox(proj, fk_rep, fq_rows, batch=batch, s_len=s_len),
            _diff(proj, diff_lam[l], diff_norm_g[l], rope_diff, lam_init, batch=batch, s_len=s_len),
            _moba(proj, rope_moba, batch=batch, s_len=s_len),
            _gla(proj, p_cum, gla_norm_g[l], batch=batch, s_len=s_len),
        )
        x2 = _outproj(x2, mixers, w_out_b, l, final_norm_g, final=(l == depth - 1), tm=tm_out)
    return x2.reshape(batch, s_len, d_model)
```

```python
import functools
import math

import jax
import jax.numpy as jnp
from jax import lax
from jax.experimental import pallas as pl
from jax.experimental.pallas import tpu as pltpu

F32 = jnp.float32
BF16 = jnp.bfloat16

LANES = 128
HEAD_DIM = 128
N_HEADS = 4
GROUP_W = N_HEADS * HEAD_DIM
DIFF_QK_DIM = HEAD_DIM // 2
MOBA_BLOCK = 256
MOBA_TOPK = 3
GLA_DK = HEAD_DIM // 2
GLA_RANK = 16
GLA_TAU = 16.0
GLA_CHUNK = 64
ROPE_THETA = 500000.0
ROPE_FRACTION = 4
RMS_EPS = 1e-6
MASKED_LOGIT = -1e30

_SEGMENTS = (
    ("fox_q", GROUP_W), ("fox_k", GROUP_W), ("fox_v", GROUP_W),
    ("fox_f", N_HEADS), ("fox_g", GROUP_W),
    ("diff_q", GROUP_W), ("diff_k", GROUP_W), ("diff_v", GROUP_W), ("diff_g", GROUP_W),
    ("moba_q", GROUP_W), ("moba_k", GROUP_W), ("moba_v", GROUP_W), ("moba_g", GROUP_W),
    ("gla_q", N_HEADS * GLA_DK), ("gla_k", N_HEADS * GLA_DK), ("gla_v", GROUP_W),
    ("gla_a", GLA_RANK), ("gla_g", GROUP_W),
)
_SEG_OFF = {}
_off = 0
for _name, _w in _SEGMENTS:
    _SEG_OFF[_name] = (_off, _w)
    _off += _w

_SLABS = ("fox_q", "fox_k", "fox_v", "fox_g", "diff_q", "diff_k", "diff_v", "diff_g",
          "moba_q", "moba_k", "moba_v", "moba_g", "gla_qk", "gla_v", "gla_g")
_SLAB = {name: i for i, name in enumerate(_SLABS)}
N_MAIN = len(_SLABS) * GROUP_W
_MAIN_RUNS = ((0, _SEG_OFF["fox_f"][0]),
              (_SEG_OFF["fox_g"][0], _SEG_OFF["gla_a"][0]),
              (_SEG_OFF["gla_g"][0], _off))
assert sum(b - a for a, b in _MAIN_RUNS) == N_MAIN

Q_TILE = 256
K_TILE = 256
CUM_BLOCK = 128
GATE_ROWS = 16
GLA_GROUP = 4


def _silu(x):
    return x * (1.0 / (1.0 + jnp.exp(-x)))


def _log_sigmoid(x):
    return jnp.minimum(x, 0.0) - jnp.log1p(jnp.exp(-jnp.abs(x)))


def _nt_dot(a, b):
    return lax.dot_general(a, b, (((1,), (1,)), ((), ())), preferred_element_type=F32)


def _inproj_kernel(x_ref, g_ref, w_ref, ws_ref, o_ref, small_ref, h_scr):
    @pl.when(pl.program_id(1) == 0)
    def _():
        x = x_ref[...]
        y = x * lax.rsqrt(jnp.mean(x * x, axis=-1, keepdims=True) + RMS_EPS) * g_ref[...]
        h = y.astype(BF16)
        h_scr[...] = h
        small_ref[...] = jnp.dot(h, ws_ref[...], preferred_element_type=F32)

    acc = jnp.dot(h_scr[...], w_ref[...], preferred_element_type=F32)
    for c in range(o_ref.shape[0]):
        o_ref[c] = acc[:, c * LANES:(c + 1) * LANES].astype(BF16)


def _inproj(x2, norm_g, w_main, w_small, layer, *, tm, tn):
    t, d = x2.shape
    n_main = w_main.shape[2]
    return pl.pallas_call(
        _inproj_kernel,
        grid=(t // tm, n_main // tn),
        in_specs=[
            pl.BlockSpec((tm, d), lambda i, j: (i, 0)),
            pl.BlockSpec((1, d), lambda i, j: (0, 0)),
            pl.BlockSpec((None, d, tn), lambda i, j: (layer, 0, j)),
            pl.BlockSpec((None, d, LANES), lambda i, j: (layer, 0, 0)),
        ],
        out_specs=[
            pl.BlockSpec((tn // LANES, tm, LANES), lambda i, j: (j, i, 0)),
            pl.BlockSpec((tm, LANES), lambda i, j: (i, 0)),
        ],
        out_shape=[
            jax.ShapeDtypeStruct((n_main // LANES, t, LANES), BF16),
            jax.ShapeDtypeStruct((t, LANES), F32),
        ],
        scratch_shapes=[pltpu.VMEM((tm, d), BF16)],
        compiler_params=pltpu.CompilerParams(
            dimension_semantics=("arbitrary", "arbitrary"),
            vmem_limit_bytes=56 * 1024 * 1024),
        name="inproj",
    )(x2, norm_g.reshape(1, d), w_main, w_small)


def _gates_kernel(small_ref, fb_ref, wa_ref, ba_ref, fk_ref, fq_ref, p_ref):
    s_len = small_ref.shape[0]
    small = small_ref[...]
    log_f = _log_sigmoid(small + fb_ref[...])
    a_logit = jnp.dot(small.astype(BF16), wa_ref[...], preferred_element_type=F32) + ba_ref[...]
    log_a = _log_sigmoid(a_logit) * (1.0 / GLA_TAU)
    both = jnp.concatenate([log_f, log_a], axis=-1)

    row = lax.broadcasted_iota(jnp.int32, (CUM_BLOCK, CUM_BLOCK), 0)
    col = lax.broadcasted_iota(jnp.int32, (CUM_BLOCK, CUM_BLOCK), 1)
    tri = (col <= row).astype(F32)
    carry = jnp.zeros((1, both.shape[1]), F32)
    for c in range(s_len // CUM_BLOCK):
        blk = both[c * CUM_BLOCK:(c + 1) * CUM_BLOCK]
        cum = jnp.dot(tri, blk, precision=lax.Precision.HIGHEST,
                      preferred_element_type=F32) + carry
        carry = cum[CUM_BLOCK - 1:CUM_BLOCK]
        rows = slice(c * CUM_BLOCK, (c + 1) * CUM_BLOCK)
        f_cum = cum[:, :LANES]
        fq_ref[:, rows] = f_cum.T[:GATE_ROWS]
        for h in range(N_HEADS):
            fk_ref[h, rows, :] = jnp.broadcast_to(f_cum[:, h:h + 1], (CUM_BLOCK, LANES))
            p_ref[h, rows, :] = cum[:, (h + 1) * LANES:(h + 2) * LANES]


def _gates(small, fb_pad, wa_pad, ba_dup, *, batch, s_len):
    t = small.shape[0]
    return pl.pallas_call(
        _gates_kernel,
        grid=(batch,),
        in_specs=[
            pl.BlockSpec((s_len, LANES), lambda b: (b, 0)),
            pl.BlockSpec((1, LANES), lambda b: (0, 0)),
            pl.BlockSpec((LANES, GROUP_W), lambda b: (0, 0)),
            pl.BlockSpec((1, GROUP_W), lambda b: (0, 0)),
        ],
        out_specs=[
            pl.BlockSpec((N_HEADS, s_len, LANES), lambda b: (0, b, 0)),
            pl.BlockSpec((None, GATE_ROWS, s_len), lambda b: (b, 0, 0)),
            pl.BlockSpec((N_HEADS, s_len, LANES), lambda b: (0, b, 0)),
        ],
        out_shape=[
            jax.ShapeDtypeStruct((N_HEADS, t, LANES), F32),
            jax.ShapeDtypeStruct((batch, GATE_ROWS, s_len), F32),
            jax.ShapeDtypeStruct((N_HEADS, t, LANES), F32),
        ],
        compiler_params=pltpu.CompilerParams(dimension_semantics=("arbitrary",)),
        name="gates",
    )(small, fb_pad, wa_pad, ba_dup)


def _causal_mask_t(tk, width, tq):
    key = lax.broadcasted_iota(jnp.int32, (tk, width), 0)
    qry = lax.broadcasted_iota(jnp.int32, (tk, width), 1)
    if width != tq:
        qry = qry & (tq - 1)
    return key <= qry


def _transpose_values(v_ref, vt_scr):
    for h in range(v_ref.shape[0]):
        for j in range(v_ref.shape[1] // K_TILE):
            blk = v_ref[h, j * K_TILE:(j + 1) * K_TILE, :].astype(F32)
            vt_scr[h, j] = blk.T.astype(BF16)


def _col_reduce(z, op):
    rows = z.shape[0]
    while rows > 8:
        rows //= 2
        z = op(z[:rows], z[rows:])
    if op is jnp.add:
        return jnp.sum(z, axis=0, keepdims=True)
    return jnp.max(z, axis=0, keepdims=True)


def _online_softmax_t(last, logits_fn, vt_fn, tq, z_scr, acc_scr):
    n, _, width = acc_scr.shape

    def update(j, slot, ms, ls, tile_max, masked):
        stats = []
        for s in range(n):
            z = z_scr[slot, s]
            if masked:
                z = jnp.where(_causal_mask_t(K_TILE, width, tq), z, -jnp.inf)
                m_new = jnp.maximum(ms[s], _col_reduce(z, jnp.maximum))
            else:
                m_new = jnp.maximum(ms[s], tile_max[s])
            alpha = jnp.exp(ms[s] - m_new)
            p = jnp.exp(z - m_new)
            stats.append((m_new, alpha * ls[s] + _col_reduce(p, jnp.add), alpha, p.astype(BF16)))
        for s in range(n):
            pv = jnp.dot(vt_fn(s, j), stats[s][3], preferred_element_type=F32)
            acc_scr[s] = stats[s][2] * acc_scr[s] + pv
        return tuple(st[0] for st in stats), tuple(st[1] for st in stats)

    first = [logits_fn(s, jnp.int32(0)) for s in range(n)]
    for s in range(n):
        z_scr[0, s] = first[s]
        acc_scr[s] = jnp.zeros(acc_scr.shape[1:], F32)

    def body(j, state):
        ms, ls, tile_max = state
        slot = lax.rem(j, 2)
        z_next = [logits_fn(s, j + 1) for s in range(n)]
        ms, ls = update(j, slot, ms, ls, tile_max, False)
        for s in range(n):
            z_scr[1 - slot, s] = z_next[s]
        return ms, ls, tuple(_col_reduce(z, jnp.maximum) for z in z_next)

    init = (tuple(jnp.full((1, width), -jnp.inf, F32) for _ in range(n)),
            tuple(jnp.zeros((1, width), F32) for _ in range(n)),
            tuple(_col_reduce(z, jnp.maximum) for z in first))
    ms, ls, _ = lax.fori_loop(0, last, body, init)
    _, ls = update(last, lax.rem(last, 2), ms, ls, None, True)
    return ls


def _rope(x, cos_t, sin_up, sin_dn, half):
    return (x * cos_t + pltpu.roll(x, LANES - half, 1) * sin_up
            + pltpu.roll(x, half, 1) * sin_dn)


def _key_rows(j):
    return pl.ds(pl.multiple_of(j * K_TILE, K_TILE), K_TILE)


def _mixer_specs(s_len, nq):
    qspec = lambda slab: pl.BlockSpec((N_HEADS, Q_TILE, LANES), lambda b, i: (slab, b * nq + i, 0))
    kspec = lambda slab: pl.BlockSpec((N_HEADS, s_len, LANES), lambda b, i: (slab, b, 0))
    ospec = pl.BlockSpec((N_HEADS, Q_TILE, LANES), lambda b, i: (0, b * nq + i, 0))
    return qspec, kspec, ospec


def _fox_kernel(q_ref, k_ref, v_ref, g_ref, fk_ref, fq_ref, o_ref, vt_scr, z_scr, acc_scr):
    i = pl.program_id(1)
    tq = q_ref.shape[1]
    scale = HEAD_DIM ** -0.5

    @pl.when(i == 0)
    def _():
        _transpose_values(v_ref, vt_scr)

    fq_all = fq_ref[...]
    qs = [q_ref[h] for h in range(N_HEADS)]
    fqs = [fq_all[h:h + 1, :] for h in range(N_HEADS)]

    def logits(h, j):
        fk = fk_ref[h, _key_rows(j), :]
        fk = jnp.concatenate([fk] * (tq // LANES), axis=-1)
        return _nt_dot(k_ref[h, _key_rows(j), :], qs[h]) * scale + (fqs[h] - fk)

    ls = _online_softmax_t(i, logits, lambda h, j: vt_scr[h, j], tq, z_scr, acc_scr)
    for h in range(N_HEADS):
        o = (acc_scr[h] / ls[h]).T
        o_ref[h] = (o * _silu(g_ref[h].astype(F32))).astype(BF16)


def _fox(proj, fk_rep, fq_rows, *, batch, s_len):
    t = proj.shape[1]
    nq = s_len // Q_TILE
    nk = s_len // K_TILE
    qspec, kspec, ospec = _mixer_specs(s_len, nq)
    return pl.pallas_call(
        _fox_kernel,
        grid=(batch, nq),
        in_specs=[
            qspec(_SLAB["fox_q"]), kspec(_SLAB["fox_k"]), kspec(_SLAB["fox_v"]),
            qspec(_SLAB["fox_g"]),
            pl.BlockSpec((N_HEADS, s_len, LANES), lambda b, i: (0, b, 0)),
            pl.BlockSpec((None, GATE_ROWS, Q_TILE), lambda b, i: (b, 0, i)),
        ],
        out_specs=ospec,
        out_shape=jax.ShapeDtypeStruct((N_HEADS, t, LANES), BF16),
        scratch_shapes=[pltpu.VMEM((N_HEADS, nk, HEAD_DIM, K_TILE), BF16),
                        pltpu.VMEM((2, N_HEADS, K_TILE, Q_TILE), F32),
                        pltpu.VMEM((N_HEADS, HEAD_DIM, Q_TILE), F32)],
        compiler_params=pltpu.CompilerParams(dimension_semantics=("arbitrary", "arbitrary")),
        name="fox",
    )(proj, proj, proj, proj, fk_rep, fq_rows)


def _diff_kernel(lam_init, q_ref, k_ref, v_ref, g_ref, lam_ref, ng_ref,
                 cq_ref, uq_ref, dq_ref, ck_ref, uk_ref, dk_ref, o_ref,
                 kr_scr, vt_scr, z_scr, acc_scr):
    i = pl.program_id(1)
    tq = q_ref.shape[1]
    half = DIFF_QK_DIM // ROPE_FRACTION // 2
    scale = DIFF_QK_DIM ** -0.5

    @pl.when(i == 0)
    def _():
        _transpose_values(v_ref, vt_scr)
        for h in range(N_HEADS):
            kr_scr[h] = _rope(k_ref[h].astype(F32), ck_ref[...], uk_ref[...], dk_ref[...],
                              half).astype(BF16)

    lf = lam_ref[...]
    lam = (jnp.exp(jnp.sum(lf[0:1] * lf[1:2], axis=-1, keepdims=True))
           - jnp.exp(jnp.sum(lf[2:3] * lf[3:4], axis=-1, keepdims=True)) + lam_init)

    first = lax.broadcasted_iota(jnp.int32, (tq, LANES), 1) < DIFF_QK_DIM
    q12 = []
    for h in range(N_HEADS):
        q = _rope(q_ref[h].astype(F32), cq_ref[...], uq_ref[...], dq_ref[...], half)
        q12.append(jnp.concatenate([jnp.where(first, q, 0.0),
                                    jnp.where(first, 0.0, q)], axis=0).astype(BF16))

    def logits(h, j):
        return _nt_dot(kr_scr[h, _key_rows(j), :], q12[h]) * scale

    ls = _online_softmax_t(i, logits, lambda h, j: vt_scr[h, j], tq, z_scr, acc_scr)
    for h in range(N_HEADS):
        a = acc_scr[h] / ls[h]
        o = (a[:, :tq] - lam * a[:, tq:]).T
        o = o * lax.rsqrt(jnp.mean(o * o, axis=-1, keepdims=True) + RMS_EPS) * ng_ref[...]
        o = o * (1.0 - lam_init)
        o_ref[h] = (o * _silu(g_ref[h].astype(F32))).astype(BF16)


def _diff(proj, diff_lam, diff_norm_g, tables, lam_init, *, batch, s_len):
    t = proj.shape[1]
    nq = s_len // Q_TILE
    nk = s_len // K_TILE
    qspec, kspec, ospec = _mixer_specs(s_len, nq)
    tq_spec = pl.BlockSpec((Q_TILE, LANES), lambda b, i: (i, 0))
    tk_spec = pl.BlockSpec((s_len, LANES), lambda b, i: (0, 0))
    return pl.pallas_call(
        functools.partial(_diff_kernel, lam_init),
        grid=(batch, nq),
        in_specs=[
            qspec(_SLAB["diff_q"]), kspec(_SLAB["diff_k"]), kspec(_SLAB["diff_v"]),
            qspec(_SLAB["diff_g"]),
            pl.BlockSpec((4, DIFF_QK_DIM), lambda b, i: (0, 0)),
            pl.BlockSpec((1, HEAD_DIM), lambda b, i: (0, 0)),
            tq_spec, tq_spec, tq_spec, tk_spec, tk_spec, tk_spec,
        ],
        out_specs=ospec,
        out_shape=jax.ShapeDtypeStruct((N_HEADS, t, LANES), BF16),
        scratch_shapes=[pltpu.VMEM((N_HEADS, s_len, LANES), BF16),
                        pltpu.VMEM((N_HEADS, nk, HEAD_DIM, K_TILE), BF16),
                        pltpu.VMEM((2, N_HEADS, K_TILE, 2 * Q_TILE), F32),
                        pltpu.VMEM((N_HEADS, HEAD_DIM, 2 * Q_TILE), F32)],
        compiler_params=pltpu.CompilerParams(dimension_semantics=("arbitrary", "arbitrary")),
        name="diff",
    )(proj, proj, proj, proj, diff_lam, diff_norm_g.reshape(1, HEAD_DIM), *tables, *tables)


def _moba_kernel(q_ref, k_ref, v_ref, g_ref, cq_ref, uq_ref, dq_ref, ck_ref, uk_ref, dk_ref,
                 o_ref, kr_scr, kmean_scr, vt_scr, z_scr, acc_scr):
    i = pl.program_id(1)
    tq = q_ref.shape[1]
    half = HEAD_DIM // ROPE_FRACTION // 2
    scale = HEAD_DIM ** -0.5
    n_blk = k_ref.shape[1] // MOBA_BLOCK

    @pl.when(i == 0)
    def _():
        _transpose_values(v_ref, vt_scr)
        kmean_scr[...] = jnp.zeros(kmean_scr.shape, F32)
        for h in range(N_HEADS):
            kr = _rope(k_ref[h].astype(F32), ck_ref[...], uk_ref[...], dk_ref[...], half)
            kr_scr[h] = kr.astype(BF16)
            for n in range(n_blk):
                blk = kr[n * MOBA_BLOCK:(n + 1) * MOBA_BLOCK]
                kmean_scr[h, n:n + 1, :] = jnp.sum(blk, axis=0, keepdims=True) * (1.0 / MOBA_BLOCK)

    blk_id = lax.broadcasted_iota(jnp.int32, (GATE_ROWS, tq), 0).astype(F32)
    past = blk_id < i.astype(F32)
    qs, biases = [], []
    for h in range(N_HEADS):
        q = _rope(q_ref[h].astype(F32), cq_ref[...], uq_ref[...], dq_ref[...], half).astype(BF16)
        gate = _nt_dot(kmean_scr[h].astype(BF16), q)
        gate = jnp.where(past, gate, -jnp.inf)
        open_ = past
        sel = jnp.zeros((GATE_ROWS, tq), jnp.bool_)
        for _ in range(MOBA_TOPK):
            top = jnp.max(gate, axis=0, keepdims=True)
            idx = jnp.min(jnp.where(gate == top, blk_id, float(GATE_ROWS)), axis=0, keepdims=True)
            pick = (blk_id == idx) & open_
            sel = sel | pick
            open_ = open_ & jnp.logical_not(pick)
            gate = jnp.where(pick, -jnp.inf, gate)
        qs.append(q)
        biases.append(jnp.where(sel, 0.0, MASKED_LOGIT))

    def logits(h, j):
        bias = jnp.sum(jnp.where(blk_id == j.astype(F32), biases[h], 0.0), axis=0, keepdims=True)
        bias = jnp.where(j == i, 0.0, bias)
        return _nt_dot(kr_scr[h, _key_rows(j), :], qs[h]) * scale + bias

    ls = _online_softmax_t(i, logits, lambda h, j: vt_scr[h, j], tq, z_scr, acc_scr)
    for h in range(N_HEADS):
        o = (acc_scr[h] / ls[h]).T
        o_ref[h] = (o * _silu(g_ref[h].astype(F32))).astype(BF16)


def _moba(proj, tables, *, batch, s_len):
    t = proj.shape[1]
    nq = s_len // Q_TILE
    nk = s_len // K_TILE
    qspec, kspec, ospec = _mixer_specs(s_len, nq)
    tq_spec = pl.BlockSpec((Q_TILE, LANES), lambda b, i: (i, 0))
    tk_spec = pl.BlockSpec((s_len, LANES), lambda b, i: (0, 0))
    return pl.pallas_call(
        _moba_kernel,
        grid=(batch, nq),
        in_specs=[
            qspec(_SLAB["moba_q"]), kspec(_SLAB["moba_k"]), kspec(_SLAB["moba_v"]),
            qspec(_SLAB["moba_g"]),
            tq_spec, tq_spec, tq_spec, tk_spec, tk_spec, tk_spec,
        ],
        out_specs=ospec,
        out_shape=jax.ShapeDtypeStruct((N_HEADS, t, LANES), BF16),
        scratch_shapes=[pltpu.VMEM((N_HEADS, s_len, LANES), BF16),
                        pltpu.VMEM((N_HEADS, GATE_ROWS, LANES), F32),
                        pltpu.VMEM((N_HEADS, nk, HEAD_DIM, K_TILE), BF16),
                        pltpu.VMEM((2, N_HEADS, K_TILE, Q_TILE), F32),
                        pltpu.VMEM((N_HEADS, HEAD_DIM, Q_TILE), F32)],
        compiler_params=pltpu.CompilerParams(dimension_semantics=("arbitrary", "arbitrary")),
        name="moba",
    )(proj, proj, proj, proj, *tables, *tables)


def _gla_kernel(x_ref, v_ref, g_ref, p_ref, ng_ref, o_ref):
    s_len = x_ref.shape[1]
    c_len = GLA_CHUNK
    scale = GLA_DK ** -0.5
    lane = lax.broadcasted_iota(jnp.int32, (c_len, LANES), 1)
    qhalf = lane < GLA_DK
    qhalf_row = lax.broadcasted_iota(jnp.int32, (1, LANES), 1) < GLA_DK
    row = lax.broadcasted_iota(jnp.int32, (c_len, c_len), 0)
    col = lax.broadcasted_iota(jnp.int32, (c_len, c_len), 1)
    causal = col <= row

    group = GLA_GROUP
    pairs = [(h, c) for h in range(N_HEADS) for c in range(group)]

    def body(g, carries):
        prep = {}
        for h, c in pairs:
            rows = pl.ds(pl.multiple_of(g * (group * c_len), group * c_len) + c * c_len, c_len)
            qcb = x_ref[h // 2, rows, :].astype(F32)
            kcb = x_ref[2 + h // 2, rows, :].astype(F32)
            if h % 2 == 0:
                x = jnp.where(qhalf, qcb, pltpu.roll(kcb, GLA_DK, 1))
            else:
                x = jnp.where(qhalf, pltpu.roll(qcb, GLA_DK, 1), kcb)
            p = p_ref[h, rows, :]
            base = carries[h][1] if c == 0 else prep[h, c - 1]["p_last"]
            cum = p - base
            xt = x * jnp.exp(jnp.where(qhalf, cum, -cum))
            kr = pltpu.roll(xt, GLA_DK, 1)
            decay = jnp.where(qhalf_row, jnp.exp(cum[c_len - 1:c_len]), 0.0)
            prep[h, c] = dict(
                rows=rows, qz=jnp.where(qhalf, xt, 0.0).astype(BF16), kr=kr.astype(BF16),
                khat=(jnp.where(qhalf, kr, 0.0) * decay).astype(BF16),
                decay=decay, v=v_ref[h, rows, :], p_last=p[c_len - 1:c_len])
        scores, upd = {}, {}
        for h, c in pairs:
            d = prep[h, c]
            scores[h, c] = _nt_dot(d["qz"], d["kr"])
            upd[h, c] = lax.dot_general(d["v"], d["khat"], (((0,), (0,)), ((), ())),
                                        preferred_element_type=F32)
        states = {}
        for h in range(N_HEADS):
            states[h, 0] = carries[h][0]
            for c in range(group):
                states[h, c + 1] = states[h, c] * prep[h, c]["decay"] + upd[h, c]
        outs = {}
        for h, c in pairs:
            sc = jnp.where(causal, scores[h, c] * scale, 0.0).astype(BF16)
            outs[h, c] = (jnp.dot(sc, prep[h, c]["v"], preferred_element_type=F32),
                          _nt_dot(prep[h, c]["qz"], states[h, c].astype(BF16)))
        for h, c in pairs:
            o = outs[h, c][0] + outs[h, c][1] * scale
            y = o * lax.rsqrt(jnp.mean(o * o, axis=-1, keepdims=True) + RMS_EPS) * ng_ref[...]
            rows = prep[h, c]["rows"]
            o_ref[h, rows, :] = (y * _silu(g_ref[h, rows, :].astype(F32))).astype(BF16)
        return tuple((states[h, group], prep[h, group - 1]["p_last"]) for h in range(N_HEADS))

    init = tuple((jnp.zeros((HEAD_DIM, LANES), F32), jnp.zeros((1, LANES), F32))
                 for _ in range(N_HEADS))
    lax.fori_loop(0, s_len // (group * c_len), body, init)


def _gla(proj, p_cum, gla_norm_g, *, batch, s_len):
    t = proj.shape[1]
    spec = lambda slab: pl.BlockSpec((N_HEADS, s_len, LANES), lambda b: (slab, b, 0))
    return pl.pallas_call(
        _gla_kernel,
        grid=(batch,),
        in_specs=[
            spec(_SLAB["gla_qk"]), spec(_SLAB["gla_v"]), spec(_SLAB["gla_g"]),
            pl.BlockSpec((N_HEADS, s_len, LANES), lambda b: (0, b, 0)),
            pl.BlockSpec((1, HEAD_DIM), lambda b: (0, 0)),
        ],
        out_specs=pl.BlockSpec((N_HEADS, s_len, LANES), lambda b: (0, b, 0)),
        out_shape=jax.ShapeDtypeStruct((N_HEADS, t, LANES), BF16),
        compiler_params=pltpu.CompilerParams(dimension_semantics=("arbitrary",)),
        name="gla",
    )(proj, proj, proj, p_cum, gla_norm_g.reshape(1, HEAD_DIM))


def _outproj_kernel(final, x_ref, a_ref, b_ref, c_ref, d_ref, w_ref, fg_ref, o_ref):
    parts = [r[h] for r in (a_ref, b_ref, c_ref, d_ref) for h in range(N_HEADS)]
    mixed = jnp.concatenate(parts, axis=-1)
    y = x_ref[...] + jnp.dot(mixed, w_ref[...], preferred_element_type=F32)
    if final:
        y = y * lax.rsqrt(jnp.mean(y * y, axis=-1, keepdims=True) + RMS_EPS) * fg_ref[...]
    o_ref[...] = y


def _outproj(x2, mixers, w_out, layer, final_g, *, final, tm):
    t, d = x2.shape
    mspec = pl.BlockSpec((N_HEADS, tm, LANES), lambda i: (0, i, 0))
    return pl.pallas_call(
        functools.partial(_outproj_kernel, final),
        grid=(t // tm,),
        in_specs=[
            pl.BlockSpec((tm, d), lambda i: (i, 0)),
            mspec, mspec, mspec, mspec,
            pl.BlockSpec((None,) + w_out.shape[1:], lambda i: (layer, 0, 0)),
            pl.BlockSpec((1, d), lambda i: (0, 0)),
        ],
        out_specs=pl.BlockSpec((tm, d), lambda i: (i, 0)),
        out_shape=jax.ShapeDtypeStruct((t, d), F32),
        compiler_params=pltpu.CompilerParams(
            dimension_semantics=("arbitrary",), vmem_limit_bytes=56 * 1024 * 1024),
        name="outproj",
    )(x2, *mixers, w_out, final_g.reshape(1, d))


def _relayout_kernel(w_ref, o_ref, small_ref):
    w = w_ref[...]
    pieces = [w[:, a:b] for a, b in _MAIN_RUNS]
    o_ref[...] = jnp.concatenate(pieces, axis=-1)
    f_off, a_off = _SEG_OFF["fox_f"][0], _SEG_OFF["gla_a"][0] - N_HEADS
    lane = lax.broadcasted_iota(jnp.int32, (w.shape[0], LANES), 1)
    small_ref[...] = jnp.where(lane < N_HEADS, w[:, f_off:f_off + LANES],
                               jnp.where(lane < N_HEADS + GLA_RANK, w[:, a_off:a_off + LANES],
                                         jnp.zeros((), BF16)))


def _relayout_w_in(w_in, *, rows):
    depth, d, d_in = w_in.shape
    assert _SEG_OFF["fox_f"][0] % LANES == 0 and (_SEG_OFF["gla_a"][0] - N_HEADS) % LANES == 0
    return pl.pallas_call(
        _relayout_kernel,
        grid=(depth, d // rows),
        in_specs=[pl.BlockSpec((None, rows, d_in), lambda l, r: (l, r, 0))],
        out_specs=[pl.BlockSpec((None, rows, N_MAIN), lambda l, r: (l, r, 0)),
                   pl.BlockSpec((None, rows, LANES), lambda l, r: (l, r, 0))],
        out_shape=[jax.ShapeDtypeStruct((depth, d, N_MAIN), BF16),
                   jax.ShapeDtypeStruct((depth, d, LANES), BF16)],
        compiler_params=pltpu.CompilerParams(dimension_semantics=("arbitrary", "arbitrary")),
        name="relayout",
    )(w_in)


def _dup_halves(a):
    parts = []
    for h in range(N_HEADS):
        blk = a[..., h * GLA_DK:(h + 1) * GLA_DK]
        parts += [blk, blk]
    return jnp.concatenate(parts, axis=-1)


def _rope_tables(s_len, comp_dim, rot_dim):
    half = rot_dim // 2
    inv_freq = ROPE_THETA ** (-jnp.arange(0, rot_dim, 2, dtype=F32) / rot_dim)
    ang = jnp.arange(s_len, dtype=F32)[:, None] * inv_freq[None, :]
    cos, sin = jnp.cos(ang), jnp.sin(ang)
    zeros = jnp.zeros((s_len, comp_dim - rot_dim), F32)
    ones = jnp.ones((s_len, comp_dim - rot_dim), F32)
    reps = LANES // comp_dim
    cos_t = jnp.tile(jnp.concatenate([cos, cos, ones], axis=-1), (1, reps))
    sin_up = jnp.tile(jnp.concatenate([-sin, jnp.zeros_like(sin), zeros], axis=-1), (1, reps))
    sin_dn = jnp.tile(jnp.concatenate([jnp.zeros_like(sin), sin, zeros], axis=-1), (1, reps))
    return cos_t, sin_up, sin_dn


def kernel(x, norm_g, w_in, fox_fb, diff_lam, diff_norm_g, gla_wa2, gla_ba, gla_norm_g, w_out,
           final_norm_g):
    batch, s_len, d_model = x.shape
    depth = w_in.shape[0]
    assert s_len % Q_TILE == 0 and Q_TILE == K_TILE == MOBA_BLOCK
    assert s_len // MOBA_BLOCK <= GATE_ROWS
    t = batch * s_len
    tm_in = min(1024, t)
    tn_in = 1536
    tm_out = min(512, t)

    w_main, w_small = _relayout_w_in(w_in.astype(BF16), rows=min(256, d_model))
    w_out_b = w_out.astype(BF16)
    fb_pad = jnp.pad(fox_fb, ((0, 0), (0, LANES - N_HEADS)))[:, None, :]
    wa_pad = jnp.pad(_dup_halves(gla_wa2), ((0, 0), (N_HEADS, LANES - N_HEADS - GLA_RANK), (0, 0))
                     ).astype(BF16)
    ba_dup = _dup_halves(gla_ba)[:, None, :]
    rope_diff = _rope_tables(s_len, DIFF_QK_DIM, DIFF_QK_DIM // ROPE_FRACTION)
    rope_moba = _rope_tables(s_len, HEAD_DIM, HEAD_DIM // ROPE_FRACTION)

    x2 = x.reshape(t, d_model)
    for l in range(depth):
        proj, small = _inproj(x2, norm_g[l], w_main, w_small, l, tm=tm_in, tn=tn_in)
        fk_rep, fq_rows, p_cum = _gates(small, fb_pad[l], wa_pad[l], ba_dup[l],
                                        batch=batch, s_len=s_len)
        lam_init = 0.8 - 0.6 * math.exp(-0.3 * l)
        mixers = (
            _fox(proj, fk_rep, fq_rows, batch=batch, s_len=s_len),
            _diff(proj, diff_lam[l], diff_norm_g[l], rope_diff, lam_init, batch=batch, s_len=s_len),
            _moba(proj, rope_moba, batch=batch, s_len=s_len),
            _gla(proj, p_cum, gla_norm_g[l], batch=batch, s_len=s_len),
        )
        x2 = _outproj(x2, mixers, w_out_b, l, final_norm_g, final=(l == depth - 1), tm=tm_out)
    return x2.reshape(batch, s_len, d_model)
```

```python
import functools
import math

import jax
import jax.numpy as jnp
from jax import lax
from jax.experimental import pallas as pl
from jax.experimental.pallas import tpu as pltpu

F32 = jnp.float32
BF16 = jnp.bfloat16

LANES = 128
HEAD_DIM = 128
N_HEADS = 4
GROUP_W = N_HEADS * HEAD_DIM
DIFF_QK_DIM = HEAD_DIM // 2
MOBA_BLOCK = 256
MOBA_TOPK = 3
GLA_DK = HEAD_DIM // 2
GLA_RANK = 16
GLA_TAU = 16.0
GLA_CHUNK = 64
ROPE_THETA = 500000.0
ROPE_FRACTION = 4
RMS_EPS = 1e-6
MASKED_LOGIT = -1e30

_SEGMENTS = (
    ("fox_q", GROUP_W), ("fox_k", GROUP_W), ("fox_v", GROUP_W),
    ("fox_f", N_HEADS), ("fox_g", GROUP_W),
    ("diff_q", GROUP_W), ("diff_k", GROUP_W), ("diff_v", GROUP_W), ("diff_g", GROUP_W),
    ("moba_q", GROUP_W), ("moba_k", GROUP_W), ("moba_v", GROUP_W), ("moba_g", GROUP_W),
    ("gla_q", N_HEADS * GLA_DK), ("gla_k", N_HEADS * GLA_DK), ("gla_v", GROUP_W),
    ("gla_a", GLA_RANK), ("gla_g", GROUP_W),
)
_SEG_OFF = {}
_off = 0
for _name, _w in _SEGMENTS:
    _SEG_OFF[_name] = (_off, _w)
    _off += _w

_SLABS = ("fox_q", "fox_k", "fox_v", "fox_g", "diff_q", "diff_k", "diff_v", "diff_g",
          "moba_q", "moba_k", "moba_v", "moba_g", "gla_qk", "gla_v", "gla_g")
_SLAB = {name: i for i, name in enumerate(_SLABS)}
N_MAIN = len(_SLABS) * GROUP_W
_MAIN_RUNS = ((0, _SEG_OFF["fox_f"][0]),
              (_SEG_OFF["fox_g"][0], _SEG_OFF["gla_a"][0]),
              (_SEG_OFF["gla_g"][0], _off))
assert sum(b - a for a, b in _MAIN_RUNS) == N_MAIN

Q_TILE = 256
K_TILE = 256
CUM_BLOCK = 128
GATE_ROWS = 16
GLA_GROUP = 4


def _silu(x):
    return x * (1.0 / (1.0 + jnp.exp(-x)))


def _log_sigmoid(x):
    return jnp.minimum(x, 0.0) - jnp.log1p(jnp.exp(-jnp.abs(x)))


def _nt_dot(a, b):
    return lax.dot_general(a, b, (((1,), (1,)), ((), ())), preferred_element_type=F32)


def _inproj_kernel(x_ref, g_ref, w_ref, ws_ref, o_ref, small_ref, h_scr):
    @pl.when(pl.program_id(1) == 0)
    def _():
        x = x_ref[...]
        y = x * lax.rsqrt(jnp.mean(x * x, axis=-1, keepdims=True) + RMS_EPS) * g_ref[...]
        h = y.astype(BF16)
        h_scr[...] = h
        small_ref[...] = _nt_dot(h, ws_ref[...])

    acc = _nt_dot(h_scr[...], w_ref[...])
    for c in range(o_ref.shape[0]):
        o_ref[c] = acc[:, c * LANES:(c + 1) * LANES].astype(BF16)


def _inproj(x2, norm_g, w_main, w_small, layer, *, tm, tn):
    t, d = x2.shape
    n_main = w_main.shape[1]
    return pl.pallas_call(
        _inproj_kernel,
        grid=(t // tm, n_main // tn),
        in_specs=[
            pl.BlockSpec((tm, d), lambda i, j: (i, 0)),
            pl.BlockSpec((1, d), lambda i, j: (0, 0)),
            pl.BlockSpec((None, tn, d), lambda i, j: (layer, j, 0)),
            pl.BlockSpec((None, LANES, d), lambda i, j: (layer, 0, 0)),
        ],
        out_specs=[
            pl.BlockSpec((tn // LANES, tm, LANES), lambda i, j: (j, i, 0)),
            pl.BlockSpec((tm, LANES), lambda i, j: (i, 0)),
        ],
        out_shape=[
            jax.ShapeDtypeStruct((n_main // LANES, t, LANES), BF16),
            jax.ShapeDtypeStruct((t, LANES), F32),
        ],
        scratch_shapes=[pltpu.VMEM((tm, d), BF16)],
        compiler_params=pltpu.CompilerParams(
            dimension_semantics=("arbitrary", "arbitrary"),
            vmem_limit_bytes=56 * 1024 * 1024),
        name="inproj",
    )(x2, norm_g.reshape(1, d), w_main, w_small)


def _gates_kernel(small_ref, fb_ref, wa_ref, ba_ref, fk_ref, fq_ref, p_ref):
    s_len = small_ref.shape[0]
    small = small_ref[...]
    log_f = _log_sigmoid(small + fb_ref[...])
    a_logit = jnp.dot(small.astype(BF16), wa_ref[...], preferred_element_type=F32) + ba_ref[...]
    log_a = _log_sigmoid(a_logit) * (1.0 / GLA_TAU)
    both = jnp.concatenate([log_f, log_a], axis=-1)

    row = lax.broadcasted_iota(jnp.int32, (CUM_BLOCK, CUM_BLOCK), 0)
    col = lax.broadcasted_iota(jnp.int32, (CUM_BLOCK, CUM_BLOCK), 1)
    tri = (col <= row).astype(F32)
    carry = jnp.zeros((1, both.shape[1]), F32)
    for c in range(s_len // CUM_BLOCK):
        blk = both[c * CUM_BLOCK:(c + 1) * CUM_BLOCK]
        cum = jnp.dot(tri, blk, precision=lax.Precision.HIGHEST,
                      preferred_element_type=F32) + carry
        carry = cum[CUM_BLOCK - 1:CUM_BLOCK]
        rows = slice(c * CUM_BLOCK, (c + 1) * CUM_BLOCK)
        f_cum = cum[:, :LANES]
        fq_ref[:, rows] = f_cum.T[:GATE_ROWS]
        for h in range(N_HEADS):
            fk_ref[h, rows, :] = jnp.broadcast_to(f_cum[:, h:h + 1], (CUM_BLOCK, LANES))
            p_ref[h, rows, :] = cum[:, (h + 1) * LANES:(h + 2) * LANES]


def _gates(small, fb_pad, wa_pad, ba_dup, *, batch, s_len):
    t = small.shape[0]
    return pl.pallas_call(
        _gates_kernel,
        grid=(batch,),
        in_specs=[
            pl.BlockSpec((s_len, LANES), lambda b: (b, 0)),
            pl.BlockSpec((1, LANES), lambda b: (0, 0)),
            pl.BlockSpec((LANES, GROUP_W), lambda b: (0, 0)),
            pl.BlockSpec((1, GROUP_W), lambda b: (0, 0)),
        ],
        out_specs=[
            pl.BlockSpec((N_HEADS, s_len, LANES), lambda b: (0, b, 0)),
            pl.BlockSpec((None, GATE_ROWS, s_len), lambda b: (b, 0, 0)),
            pl.BlockSpec((N_HEADS, s_len, LANES), lambda b: (0, b, 0)),
        ],
        out_shape=[
            jax.ShapeDtypeStruct((N_HEADS, t, LANES), F32),
            jax.ShapeDtypeStruct((batch, GATE_ROWS, s_len), F32),
            jax.ShapeDtypeStruct((N_HEADS, t, LANES), F32),
        ],
        compiler_params=pltpu.CompilerParams(dimension_semantics=("arbitrary",)),
        name="gates",
    )(small, fb_pad, wa_pad, ba_dup)


def _causal_mask_t(tk, width, tq):
    key = lax.broadcasted_iota(jnp.int32, (tk, width), 0)
    qry = lax.broadcasted_iota(jnp.int32, (tk, width), 1)
    if width != tq:
        qry = qry & (tq - 1)
    return key <= qry


def _transpose_values(v_ref, vt_scr):
    for h in range(v_ref.shape[0]):
        for j in range(v_ref.shape[1] // K_TILE):
            blk = v_ref[h, j * K_TILE:(j + 1) * K_TILE, :].astype(F32)
            vt_scr[h, j] = blk.T.astype(BF16)


def _col_reduce(z, op):
    rows = z.shape[0]
    while rows > 8:
        rows //= 2
        z = op(z[:rows], z[rows:])
    if op is jnp.add:
        return jnp.sum(z, axis=0, keepdims=True)
    return jnp.max(z, axis=0, keepdims=True)


def _online_softmax_t(last, logits_fn, vt_fn, tq, z_scr, acc_scr):
    n, _, width = acc_scr.shape

    def update(j, slot, ms, ls, tile_max, masked):
        stats = []
        for s in range(n):
            z = z_scr[slot, s]
            if masked:
                z = jnp.where(_causal_mask_t(K_TILE, width, tq), z, -jnp.inf)
                m_new = jnp.maximum(ms[s], _col_reduce(z, jnp.maximum))
            else:
                m_new = jnp.maximum(ms[s], tile_max[s])
            alpha = jnp.exp(ms[s] - m_new)
            p = jnp.exp(z - m_new)
            stats.append((m_new, alpha * ls[s] + _col_reduce(p, jnp.add), alpha, p.astype(BF16)))
        for s in range(n):
            pv = jnp.dot(vt_fn(s, j), stats[s][3], preferred_element_type=F32)
            acc_scr[s] = stats[s][2] * acc_scr[s] + pv
        return tuple(st[0] for st in stats), tuple(st[1] for st in stats)

    first = [logits_fn(s, jnp.int32(0)) for s in range(n)]
    for s in range(n):
        z_scr[0, s] = first[s]
        acc_scr[s] = jnp.zeros(acc_scr.shape[1:], F32)

    def body(j, state):
        ms, ls, tile_max = state
        slot = lax.rem(j, 2)
        z_next = [logits_fn(s, j + 1) for s in range(n)]
        ms, ls = update(j, slot, ms, ls, tile_max, False)
        for s in range(n):
            z_scr[1 - slot, s] = z_next[s]
        return ms, ls, tuple(_col_reduce(z, jnp.maximum) for z in z_next)

    init = (tuple(jnp.full((1, width), -jnp.inf, F32) for _ in range(n)),
            tuple(jnp.zeros((1, width), F32) for _ in range(n)),
            tuple(_col_reduce(z, jnp.maximum) for z in first))
    ms, ls, _ = lax.fori_loop(0, last, body, init)
    _, ls = update(last, lax.rem(last, 2), ms, ls, None, True)
    return ls


def _rope(x, cos_t, sin_up, sin_dn, half):
    return (x * cos_t + pltpu.roll(x, LANES - half, 1) * sin_up
            + pltpu.roll(x, half, 1) * sin_dn)


def _key_rows(j):
    return pl.ds(pl.multiple_of(j * K_TILE, K_TILE), K_TILE)


def _mixer_specs(s_len, nq):
    qspec = lambda slab: pl.BlockSpec((N_HEADS, Q_TILE, LANES), lambda b, i: (slab, b * nq + i, 0))
    kspec = lambda slab: pl.BlockSpec((N_HEADS, s_len, LANES), lambda b, i: (slab, b, 0))
    ospec = pl.BlockSpec((N_HEADS, Q_TILE, LANES), lambda b, i: (0, b * nq + i, 0))
    return qspec, kspec, ospec


def _fox_kernel(q_ref, k_ref, v_ref, g_ref, fk_ref, fq_ref, o_ref, vt_scr, z_scr, acc_scr):
    i = pl.program_id(1)
    tq = q_ref.shape[1]
    scale = HEAD_DIM ** -0.5

    @pl.when(i == 0)
    def _():
        _transpose_values(v_ref, vt_scr)

    fq_all = fq_ref[...]
    qs = [q_ref[h] for h in range(N_HEADS)]
    fqs = [fq_all[h:h + 1, :] for h in range(N_HEADS)]

    def logits(h, j):
        fk = fk_ref[h, _key_rows(j), :]
        fk = jnp.concatenate([fk] * (tq // LANES), axis=-1)
        return _nt_dot(k_ref[h, _key_rows(j), :], qs[h]) * scale + (fqs[h] - fk)

    ls = _online_softmax_t(i, logits, lambda h, j: vt_scr[h, j], tq, z_scr, acc_scr)
    for h in range(N_HEADS):
        o = (acc_scr[h] / ls[h]).T
        o_ref[h] = (o * _silu(g_ref[h].astype(F32))).astype(BF16)


def _fox(proj, fk_rep, fq_rows, *, batch, s_len):
    t = proj.shape[1]
    nq = s_len // Q_TILE
    nk = s_len // K_TILE
    qspec, kspec, ospec = _mixer_specs(s_len, nq)
    return pl.pallas_call(
        _fox_kernel,
        grid=(batch, nq),
        in_specs=[
            qspec(_SLAB["fox_q"]), kspec(_SLAB["fox_k"]), kspec(_SLAB["fox_v"]),
            qspec(_SLAB["fox_g"]),
            pl.BlockSpec((N_HEADS, s_len, LANES), lambda b, i: (0, b, 0)),
            pl.BlockSpec((None, GATE_ROWS, Q_TILE), lambda b, i: (b, 0, i)),
        ],
        out_specs=ospec,
        out_shape=jax.ShapeDtypeStruct((N_HEADS, t, LANES), BF16),
        scratch_shapes=[pltpu.VMEM((N_HEADS, nk, HEAD_DIM, K_TILE), BF16),
                        pltpu.VMEM((2, N_HEADS, K_TILE, Q_TILE), F32),
                        pltpu.VMEM((N_HEADS, HEAD_DIM, Q_TILE), F32)],
        compiler_params=pltpu.CompilerParams(dimension_semantics=("arbitrary", "arbitrary")),
        name="fox",
    )(proj, proj, proj, proj, fk_rep, fq_rows)


def _diff_kernel(lam_init, q_ref, k_ref, v_ref, g_ref, lam_ref, ng_ref,
                 cq_ref, uq_ref, dq_ref, ck_ref, uk_ref, dk_ref, o_ref,
                 kr_scr, vt_scr, z_scr, acc_scr):
    i = pl.program_id(1)
    tq = q_ref.shape[1]
    half = DIFF_QK_DIM // ROPE_FRACTION // 2
    scale = DIFF_QK_DIM ** -0.5

    @pl.when(i == 0)
    def _():
        _transpose_values(v_ref, vt_scr)
        for h in range(N_HEADS):
            kr_scr[h] = _rope(k_ref[h].astype(F32), ck_ref[...], uk_ref[...], dk_ref[...],
                              half).astype(BF16)

    lf = lam_ref[...]
    lam = (jnp.exp(jnp.sum(lf[0:1] * lf[1:2], axis=-1, keepdims=True))
           - jnp.exp(jnp.sum(lf[2:3] * lf[3:4], axis=-1, keepdims=True)) + lam_init)

    first = lax.broadcasted_iota(jnp.int32, (tq, LANES), 1) < DIFF_QK_DIM
    q12 = []
    for h in range(N_HEADS):
        q = _rope(q_ref[h].astype(F32), cq_ref[...], uq_ref[...], dq_ref[...], half)
        q12.append(jnp.concatenate([jnp.where(first, q, 0.0),
                                    jnp.where(first, 0.0, q)], axis=0).astype(BF16))

    def logits(h, j):
        return _nt_dot(kr_scr[h, _key_rows(j), :], q12[h]) * scale

    ls = _online_softmax_t(i, logits, lambda h, j: vt_scr[h, j], tq, z_scr, acc_scr)
    for h in range(N_HEADS):
        a = acc_scr[h] / ls[h]
        o = (a[:, :tq] - lam * a[:, tq:]).T
        o = o * lax.rsqrt(jnp.mean(o * o, axis=-1, keepdims=True) + RMS_EPS) * ng_ref[...]
        o = o * (1.0 - lam_init)
        o_ref[h] = (o * _silu(g_ref[h].astype(F32))).astype(BF16)


def _diff(proj, diff_lam, diff_norm_g, tables, lam_init, *, batch, s_len):
    t = proj.shape[1]
    nq = s_len // Q_TILE
    nk = s_len // K_TILE
    qspec, kspec, ospec = _mixer_specs(s_len, nq)
    tq_spec = pl.BlockSpec((Q_TILE, LANES), lambda b, i: (i, 0))
    tk_spec = pl.BlockSpec((s_len, LANES), lambda b, i: (0, 0))
    return pl.pallas_call(
        functools.partial(_diff_kernel, lam_init),
        grid=(batch, nq),
        in_specs=[
            qspec(_SLAB["diff_q"]), kspec(_SLAB["diff_k"]), kspec(_SLAB["diff_v"]),
            qspec(_SLAB["diff_g"]),
            pl.BlockSpec((4, DIFF_QK_DIM), lambda b, i: (0, 0)),
            pl.BlockSpec((1, HEAD_DIM), lambda b, i: (0, 0)),
            tq_spec, tq_spec, tq_spec, tk_spec, tk_spec, tk_spec,
        ],
        out_specs=ospec,
        out_shape=jax.ShapeDtypeStruct((N_HEADS, t, LANES), BF16),
        scratch_shapes=[pltpu.VMEM((N_HEADS, s_len, LANES), BF16),
                        pltpu.VMEM((N_HEADS, nk, HEAD_DIM, K_TILE), BF16),
                        pltpu.VMEM((2, N_HEADS, K_TILE, 2 * Q_TILE), F32),
                        pltpu.VMEM((N_HEADS, HEAD_DIM, 2 * Q_TILE), F32)],
        compiler_params=pltpu.CompilerParams(dimension_semantics=("arbitrary", "arbitrary")),
        name="diff",
    )(proj, proj, proj, proj, diff_lam, diff_norm_g.reshape(1, HEAD_DIM), *tables, *tables)


def _moba_kernel(q_ref, k_ref, v_ref, g_ref, cq_ref, uq_ref, dq_ref, ck_ref, uk_ref, dk_ref,
                 o_ref, kr_scr, kmean_scr, vt_scr, z_scr, acc_scr):
    i = pl.program_id(1)
    tq = q_ref.shape[1]
    half = HEAD_DIM // ROPE_FRACTION // 2
    scale = HEAD_DIM ** -0.5
    n_blk = k_ref.shape[1] // MOBA_BLOCK

    @pl.when(i == 0)
    def _():
        _transpose_values(v_ref, vt_scr)
        kmean_scr[...] = jnp.zeros(kmean_scr.shape, F32)
        for h in range(N_HEADS):
            kr = _rope(k_ref[h].astype(F32), ck_ref[...], uk_ref[...], dk_ref[...], half)
            kr_scr[h] = kr.astype(BF16)
            for n in range(n_blk):
                blk = kr[n * MOBA_BLOCK:(n + 1) * MOBA_BLOCK]
                kmean_scr[h, n:n + 1, :] = jnp.sum(blk, axis=0, keepdims=True) * (1.0 / MOBA_BLOCK)

    blk_id = lax.broadcasted_iota(jnp.int32, (GATE_ROWS, tq), 0).astype(F32)
    past = blk_id < i.astype(F32)
    qs, biases = [], []
    for h in range(N_HEADS):
        q = _rope(q_ref[h].astype(F32), cq_ref[...], uq_ref[...], dq_ref[...], half).astype(BF16)
        gate = _nt_dot(kmean_scr[h].astype(BF16), q)
        gate = jnp.where(past, gate, -jnp.inf)
        open_ = past
        sel = jnp.zeros((GATE_ROWS, tq), jnp.bool_)
        for _ in range(MOBA_TOPK):
            top = jnp.max(gate, axis=0, keepdims=True)
            idx = jnp.min(jnp.where(gate == top, blk_id, float(GATE_ROWS)), axis=0, keepdims=True)
            pick = (blk_id == idx) & open_
            sel = sel | pick
            open_ = open_ & jnp.logical_not(pick)
            gate = jnp.where(pick, -jnp.inf, gate)
        qs.append(q)
        biases.append(jnp.where(sel, 0.0, MASKED_LOGIT))

    def logits(h, j):
        bias = jnp.sum(jnp.where(blk_id == j.astype(F32), biases[h], 0.0), axis=0, keepdims=True)
        bias = jnp.where(j == i, 0.0, bias)
        return _nt_dot(kr_scr[h, _key_rows(j), :], qs[h]) * scale + bias

    ls = _online_softmax_t(i, logits, lambda h, j: vt_scr[h, j], tq, z_scr, acc_scr)
    for h in range(N_HEADS):
        o = (acc_scr[h] / ls[h]).T
        o_ref[h] = (o * _silu(g_ref[h].astype(F32))).astype(BF16)


def _moba(proj, tables, *, batch, s_len):
    t = proj.shape[1]
    nq = s_len // Q_TILE
    nk = s_len // K_TILE
    qspec, kspec, ospec = _mixer_specs(s_len, nq)
    tq_spec = pl.BlockSpec((Q_TILE, LANES), lambda b, i: (i, 0))
    tk_spec = pl.BlockSpec((s_len, LANES), lambda b, i: (0, 0))
    return pl.pallas_call(
        _moba_kernel,
        grid=(batch, nq),
        in_specs=[
            qspec(_SLAB["moba_q"]), kspec(_SLAB["moba_k"]), kspec(_SLAB["moba_v"]),
            qspec(_SLAB["moba_g"]),
            tq_spec, tq_spec, tq_spec, tk_spec, tk_spec, tk_spec,
        ],
        out_specs=ospec,
        out_shape=jax.ShapeDtypeStruct((N_HEADS, t, LANES), BF16),
        scratch_shapes=[pltpu.VMEM((N_HEADS, s_len, LANES), BF16),
                        pltpu.VMEM((N_HEADS, GATE_ROWS, LANES), F32),
                        pltpu.VMEM((N_HEADS, nk, HEAD_DIM, K_TILE), BF16),
                        pltpu.VMEM((2, N_HEADS, K_TILE, Q_TILE), F32),
                        pltpu.VMEM((N_HEADS, HEAD_DIM, Q_TILE), F32)],
        compiler_params=pltpu.CompilerParams(dimension_semantics=("arbitrary", "arbitrary")),
        name="moba",
    )(proj, proj, proj, proj, *tables, *tables)


def _gla_kernel(x_ref, v_ref, g_ref, p_ref, ng_ref, o_ref):
    s_len = x_ref.shape[1]
    c_len = GLA_CHUNK
    scale = GLA_DK ** -0.5
    lane = lax.broadcasted_iota(jnp.int32, (c_len, LANES), 1)
    qhalf = lane < GLA_DK
    qhalf_row = lax.broadcasted_iota(jnp.int32, (1, LANES), 1) < GLA_DK
    row = lax.broadcasted_iota(jnp.int32, (c_len, c_len), 0)
    col = lax.broadcasted_iota(jnp.int32, (c_len, c_len), 1)
    causal = col <= row

    group = GLA_GROUP
    pairs = [(h, c) for h in range(N_HEADS) for c in range(group)]

    def body(g, carries):
        prep = {}
        for h, c in pairs:
            rows = pl.ds(pl.multiple_of(g * (group * c_len), group * c_len) + c * c_len, c_len)
            qcb = x_ref[h // 2, rows, :].astype(F32)
            kcb = x_ref[2 + h // 2, rows, :].astype(F32)
            if h % 2 == 0:
                x = jnp.where(qhalf, qcb, pltpu.roll(kcb, GLA_DK, 1))
            else:
                x = jnp.where(qhalf, pltpu.roll(qcb, GLA_DK, 1), kcb)
            p = p_ref[h, rows, :]
            base = carries[h][1] if c == 0 else prep[h, c - 1]["p_last"]
            cum = p - base
            xt = x * jnp.exp(jnp.where(qhalf, cum, -cum))
            kr = pltpu.roll(xt, GLA_DK, 1)
            decay = jnp.where(qhalf_row, jnp.exp(cum[c_len - 1:c_len]), 0.0)
            prep[h, c] = dict(
                rows=rows, qz=jnp.where(qhalf, xt, 0.0).astype(BF16), kr=kr.astype(BF16),
                khat=(jnp.where(qhalf, kr, 0.0) * decay).astype(BF16),
                decay=decay, v=v_ref[h, rows, :], p_last=p[c_len - 1:c_len])
        scores, upd = {}, {}
        for h, c in pairs:
            d = prep[h, c]
            scores[h, c] = _nt_dot(d["qz"], d["kr"])
            upd[h, c] = lax.dot_general(d["v"], d["khat"], (((0,), (0,)), ((), ())),
                                        preferred_element_type=F32)
        states = {}
        for h in range(N_HEADS):
            states[h, 0] = carries[h][0]
            for c in range(group):
                states[h, c + 1] = states[h, c] * prep[h, c]["decay"] + upd[h, c]
        outs = {}
        for h, c in pairs:
            sc = jnp.where(causal, scores[h, c] * scale, 0.0).astype(BF16)
            outs[h, c] = (jnp.dot(sc, prep[h, c]["v"], preferred_element_type=F32),
                          _nt_dot(prep[h, c]["qz"], states[h, c].astype(BF16)))
        for h, c in pairs:
            o = outs[h, c][0] + outs[h, c][1] * scale
            y = o * lax.rsqrt(jnp.mean(o * o, axis=-1, keepdims=True) + RMS_EPS) * ng_ref[...]
            rows = prep[h, c]["rows"]
            o_ref[h, rows, :] = (y * _silu(g_ref[h, rows, :].astype(F32))).astype(BF16)
        return tuple((states[h, group], prep[h, group - 1]["p_last"]) for h in range(N_HEADS))

    init = tuple((jnp.zeros((HEAD_DIM, LANES), F32), jnp.zeros((1, LANES), F32))
                 for _ in range(N_HEADS))
    lax.fori_loop(0, s_len // (group * c_len), body, init)


def _gla(proj, p_cum, gla_norm_g, *, batch, s_len):
    t = proj.shape[1]
    spec = lambda slab: pl.BlockSpec((N_HEADS, s_len, LANES), lambda b: (slab, b, 0))
    return pl.pallas_call(
        _gla_kernel,
        grid=(batch,),
        in_specs=[
            spec(_SLAB["gla_qk"]), spec(_SLAB["gla_v"]), spec(_SLAB["gla_g"]),
            pl.BlockSpec((N_HEADS, s_len, LANES), lambda b: (0, b, 0)),
            pl.BlockSpec((1, HEAD_DIM), lambda b: (0, 0)),
        ],
        out_specs=pl.BlockSpec((N_HEADS, s_len, LANES), lambda b: (0, b, 0)),
        out_shape=jax.ShapeDtypeStruct((N_HEADS, t, LANES), BF16),
        compiler_params=pltpu.CompilerParams(dimension_semantics=("arbitrary",)),
        name="gla",
    )(proj, proj, proj, p_cum, gla_norm_g.reshape(1, HEAD_DIM))


def _outproj_kernel(final, x_ref, a_ref, b_ref, c_ref, d_ref, w_ref, fg_ref, o_ref):
    parts = [r[h] for r in (a_ref, b_ref, c_ref, d_ref) for h in range(N_HEADS)]
    mixed = jnp.concatenate(parts, axis=-1)
    y = x_ref[...] + jnp.dot(mixed, w_ref[...], preferred_element_type=F32)
    if final:
        y = y * lax.rsqrt(jnp.mean(y * y, axis=-1, keepdims=True) + RMS_EPS) * fg_ref[...]
    o_ref[...] = y


def _outproj(x2, mixers, w_out, layer, final_g, *, final, tm):
    t, d = x2.shape
    mspec = pl.BlockSpec((N_HEADS, tm, LANES), lambda i: (0, i, 0))
    return pl.pallas_call(
        functools.partial(_outproj_kernel, final),
        grid=(t // tm,),
        in_specs=[
            pl.BlockSpec((tm, d), lambda i: (i, 0)),
            mspec, mspec, mspec, mspec,
            pl.BlockSpec((None,) + w_out.shape[1:], lambda i: (layer, 0, 0)),
            pl.BlockSpec((1, d), lambda i: (0, 0)),
        ],
        out_specs=pl.BlockSpec((tm, d), lambda i: (i, 0)),
        out_shape=jax.ShapeDtypeStruct((t, d), F32),
        compiler_params=pltpu.CompilerParams(
            dimension_semantics=("arbitrary",), vmem_limit_bytes=56 * 1024 * 1024),
        name="outproj",
    )(x2, *mixers, w_out, final_g.reshape(1, d))


RELAYOUT_ROWS = 512
RELAYOUT_TAIL = 32


def _relayout_kernel(a_ref, b_ref, o_ref):
    r = pl.program_id(1)
    rows = o_ref.shape[0]
    x = pltpu.bitcast(jnp.concatenate([a_ref[...], b_ref[...]], axis=0), jnp.uint32)
    out_lo = 0
    for (src_lo, src_hi) in _MAIN_RUNS:
        shift = src_lo - out_lo
        out_hi = out_lo + (src_hi - src_lo)
        assert shift % 2 == 0 and shift <= RELAYOUT_TAIL and out_lo % rows == 0 and out_hi % rows == 0

        @pl.when((r >= out_lo // rows) & (r < out_hi // rows))
        def _(shift=shift):
            o_ref[...] = pltpu.bitcast(x[shift // 2:shift // 2 + rows // 2], BF16)

        out_lo = out_hi


def _relayout_w_in(wt):
    depth, d_in, d = wt.shape
    rows, tail = RELAYOUT_ROWS, RELAYOUT_TAIL
    main = pl.pallas_call(
        _relayout_kernel,
        grid=(depth, N_MAIN // rows),
        in_specs=[pl.BlockSpec((None, rows, d), lambda l, r: (l, r, 0)),
                  pl.BlockSpec((None, tail, d), lambda l, r: (l, (r + 1) * (rows // tail), 0))],
        out_specs=pl.BlockSpec((None, rows, d), lambda l, r: (l, r, 0)),
        out_shape=jax.ShapeDtypeStruct((depth, N_MAIN, d), BF16),
        compiler_params=pltpu.CompilerParams(dimension_semantics=("arbitrary", "arbitrary")),
        name="relayout",
    )(wt, wt)
    f_off, a_off = _SEG_OFF["fox_f"][0], _SEG_OFF["gla_a"][0]
    small = jnp.concatenate(
        [wt[:, f_off:f_off + N_HEADS], wt[:, a_off:a_off + GLA_RANK],
         jnp.zeros((depth, LANES - N_HEADS - GLA_RANK, d), BF16)], axis=1)
    return main, small


def _dup_halves(a):
    parts = []
    for h in range(N_HEADS):
        blk = a[..., h * GLA_DK:(h + 1) * GLA_DK]
        parts += [blk, blk]
    return jnp.concatenate(parts, axis=-1)


def _rope_tables(s_len, comp_dim, rot_dim):
    half = rot_dim // 2
    inv_freq = ROPE_THETA ** (-jnp.arange(0, rot_dim, 2, dtype=F32) / rot_dim)
    ang = jnp.arange(s_len, dtype=F32)[:, None] * inv_freq[None, :]
    cos, sin = jnp.cos(ang), jnp.sin(ang)
    zeros = jnp.zeros((s_len, comp_dim - rot_dim), F32)
    ones = jnp.ones((s_len, comp_dim - rot_dim), F32)
    reps = LANES // comp_dim
    cos_t = jnp.tile(jnp.concatenate([cos, cos, ones], axis=-1), (1, reps))
    sin_up = jnp.tile(jnp.concatenate([-sin, jnp.zeros_like(sin), zeros], axis=-1), (1, reps))
    sin_dn = jnp.tile(jnp.concatenate([jnp.zeros_like(sin), sin, zeros], axis=-1), (1, reps))
    return cos_t, sin_up, sin_dn


def kernel(x, norm_g, w_in, fox_fb, diff_lam, diff_norm_g, gla_wa2, gla_ba, gla_norm_g, w_out,
           final_norm_g):
    batch, s_len, d_model = x.shape
    depth = w_in.shape[0]
    assert s_len % Q_TILE == 0 and Q_TILE == K_TILE == MOBA_BLOCK
    assert s_len // MOBA_BLOCK <= GATE_ROWS
    t = batch * s_len
    tm_in = min(1024, t)
    tn_in = 1536
    tm_out = min(512, t)

    w_main, w_small = _relayout_w_in(jnp.transpose(w_in, (0, 2, 1)).astype(BF16))
    w_out_b = w_out.astype(BF16)
    fb_pad = jnp.pad(fox_fb, ((0, 0), (0, LANES - N_HEADS)))[:, None, :]
    wa_pad = jnp.pad(_dup_halves(gla_wa2), ((0, 0), (N_HEADS, LANES - N_HEADS - GLA_RANK), (0, 0))
                     ).astype(BF16)
    ba_dup = _dup_halves(gla_ba)[:, None, :]
    rope_diff = _rope_tables(s_len, DIFF_QK_DIM, DIFF_QK_DIM // ROPE_FRACTION)
    rope_moba = _rope_tables(s_len, HEAD_DIM, HEAD_DIM // ROPE_FRACTION)

    x2 = x.reshape(t, d_model)
    for l in range(depth):
        proj, small = _inproj(x2, norm_g[l], w_main, w_small, l, tm=tm_in, tn=tn_in)
        fk_rep, fq_rows, p_cum = _gates(small, fb_pad[l], wa_pad[l], ba_dup[l],
                                        batch=batch, s_len=s_len)
        lam_init = 0.8 - 0.6 * math.exp(-0.3 * l)
        mixers = (
            _fox(proj, fk_rep, fq_rows, batch=batch, s_len=s_len),
            _diff(proj, diff_lam[l], diff_norm_g[l], rope_diff, lam_init, batch=batch, s_len=s_len),
            _moba(proj, rope_moba, batch=batch, s_len=s_len),
            _gla(proj, p_cum, gla_norm_g[l], batch=batch, s_len=s_len),
        )
        x2 = _outproj(x2, mixers, w_out_b, l, final_norm_g, final=(l == depth - 1), tm=tm_out)
    return x2.reshape(batch, s_len, d_model)
```

```python
import functools
import math

import jax
import jax.numpy as jnp
from jax import lax
from jax.experimental import pallas as pl
from jax.experimental.pallas import tpu as pltpu

F32 = jnp.float32
BF16 = jnp.bfloat16

LANES = 128
HEAD_DIM = 128
N_HEADS = 4
GROUP_W = N_HEADS * HEAD_DIM
DIFF_QK_DIM = HEAD_DIM // 2
MOBA_BLOCK = 256
MOBA_TOPK = 3
GLA_DK = HEAD_DIM // 2
GLA_RANK = 16
GLA_TAU = 16.0
GLA_CHUNK = 64
ROPE_THETA = 500000.0
ROPE_FRACTION = 4
RMS_EPS = 1e-6
MASKED_LOGIT = -1e30

_SEGMENTS = (
    ("fox_q", GROUP_W), ("fox_k", GROUP_W), ("fox_v", GROUP_W),
    ("fox_f", N_HEADS), ("fox_g", GROUP_W),
    ("diff_q", GROUP_W), ("diff_k", GROUP_W), ("diff_v", GROUP_W), ("diff_g", GROUP_W),
    ("moba_q", GROUP_W), ("moba_k", GROUP_W), ("moba_v", GROUP_W), ("moba_g", GROUP_W),
    ("gla_q", N_HEADS * GLA_DK), ("gla_k", N_HEADS * GLA_DK), ("gla_v", GROUP_W),
    ("gla_a", GLA_RANK), ("gla_g", GROUP_W),
)
_SEG_OFF = {}
_off = 0
for _name, _w in _SEGMENTS:
    _SEG_OFF[_name] = (_off, _w)
    _off += _w

_SLABS = ("fox_q", "fox_k", "fox_v", "fox_g", "diff_q", "diff_k", "diff_v", "diff_g",
          "moba_q", "moba_k", "moba_v", "moba_g", "gla_qk", "gla_v", "gla_g")
_SLAB = {name: i for i, name in enumerate(_SLABS)}
N_MAIN = len(_SLABS) * GROUP_W
_MAIN_RUNS = ((0, _SEG_OFF["fox_f"][0]),
              (_SEG_OFF["fox_g"][0], _SEG_OFF["gla_a"][0]),
              (_SEG_OFF["gla_g"][0], _off))
assert sum(b - a for a, b in _MAIN_RUNS) == N_MAIN

Q_TILE = 256
K_TILE = 256
CUM_BLOCK = 128
GATE_ROWS = 16
GLA_GROUP = 4


def _silu(x):
    return x * (1.0 / (1.0 + jnp.exp(-x)))


def _log_sigmoid(x):
    return jnp.minimum(x, 0.0) - jnp.log1p(jnp.exp(-jnp.abs(x)))


def _nt_dot(a, b):
    return lax.dot_general(a, b, (((1,), (1,)), ((), ())), preferred_element_type=F32)


def _inproj_kernel(x_ref, g_ref, w_ref, ws_ref, o_ref, small_ref, h_scr):
    @pl.when(pl.program_id(1) == 0)
    def _():
        x = x_ref[...]
        y = x * lax.rsqrt(jnp.mean(x * x, axis=-1, keepdims=True) + RMS_EPS) * g_ref[...]
        h = y.astype(BF16)
        h_scr[...] = h
        small_ref[...] = _nt_dot(h, ws_ref[...])

    acc = _nt_dot(h_scr[...], w_ref[...])
    for c in range(o_ref.shape[0]):
        o_ref[c] = acc[:, c * LANES:(c + 1) * LANES].astype(BF16)


def _inproj(x2, norm_g, w_main, w_small, layer, *, tm, tn):
    t, d = x2.shape
    n_main = w_main.shape[1]
    return pl.pallas_call(
        _inproj_kernel,
        grid=(t // tm, n_main // tn),
        in_specs=[
            pl.BlockSpec((tm, d), lambda i, j: (i, 0)),
            pl.BlockSpec((1, d), lambda i, j: (0, 0)),
            pl.BlockSpec((None, tn, d), lambda i, j: (layer, j, 0)),
            pl.BlockSpec((None, LANES, d), lambda i, j: (layer, 0, 0)),
        ],
        out_specs=[
            pl.BlockSpec((tn // LANES, tm, LANES), lambda i, j: (j, i, 0)),
            pl.BlockSpec((tm, LANES), lambda i, j: (i, 0)),
        ],
        out_shape=[
            jax.ShapeDtypeStruct((n_main // LANES, t, LANES), BF16),
            jax.ShapeDtypeStruct((t, LANES), F32),
        ],
        scratch_shapes=[pltpu.VMEM((tm, d), BF16)],
        compiler_params=pltpu.CompilerParams(
            dimension_semantics=("arbitrary", "arbitrary"),
            vmem_limit_bytes=56 * 1024 * 1024),
        name="inproj",
    )(x2, norm_g.reshape(1, d), w_main, w_small)


def _gates_kernel(small_ref, fb_ref, wa_ref, ba_ref, fk_ref, fq_ref, p_ref):
    s_len = small_ref.shape[0]
    small = small_ref[...]
    log_f = _log_sigmoid(small + fb_ref[...])
    a_logit = jnp.dot(small.astype(BF16), wa_ref[...], preferred_element_type=F32) + ba_ref[...]
    log_a = _log_sigmoid(a_logit) * (1.0 / GLA_TAU)
    both = jnp.concatenate([log_f, log_a], axis=-1)

    row = lax.broadcasted_iota(jnp.int32, (CUM_BLOCK, CUM_BLOCK), 0)
    col = lax.broadcasted_iota(jnp.int32, (CUM_BLOCK, CUM_BLOCK), 1)
    tri = (col <= row).astype(F32)
    carry = jnp.zeros((1, both.shape[1]), F32)
    for c in range(s_len // CUM_BLOCK):
        blk = both[c * CUM_BLOCK:(c + 1) * CUM_BLOCK]
        cum = jnp.dot(tri, blk, precision=lax.Precision.HIGHEST,
                      preferred_element_type=F32) + carry
        carry = cum[CUM_BLOCK - 1:CUM_BLOCK]
        rows = slice(c * CUM_BLOCK, (c + 1) * CUM_BLOCK)
        f_cum = cum[:, :LANES]
        fq_ref[:, rows] = f_cum.T[:GATE_ROWS]
        for h in range(N_HEADS):
            fk_ref[h, rows, :] = jnp.broadcast_to(f_cum[:, h:h + 1], (CUM_BLOCK, LANES))
            p_ref[h, rows, :] = cum[:, (h + 1) * LANES:(h + 2) * LANES]


def _gates(small, fb_pad, wa_pad, ba_dup, *, batch, s_len):
    t = small.shape[0]
    return pl.pallas_call(
        _gates_kernel,
        grid=(batch,),
        in_specs=[
            pl.BlockSpec((s_len, LANES), lambda b: (b, 0)),
            pl.BlockSpec((1, LANES), lambda b: (0, 0)),
            pl.BlockSpec((LANES, GROUP_W), lambda b: (0, 0)),
            pl.BlockSpec((1, GROUP_W), lambda b: (0, 0)),
        ],
        out_specs=[
            pl.BlockSpec((N_HEADS, s_len, LANES), lambda b: (0, b, 0)),
            pl.BlockSpec((None, GATE_ROWS, s_len), lambda b: (b, 0, 0)),
            pl.BlockSpec((N_HEADS, s_len, LANES), lambda b: (0, b, 0)),
        ],
        out_shape=[
            jax.ShapeDtypeStruct((N_HEADS, t, LANES), F32),
            jax.ShapeDtypeStruct((batch, GATE_ROWS, s_len), F32),
            jax.ShapeDtypeStruct((N_HEADS, t, LANES), F32),
        ],
        compiler_params=pltpu.CompilerParams(dimension_semantics=("arbitrary",)),
        name="gates",
    )(small, fb_pad, wa_pad, ba_dup)


def _causal_mask_t(tk, width, tq):
    key = lax.broadcasted_iota(jnp.int32, (tk, width), 0)
    qry = lax.broadcasted_iota(jnp.int32, (tk, width), 1)
    if width != tq:
        qry = qry & (tq - 1)
    return key <= qry


def _transpose_values(v_ref, vt_scr):
    for h in range(v_ref.shape[0]):
        for j in range(v_ref.shape[1] // K_TILE):
            blk = v_ref[h, j * K_TILE:(j + 1) * K_TILE, :].astype(F32)
            vt_scr[h, j] = blk.T.astype(BF16)


def _col_reduce(z, op):
    rows = z.shape[0]
    while rows > 8:
        rows //= 2
        z = op(z[:rows], z[rows:])
    if op is jnp.add:
        return jnp.sum(z, axis=0, keepdims=True)
    return jnp.max(z, axis=0, keepdims=True)


def _online_softmax_t(last, logits_fn, vt_fn, tq, z_scr, acc_scr):
    n, _, width = acc_scr.shape

    def update(j, slot, ms, ls, tile_max, masked):
        stats = []
        for s in range(n):
            z = z_scr[slot, s]
            if masked:
                z = jnp.where(_causal_mask_t(K_TILE, width, tq), z, -jnp.inf)
                m_new = jnp.maximum(ms[s], _col_reduce(z, jnp.maximum))
            else:
                m_new = jnp.maximum(ms[s], tile_max[s])
            alpha = jnp.exp(ms[s] - m_new)
            p = jnp.exp(z - m_new)
            stats.append((m_new, alpha * ls[s] + _col_reduce(p, jnp.add), alpha, p.astype(BF16)))
        for s in range(n):
            pv = jnp.dot(vt_fn(s, j), stats[s][3], preferred_element_type=F32)
            acc_scr[s] = stats[s][2] * acc_scr[s] + pv
        return tuple(st[0] for st in stats), tuple(st[1] for st in stats)

    first = [logits_fn(s, jnp.int32(0)) for s in range(n)]
    for s in range(n):
        z_scr[0, s] = first[s]
        acc_scr[s] = jnp.zeros(acc_scr.shape[1:], F32)

    def body(j, state):
        ms, ls, tile_max = state
        slot = lax.rem(j, 2)
        z_next = [logits_fn(s, j + 1) for s in range(n)]
        ms, ls = update(j, slot, ms, ls, tile_max, False)
        for s in range(n):
            z_scr[1 - slot, s] = z_next[s]
        return ms, ls, tuple(_col_reduce(z, jnp.maximum) for z in z_next)

    init = (tuple(jnp.full((1, width), -jnp.inf, F32) for _ in range(n)),
            tuple(jnp.zeros((1, width), F32) for _ in range(n)),
            tuple(_col_reduce(z, jnp.maximum) for z in first))
    ms, ls, _ = lax.fori_loop(0, last, body, init)
    _, ls = update(last, lax.rem(last, 2), ms, ls, None, True)
    return ls


def _rope(x, cos_t, sin_up, sin_dn, half):
    return (x * cos_t + pltpu.roll(x, LANES - half, 1) * sin_up
            + pltpu.roll(x, half, 1) * sin_dn)


def _key_rows(j):
    return pl.ds(pl.multiple_of(j * K_TILE, K_TILE), K_TILE)


def _mixer_specs(s_len, nq):
    qspec = lambda slab: pl.BlockSpec((N_HEADS, Q_TILE, LANES), lambda b, i: (slab, b * nq + i, 0))
    kspec = lambda slab: pl.BlockSpec((N_HEADS, s_len, LANES), lambda b, i: (slab, b, 0))
    ospec = pl.BlockSpec((N_HEADS, Q_TILE, LANES), lambda b, i: (0, b * nq + i, 0))
    return qspec, kspec, ospec


def _fox_kernel(q_ref, k_ref, v_ref, g_ref, fk_ref, fq_ref, o_ref, vt_scr, z_scr, acc_scr):
    i = pl.program_id(1)
    tq = q_ref.shape[1]
    scale = HEAD_DIM ** -0.5

    @pl.when(i == 0)
    def _():
        _transpose_values(v_ref, vt_scr)

    fq_all = fq_ref[...]
    qs = [q_ref[h] for h in range(N_HEADS)]
    fqs = [fq_all[h:h + 1, :] for h in range(N_HEADS)]

    def logits(h, j):
        fk = fk_ref[h, _key_rows(j), :]
        fk = jnp.concatenate([fk] * (tq // LANES), axis=-1)
        return _nt_dot(k_ref[h, _key_rows(j), :], qs[h]) * scale + (fqs[h] - fk)

    ls = _online_softmax_t(i, logits, lambda h, j: vt_scr[h, j], tq, z_scr, acc_scr)
    for h in range(N_HEADS):
        o = (acc_scr[h] / ls[h]).T
        o_ref[h] = (o * _silu(g_ref[h].astype(F32))).astype(BF16)


def _fox(proj, fk_rep, fq_rows, *, batch, s_len):
    t = proj.shape[1]
    nq = s_len // Q_TILE
    nk = s_len // K_TILE
    qspec, kspec, ospec = _mixer_specs(s_len, nq)
    return pl.pallas_call(
        _fox_kernel,
        grid=(batch, nq),
        in_specs=[
            qspec(_SLAB["fox_q"]), kspec(_SLAB["fox_k"]), kspec(_SLAB["fox_v"]),
            qspec(_SLAB["fox_g"]),
            pl.BlockSpec((N_HEADS, s_len, LANES), lambda b, i: (0, b, 0)),
            pl.BlockSpec((None, GATE_ROWS, Q_TILE), lambda b, i: (b, 0, i)),
        ],
        out_specs=ospec,
        out_shape=jax.ShapeDtypeStruct((N_HEADS, t, LANES), BF16),
        scratch_shapes=[pltpu.VMEM((N_HEADS, nk, HEAD_DIM, K_TILE), BF16),
                        pltpu.VMEM((2, N_HEADS, K_TILE, Q_TILE), F32),
                        pltpu.VMEM((N_HEADS, HEAD_DIM, Q_TILE), F32)],
        compiler_params=pltpu.CompilerParams(dimension_semantics=("arbitrary", "arbitrary")),
        name="fox",
    )(proj, proj, proj, proj, fk_rep, fq_rows)


def _diff_kernel(lam_init, q_ref, k_ref, v_ref, g_ref, lam_ref, ng_ref,
                 cq_ref, uq_ref, dq_ref, ck_ref, uk_ref, dk_ref, o_ref,
                 kr_scr, vt_scr, z_scr, acc_scr):
    i = pl.program_id(1)
    tq = q_ref.shape[1]
    half = DIFF_QK_DIM // ROPE_FRACTION // 2
    scale = DIFF_QK_DIM ** -0.5

    @pl.when(i == 0)
    def _():
        _transpose_values(v_ref, vt_scr)
        for h in range(N_HEADS):
            kr_scr[h] = _rope(k_ref[h].astype(F32), ck_ref[...], uk_ref[...], dk_ref[...],
                              half).astype(BF16)

    lf = lam_ref[...]
    lam = (jnp.exp(jnp.sum(lf[0:1] * lf[1:2], axis=-1, keepdims=True))
           - jnp.exp(jnp.sum(lf[2:3] * lf[3:4], axis=-1, keepdims=True)) + lam_init)

    first = lax.broadcasted_iota(jnp.int32, (tq, LANES), 1) < DIFF_QK_DIM
    q12 = []
    for h in range(N_HEADS):
        q = _rope(q_ref[h].astype(F32), cq_ref[...], uq_ref[...], dq_ref[...], half)
        q12.append(jnp.concatenate([jnp.where(first, q, 0.0),
                                    jnp.where(first, 0.0, q)], axis=0).astype(BF16))

    def logits(h, j):
        return _nt_dot(kr_scr[h, _key_rows(j), :], q12[h]) * scale

    ls = _online_softmax_t(i, logits, lambda h, j: vt_scr[h, j], tq, z_scr, acc_scr)
    for h in range(N_HEADS):
        a = acc_scr[h] / ls[h]
        o = (a[:, :tq] - lam * a[:, tq:]).T
        o = o * lax.rsqrt(jnp.mean(o * o, axis=-1, keepdims=True) + RMS_EPS) * ng_ref[...]
        o = o * (1.0 - lam_init)
        o_ref[h] = (o * _silu(g_ref[h].astype(F32))).astype(BF16)


def _diff(proj, diff_lam, diff_norm_g, tables, lam_init, *, batch, s_len):
    t = proj.shape[1]
    nq = s_len // Q_TILE
    nk = s_len // K_TILE
    qspec, kspec, ospec = _mixer_specs(s_len, nq)
    tq_spec = pl.BlockSpec((Q_TILE, LANES), lambda b, i: (i, 0))
    tk_spec = pl.BlockSpec((s_len, LANES), lambda b, i: (0, 0))
    return pl.pallas_call(
        functools.partial(_diff_kernel, lam_init),
        grid=(batch, nq),
        in_specs=[
            qspec(_SLAB["diff_q"]), kspec(_SLAB["diff_k"]), kspec(_SLAB["diff_v"]),
            qspec(_SLAB["diff_g"]),
            pl.BlockSpec((4, DIFF_QK_DIM), lambda b, i: (0, 0)),
            pl.BlockSpec((1, HEAD_DIM), lambda b, i: (0, 0)),
            tq_spec, tq_spec, tq_spec, tk_spec, tk_spec, tk_spec,
        ],
        out_specs=ospec,
        out_shape=jax.ShapeDtypeStruct((N_HEADS, t, LANES), BF16),
        scratch_shapes=[pltpu.VMEM((N_HEADS, s_len, LANES), BF16),
                        pltpu.VMEM((N_HEADS, nk, HEAD_DIM, K_TILE), BF16),
                        pltpu.VMEM((2, N_HEADS, K_TILE, 2 * Q_TILE), F32),
                        pltpu.VMEM((N_HEADS, HEAD_DIM, 2 * Q_TILE), F32)],
        compiler_params=pltpu.CompilerParams(dimension_semantics=("arbitrary", "arbitrary")),
        name="diff",
    )(proj, proj, proj, proj, diff_lam, diff_norm_g.reshape(1, HEAD_DIM), *tables, *tables)


def _moba_kernel(q_ref, k_ref, v_ref, g_ref, cq_ref, uq_ref, dq_ref, ck_ref, uk_ref, dk_ref,
                 o_ref, kr_scr, kmean_scr, vt_scr, z_scr, acc_scr):
    i = pl.program_id(1)
    tq = q_ref.shape[1]
    half = HEAD_DIM // ROPE_FRACTION // 2
    scale = HEAD_DIM ** -0.5
    n_blk = k_ref.shape[1] // MOBA_BLOCK

    @pl.when(i == 0)
    def _():
        _transpose_values(v_ref, vt_scr)
        kmean_scr[...] = jnp.zeros(kmean_scr.shape, F32)
        for h in range(N_HEADS):
            kr = _rope(k_ref[h].astype(F32), ck_ref[...], uk_ref[...], dk_ref[...], half)
            kr_scr[h] = kr.astype(BF16)
            for n in range(n_blk):
                blk = kr[n * MOBA_BLOCK:(n + 1) * MOBA_BLOCK]
                kmean_scr[h, n:n + 1, :] = jnp.sum(blk, axis=0, keepdims=True) * (1.0 / MOBA_BLOCK)

    blk_id = lax.broadcasted_iota(jnp.int32, (GATE_ROWS, tq), 0).astype(F32)
    past = blk_id < i.astype(F32)
    qs, biases = [], []
    for h in range(N_HEADS):
        q = _rope(q_ref[h].astype(F32), cq_ref[...], uq_ref[...], dq_ref[...], half).astype(BF16)
        gate = _nt_dot(kmean_scr[h].astype(BF16), q)
        gate = jnp.where(past, gate, -jnp.inf)
        open_ = past
        sel = jnp.zeros((GATE_ROWS, tq), jnp.bool_)
        for _ in range(MOBA_TOPK):
            top = jnp.max(gate, axis=0, keepdims=True)
            idx = jnp.min(jnp.where(gate == top, blk_id, float(GATE_ROWS)), axis=0, keepdims=True)
            pick = (blk_id == idx) & open_
            sel = sel | pick
            open_ = open_ & jnp.logical_not(pick)
            gate = jnp.where(pick, -jnp.inf, gate)
        qs.append(q)
        biases.append(jnp.where(sel, 0.0, MASKED_LOGIT))

    def logits(h, j):
        bias = jnp.sum(jnp.where(blk_id == j.astype(F32), biases[h], 0.0), axis=0, keepdims=True)
        bias = jnp.where(j == i, 0.0, bias)
        return _nt_dot(kr_scr[h, _key_rows(j), :], qs[h]) * scale + bias

    ls = _online_softmax_t(i, logits, lambda h, j: vt_scr[h, j], tq, z_scr, acc_scr)
    for h in range(N_HEADS):
        o = (acc_scr[h] / ls[h]).T
        o_ref[h] = (o * _silu(g_ref[h].astype(F32))).astype(BF16)


def _moba(proj, tables, *, batch, s_len):
    t = proj.shape[1]
    nq = s_len // Q_TILE
    nk = s_len // K_TILE
    qspec, kspec, ospec = _mixer_specs(s_len, nq)
    tq_spec = pl.BlockSpec((Q_TILE, LANES), lambda b, i: (i, 0))
    tk_spec = pl.BlockSpec((s_len, LANES), lambda b, i: (0, 0))
    return pl.pallas_call(
        _moba_kernel,
        grid=(batch, nq),
        in_specs=[
            qspec(_SLAB["moba_q"]), kspec(_SLAB["moba_k"]), kspec(_SLAB["moba_v"]),
            qspec(_SLAB["moba_g"]),
            tq_spec, tq_spec, tq_spec, tk_spec, tk_spec, tk_spec,
        ],
        out_specs=ospec,
        out_shape=jax.ShapeDtypeStruct((N_HEADS, t, LANES), BF16),
        scratch_shapes=[pltpu.VMEM((N_HEADS, s_len, LANES), BF16),
                        pltpu.VMEM((N_HEADS, GATE_ROWS, LANES), F32),
                        pltpu.VMEM((N_HEADS, nk, HEAD_DIM, K_TILE), BF16),
                        pltpu.VMEM((2, N_HEADS, K_TILE, Q_TILE), F32),
                        pltpu.VMEM((N_HEADS, HEAD_DIM, Q_TILE), F32)],
        compiler_params=pltpu.CompilerParams(dimension_semantics=("arbitrary", "arbitrary")),
        name="moba",
    )(proj, proj, proj, proj, *tables, *tables)


def _gla_kernel(x_ref, v_ref, g_ref, p_ref, ng_ref, o_ref):
    s_len = x_ref.shape[1]
    c_len = GLA_CHUNK
    scale = GLA_DK ** -0.5
    lane = lax.broadcasted_iota(jnp.int32, (c_len, LANES), 1)
    qhalf = lane < GLA_DK
    qhalf_row = lax.broadcasted_iota(jnp.int32, (1, LANES), 1) < GLA_DK
    row = lax.broadcasted_iota(jnp.int32, (c_len, c_len), 0)
    col = lax.broadcasted_iota(jnp.int32, (c_len, c_len), 1)
    causal = col <= row

    group = GLA_GROUP
    pairs = [(h, c) for h in range(N_HEADS) for c in range(group)]

    def body(g, carries):
        prep = {}
        for h, c in pairs:
            rows = pl.ds(pl.multiple_of(g * (group * c_len), group * c_len) + c * c_len, c_len)
            qcb = x_ref[h // 2, rows, :].astype(F32)
            kcb = x_ref[2 + h // 2, rows, :].astype(F32)
            if h % 2 == 0:
                x = jnp.where(qhalf, qcb, pltpu.roll(kcb, GLA_DK, 1))
            else:
                x = jnp.where(qhalf, pltpu.roll(qcb, GLA_DK, 1), kcb)
            p = p_ref[h, rows, :]
            base = carries[h][1] if c == 0 else prep[h, c - 1]["p_last"]
            cum = p - base
            xt = x * jnp.exp(jnp.where(qhalf, cum, -cum))
            kr = pltpu.roll(xt, GLA_DK, 1)
            decay = jnp.where(qhalf_row, jnp.exp(cum[c_len - 1:c_len]), 0.0)
            prep[h, c] = dict(
                rows=rows, qz=jnp.where(qhalf, xt, 0.0).astype(BF16), kr=kr.astype(BF16),
                khat=(jnp.where(qhalf, kr, 0.0) * decay).astype(BF16),
                decay=decay, v=v_ref[h, rows, :], p_last=p[c_len - 1:c_len])
        scores, upd = {}, {}
        for h, c in pairs:
            d = prep[h, c]
            scores[h, c] = _nt_dot(d["qz"], d["kr"])
            upd[h, c] = lax.dot_general(d["v"], d["khat"], (((0,), (0,)), ((), ())),
                                        preferred_element_type=F32)
        states = {}
        for h in range(N_HEADS):
            states[h, 0] = carries[h][0]
            for c in range(group):
                states[h, c + 1] = states[h, c] * prep[h, c]["decay"] + upd[h, c]
        outs = {}
        for h, c in pairs:
            sc = jnp.where(causal, scores[h, c] * scale, 0.0).astype(BF16)
            outs[h, c] = (jnp.dot(sc, prep[h, c]["v"], preferred_element_type=F32),
                          _nt_dot(prep[h, c]["qz"], states[h, c].astype(BF16)))
        for h, c in pairs:
            o = outs[h, c][0] + outs[h, c][1] * scale
            y = o * lax.rsqrt(jnp.mean(o * o, axis=-1, keepdims=True) + RMS_EPS) * ng_ref[...]
            rows = prep[h, c]["rows"]
            o_ref[h, rows, :] = (y * _silu(g_ref[h, rows, :].astype(F32))).astype(BF16)
        return tuple((states[h, group], prep[h, group - 1]["p_last"]) for h in range(N_HEADS))

    init = tuple((jnp.zeros((HEAD_DIM, LANES), F32), jnp.zeros((1, LANES), F32))
                 for _ in range(N_HEADS))
    lax.fori_loop(0, s_len // (group * c_len), body, init)


def _gla(proj, p_cum, gla_norm_g, *, batch, s_len):
    t = proj.shape[1]
    spec = lambda slab: pl.BlockSpec((N_HEADS, s_len, LANES), lambda b: (slab, b, 0))
    return pl.pallas_call(
        _gla_kernel,
        grid=(batch,),
        in_specs=[
            spec(_SLAB["gla_qk"]), spec(_SLAB["gla_v"]), spec(_SLAB["gla_g"]),
            pl.BlockSpec((N_HEADS, s_len, LANES), lambda b: (0, b, 0)),
            pl.BlockSpec((1, HEAD_DIM), lambda b: (0, 0)),
        ],
        out_specs=pl.BlockSpec((N_HEADS, s_len, LANES), lambda b: (0, b, 0)),
        out_shape=jax.ShapeDtypeStruct((N_HEADS, t, LANES), BF16),
        compiler_params=pltpu.CompilerParams(dimension_semantics=("arbitrary",)),
        name="gla",
    )(proj, proj, proj, p_cum, gla_norm_g.reshape(1, HEAD_DIM))


def _outproj_kernel(final, x_ref, a_ref, b_ref, c_ref, d_ref, w_ref, fg_ref, o_ref):
    parts = [r[h] for r in (a_ref, b_ref, c_ref, d_ref) for h in range(N_HEADS)]
    mixed = jnp.concatenate(parts, axis=-1)
    y = x_ref[...] + jnp.dot(mixed, w_ref[...], preferred_element_type=F32)
    if final:
        y = y * lax.rsqrt(jnp.mean(y * y, axis=-1, keepdims=True) + RMS_EPS) * fg_ref[...]
    o_ref[...] = y


def _outproj(x2, mixers, w_out, layer, final_g, *, final, tm):
    t, d = x2.shape
    mspec = pl.BlockSpec((N_HEADS, tm, LANES), lambda i: (0, i, 0))
    return pl.pallas_call(
        functools.partial(_outproj_kernel, final),
        grid=(t // tm,),
        in_specs=[
            pl.BlockSpec((tm, d), lambda i: (i, 0)),
            mspec, mspec, mspec, mspec,
            pl.BlockSpec((None,) + w_out.shape[1:], lambda i: (layer, 0, 0)),
            pl.BlockSpec((1, d), lambda i: (0, 0)),
        ],
        out_specs=pl.BlockSpec((tm, d), lambda i: (i, 0)),
        out_shape=jax.ShapeDtypeStruct((t, d), F32),
        compiler_params=pltpu.CompilerParams(
            dimension_semantics=("arbitrary",), vmem_limit_bytes=56 * 1024 * 1024),
        name="outproj",
    )(x2, *mixers, w_out, final_g.reshape(1, d))


RELAYOUT_ROWS = 256
RELAYOUT_TAIL = 32


def _relayout_kernel(a_ref, b_ref, o_ref):
    r = pl.program_id(0)
    depth, rows, _ = o_ref.shape
    for l in range(depth):
        x = jnp.concatenate([a_ref[:, l, :], b_ref[:, l, :]], axis=0)
        out_lo = 0
        for (src_lo, src_hi) in _MAIN_RUNS:
            shift = src_lo - out_lo
            out_hi = out_lo + (src_hi - src_lo)
            assert shift <= RELAYOUT_TAIL and out_lo % rows == 0 and out_hi % rows == 0

            @pl.when((r >= out_lo // rows) & (r < out_hi // rows))
            def _(shift=shift, l=l, x=x):
                o_ref[l] = x[shift:shift + rows].astype(BF16)

            out_lo = out_hi


def _relayout_w_in(w_in):
    depth, d, d_in = w_in.shape
    wt = jnp.transpose(w_in, (2, 0, 1))
    rows, tail = RELAYOUT_ROWS, RELAYOUT_TAIL
    main = pl.pallas_call(
        _relayout_kernel,
        grid=(N_MAIN // rows,),
        in_specs=[pl.BlockSpec((rows, depth, d), lambda r: (r, 0, 0)),
                  pl.BlockSpec((tail, depth, d), lambda r: ((r + 1) * (rows // tail), 0, 0))],
        out_specs=pl.BlockSpec((depth, rows, d), lambda r: (0, r, 0)),
        out_shape=jax.ShapeDtypeStruct((depth, N_MAIN, d), BF16),
        compiler_params=pltpu.CompilerParams(dimension_semantics=("arbitrary",)),
        name="relayout",
    )(wt, wt)
    f_off, a_off = _SEG_OFF["fox_f"][0], _SEG_OFF["gla_a"][0]
    gate_cols = jnp.concatenate([w_in[..., f_off:f_off + N_HEADS],
                                 w_in[..., a_off:a_off + GLA_RANK]], axis=-1)
    small = jnp.pad(jnp.transpose(gate_cols, (0, 2, 1)),
                    ((0, 0), (0, LANES - N_HEADS - GLA_RANK), (0, 0))).astype(BF16)
    return main, small


def _dup_halves(a):
    parts = []
    for h in range(N_HEADS):
        blk = a[..., h * GLA_DK:(h + 1) * GLA_DK]
        parts += [blk, blk]
    return jnp.concatenate(parts, axis=-1)


def _rope_tables(s_len, comp_dim, rot_dim):
    half = rot_dim // 2
    inv_freq = ROPE_THETA ** (-jnp.arange(0, rot_dim, 2, dtype=F32) / rot_dim)
    ang = jnp.arange(s_len, dtype=F32)[:, None] * inv_freq[None, :]
    cos, sin = jnp.cos(ang), jnp.sin(ang)
    zeros = jnp.zeros((s_len, comp_dim - rot_dim), F32)
    ones = jnp.ones((s_len, comp_dim - rot_dim), F32)
    reps = LANES // comp_dim
    cos_t = jnp.tile(jnp.concatenate([cos, cos, ones], axis=-1), (1, reps))
    sin_up = jnp.tile(jnp.concatenate([-sin, jnp.zeros_like(sin), zeros], axis=-1), (1, reps))
    sin_dn = jnp.tile(jnp.concatenate([jnp.zeros_like(sin), sin, zeros], axis=-1), (1, reps))
    return cos_t, sin_up, sin_dn


def kernel(x, norm_g, w_in, fox_fb, diff_lam, diff_norm_g, gla_wa2, gla_ba, gla_norm_g, w_out,
           final_norm_g):
    batch, s_len, d_model = x.shape
    depth = w_in.shape[0]
    assert s_len % Q_TILE == 0 and Q_TILE == K_TILE == MOBA_BLOCK
    assert s_len // MOBA_BLOCK <= GATE_ROWS
    t = batch * s_len
    tm_in = min(1024, t)
    tn_in = 1536
    tm_out = min(512, t)

    w_main, w_small = _relayout_w_in(w_in)
    w_out_b = w_out.astype(BF16)
    fb_pad = jnp.pad(fox_fb, ((0, 0), (0, LANES - N_HEADS)))[:, None, :]
    wa_pad = jnp.pad(_dup_halves(gla_wa2), ((0, 0), (N_HEADS, LANES - N_HEADS - GLA_RANK), (0, 0))
                     ).astype(BF16)
    ba_dup = _dup_halves(gla_ba)[:, None, :]
    rope_diff = _rope_tables(s_len, DIFF_QK_DIM, DIFF_QK_DIM // ROPE_FRACTION)
    rope_moba = _rope_tables(s_len, HEAD_DIM, HEAD_DIM // ROPE_FRACTION)

    x2 = x.reshape(t, d_model)
    for l in range(depth):
        proj, small = _inproj(x2, norm_g[l], w_main, w_small, l, tm=tm_in, tn=tn_in)
        fk_rep, fq_rows, p_cum = _gates(small, fb_pad[l], wa_pad[l], ba_dup[l],
                                        batch=batch, s_len=s_len)
        lam_init = 0.8 - 0.6 * math.exp(-0.3 * l)
        mixers = (
            _fox(proj, fk_rep, fq_rows, batch=batch, s_len=s_len),
            _diff(proj, diff_lam[l], diff_norm_g[l], rope_diff, lam_init, batch=batch, s_len=s_len),
            _moba(proj, rope_moba, batch=batch, s_len=s_len),
            _gla(proj, p_cum, gla_norm_g[l], batch=batch, s_len=s_len),
        )
        x2 = _outproj(x2, mixers, w_out_b, l, final_norm_g, final=(l == depth - 1), tm=tm_out)
    return x2.reshape(batch, s_len, d_model)
```

```python
import functools
import math

import jax
import jax.numpy as jnp
from jax import lax
from jax.experimental import pallas as pl
from jax.experimental.pallas import tpu as pltpu

F32 = jnp.float32
BF16 = jnp.bfloat16

LANES = 128
HEAD_DIM = 128
N_HEADS = 4
GROUP_W = N_HEADS * HEAD_DIM
DIFF_QK_DIM = HEAD_DIM // 2
MOBA_BLOCK = 256
MOBA_TOPK = 3
GLA_DK = HEAD_DIM // 2
GLA_RANK = 16
GLA_TAU = 16.0
GLA_CHUNK = 64
ROPE_THETA = 500000.0
ROPE_FRACTION = 4
RMS_EPS = 1e-6
MASKED_LOGIT = -1e30

_SEGMENTS = (
    ("fox_q", GROUP_W), ("fox_k", GROUP_W), ("fox_v", GROUP_W),
    ("fox_f", N_HEADS), ("fox_g", GROUP_W),
    ("diff_q", GROUP_W), ("diff_k", GROUP_W), ("diff_v", GROUP_W), ("diff_g", GROUP_W),
    ("moba_q", GROUP_W), ("moba_k", GROUP_W), ("moba_v", GROUP_W), ("moba_g", GROUP_W),
    ("gla_q", N_HEADS * GLA_DK), ("gla_k", N_HEADS * GLA_DK), ("gla_v", GROUP_W),
    ("gla_a", GLA_RANK), ("gla_g", GROUP_W),
)
_SEG_OFF = {}
_off = 0
for _name, _w in _SEGMENTS:
    _SEG_OFF[_name] = (_off, _w)
    _off += _w

_SLABS = ("fox_q", "fox_k", "fox_v", "fox_g", "diff_q", "diff_k", "diff_v", "diff_g",
          "moba_q", "moba_k", "moba_v", "moba_g", "gla_qk", "gla_v", "gla_g")
_SLAB = {name: i for i, name in enumerate(_SLABS)}
N_MAIN = len(_SLABS) * GROUP_W
_MAIN_RUNS = ((0, _SEG_OFF["fox_f"][0]),
              (_SEG_OFF["fox_g"][0], _SEG_OFF["gla_a"][0]),
              (_SEG_OFF["gla_g"][0], _off))
assert sum(b - a for a, b in _MAIN_RUNS) == N_MAIN

Q_TILE = 256
K_TILE = 256
CUM_BLOCK = 128
GATE_ROWS = 16
GLA_GROUP = 4
ONES_ROWS = 16
LOG2E = math.log2(math.e)


def _silu(x):
    return x * (1.0 / (1.0 + jnp.exp(-x)))


def _log_sigmoid(x):
    return jnp.minimum(x, 0.0) - jnp.log1p(jnp.exp(-jnp.abs(x)))


def _nt_dot(a, b):
    return lax.dot_general(a, b, (((1,), (1,)), ((), ())), preferred_element_type=F32)


def _inproj_kernel(x_ref, g_ref, w_ref, ws_ref, o_ref, small_ref, h_scr):
    @pl.when(pl.program_id(1) == 0)
    def _():
        x = x_ref[...]
        y = x * lax.rsqrt(jnp.mean(x * x, axis=-1, keepdims=True) + RMS_EPS) * g_ref[...]
        h = y.astype(BF16)
        h_scr[...] = h
        small_ref[...] = _nt_dot(h, ws_ref[...])

    acc = _nt_dot(h_scr[...], w_ref[...])
    for c in range(o_ref.shape[0]):
        o_ref[c] = acc[:, c * LANES:(c + 1) * LANES].astype(BF16)


def _inproj(x2, norm_g, w_main, w_small, layer, *, tm, tn):
    t, d = x2.shape
    n_main = w_main.shape[1]
    return pl.pallas_call(
        _inproj_kernel,
        grid=(t // tm, n_main // tn),
        in_specs=[
            pl.BlockSpec((tm, d), lambda i, j: (i, 0)),
            pl.BlockSpec((1, d), lambda i, j: (0, 0)),
            pl.BlockSpec((None, tn, d), lambda i, j: (layer, j, 0)),
            pl.BlockSpec((None, LANES, d), lambda i, j: (layer, 0, 0)),
        ],
        out_specs=[
            pl.BlockSpec((tn // LANES, tm, LANES), lambda i, j: (j, i, 0)),
            pl.BlockSpec((tm, LANES), lambda i, j: (i, 0)),
        ],
        out_shape=[
            jax.ShapeDtypeStruct((n_main // LANES, t, LANES), BF16),
            jax.ShapeDtypeStruct((t, LANES), F32),
        ],
        scratch_shapes=[pltpu.VMEM((tm, d), BF16)],
        compiler_params=pltpu.CompilerParams(
            dimension_semantics=("arbitrary", "arbitrary"),
            vmem_limit_bytes=56 * 1024 * 1024),
        name="inproj",
    )(x2, norm_g.reshape(1, d), w_main, w_small)


def _gates_kernel(small_ref, fb_ref, wa_ref, ba_ref, fk_ref, fq_ref, p_ref):
    s_len = small_ref.shape[0]
    small = small_ref[...]
    log_f = _log_sigmoid(small + fb_ref[...])
    a_logit = jnp.dot(small.astype(BF16), wa_ref[...], preferred_element_type=F32) + ba_ref[...]
    log_a = _log_sigmoid(a_logit) * (1.0 / GLA_TAU)
    both = jnp.concatenate([log_f, log_a], axis=-1)

    row = lax.broadcasted_iota(jnp.int32, (CUM_BLOCK, CUM_BLOCK), 0)
    col = lax.broadcasted_iota(jnp.int32, (CUM_BLOCK, CUM_BLOCK), 1)
    tri = (col <= row).astype(F32)
    carry = jnp.zeros((1, both.shape[1]), F32)
    for c in range(s_len // CUM_BLOCK):
        blk = both[c * CUM_BLOCK:(c + 1) * CUM_BLOCK]
        cum = jnp.dot(tri, blk, precision=lax.Precision.HIGHEST,
                      preferred_element_type=F32) + carry
        carry = cum[CUM_BLOCK - 1:CUM_BLOCK]
        rows = slice(c * CUM_BLOCK, (c + 1) * CUM_BLOCK)
        f_cum = cum[:, :LANES] * LOG2E
        fq_ref[:, rows] = f_cum.T[:GATE_ROWS]
        for h in range(N_HEADS):
            fk_ref[h, rows, :] = jnp.broadcast_to(f_cum[:, h:h + 1], (CUM_BLOCK, LANES))
            p_ref[h, rows, :] = cum[:, (h + 1) * LANES:(h + 2) * LANES]


def _gates(small, fb_pad, wa_pad, ba_dup, *, batch, s_len):
    t = small.shape[0]
    return pl.pallas_call(
        _gates_kernel,
        grid=(batch,),
        in_specs=[
            pl.BlockSpec((s_len, LANES), lambda b: (b, 0)),
            pl.BlockSpec((1, LANES), lambda b: (0, 0)),
            pl.BlockSpec((LANES, GROUP_W), lambda b: (0, 0)),
            pl.BlockSpec((1, GROUP_W), lambda b: (0, 0)),
        ],
        out_specs=[
            pl.BlockSpec((N_HEADS, s_len, LANES), lambda b: (0, b, 0)),
            pl.BlockSpec((None, GATE_ROWS, s_len), lambda b: (b, 0, 0)),
            pl.BlockSpec((N_HEADS, s_len, LANES), lambda b: (0, b, 0)),
        ],
        out_shape=[
            jax.ShapeDtypeStruct((N_HEADS, t, LANES), F32),
            jax.ShapeDtypeStruct((batch, GATE_ROWS, s_len), F32),
            jax.ShapeDtypeStruct((N_HEADS, t, LANES), F32),
        ],
        compiler_params=pltpu.CompilerParams(dimension_semantics=("arbitrary",)),
        name="gates",
    )(small, fb_pad, wa_pad, ba_dup)


def _causal_mask_t(tk, width, tq):
    key = lax.broadcasted_iota(jnp.int32, (tk, width), 0)
    qry = lax.broadcasted_iota(jnp.int32, (tk, width), 1)
    if width != tq:
        qry = qry & (tq - 1)
    return key <= qry


def _transpose_values(v_ref, vt_scr):
    dv = v_ref.shape[2]
    for h in range(v_ref.shape[0]):
        for j in range(v_ref.shape[1] // K_TILE):
            blk = v_ref[h, j * K_TILE:(j + 1) * K_TILE, :].astype(F32)
            vt_scr[h, j, :dv, :] = blk.T.astype(BF16)
            vt_scr[h, j, dv:, :] = jnp.ones((ONES_ROWS, K_TILE), BF16)


def _col_max(z):
    rows = z.shape[0]
    while rows > 8:
        rows //= 2
        z = jnp.maximum(z[:rows], z[rows:])
    return jnp.max(z, axis=0, keepdims=True)


def _online_softmax_t(last, logits_fn, vt_fn, tq, z_scr, acc_scr):
    n, _, width = acc_scr.shape

    def update(j, slot, ms, tile_max, masked):
        stats = []
        for s in range(n):
            z = z_scr[slot, s]
            if masked:
                z = jnp.where(_causal_mask_t(K_TILE, width, tq), z, -jnp.inf)
                m_new = jnp.maximum(ms[s], _col_max(z))
            else:
                m_new = jnp.maximum(ms[s], tile_max[s])
            stats.append((m_new, jnp.exp2(ms[s] - m_new), jnp.exp2(z - m_new).astype(BF16)))
        for s in range(n):
            pv = jnp.dot(vt_fn(s, j), stats[s][2], preferred_element_type=F32)
            acc_scr[s] = stats[s][1] * acc_scr[s] + pv
        return tuple(st[0] for st in stats)

    first = [logits_fn(s, jnp.int32(0)) for s in range(n)]
    for s in range(n):
        z_scr[0, s] = first[s]
        acc_scr[s] = jnp.zeros(acc_scr.shape[1:], F32)

    def body(j, state):
        ms, tile_max = state
        slot = lax.rem(j, 2)
        z_next = [logits_fn(s, j + 1) for s in range(n)]
        ms = update(j, slot, ms, tile_max, False)
        for s in range(n):
            z_scr[1 - slot, s] = z_next[s]
        return ms, tuple(_col_max(z) for z in z_next)

    init = (tuple(jnp.full((1, width), -jnp.inf, F32) for _ in range(n)),
            tuple(_col_max(z) for z in first))
    ms, _ = lax.fori_loop(0, last, body, init)
    update(last, lax.rem(last, 2), ms, None, True)


def _softmax_result_t(acc):
    dv = acc.shape[0] - ONES_ROWS
    return acc[:dv] / acc[dv:dv + 1]


def _rope(x, cos_t, sin_up, sin_dn, half):
    return (x * cos_t + pltpu.roll(x, LANES - half, 1) * sin_up
            + pltpu.roll(x, half, 1) * sin_dn)


def _key_rows(j):
    return pl.ds(pl.multiple_of(j * K_TILE, K_TILE), K_TILE)


def _mixer_specs(s_len, nq):
    qspec = lambda slab: pl.BlockSpec((N_HEADS, Q_TILE, LANES), lambda b, i: (slab, b * nq + i, 0))
    kspec = lambda slab: pl.BlockSpec((N_HEADS, s_len, LANES), lambda b, i: (slab, b, 0))
    ospec = pl.BlockSpec((N_HEADS, Q_TILE, LANES), lambda b, i: (0, b * nq + i, 0))
    return qspec, kspec, ospec


def _fox_kernel(q_ref, k_ref, v_ref, g_ref, fk_ref, fq_ref, o_ref, vt_scr, z_scr, acc_scr):
    i = pl.program_id(1)
    tq = q_ref.shape[1]
    scale = HEAD_DIM ** -0.5 * LOG2E

    @pl.when(i == 0)
    def _():
        _transpose_values(v_ref, vt_scr)

    fq_all = fq_ref[...]
    qs = [q_ref[h] for h in range(N_HEADS)]
    fqs = [fq_all[h:h + 1, :] for h in range(N_HEADS)]

    def logits(h, j):
        fk = fk_ref[h, _key_rows(j), :]
        fk = jnp.concatenate([fk] * (tq // LANES), axis=-1)
        return _nt_dot(k_ref[h, _key_rows(j), :], qs[h]) * scale + (fqs[h] - fk)

    _online_softmax_t(i, logits, lambda h, j: vt_scr[h, j], tq, z_scr, acc_scr)
    for h in range(N_HEADS):
        o = _softmax_result_t(acc_scr[h]).T
        o_ref[h] = (o * _silu(g_ref[h].astype(F32))).astype(BF16)


def _fox(proj, fk_rep, fq_rows, *, batch, s_len):
    t = proj.shape[1]
    nq = s_len // Q_TILE
    nk = s_len // K_TILE
    qspec, kspec, ospec = _mixer_specs(s_len, nq)
    return pl.pallas_call(
        _fox_kernel,
        grid=(batch, nq),
        in_specs=[
            qspec(_SLAB["fox_q"]), kspec(_SLAB["fox_k"]), kspec(_SLAB["fox_v"]),
            qspec(_SLAB["fox_g"]),
            pl.BlockSpec((N_HEADS, s_len, LANES), lambda b, i: (0, b, 0)),
            pl.BlockSpec((None, GATE_ROWS, Q_TILE), lambda b, i: (b, 0, i)),
        ],
        out_specs=ospec,
        out_shape=jax.ShapeDtypeStruct((N_HEADS, t, LANES), BF16),
        scratch_shapes=[pltpu.VMEM((N_HEADS, nk, HEAD_DIM + ONES_ROWS, K_TILE), BF16),
                        pltpu.VMEM((2, N_HEADS, K_TILE, Q_TILE), F32),
                        pltpu.VMEM((N_HEADS, HEAD_DIM + ONES_ROWS, Q_TILE), F32)],
        compiler_params=pltpu.CompilerParams(dimension_semantics=("arbitrary", "arbitrary")),
        name="fox",
    )(proj, proj, proj, proj, fk_rep, fq_rows)


def _diff_kernel(lam_init, q_ref, k_ref, v_ref, g_ref, lam_ref, ng_ref,
                 cq_ref, uq_ref, dq_ref, ck_ref, uk_ref, dk_ref, o_ref,
                 kr_scr, vt_scr, z_scr, acc_scr):
    i = pl.program_id(1)
    tq = q_ref.shape[1]
    half = DIFF_QK_DIM // ROPE_FRACTION // 2
    scale = DIFF_QK_DIM ** -0.5 * LOG2E

    @pl.when(i == 0)
    def _():
        _transpose_values(v_ref, vt_scr)
        for h in range(N_HEADS):
            kr_scr[h] = _rope(k_ref[h].astype(F32), ck_ref[...], uk_ref[...], dk_ref[...],
                              half).astype(BF16)

    lf = lam_ref[...]
    lam = (jnp.exp(jnp.sum(lf[0:1] * lf[1:2], axis=-1, keepdims=True))
           - jnp.exp(jnp.sum(lf[2:3] * lf[3:4], axis=-1, keepdims=True)) + lam_init)

    first = lax.broadcasted_iota(jnp.int32, (tq, LANES), 1) < DIFF_QK_DIM
    q12 = []
    for h in range(N_HEADS):
        q = _rope(q_ref[h].astype(F32), cq_ref[...], uq_ref[...], dq_ref[...], half)
        q12.append(jnp.concatenate([jnp.where(first, q, 0.0),
                                    jnp.where(first, 0.0, q)], axis=0).astype(BF16))

    def logits(h, j):
        return _nt_dot(kr_scr[h, _key_rows(j), :], q12[h]) * scale

    _online_softmax_t(i, logits, lambda h, j: vt_scr[h, j], tq, z_scr, acc_scr)
    for h in range(N_HEADS):
        a = _softmax_result_t(acc_scr[h])
        o = (a[:, :tq] - lam * a[:, tq:]).T
        o = o * lax.rsqrt(jnp.mean(o * o, axis=-1, keepdims=True) + RMS_EPS) * ng_ref[...]
        o = o * (1.0 - lam_init)
        o_ref[h] = (o * _silu(g_ref[h].astype(F32))).astype(BF16)


def _diff(proj, diff_lam, diff_norm_g, tables, lam_init, *, batch, s_len):
    t = proj.shape[1]
    nq = s_len // Q_TILE
    nk = s_len // K_TILE
    qspec, kspec, ospec = _mixer_specs(s_len, nq)
    tq_spec = pl.BlockSpec((Q_TILE, LANES), lambda b, i: (i, 0))
    tk_spec = pl.BlockSpec((s_len, LANES), lambda b, i: (0, 0))
    return pl.pallas_call(
        functools.partial(_diff_kernel, lam_init),
        grid=(batch, nq),
        in_specs=[
            qspec(_SLAB["diff_q"]), kspec(_SLAB["diff_k"]), kspec(_SLAB["diff_v"]),
            qspec(_SLAB["diff_g"]),
            pl.BlockSpec((4, DIFF_QK_DIM), lambda b, i: (0, 0)),
            pl.BlockSpec((1, HEAD_DIM), lambda b, i: (0, 0)),
            tq_spec, tq_spec, tq_spec, tk_spec, tk_spec, tk_spec,
        ],
        out_specs=ospec,
        out_shape=jax.ShapeDtypeStruct((N_HEADS, t, LANES), BF16),
        scratch_shapes=[pltpu.VMEM((N_HEADS, s_len, LANES), BF16),
                        pltpu.VMEM((N_HEADS, nk, HEAD_DIM + ONES_ROWS, K_TILE), BF16),
                        pltpu.VMEM((2, N_HEADS, K_TILE, 2 * Q_TILE), F32),
                        pltpu.VMEM((N_HEADS, HEAD_DIM + ONES_ROWS, 2 * Q_TILE), F32)],
        compiler_params=pltpu.CompilerParams(dimension_semantics=("arbitrary", "arbitrary")),
        name="diff",
    )(proj, proj, proj, proj, diff_lam, diff_norm_g.reshape(1, HEAD_DIM), *tables, *tables)


def _moba_kernel(q_ref, k_ref, v_ref, g_ref, cq_ref, uq_ref, dq_ref, ck_ref, uk_ref, dk_ref,
                 o_ref, kr_scr, kmean_scr, vt_scr, z_scr, acc_scr):
    i = pl.program_id(1)
    tq = q_ref.shape[1]
    half = HEAD_DIM // ROPE_FRACTION // 2
    scale = HEAD_DIM ** -0.5 * LOG2E
    n_blk = k_ref.shape[1] // MOBA_BLOCK

    @pl.when(i == 0)
    def _():
        _transpose_values(v_ref, vt_scr)
        kmean_scr[...] = jnp.zeros(kmean_scr.shape, F32)
        for h in range(N_HEADS):
            kr = _rope(k_ref[h].astype(F32), ck_ref[...], uk_ref[...], dk_ref[...], half)
            kr_scr[h] = kr.astype(BF16)
            for n in range(n_blk):
                blk = kr[n * MOBA_BLOCK:(n + 1) * MOBA_BLOCK]
                kmean_scr[h, n:n + 1, :] = jnp.sum(blk, axis=0, keepdims=True) * (1.0 / MOBA_BLOCK)

    blk_id = lax.broadcasted_iota(jnp.int32, (GATE_ROWS, tq), 0).astype(F32)
    past = blk_id < i.astype(F32)
    qs, biases = [], []
    for h in range(N_HEADS):
        q = _rope(q_ref[h].astype(F32), cq_ref[...], uq_ref[...], dq_ref[...], half).astype(BF16)
        gate = _nt_dot(kmean_scr[h].astype(BF16), q)
        gate = jnp.where(past, gate, -jnp.inf)
        open_ = past
        sel = jnp.zeros((GATE_ROWS, tq), jnp.bool_)
        for _ in range(MOBA_TOPK):
            top = jnp.max(gate, axis=0, keepdims=True)
            idx = jnp.min(jnp.where(gate == top, blk_id, float(GATE_ROWS)), axis=0, keepdims=True)
            pick = (blk_id == idx) & open_
            sel = sel | pick
            open_ = open_ & jnp.logical_not(pick)
            gate = jnp.where(pick, -jnp.inf, gate)
        qs.append(q)
        biases.append(jnp.where(sel, 0.0, MASKED_LOGIT))

    def logits(h, j):
        bias = jnp.sum(jnp.where(blk_id == j.astype(F32), biases[h], 0.0), axis=0, keepdims=True)
        bias = jnp.where(j == i, 0.0, bias)
        return _nt_dot(kr_scr[h, _key_rows(j), :], qs[h]) * scale + bias

    _online_softmax_t(i, logits, lambda h, j: vt_scr[h, j], tq, z_scr, acc_scr)
    for h in range(N_HEADS):
        o = _softmax_result_t(acc_scr[h]).T
        o_ref[h] = (o * _silu(g_ref[h].astype(F32))).astype(BF16)


def _moba(proj, tables, *, batch, s_len):
    t = proj.shape[1]
    nq = s_len // Q_TILE
    nk = s_len // K_TILE
    qspec, kspec, ospec = _mixer_specs(s_len, nq)
    tq_spec = pl.BlockSpec((Q_TILE, LANES), lambda b, i: (i, 0))
    tk_spec = pl.BlockSpec((s_len, LANES), lambda b, i: (0, 0))
    return pl.pallas_call(
        _moba_kernel,
        grid=(batch, nq),
        in_specs=[
            qspec(_SLAB["moba_q"]), kspec(_SLAB["moba_k"]), kspec(_SLAB["moba_v"]),
            qspec(_SLAB["moba_g"]),
            tq_spec, tq_spec, tq_spec, tk_spec, tk_spec, tk_spec,
        ],
        out_specs=ospec,
        out_shape=jax.ShapeDtypeStruct((N_HEADS, t, LANES), BF16),
        scratch_shapes=[pltpu.VMEM((N_HEADS, s_len, LANES), BF16),
                        pltpu.VMEM((N_HEADS, GATE_ROWS, LANES), F32),
                        pltpu.VMEM((N_HEADS, nk, HEAD_DIM + ONES_ROWS, K_TILE), BF16),
                        pltpu.VMEM((2, N_HEADS, K_TILE, Q_TILE), F32),
                        pltpu.VMEM((N_HEADS, HEAD_DIM + ONES_ROWS, Q_TILE), F32)],
        compiler_params=pltpu.CompilerParams(dimension_semantics=("arbitrary", "arbitrary")),
        name="moba",
    )(proj, proj, proj, proj, *tables, *tables)


def _gla_kernel(x_ref, v_ref, g_ref, p_ref, ng_ref, o_ref):
    s_len = x_ref.shape[1]
    c_len = GLA_CHUNK
    scale = GLA_DK ** -0.5
    lane = lax.broadcasted_iota(jnp.int32, (c_len, LANES), 1)
    qhalf = lane < GLA_DK
    qhalf_row = lax.broadcasted_iota(jnp.int32, (1, LANES), 1) < GLA_DK
    row = lax.broadcasted_iota(jnp.int32, (c_len, c_len), 0)
    col = lax.broadcasted_iota(jnp.int32, (c_len, c_len), 1)
    causal = col <= row

    group = GLA_GROUP
    pairs = [(h, c) for h in range(N_HEADS) for c in range(group)]

    def body(g, carries):
        prep = {}
        for h, c in pairs:
            rows = pl.ds(pl.multiple_of(g * (group * c_len), group * c_len) + c * c_len, c_len)
            qcb = x_ref[h // 2, rows, :].astype(F32)
            kcb = x_ref[2 + h // 2, rows, :].astype(F32)
            if h % 2 == 0:
                x = jnp.where(qhalf, qcb, pltpu.roll(kcb, GLA_DK, 1))
            else:
                x = jnp.where(qhalf, pltpu.roll(qcb, GLA_DK, 1), kcb)
            p = p_ref[h, rows, :]
            base = carries[h][1] if c == 0 else prep[h, c - 1]["p_last"]
            cum = p - base
            xt = x * jnp.exp(jnp.where(qhalf, cum, -cum))
            kr = pltpu.roll(xt, GLA_DK, 1)
            decay = jnp.where(qhalf_row, jnp.exp(cum[c_len - 1:c_len]), 0.0)
            prep[h, c] = dict(
                rows=rows, qz=jnp.where(qhalf, xt, 0.0).astype(BF16), kr=kr.astype(BF16),
                khat=(jnp.where(qhalf, kr, 0.0) * decay).astype(BF16),
                decay=decay, v=v_ref[h, rows, :], p_last=p[c_len - 1:c_len])
        scores, upd = {}, {}
        for h, c in pairs:
            d = prep[h, c]
            scores[h, c] = _nt_dot(d["qz"], d["kr"])
            upd[h, c] = lax.dot_general(d["v"], d["khat"], (((0,), (0,)), ((), ())),
                                        preferred_element_type=F32)
        states = {}
        for h in range(N_HEADS):
            states[h, 0] = carries[h][0]
            for c in range(group):
                states[h, c + 1] = states[h, c] * prep[h, c]["decay"] + upd[h, c]
        outs = {}
        for h, c in pairs:
            sc = jnp.where(causal, scores[h, c] * scale, 0.0).astype(BF16)
            outs[h, c] = (jnp.dot(sc, prep[h, c]["v"], preferred_element_type=F32),
                          _nt_dot(prep[h, c]["qz"], states[h, c].astype(BF16)))
        for h, c in pairs:
            o = outs[h, c][0] + outs[h, c][1] * scale
            y = o * lax.rsqrt(jnp.mean(o * o, axis=-1, keepdims=True) + RMS_EPS) * ng_ref[...]
            rows = prep[h, c]["rows"]
            o_ref[h, rows, :] = (y * _silu(g_ref[h, rows, :].astype(F32))).astype(BF16)
        return tuple((states[h, group], prep[h, group - 1]["p_last"]) for h in range(N_HEADS))

    init = tuple((jnp.zeros((HEAD_DIM, LANES), F32), jnp.zeros((1, LANES), F32))
                 for _ in range(N_HEADS))
    lax.fori_loop(0, s_len // (group * c_len), body, init)


def _gla(proj, p_cum, gla_norm_g, *, batch, s_len):
    t = proj.shape[1]
    spec = lambda slab: pl.BlockSpec((N_HEADS, s_len, LANES), lambda b: (slab, b, 0))
    return pl.pallas_call(
        _gla_kernel,
        grid=(batch,),
        in_specs=[
            spec(_SLAB["gla_qk"]), spec(_SLAB["gla_v"]), spec(_SLAB["gla_g"]),
            pl.BlockSpec((N_HEADS, s_len, LANES), lambda b: (0, b, 0)),
            pl.BlockSpec((1, HEAD_DIM), lambda b: (0, 0)),
        ],
        out_specs=pl.BlockSpec((N_HEADS, s_len, LANES), lambda b: (0, b, 0)),
        out_shape=jax.ShapeDtypeStruct((N_HEADS, t, LANES), BF16),
        compiler_params=pltpu.CompilerParams(dimension_semantics=("arbitrary",)),
        name="gla",
    )(proj, proj, proj, p_cum, gla_norm_g.reshape(1, HEAD_DIM))


def _outproj_kernel(final, x_ref, a_ref, b_ref, c_ref, d_ref, w_ref, fg_ref, o_ref):
    parts = [r[h] for r in (a_ref, b_ref, c_ref, d_ref) for h in range(N_HEADS)]
    mixed = jnp.concatenate(parts, axis=-1)
    y = x_ref[...] + jnp.dot(mixed, w_ref[...], preferred_element_type=F32)
    if final:
        y = y * lax.rsqrt(jnp.mean(y * y, axis=-1, keepdims=True) + RMS_EPS) * fg_ref[...]
    o_ref[...] = y


def _outproj(x2, mixers, w_out, layer, final_g, *, final, tm):
    t, d = x2.shape
    mspec = pl.BlockSpec((N_HEADS, tm, LANES), lambda i: (0, i, 0))
    return pl.pallas_call(
        functools.partial(_outproj_kernel, final),
        grid=(t // tm,),
        in_specs=[
            pl.BlockSpec((tm, d), lambda i: (i, 0)),
            mspec, mspec, mspec, mspec,
            pl.BlockSpec((None,) + w_out.shape[1:], lambda i: (layer, 0, 0)),
            pl.BlockSpec((1, d), lambda i: (0, 0)),
        ],
        out_specs=pl.BlockSpec((tm, d), lambda i: (i, 0)),
        out_shape=jax.ShapeDtypeStruct((t, d), F32),
        compiler_params=pltpu.CompilerParams(
            dimension_semantics=("arbitrary",), vmem_limit_bytes=56 * 1024 * 1024),
        name="outproj",
    )(x2, *mixers, w_out, final_g.reshape(1, d))


RELAYOUT_ROWS = 256
RELAYOUT_TAIL = 32


def _relayout_kernel(a_ref, b_ref, o_ref):
    r = pl.program_id(0)
    depth, rows, _ = o_ref.shape
    for l in range(depth):
        x = jnp.concatenate([a_ref[:, l, :], b_ref[:, l, :]], axis=0)
        out_lo = 0
        for (src_lo, src_hi) in _MAIN_RUNS:
            shift = src_lo - out_lo
            out_hi = out_lo + (src_hi - src_lo)
            assert shift <= RELAYOUT_TAIL and out_lo % rows == 0 and out_hi % rows == 0

            @pl.when((r >= out_lo // rows) & (r < out_hi // rows))
            def _(shift=shift, l=l, x=x):
                o_ref[l] = x[shift:shift + rows].astype(BF16)

            out_lo = out_hi


def _relayout_w_in(w_in):
    depth, d, d_in = w_in.shape
    wt = jnp.transpose(w_in, (2, 0, 1))
    rows, tail = RELAYOUT_ROWS, RELAYOUT_TAIL
    main = pl.pallas_call(
        _relayout_kernel,
        grid=(N_MAIN // rows,),
        in_specs=[pl.BlockSpec((rows, depth, d), lambda r: (r, 0, 0)),
                  pl.BlockSpec((tail, depth, d), lambda r: ((r + 1) * (rows // tail), 0, 0))],
        out_specs=pl.BlockSpec((depth, rows, d), lambda r: (0, r, 0)),
        out_shape=jax.ShapeDtypeStruct((depth, N_MAIN, d), BF16),
        compiler_params=pltpu.CompilerParams(dimension_semantics=("arbitrary",)),
        name="relayout",
    )(wt, wt)
    f_off, a_off = _SEG_OFF["fox_f"][0], _SEG_OFF["gla_a"][0]
    gate_cols = jnp.concatenate([w_in[..., f_off:f_off + N_HEADS],
                                 w_in[..., a_off:a_off + GLA_RANK]], axis=-1)
    small = jnp.pad(jnp.transpose(gate_cols, (0, 2, 1)),
                    ((0, 0), (0, LANES - N_HEADS - GLA_RANK), (0, 0))).astype(BF16)
    return main, small


def _dup_halves(a):
    parts = []
    for h in range(N_HEADS):
        blk = a[..., h * GLA_DK:(h + 1) * GLA_DK]
        parts += [blk, blk]
    return jnp.concatenate(parts, axis=-1)


def _rope_tables(s_len, comp_dim, rot_dim):
    half = rot_dim // 2
    inv_freq = ROPE_THETA ** (-jnp.arange(0, rot_dim, 2, dtype=F32) / rot_dim)
    ang = jnp.arange(s_len, dtype=F32)[:, None] * inv_freq[None, :]
    cos, sin = jnp.cos(ang), jnp.sin(ang)
    zeros = jnp.zeros((s_len, comp_dim - rot_dim), F32)
    ones = jnp.ones((s_len, comp_dim - rot_dim), F32)
    reps = LANES // comp_dim
    cos_t = jnp.tile(jnp.concatenate([cos, cos, ones], axis=-1), (1, reps))
    sin_up = jnp.tile(jnp.concatenate([-sin, jnp.zeros_like(sin), zeros], axis=-1), (1, reps))
    sin_dn = jnp.tile(jnp.concatenate([jnp.zeros_like(sin), sin, zeros], axis=-1), (1, reps))
    return cos_t, sin_up, sin_dn


def kernel(x, norm_g, w_in, fox_fb, diff_lam, diff_norm_g, gla_wa2, gla_ba, gla_norm_g, w_out,
           final_norm_g):
    batch, s_len, d_model = x.shape
    depth = w_in.shape[0]
    assert s_len % Q_TILE == 0 and Q_TILE == K_TILE == MOBA_BLOCK
    assert s_len // MOBA_BLOCK <= GATE_ROWS
    t = batch * s_len
    tm_in = min(1024, t)
    tn_in = 1536
    tm_out = min(512, t)

    w_main, w_small = _relayout_w_in(w_in)
    w_out_b = w_out.astype(BF16)
    fb_pad = jnp.pad(fox_fb, ((0, 0), (0, LANES - N_HEADS)))[:, None, :]
    wa_pad = jnp.pad(_dup_halves(gla_wa2), ((0, 0), (N_HEADS, LANES - N_HEADS - GLA_RANK), (0, 0))
                     ).astype(BF16)
    ba_dup = _dup_halves(gla_ba)[:, None, :]
    rope_diff = _rope_tables(s_len, DIFF_QK_DIM, DIFF_QK_DIM // ROPE_FRACTION)
    rope_moba = _rope_tables(s_len, HEAD_DIM, HEAD_DIM // ROPE_FRACTION)

    x2 = x.reshape(t, d_model)
    for l in range(depth):
        proj, small = _inproj(x2, norm_g[l], w_main, w_small, l, tm=tm_in, tn=tn_in)
        fk_rep, fq_rows, p_cum = _gates(small, fb_pad[l], wa_pad[l], ba_dup[l],
                                        batch=batch, s_len=s_len)
        lam_init = 0.8 - 0.6 * math.exp(-0.3 * l)
        mixers = (
            _fox(proj, fk_rep, fq_rows, batch=batch, s_len=s_len),
            _diff(proj, diff_lam[l], diff_norm_g[l], rope_diff, lam_init, batch=batch, s_len=s_len),
            _moba(proj, rope_moba, batch=batch, s_len=s_len),
            _gla(proj, p_cum, gla_norm_g[l], batch=batch, s_len=s_len),
        )
        x2 = _outproj(x2, mixers, w_out_b, l, final_norm_g, final=(l == depth - 1), tm=tm_out)
    return x2.reshape(batch, s_len, d_model)
```

```python
import functools
import math

import jax
import jax.numpy as jnp
from jax import lax
from jax.experimental import pallas as pl
from jax.experimental.pallas import tpu as pltpu

F32 = jnp.float32
BF16 = jnp.bfloat16

LANES = 128
HEAD_DIM = 128
N_HEADS = 4
GROUP_W = N_HEADS * HEAD_DIM
DIFF_QK_DIM = HEAD_DIM // 2
MOBA_BLOCK = 256
MOBA_TOPK = 3
GLA_DK = HEAD_DIM // 2
GLA_RANK = 16
GLA_TAU = 16.0
GLA_CHUNK = 64
ROPE_THETA = 500000.0
ROPE_FRACTION = 4
RMS_EPS = 1e-6
MASKED_LOGIT = -1e30

_SEGMENTS = (
    ("fox_q", GROUP_W), ("fox_k", GROUP_W), ("fox_v", GROUP_W),
    ("fox_f", N_HEADS), ("fox_g", GROUP_W),
    ("diff_q", GROUP_W), ("diff_k", GROUP_W), ("diff_v", GROUP_W), ("diff_g", GROUP_W),
    ("moba_q", GROUP_W), ("moba_k", GROUP_W), ("moba_v", GROUP_W), ("moba_g", GROUP_W),
    ("gla_q", N_HEADS * GLA_DK), ("gla_k", N_HEADS * GLA_DK), ("gla_v", GROUP_W),
    ("gla_a", GLA_RANK), ("gla_g", GROUP_W),
)
_SEG_OFF = {}
_off = 0
for _name, _w in _SEGMENTS:
    _SEG_OFF[_name] = (_off, _w)
    _off += _w

_SLABS = ("fox_q", "fox_k", "fox_v", "fox_g", "diff_q", "diff_k", "diff_v", "diff_g",
          "moba_q", "moba_k", "moba_v", "moba_g", "gla_qk", "gla_v", "gla_g")
_SLAB = {name: i for i, name in enumerate(_SLABS)}
N_MAIN = len(_SLABS) * GROUP_W
_MAIN_RUNS = ((0, _SEG_OFF["fox_f"][0]),
              (_SEG_OFF["fox_g"][0], _SEG_OFF["gla_a"][0]),
              (_SEG_OFF["gla_g"][0], _off))
assert sum(b - a for a, b in _MAIN_RUNS) == N_MAIN

Q_TILE = 256
K_TILE = 256
CUM_BLOCK = 128
GATE_ROWS = 16
GLA_GROUP = 4
ONES_ROWS = 16
LOG2E = math.log2(math.e)


def _silu(x):
    return x * (1.0 / (1.0 + jnp.exp(-x)))


def _log_sigmoid(x):
    return jnp.minimum(x, 0.0) - jnp.log1p(jnp.exp(-jnp.abs(x)))


def _nt_dot(a, b):
    return lax.dot_general(a, b, (((1,), (1,)), ((), ())), preferred_element_type=F32)


def _inproj_kernel(x_ref, g_ref, w_ref, ws_ref, o_ref, small_ref, h_scr):
    @pl.when(pl.program_id(1) == 0)
    def _():
        x = x_ref[...]
        y = x * lax.rsqrt(jnp.mean(x * x, axis=-1, keepdims=True) + RMS_EPS) * g_ref[...]
        h = y.astype(BF16)
        h_scr[...] = h
        small_ref[...] = _nt_dot(h, ws_ref[...])

    acc = _nt_dot(h_scr[...], w_ref[...])
    for c in range(o_ref.shape[0]):
        o_ref[c] = acc[:, c * LANES:(c + 1) * LANES].astype(BF16)


def _inproj(x2, norm_g, w_main, w_small, layer, *, tm, tn):
    t, d = x2.shape
    n_main = w_main.shape[1]
    return pl.pallas_call(
        _inproj_kernel,
        grid=(t // tm, n_main // tn),
        in_specs=[
            pl.BlockSpec((tm, d), lambda i, j: (i, 0)),
            pl.BlockSpec((1, d), lambda i, j: (0, 0)),
            pl.BlockSpec((None, tn, d), lambda i, j: (layer, j, 0)),
            pl.BlockSpec((None, LANES, d), lambda i, j: (layer, 0, 0)),
        ],
        out_specs=[
            pl.BlockSpec((tn // LANES, tm, LANES), lambda i, j: (j, i, 0)),
            pl.BlockSpec((tm, LANES), lambda i, j: (i, 0)),
        ],
        out_shape=[
            jax.ShapeDtypeStruct((n_main // LANES, t, LANES), BF16),
            jax.ShapeDtypeStruct((t, LANES), F32),
        ],
        scratch_shapes=[pltpu.VMEM((tm, d), BF16)],
        compiler_params=pltpu.CompilerParams(
            dimension_semantics=("arbitrary", "arbitrary"),
            vmem_limit_bytes=56 * 1024 * 1024),
        name="inproj",
    )(x2, norm_g.reshape(1, d), w_main, w_small)


def _gates_kernel(small_ref, fb_ref, wa_ref, ba_ref, fk_ref, fq_ref, p_ref):
    s_len = small_ref.shape[0]
    small = small_ref[...]
    log_f = _log_sigmoid(small + fb_ref[...])
    a_logit = jnp.dot(small.astype(BF16), wa_ref[...], preferred_element_type=F32) + ba_ref[...]
    log_a = _log_sigmoid(a_logit) * (1.0 / GLA_TAU)
    both = jnp.concatenate([log_f, log_a], axis=-1)

    row = lax.broadcasted_iota(jnp.int32, (CUM_BLOCK, CUM_BLOCK), 0)
    col = lax.broadcasted_iota(jnp.int32, (CUM_BLOCK, CUM_BLOCK), 1)
    tri = (col <= row).astype(F32)
    carry = jnp.zeros((1, both.shape[1]), F32)
    for c in range(s_len // CUM_BLOCK):
        blk = both[c * CUM_BLOCK:(c + 1) * CUM_BLOCK]
        cum = jnp.dot(tri, blk, precision=lax.Precision.HIGHEST,
                      preferred_element_type=F32) + carry
        carry = cum[CUM_BLOCK - 1:CUM_BLOCK]
        rows = slice(c * CUM_BLOCK, (c + 1) * CUM_BLOCK)
        f_cum = cum[:, :LANES] * LOG2E
        per_tile = Q_TILE // CUM_BLOCK
        fq_ref[c // per_tile, :, (c % per_tile) * CUM_BLOCK:(c % per_tile + 1) * CUM_BLOCK] = (
            f_cum.T[:GATE_ROWS])
        for h in range(N_HEADS):
            fk_ref[h, rows, :] = jnp.broadcast_to(f_cum[:, h:h + 1], (CUM_BLOCK, LANES))
            p_ref[h, rows, :] = cum[:, (h + 1) * LANES:(h + 2) * LANES]


def _gates(small, fb_pad, wa_pad, ba_dup, *, batch, s_len):
    t = small.shape[0]
    return pl.pallas_call(
        _gates_kernel,
        grid=(batch,),
        in_specs=[
            pl.BlockSpec((s_len, LANES), lambda b: (b, 0)),
            pl.BlockSpec((1, LANES), lambda b: (0, 0)),
            pl.BlockSpec((LANES, GROUP_W), lambda b: (0, 0)),
            pl.BlockSpec((1, GROUP_W), lambda b: (0, 0)),
        ],
        out_specs=[
            pl.BlockSpec((N_HEADS, s_len, LANES), lambda b: (0, b, 0)),
            pl.BlockSpec((None, s_len // Q_TILE, GATE_ROWS, Q_TILE), lambda b: (b, 0, 0, 0)),
            pl.BlockSpec((N_HEADS, s_len, LANES), lambda b: (0, b, 0)),
        ],
        out_shape=[
            jax.ShapeDtypeStruct((N_HEADS, t, LANES), F32),
            jax.ShapeDtypeStruct((batch, s_len // Q_TILE, GATE_ROWS, Q_TILE), F32),
            jax.ShapeDtypeStruct((N_HEADS, t, LANES), F32),
        ],
        compiler_params=pltpu.CompilerParams(dimension_semantics=("arbitrary",)),
        name="gates",
    )(small, fb_pad, wa_pad, ba_dup)


def _causal_mask_t(tk, width, tq):
    key = lax.broadcasted_iota(jnp.int32, (tk, width), 0)
    qry = lax.broadcasted_iota(jnp.int32, (tk, width), 1)
    if width != tq:
        qry = qry & (tq - 1)
    return key <= qry


def _transpose_values(v_ref, vt_scr):
    dv = v_ref.shape[2]
    for h in range(v_ref.shape[0]):
        for j in range(v_ref.shape[1] // K_TILE):
            blk = v_ref[h, j * K_TILE:(j + 1) * K_TILE, :].astype(F32)
            vt_scr[h, j, :dv, :] = blk.T.astype(BF16)
            vt_scr[h, j, dv:, :] = jnp.ones((ONES_ROWS, K_TILE), BF16)


def _col_max(z):
    rows = z.shape[0]
    while rows > 8:
        rows //= 2
        z = jnp.maximum(z[:rows], z[rows:])
    return jnp.max(z, axis=0, keepdims=True)


def _causal_tiles_t(n_tiles, prepare_fn, logits_fn, vt_fn, finish_fn, tq, z_scr, acc_scr):
    n, _, width = acc_scr.shape

    def update(j, slot, ms, tile_max, masked):
        stats = []
        for s in range(n):
            z = z_scr[slot, s]
            if masked:
                z = jnp.where(_causal_mask_t(K_TILE, width, tq), z, -jnp.inf)
                m_new = jnp.maximum(ms[s], _col_max(z))
            else:
                m_new = jnp.maximum(ms[s], tile_max[s])
            stats.append((m_new, jnp.exp2(ms[s] - m_new), jnp.exp2(z - m_new).astype(BF16)))
        for s in range(n):
            pv = jnp.dot(vt_fn(s, j), stats[s][2], preferred_element_type=F32)
            acc_scr[s] = stats[s][1] * acc_scr[s] + pv
        return tuple(st[0] for st in stats)

    def first_logits(i):
        ctx = prepare_fn(i)
        return ctx, [logits_fn(ctx, s, jnp.int32(0)) for s in range(n)]

    def open_tile(zs):
        for s in range(n):
            z_scr[0, s] = zs[s]
            acc_scr[s] = jnp.zeros(acc_scr.shape[1:], F32)
        return (tuple(jnp.full((1, width), -jnp.inf, F32) for _ in range(n)),
                tuple(_col_max(z) for z in zs))

    def close_tile(i, ms):
        update(i, lax.rem(i, 2), ms, None, True)
        finish_fn(i)

    def off_diagonal(i, ctx, state):
        def body(j, state):
            ms, tile_max = state
            slot = lax.rem(j, 2)
            z_next = [logits_fn(ctx, s, j + 1) for s in range(n)]
            ms = update(j, slot, ms, tile_max, False)
            for s in range(n):
                z_scr[1 - slot, s] = z_next[s]
            return ms, tuple(_col_max(z) for z in z_next)

        return lax.fori_loop(0, i, body, state)[0]

    _, zs = first_logits(jnp.int32(0))
    ms, _ = open_tile(zs)

    def outer(i, ms_prev):
        ctx, zs = first_logits(i)
        close_tile(i - 1, ms_prev)
        return off_diagonal(i, ctx, open_tile(zs))

    ms = lax.fori_loop(1, n_tiles, outer, ms)
    close_tile(jnp.int32(n_tiles - 1), ms)


def _softmax_result_t(acc):
    dv = acc.shape[0] - ONES_ROWS
    return acc[:dv] / acc[dv:dv + 1]


def _rope(x, cos_t, sin_up, sin_dn, half):
    return (x * cos_t + pltpu.roll(x, LANES - half, 1) * sin_up
            + pltpu.roll(x, half, 1) * sin_dn)


def _key_rows(j):
    return pl.ds(pl.multiple_of(j * K_TILE, K_TILE), K_TILE)


def _query_rows(i):
    return pl.ds(pl.multiple_of(i * Q_TILE, Q_TILE), Q_TILE)


def _batch_spec(s_len):
    return lambda slab: pl.BlockSpec((N_HEADS, s_len, LANES), lambda b: (slab, b, 0))


def _fox_kernel(q_ref, k_ref, v_ref, g_ref, fk_ref, fq_ref, o_ref, vt_scr, z_scr, acc_scr):
    scale = HEAD_DIM ** -0.5 * LOG2E
    _transpose_values(v_ref, vt_scr)

    def prepare(i):
        fq_all = fq_ref[i]
        return ([q_ref[h, _query_rows(i), :] for h in range(N_HEADS)],
                [fq_all[h:h + 1, :] for h in range(N_HEADS)])

    def logits(ctx, h, j):
        qs, fqs = ctx
        fk = fk_ref[h, _key_rows(j), :]
        fk = jnp.concatenate([fk] * (Q_TILE // LANES), axis=-1)
        return _nt_dot(k_ref[h, _key_rows(j), :], qs[h]) * scale + (fqs[h] - fk)

    def finish(i):
        for h in range(N_HEADS):
            o = _softmax_result_t(acc_scr[h]).T
            gate = _silu(g_ref[h, _query_rows(i), :].astype(F32))
            o_ref[h, _query_rows(i), :] = (o * gate).astype(BF16)

    _causal_tiles_t(q_ref.shape[1] // Q_TILE, prepare, logits, lambda h, j: vt_scr[h, j], finish,
                    Q_TILE, z_scr, acc_scr)


def _fox(proj, fk_rep, fq_rows, *, batch, s_len):
    t = proj.shape[1]
    nq = s_len // Q_TILE
    nk = s_len // K_TILE
    spec = _batch_spec(s_len)
    return pl.pallas_call(
        _fox_kernel,
        grid=(batch,),
        in_specs=[
            spec(_SLAB["fox_q"]), spec(_SLAB["fox_k"]), spec(_SLAB["fox_v"]), spec(_SLAB["fox_g"]),
            spec(0),
            pl.BlockSpec((None, nq, GATE_ROWS, Q_TILE), lambda b: (b, 0, 0, 0)),
        ],
        out_specs=spec(0),
        out_shape=jax.ShapeDtypeStruct((N_HEADS, t, LANES), BF16),
        scratch_shapes=[pltpu.VMEM((N_HEADS, nk, HEAD_DIM + ONES_ROWS, K_TILE), BF16),
                        pltpu.VMEM((2, N_HEADS, K_TILE, Q_TILE), F32),
                        pltpu.VMEM((N_HEADS, HEAD_DIM + ONES_ROWS, Q_TILE), F32)],
        compiler_params=pltpu.CompilerParams(dimension_semantics=("arbitrary",)),
        name="fox",
    )(proj, proj, proj, proj, fk_rep, fq_rows)


def _diff_kernel(lam_init, q_ref, k_ref, v_ref, g_ref, lam_ref, ng_ref, c_ref, u_ref, d_ref, o_ref,
                 kr_scr, vt_scr, z_scr, acc_scr):
    tq = Q_TILE
    half = DIFF_QK_DIM // ROPE_FRACTION // 2
    scale = DIFF_QK_DIM ** -0.5 * LOG2E

    _transpose_values(v_ref, vt_scr)
    for h in range(N_HEADS):
        kr_scr[h] = _rope(k_ref[h].astype(F32), c_ref[...], u_ref[...], d_ref[...], half).astype(BF16)

    lf = lam_ref[...]
    lam = (jnp.exp(jnp.sum(lf[0:1] * lf[1:2], axis=-1, keepdims=True))
           - jnp.exp(jnp.sum(lf[2:3] * lf[3:4], axis=-1, keepdims=True)) + lam_init)
    first = lax.broadcasted_iota(jnp.int32, (tq, LANES), 1) < DIFF_QK_DIM

    def prepare(i):
        rows = _query_rows(i)
        q12 = []
        for h in range(N_HEADS):
            q = _rope(q_ref[h, rows, :].astype(F32), c_ref[rows, :], u_ref[rows, :], d_ref[rows, :], half)
            q12.append(jnp.concatenate([jnp.where(first, q, 0.0),
                                        jnp.where(first, 0.0, q)], axis=0).astype(BF16))
        return q12

    def logits(q12, h, j):
        return _nt_dot(kr_scr[h, _key_rows(j), :], q12[h]) * scale

    def finish(i):
        for h in range(N_HEADS):
            a = _softmax_result_t(acc_scr[h])
            o = (a[:, :tq] - lam * a[:, tq:]).T
            o = o * lax.rsqrt(jnp.mean(o * o, axis=-1, keepdims=True) + RMS_EPS) * ng_ref[...]
            o = o * (1.0 - lam_init)
            gate = _silu(g_ref[h, _query_rows(i), :].astype(F32))
            o_ref[h, _query_rows(i), :] = (o * gate).astype(BF16)

    _causal_tiles_t(q_ref.shape[1] // tq, prepare, logits, lambda h, j: vt_scr[h, j], finish,
                    tq, z_scr, acc_scr)


def _diff(proj, diff_lam, diff_norm_g, tables, lam_init, *, batch, s_len):
    t = proj.shape[1]
    nk = s_len // K_TILE
    spec = _batch_spec(s_len)
    table_spec = pl.BlockSpec((s_len, LANES), lambda b: (0, 0))
    return pl.pallas_call(
        functools.partial(_diff_kernel, lam_init),
        grid=(batch,),
        in_specs=[
            spec(_SLAB["diff_q"]), spec(_SLAB["diff_k"]), spec(_SLAB["diff_v"]), spec(_SLAB["diff_g"]),
            pl.BlockSpec((4, DIFF_QK_DIM), lambda b: (0, 0)),
            pl.BlockSpec((1, HEAD_DIM), lambda b: (0, 0)),
            table_spec, table_spec, table_spec,
        ],
        out_specs=spec(0),
        out_shape=jax.ShapeDtypeStruct((N_HEADS, t, LANES), BF16),
        scratch_shapes=[pltpu.VMEM((N_HEADS, s_len, LANES), BF16),
                        pltpu.VMEM((N_HEADS, nk, HEAD_DIM + ONES_ROWS, K_TILE), BF16),
                        pltpu.VMEM((2, N_HEADS, K_TILE, 2 * Q_TILE), F32),
                        pltpu.VMEM((N_HEADS, HEAD_DIM + ONES_ROWS, 2 * Q_TILE), F32)],
        compiler_params=pltpu.CompilerParams(dimension_semantics=("arbitrary",)),
        name="diff",
    )(proj, proj, proj, proj, diff_lam, diff_norm_g.reshape(1, HEAD_DIM), *tables)


def _moba_kernel(q_ref, k_ref, v_ref, g_ref, c_ref, u_ref, d_ref,
                 o_ref, kr_scr, kmean_scr, vt_scr, z_scr, acc_scr):
    tq = Q_TILE
    half = HEAD_DIM // ROPE_FRACTION // 2
    scale = HEAD_DIM ** -0.5 * LOG2E
    n_blk = k_ref.shape[1] // MOBA_BLOCK

    _transpose_values(v_ref, vt_scr)
    kmean_scr[...] = jnp.zeros(kmean_scr.shape, F32)
    for h in range(N_HEADS):
        kr = _rope(k_ref[h].astype(F32), c_ref[...], u_ref[...], d_ref[...], half)
        kr_scr[h] = kr.astype(BF16)
        for n in range(n_blk):
            blk = kr[n * MOBA_BLOCK:(n + 1) * MOBA_BLOCK]
            kmean_scr[h, n:n + 1, :] = jnp.sum(blk, axis=0, keepdims=True) * (1.0 / MOBA_BLOCK)

    blk_id = lax.broadcasted_iota(jnp.int32, (GATE_ROWS, tq), 0).astype(F32)

    def prepare(i):
        rows = _query_rows(i)
        past = blk_id < i.astype(F32)
        qs, biases = [], []
        for h in range(N_HEADS):
            q = _rope(q_ref[h, rows, :].astype(F32), c_ref[rows, :], u_ref[rows, :], d_ref[rows, :],
                      half).astype(BF16)
            gate = _nt_dot(kmean_scr[h].astype(BF16), q)
            gate = jnp.where(past, gate, -jnp.inf)
            open_ = past
            sel = jnp.zeros((GATE_ROWS, tq), jnp.bool_)
            for _ in range(MOBA_TOPK):
                top = jnp.max(gate, axis=0, keepdims=True)
                idx = jnp.min(jnp.where(gate == top, blk_id, float(GATE_ROWS)), axis=0, keepdims=True)
                pick = (blk_id == idx) & open_
                sel = sel | pick
                open_ = open_ & jnp.logical_not(pick)
                gate = jnp.where(pick, -jnp.inf, gate)
            qs.append(q)
            biases.append(jnp.where(sel, 0.0, MASKED_LOGIT))
        return i, qs, biases

    def logits(ctx, h, j):
        i, qs, biases = ctx
        bias = jnp.sum(jnp.where(blk_id == j.astype(F32), biases[h], 0.0), axis=0, keepdims=True)
        bias = jnp.where(j == i, 0.0, bias)
        return _nt_dot(kr_scr[h, _key_rows(j), :], qs[h]) * scale + bias

    def finish(i):
        for h in range(N_HEADS):
            o = _softmax_result_t(acc_scr[h]).T
            gate = _silu(g_ref[h, _query_rows(i), :].astype(F32))
            o_ref[h, _query_rows(i), :] = (o * gate).astype(BF16)

    _causal_tiles_t(q_ref.shape[1] // tq, prepare, logits, lambda h, j: vt_scr[h, j], finish,
                    tq, z_scr, acc_scr)


def _moba(proj, tables, *, batch, s_len):
    t = proj.shape[1]
    nk = s_len // K_TILE
    spec = _batch_spec(s_len)
    table_spec = pl.BlockSpec((s_len, LANES), lambda b: (0, 0))
    return pl.pallas_call(
        _moba_kernel,
        grid=(batch,),
        in_specs=[
            spec(_SLAB["moba_q"]), spec(_SLAB["moba_k"]), spec(_SLAB["moba_v"]), spec(_SLAB["moba_g"]),
            table_spec, table_spec, table_spec,
        ],
        out_specs=spec(0),
        out_shape=jax.ShapeDtypeStruct((N_HEADS, t, LANES), BF16),
        scratch_shapes=[pltpu.VMEM((N_HEADS, s_len, LANES), BF16),
                        pltpu.VMEM((N_HEADS, GATE_ROWS, LANES), F32),
                        pltpu.VMEM((N_HEADS, nk, HEAD_DIM + ONES_ROWS, K_TILE), BF16),
                        pltpu.VMEM((2, N_HEADS, K_TILE, Q_TILE), F32),
                        pltpu.VMEM((N_HEADS, HEAD_DIM + ONES_ROWS, Q_TILE), F32)],
        compiler_params=pltpu.CompilerParams(dimension_semantics=("arbitrary",)),
        name="moba",
    )(proj, proj, proj, proj, *tables)


def _gla_kernel(x_ref, v_ref, g_ref, p_ref, ng_ref, o_ref):
    s_len = x_ref.shape[1]
    c_len = GLA_CHUNK
    scale = GLA_DK ** -0.5
    lane = lax.broadcasted_iota(jnp.int32, (c_len, LANES), 1)
    qhalf = lane < GLA_DK
    qhalf_row = lax.broadcasted_iota(jnp.int32, (1, LANES), 1) < GLA_DK
    row = lax.broadcasted_iota(jnp.int32, (c_len, c_len), 0)
    col = lax.broadcasted_iota(jnp.int32, (c_len, c_len), 1)
    causal = col <= row

    group = GLA_GROUP
    pairs = [(h, c) for h in range(N_HEADS) for c in range(group)]

    def body(g, carries):
        prep = {}
        for h, c in pairs:
            rows = pl.ds(pl.multiple_of(g * (group * c_len), group * c_len) + c * c_len, c_len)
            qcb = x_ref[h // 2, rows, :].astype(F32)
            kcb = x_ref[2 + h // 2, rows, :].astype(F32)
            if h % 2 == 0:
                x = jnp.where(qhalf, qcb, pltpu.roll(kcb, GLA_DK, 1))
            else:
                x = jnp.where(qhalf, pltpu.roll(qcb, GLA_DK, 1), kcb)
            p = p_ref[h, rows, :]
            base = carries[h][1] if c == 0 else prep[h, c - 1]["p_last"]
            cum = p - base
            xt = x * jnp.exp(jnp.where(qhalf, cum, -cum))
            kr = pltpu.roll(xt, GLA_DK, 1)
            decay = jnp.where(qhalf_row, jnp.exp(cum[c_len - 1:c_len]), 0.0)
            prep[h, c] = dict(
                rows=rows, qz=jnp.where(qhalf, xt, 0.0).astype(BF16), kr=kr.astype(BF16),
                khat=(jnp.where(qhalf, kr, 0.0) * decay).astype(BF16),
                decay=decay, v=v_ref[h, rows, :], p_last=p[c_len - 1:c_len])
        scores, upd = {}, {}
        for h, c in pairs:
            d = prep[h, c]
            scores[h, c] = _nt_dot(d["qz"], d["kr"])
            upd[h, c] = lax.dot_general(d["v"], d["khat"], (((0,), (0,)), ((), ())),
                                        preferred_element_type=F32)
        states = {}
        for h in range(N_HEADS):
            states[h, 0] = carries[h][0]
            for c in range(group):
                states[h, c + 1] = states[h, c] * prep[h, c]["decay"] + upd[h, c]
        outs = {}
        for h, c in pairs:
            sc = jnp.where(causal, scores[h, c] * scale, 0.0).astype(BF16)
            outs[h, c] = (jnp.dot(sc, prep[h, c]["v"], preferred_element_type=F32),
                          _nt_dot(prep[h, c]["qz"], states[h, c].astype(BF16)))
        for h, c in pairs:
            o = outs[h, c][0] + outs[h, c][1] * scale
            y = o * lax.rsqrt(jnp.mean(o * o, axis=-1, keepdims=True) + RMS_EPS) * ng_ref[...]
            rows = prep[h, c]["rows"]
            o_ref[h, rows, :] = (y * _silu(g_ref[h, rows, :].astype(F32))).astype(BF16)
        return tuple((states[h, group], prep[h, group - 1]["p_last"]) for h in range(N_HEADS))

    init = tuple((jnp.zeros((HEAD_DIM, LANES), F32), jnp.zeros((1, LANES), F32))
                 for _ in range(N_HEADS))
    lax.fori_loop(0, s_len // (group * c_len), body, init)


def _gla(proj, p_cum, gla_norm_g, *, batch, s_len):
    t = proj.shape[1]
    spec = lambda slab: pl.BlockSpec((N_HEADS, s_len, LANES), lambda b: (slab, b, 0))
    return pl.pallas_call(
        _gla_kernel,
        grid=(batch,),
        in_specs=[
            spec(_SLAB["gla_qk"]), spec(_SLAB["gla_v"]), spec(_SLAB["gla_g"]),
            pl.BlockSpec((N_HEADS, s_len, LANES), lambda b: (0, b, 0)),
            pl.BlockSpec((1, HEAD_DIM), lambda b: (0, 0)),
        ],
        out_specs=pl.BlockSpec((N_HEADS, s_len, LANES), lambda b: (0, b, 0)),
        out_shape=jax.ShapeDtypeStruct((N_HEADS, t, LANES), BF16),
        compiler_params=pltpu.CompilerParams(dimension_semantics=("arbitrary",)),
        name="gla",
    )(proj, proj, proj, p_cum, gla_norm_g.reshape(1, HEAD_DIM))


def _outproj_kernel(final, x_ref, a_ref, b_ref, c_ref, d_ref, w_ref, fg_ref, o_ref):
    parts = [r[h] for r in (a_ref, b_ref, c_ref, d_ref) for h in range(N_HEADS)]
    mixed = jnp.concatenate(parts, axis=-1)
    y = x_ref[...] + jnp.dot(mixed, w_ref[...], preferred_element_type=F32)
    if final:
        y = y * lax.rsqrt(jnp.mean(y * y, axis=-1, keepdims=True) + RMS_EPS) * fg_ref[...]
    o_ref[...] = y


def _outproj(x2, mixers, w_out, layer, final_g, *, final, tm):
    t, d = x2.shape
    mspec = pl.BlockSpec((N_HEADS, tm, LANES), lambda i: (0, i, 0))
    return pl.pallas_call(
        functools.partial(_outproj_kernel, final),
        grid=(t // tm,),
        in_specs=[
            pl.BlockSpec((tm, d), lambda i: (i, 0)),
            mspec, mspec, mspec, mspec,
            pl.BlockSpec((None,) + w_out.shape[1:], lambda i: (layer, 0, 0)),
            pl.BlockSpec((1, d), lambda i: (0, 0)),
        ],
        out_specs=pl.BlockSpec((tm, d), lambda i: (i, 0)),
        out_shape=jax.ShapeDtypeStruct((t, d), F32),
        compiler_params=pltpu.CompilerParams(
            dimension_semantics=("arbitrary",), vmem_limit_bytes=56 * 1024 * 1024),
        name="outproj",
    )(x2, *mixers, w_out, final_g.reshape(1, d))


RELAYOUT_ROWS = 256
RELAYOUT_TAIL = 32


def _relayout_kernel(a_ref, b_ref, o_ref):
    r = pl.program_id(0)
    depth, rows, _ = o_ref.shape
    for l in range(depth):
        x = jnp.concatenate([a_ref[:, l, :], b_ref[:, l, :]], axis=0)
        out_lo = 0
        for (src_lo, src_hi) in _MAIN_RUNS:
            shift = src_lo - out_lo
            out_hi = out_lo + (src_hi - src_lo)
            assert shift <= RELAYOUT_TAIL and out_lo % rows == 0 and out_hi % rows == 0

            @pl.when((r >= out_lo // rows) & (r < out_hi // rows))
            def _(shift=shift, l=l, x=x):
                o_ref[l] = x[shift:shift + rows].astype(BF16)

            out_lo = out_hi


def _relayout_w_in(w_in):
    depth, d, d_in = w_in.shape
    wt = jnp.transpose(w_in, (2, 0, 1))
    rows, tail = RELAYOUT_ROWS, RELAYOUT_TAIL
    main = pl.pallas_call(
        _relayout_kernel,
        grid=(N_MAIN // rows,),
        in_specs=[pl.BlockSpec((rows, depth, d), lambda r: (r, 0, 0)),
                  pl.BlockSpec((tail, depth, d), lambda r: ((r + 1) * (rows // tail), 0, 0))],
        out_specs=pl.BlockSpec((depth, rows, d), lambda r: (0, r, 0)),
        out_shape=jax.ShapeDtypeStruct((depth, N_MAIN, d), BF16),
        compiler_params=pltpu.CompilerParams(dimension_semantics=("arbitrary",)),
        name="relayout",
    )(wt, wt)
    f_off, a_off = _SEG_OFF["fox_f"][0], _SEG_OFF["gla_a"][0]
    gate_cols = jnp.concatenate([w_in[..., f_off:f_off + N_HEADS],
                                 w_in[..., a_off:a_off + GLA_RANK]], axis=-1)
    small = jnp.pad(jnp.transpose(gate_cols, (0, 2, 1)),
                    ((0, 0), (0, LANES - N_HEADS - GLA_RANK), (0, 0))).astype(BF16)
    return main, small


def _dup_halves(a):
    parts = []
    for h in range(N_HEADS):
        blk = a[..., h * GLA_DK:(h + 1) * GLA_DK]
        parts += [blk, blk]
    return jnp.concatenate(parts, axis=-1)


def _rope_tables(s_len, comp_dim, rot_dim):
    half = rot_dim // 2
    inv_freq = ROPE_THETA ** (-jnp.arange(0, rot_dim, 2, dtype=F32) / rot_dim)
    ang = jnp.arange(s_len, dtype=F32)[:, None] * inv_freq[None, :]
    cos, sin = jnp.cos(ang), jnp.sin(ang)
    zeros = jnp.zeros((s_len, comp_dim - rot_dim), F32)
    ones = jnp.ones((s_len, comp_dim - rot_dim), F32)
    reps = LANES // comp_dim
    cos_t = jnp.tile(jnp.concatenate([cos, cos, ones], axis=-1), (1, reps))
    sin_up = jnp.tile(jnp.concatenate([-sin, jnp.zeros_like(sin), zeros], axis=-1), (1, reps))
    sin_dn = jnp.tile(jnp.concatenate([jnp.zeros_like(sin), sin, zeros], axis=-1), (1, reps))
    return cos_t, sin_up, sin_dn


def kernel(x, norm_g, w_in, fox_fb, diff_lam, diff_norm_g, gla_wa2, gla_ba, gla_norm_g, w_out,
           final_norm_g):
    batch, s_len, d_model = x.shape
    depth = w_in.shape[0]
    assert s_len % Q_TILE == 0 and Q_TILE == K_TILE == MOBA_BLOCK
    assert s_len // MOBA_BLOCK <= GATE_ROWS
    t = batch * s_len
    tm_in = min(1024, t)
    tn_in = 1536
    tm_out = min(512, t)

    w_main, w_small = _relayout_w_in(w_in)
    w_out_b = w_out.astype(BF16)
    fb_pad = jnp.pad(fox_fb, ((0, 0), (0, LANES - N_HEADS)))[:, None, :]
    wa_pad = jnp.pad(_dup_halves(gla_wa2), ((0, 0), (N_HEADS, LANES - N_HEADS - GLA_RANK), (0, 0))
                     ).astype(BF16)
    ba_dup = _dup_halves(gla_ba)[:, None, :]
    rope_diff = _rope_tables(s_len, DIFF_QK_DIM, DIFF_QK_DIM // ROPE_FRACTION)
    rope_moba = _rope_tables(s_len, HEAD_DIM, HEAD_DIM // ROPE_FRACTION)

    x2 = x.reshape(t, d_model)
    for l in range(depth):
        proj, small = _inproj(x2, norm_g[l], w_main, w_small, l, tm=tm_in, tn=tn_in)
        fk_rep, fq_rows, p_cum = _gates(small, fb_pad[l], wa_pad[l], ba_dup[l],
                                        batch=batch, s_len=s_len)
        lam_init = 0.8 - 0.6 * math.exp(-0.3 * l)
        mixers = (
            _fox(proj, fk_rep, fq_rows, batch=batch, s_len=s_len),
            _diff(proj, diff_lam[l], diff_norm_g[l], rope_diff, lam_init, batch=batch, s_len=s_len),
            _moba(proj, rope_moba, batch=batch, s_len=s_len),
            _gla(proj, p_cum, gla_norm_g[l], batch=batch, s_len=s_len),
        )
        x2 = _outproj(x2, mixers, w_out_b, l, final_norm_g, final=(l == depth - 1), tm=tm_out)
    return x2.reshape(batch, s_len, d_model)
```

```python
import functools
import math

import jax
import jax.numpy as jnp
from jax import lax
from jax.experimental import pallas as pl
from jax.experimental.pallas import tpu as pltpu

F32 = jnp.float32
BF16 = jnp.bfloat16

LANES = 128
HEAD_DIM = 128
N_HEADS = 4
GROUP_W = N_HEADS * HEAD_DIM
DIFF_QK_DIM = HEAD_DIM // 2
MOBA_BLOCK = 256
MOBA_TOPK = 3
GLA_DK = HEAD_DIM // 2
GLA_RANK = 16
GLA_TAU = 16.0
GLA_CHUNK = 64
ROPE_THETA = 500000.0
ROPE_FRACTION = 4
RMS_EPS = 1e-6
MASKED_LOGIT = -1e30

_SEGMENTS = (
    ("fox_q", GROUP_W), ("fox_k", GROUP_W), ("fox_v", GROUP_W),
    ("fox_f", N_HEADS), ("fox_g", GROUP_W),
    ("diff_q", GROUP_W), ("diff_k", GROUP_W), ("diff_v", GROUP_W), ("diff_g", GROUP_W),
    ("moba_q", GROUP_W), ("moba_k", GROUP_W), ("moba_v", GROUP_W), ("moba_g", GROUP_W),
    ("gla_q", N_HEADS * GLA_DK), ("gla_k", N_HEADS * GLA_DK), ("gla_v", GROUP_W),
    ("gla_a", GLA_RANK), ("gla_g", GROUP_W),
)
_SEG_OFF = {}
_off = 0
for _name, _w in _SEGMENTS:
    _SEG_OFF[_name] = (_off, _w)
    _off += _w

_SLABS = ("fox_q", "fox_k", "fox_v", "fox_g", "diff_q", "diff_k", "diff_v", "diff_g",
          "moba_q", "moba_k", "moba_v", "moba_g", "gla_qk", "gla_v", "gla_g")
_SLAB = {name: i for i, name in enumerate(_SLABS)}
N_MAIN = len(_SLABS) * GROUP_W
_MAIN_RUNS = ((0, _SEG_OFF["fox_f"][0]),
              (_SEG_OFF["fox_g"][0], _SEG_OFF["gla_a"][0]),
              (_SEG_OFF["gla_g"][0], _off))
assert sum(b - a for a, b in _MAIN_RUNS) == N_MAIN

Q_TILE = 256
K_TILE = 256
CUM_BLOCK = 128
GATE_ROWS = 16
GLA_GROUP = 4
ONES_ROWS = 16
LOG2E = math.log2(math.e)


def _silu(x):
    return x * (1.0 / (1.0 + jnp.exp(-x)))


def _log_sigmoid(x):
    return jnp.minimum(x, 0.0) - jnp.log1p(jnp.exp(-jnp.abs(x)))


def _nt_dot(a, b):
    return lax.dot_general(a, b, (((1,), (1,)), ((), ())), preferred_element_type=F32)


def _inproj_kernel(x_ref, g_ref, w_ref, ws_ref, o_ref, small_ref, h_scr):
    @pl.when(pl.program_id(1) == 0)
    def _():
        x = x_ref[...]
        y = x * lax.rsqrt(jnp.mean(x * x, axis=-1, keepdims=True) + RMS_EPS) * g_ref[...]
        h = y.astype(BF16)
        h_scr[...] = h
        small_ref[...] = _nt_dot(h, ws_ref[...])

    acc = _nt_dot(h_scr[...], w_ref[...])
    for c in range(o_ref.shape[0]):
        o_ref[c] = acc[:, c * LANES:(c + 1) * LANES].astype(BF16)


def _inproj(x2, norm_g, w_main, w_small, layer, *, tm, tn):
    t, d = x2.shape
    n_main = w_main.shape[1]
    return pl.pallas_call(
        _inproj_kernel,
        grid=(t // tm, n_main // tn),
        in_specs=[
            pl.BlockSpec((tm, d), lambda i, j: (i, 0)),
            pl.BlockSpec((1, d), lambda i, j: (0, 0)),
            pl.BlockSpec((None, tn, d), lambda i, j: (layer, j, 0)),
            pl.BlockSpec((None, LANES, d), lambda i, j: (layer, 0, 0)),
        ],
        out_specs=[
            pl.BlockSpec((tn // LANES, tm, LANES), lambda i, j: (j, i, 0)),
            pl.BlockSpec((tm, LANES), lambda i, j: (i, 0)),
        ],
        out_shape=[
            jax.ShapeDtypeStruct((n_main // LANES, t, LANES), BF16),
            jax.ShapeDtypeStruct((t, LANES), F32),
        ],
        scratch_shapes=[pltpu.VMEM((tm, d), BF16)],
        compiler_params=pltpu.CompilerParams(
            dimension_semantics=("arbitrary", "arbitrary"),
            vmem_limit_bytes=56 * 1024 * 1024),
        name="inproj",
    )(x2, norm_g.reshape(1, d), w_main, w_small)


def _gates_kernel(small_ref, fb_ref, wa_ref, ba_ref, fk_ref, fq_ref, p_ref):
    s_len = small_ref.shape[0]
    small = small_ref[...]
    log_f = _log_sigmoid(small + fb_ref[...])
    a_logit = jnp.dot(small.astype(BF16), wa_ref[...], preferred_element_type=F32) + ba_ref[...]
    log_a = _log_sigmoid(a_logit) * (1.0 / GLA_TAU)
    both = jnp.concatenate([log_f, log_a], axis=-1)

    row = lax.broadcasted_iota(jnp.int32, (CUM_BLOCK, CUM_BLOCK), 0)
    col = lax.broadcasted_iota(jnp.int32, (CUM_BLOCK, CUM_BLOCK), 1)
    tri = (col <= row).astype(BF16)
    low_half = lax.broadcasted_iota(jnp.int32, (CUM_BLOCK, LANES), 1) < GLA_DK
    carry = jnp.zeros((1, both.shape[1]), F32)
    for c in range(s_len // CUM_BLOCK):
        blk = both[c * CUM_BLOCK:(c + 1) * CUM_BLOCK]
        hi = blk.astype(BF16)
        rest = blk - hi.astype(F32)
        mid = rest.astype(BF16)
        lo = (rest - mid.astype(F32)).astype(BF16)
        cum = carry + sum(jnp.dot(tri, part, preferred_element_type=F32) for part in (hi, mid, lo))
        carry = cum[CUM_BLOCK - 1:CUM_BLOCK]
        rows = slice(c * CUM_BLOCK, (c + 1) * CUM_BLOCK)
        f_cum = cum[:, :LANES] * LOG2E
        per_tile = Q_TILE // CUM_BLOCK
        fq_ref[c // per_tile, :, (c % per_tile) * CUM_BLOCK:(c % per_tile + 1) * CUM_BLOCK] = (
            f_cum.T[:GATE_ROWS])
        for h in range(N_HEADS):
            fk_ref[h, rows, :] = jnp.broadcast_to(f_cum[:, h:h + 1], (CUM_BLOCK, LANES))
            pair = cum[:, (1 + h // 2) * LANES:(2 + h // 2) * LANES]
            swapped = pltpu.roll(pair, GLA_DK, 1)
            p_ref[h, rows, :] = (jnp.where(low_half, pair, swapped) if h % 2 == 0
                                 else jnp.where(low_half, swapped, pair))


def _gates(small, fb_pad, wa_pad, ba_dup, *, batch, s_len):
    t = small.shape[0]
    return pl.pallas_call(
        _gates_kernel,
        grid=(batch,),
        in_specs=[
            pl.BlockSpec((s_len, LANES), lambda b: (b, 0)),
            pl.BlockSpec((1, LANES), lambda b: (0, 0)),
            pl.BlockSpec((LANES, N_HEADS * GLA_DK), lambda b: (0, 0)),
            pl.BlockSpec((1, N_HEADS * GLA_DK), lambda b: (0, 0)),
        ],
        out_specs=[
            pl.BlockSpec((N_HEADS, s_len, LANES), lambda b: (0, b, 0)),
            pl.BlockSpec((None, s_len // Q_TILE, GATE_ROWS, Q_TILE), lambda b: (b, 0, 0, 0)),
            pl.BlockSpec((N_HEADS, s_len, LANES), lambda b: (0, b, 0)),
        ],
        out_shape=[
            jax.ShapeDtypeStruct((N_HEADS, t, LANES), F32),
            jax.ShapeDtypeStruct((batch, s_len // Q_TILE, GATE_ROWS, Q_TILE), F32),
            jax.ShapeDtypeStruct((N_HEADS, t, LANES), F32),
        ],
        compiler_params=pltpu.CompilerParams(dimension_semantics=("arbitrary",)),
        name="gates",
    )(small, fb_pad, wa_pad, ba_dup)


def _causal_mask_t(tk, width, tq):
    key = lax.broadcasted_iota(jnp.int32, (tk, width), 0)
    qry = lax.broadcasted_iota(jnp.int32, (tk, width), 1)
    if width != tq:
        qry = qry & (tq - 1)
    return key <= qry


def _transpose_values(v_ref, vt_scr):
    dv = v_ref.shape[2]
    for h in range(v_ref.shape[0]):
        for j in range(v_ref.shape[1] // K_TILE):
            blk = v_ref[h, j * K_TILE:(j + 1) * K_TILE, :].astype(F32)
            vt_scr[h, j, :dv, :] = blk.T.astype(BF16)
            vt_scr[h, j, dv:, :] = jnp.ones((ONES_ROWS, K_TILE), BF16)


def _col_max(z):
    rows = z.shape[0]
    while rows > 8:
        rows //= 2
        z = jnp.maximum(z[:rows], z[rows:])
    return jnp.max(z, axis=0, keepdims=True)


def _causal_tiles_t(n_tiles, prepare_fn, logits_fn, vt_fn, finish_fn, tq, z_scr, acc_scr):
    n, _, width = acc_scr.shape

    def update(j, slot, ms, tile_max, masked):
        stats = []
        for s in range(n):
            z = z_scr[slot, s]
            if masked:
                z = jnp.where(_causal_mask_t(K_TILE, width, tq), z, -jnp.inf)
                m_new = jnp.maximum(ms[s], _col_max(z))
            else:
                m_new = jnp.maximum(ms[s], tile_max[s])
            stats.append((m_new, jnp.exp2(ms[s] - m_new), jnp.exp2(z - m_new).astype(BF16)))
        for s in range(n):
            pv = jnp.dot(vt_fn(s, j), stats[s][2], preferred_element_type=F32)
            acc_scr[s] = stats[s][1] * acc_scr[s] + pv
        return tuple(st[0] for st in stats)

    def first_logits(i):
        ctx = prepare_fn(i)
        return ctx, [logits_fn(ctx, s, jnp.int32(0)) for s in range(n)]

    def open_tile(zs):
        for s in range(n):
            z_scr[0, s] = zs[s]
            acc_scr[s] = jnp.zeros(acc_scr.shape[1:], F32)
        return (tuple(jnp.full((1, width), -jnp.inf, F32) for _ in range(n)),
                tuple(_col_max(z) for z in zs))

    def close_tile(i, ms):
        update(i, lax.rem(i, 2), ms, None, True)
        finish_fn(i)

    def off_diagonal(i, ctx, state):
        def body(j, state):
            ms, tile_max = state
            slot = lax.rem(j, 2)
            z_next = [logits_fn(ctx, s, j + 1) for s in range(n)]
            ms = update(j, slot, ms, tile_max, False)
            for s in range(n):
                z_scr[1 - slot, s] = z_next[s]
            return ms, tuple(_col_max(z) for z in z_next)

        return lax.fori_loop(0, i, body, state)[0]

    _, zs = first_logits(jnp.int32(0))
    ms, _ = open_tile(zs)

    def outer(i, ms_prev):
        ctx, zs = first_logits(i)
        close_tile(i - 1, ms_prev)
        return off_diagonal(i, ctx, open_tile(zs))

    ms = lax.fori_loop(1, n_tiles, outer, ms)
    close_tile(jnp.int32(n_tiles - 1), ms)


def _softmax_result_t(acc):
    dv = acc.shape[0] - ONES_ROWS
    return acc[:dv] / acc[dv:dv + 1]


def _rope(x, cos_t, sin_up, sin_dn, half):
    return (x * cos_t + pltpu.roll(x, LANES - half, 1) * sin_up
            + pltpu.roll(x, half, 1) * sin_dn)


def _key_rows(j):
    return pl.ds(pl.multiple_of(j * K_TILE, K_TILE), K_TILE)


def _query_rows(i):
    return pl.ds(pl.multiple_of(i * Q_TILE, Q_TILE), Q_TILE)


def _batch_spec(s_len):
    return lambda slab: pl.BlockSpec((N_HEADS, s_len, LANES), lambda b: (slab, b, 0))


def _fox_kernel(q_ref, k_ref, v_ref, g_ref, fk_ref, fq_ref, o_ref, vt_scr, z_scr, acc_scr):
    scale = HEAD_DIM ** -0.5 * LOG2E
    _transpose_values(v_ref, vt_scr)

    def prepare(i):
        fq_all = fq_ref[i]
        return ([q_ref[h, _query_rows(i), :] for h in range(N_HEADS)],
                [fq_all[h:h + 1, :] for h in range(N_HEADS)])

    def logits(ctx, h, j):
        qs, fqs = ctx
        fk = fk_ref[h, _key_rows(j), :]
        fk = jnp.concatenate([fk] * (Q_TILE // LANES), axis=-1)
        return _nt_dot(k_ref[h, _key_rows(j), :], qs[h]) * scale + (fqs[h] - fk)

    def finish(i):
        for h in range(N_HEADS):
            o = _softmax_result_t(acc_scr[h]).T
            gate = _silu(g_ref[h, _query_rows(i), :].astype(F32))
            o_ref[h, _query_rows(i), :] = (o * gate).astype(BF16)

    _causal_tiles_t(q_ref.shape[1] // Q_TILE, prepare, logits, lambda h, j: vt_scr[h, j], finish,
                    Q_TILE, z_scr, acc_scr)


def _fox(proj, fk_rep, fq_rows, *, batch, s_len):
    t = proj.shape[1]
    nq = s_len // Q_TILE
    nk = s_len // K_TILE
    spec = _batch_spec(s_len)
    return pl.pallas_call(
        _fox_kernel,
        grid=(batch,),
        in_specs=[
            spec(_SLAB["fox_q"]), spec(_SLAB["fox_k"]), spec(_SLAB["fox_v"]), spec(_SLAB["fox_g"]),
            spec(0),
            pl.BlockSpec((None, nq, GATE_ROWS, Q_TILE), lambda b: (b, 0, 0, 0)),
        ],
        out_specs=spec(0),
        out_shape=jax.ShapeDtypeStruct((N_HEADS, t, LANES), BF16),
        scratch_shapes=[pltpu.VMEM((N_HEADS, nk, HEAD_DIM + ONES_ROWS, K_TILE), BF16),
                        pltpu.VMEM((2, N_HEADS, K_TILE, Q_TILE), F32),
                        pltpu.VMEM((N_HEADS, HEAD_DIM + ONES_ROWS, Q_TILE), F32)],
        compiler_params=pltpu.CompilerParams(dimension_semantics=("arbitrary",)),
        name="fox",
    )(proj, proj, proj, proj, fk_rep, fq_rows)


def _diff_kernel(lam_init, q_ref, k_ref, v_ref, g_ref, lam_ref, ng_ref, c_ref, u_ref, d_ref, o_ref,
                 kr_scr, vt_scr, z_scr, acc_scr):
    tq = Q_TILE
    half = DIFF_QK_DIM // ROPE_FRACTION // 2
    scale = DIFF_QK_DIM ** -0.5 * LOG2E

    _transpose_values(v_ref, vt_scr)
    for h in range(N_HEADS):
        kr_scr[h] = _rope(k_ref[h].astype(F32), c_ref[...], u_ref[...], d_ref[...], half).astype(BF16)

    lf = lam_ref[...]
    lam = (jnp.exp(jnp.sum(lf[0:1] * lf[1:2], axis=-1, keepdims=True))
           - jnp.exp(jnp.sum(lf[2:3] * lf[3:4], axis=-1, keepdims=True)) + lam_init)
    first = lax.broadcasted_iota(jnp.int32, (tq, LANES), 1) < DIFF_QK_DIM

    def prepare(i):
        rows = _query_rows(i)
        q12 = []
        for h in range(N_HEADS):
            q = _rope(q_ref[h, rows, :].astype(F32), c_ref[rows, :], u_ref[rows, :], d_ref[rows, :], half)
            q12.append(jnp.concatenate([jnp.where(first, q, 0.0),
                                        jnp.where(first, 0.0, q)], axis=0).astype(BF16))
        return q12

    def logits(q12, h, j):
        return _nt_dot(kr_scr[h, _key_rows(j), :], q12[h]) * scale

    def finish(i):
        for h in range(N_HEADS):
            a = _softmax_result_t(acc_scr[h])
            o = (a[:, :tq] - lam * a[:, tq:]).T
            o = o * lax.rsqrt(jnp.mean(o * o, axis=-1, keepdims=True) + RMS_EPS) * ng_ref[...]
            o = o * (1.0 - lam_init)
            gate = _silu(g_ref[h, _query_rows(i), :].astype(F32))
            o_ref[h, _query_rows(i), :] = (o * gate).astype(BF16)

    _causal_tiles_t(q_ref.shape[1] // tq, prepare, logits, lambda h, j: vt_scr[h, j], finish,
                    tq, z_scr, acc_scr)


def _diff(proj, diff_lam, diff_norm_g, tables, lam_init, *, batch, s_len):
    t = proj.shape[1]
    nk = s_len // K_TILE
    spec = _batch_spec(s_len)
    table_spec = pl.BlockSpec((s_len, LANES), lambda b: (0, 0))
    return pl.pallas_call(
        functools.partial(_diff_kernel, lam_init),
        grid=(batch,),
        in_specs=[
            spec(_SLAB["diff_q"]), spec(_SLAB["diff_k"]), spec(_SLAB["diff_v"]), spec(_SLAB["diff_g"]),
            pl.BlockSpec((4, DIFF_QK_DIM), lambda b: (0, 0)),
            pl.BlockSpec((1, HEAD_DIM), lambda b: (0, 0)),
            table_spec, table_spec, table_spec,
        ],
        out_specs=spec(0),
        out_shape=jax.ShapeDtypeStruct((N_HEADS, t, LANES), BF16),
        scratch_shapes=[pltpu.VMEM((N_HEADS, s_len, LANES), BF16),
                        pltpu.VMEM((N_HEADS, nk, HEAD_DIM + ONES_ROWS, K_TILE), BF16),
                        pltpu.VMEM((2, N_HEADS, K_TILE, 2 * Q_TILE), F32),
                        pltpu.VMEM((N_HEADS, HEAD_DIM + ONES_ROWS, 2 * Q_TILE), F32)],
        compiler_params=pltpu.CompilerParams(dimension_semantics=("arbitrary",)),
        name="diff",
    )(proj, proj, proj, proj, diff_lam, diff_norm_g.reshape(1, HEAD_DIM), *tables)


def _moba_kernel(q_ref, k_ref, v_ref, g_ref, c_ref, u_ref, d_ref,
                 o_ref, kr_scr, kmean_scr, vt_scr, z_scr, acc_scr):
    tq = Q_TILE
    half = HEAD_DIM // ROPE_FRACTION // 2
    scale = HEAD_DIM ** -0.5 * LOG2E
    n_blk = k_ref.shape[1] // MOBA_BLOCK

    _transpose_values(v_ref, vt_scr)
    kmean_scr[...] = jnp.zeros(kmean_scr.shape, F32)
    for h in range(N_HEADS):
        kr = _rope(k_ref[h].astype(F32), c_ref[...], u_ref[...], d_ref[...], half)
        kr_scr[h] = kr.astype(BF16)
        for n in range(n_blk):
            blk = kr[n * MOBA_BLOCK:(n + 1) * MOBA_BLOCK]
            kmean_scr[h, n:n + 1, :] = jnp.sum(blk, axis=0, keepdims=True) * (1.0 / MOBA_BLOCK)

    blk_id = lax.broadcasted_iota(jnp.int32, (GATE_ROWS, tq), 0).astype(F32)

    def prepare(i):
        rows = _query_rows(i)
        past = blk_id < lax.convert_element_type(i, F32)
        qs, biases = [], []
        for h in range(N_HEADS):
            q = _rope(q_ref[h, rows, :].astype(F32), c_ref[rows, :], u_ref[rows, :], d_ref[rows, :],
                      half).astype(BF16)
            gate = _nt_dot(kmean_scr[h].astype(BF16), q)
            gate = jnp.where(past, gate, -jnp.inf)
            open_ = past
            sel = jnp.zeros((GATE_ROWS, tq), jnp.bool_)
            for _ in range(MOBA_TOPK):
                top = jnp.max(gate, axis=0, keepdims=True)
                idx = jnp.min(jnp.where(gate == top, blk_id, float(GATE_ROWS)), axis=0, keepdims=True)
                pick = (blk_id == idx) & open_
                sel = sel | pick
                open_ = open_ & jnp.logical_not(pick)
                gate = jnp.where(pick, -jnp.inf, gate)
            qs.append(q)
            biases.append(jnp.where(sel, 0.0, MASKED_LOGIT))
        return i, qs, biases

    def logits(ctx, h, j):
        i, qs, biases = ctx
        here = blk_id == lax.convert_element_type(j, F32)
        bias = jnp.sum(jnp.where(here, biases[h], 0.0), axis=0, keepdims=True)
        bias = jnp.where(j == i, 0.0, bias)
        return _nt_dot(kr_scr[h, _key_rows(j), :], qs[h]) * scale + bias

    def finish(i):
        for h in range(N_HEADS):
            o = _softmax_result_t(acc_scr[h]).T
            gate = _silu(g_ref[h, _query_rows(i), :].astype(F32))
            o_ref[h, _query_rows(i), :] = (o * gate).astype(BF16)

    _causal_tiles_t(q_ref.shape[1] // tq, prepare, logits, lambda h, j: vt_scr[h, j], finish,
                    tq, z_scr, acc_scr)


def _moba(proj, tables, *, batch, s_len):
    t = proj.shape[1]
    nk = s_len // K_TILE
    spec = _batch_spec(s_len)
    table_spec = pl.BlockSpec((s_len, LANES), lambda b: (0, 0))
    return pl.pallas_call(
        _moba_kernel,
        grid=(batch,),
        in_specs=[
            spec(_SLAB["moba_q"]), spec(_SLAB["moba_k"]), spec(_SLAB["moba_v"]), spec(_SLAB["moba_g"]),
            table_spec, table_spec, table_spec,
        ],
        out_specs=spec(0),
        out_shape=jax.ShapeDtypeStruct((N_HEADS, t, LANES), BF16),
        scratch_shapes=[pltpu.VMEM((N_HEADS, s_len, LANES), BF16),
                        pltpu.VMEM((N_HEADS, GATE_ROWS, LANES), F32),
                        pltpu.VMEM((N_HEADS, nk, HEAD_DIM + ONES_ROWS, K_TILE), BF16),
                        pltpu.VMEM((2, N_HEADS, K_TILE, Q_TILE), F32),
                        pltpu.VMEM((N_HEADS, HEAD_DIM + ONES_ROWS, Q_TILE), F32)],
        compiler_params=pltpu.CompilerParams(dimension_semantics=("arbitrary",)),
        name="moba",
    )(proj, proj, proj, proj, *tables)


def _gla_kernel(x_ref, v_ref, g_ref, p_ref, ng_ref, o_ref):
    s_len = x_ref.shape[1]
    c_len = GLA_CHUNK
    scale = GLA_DK ** -0.5
    lane = lax.broadcasted_iota(jnp.int32, (c_len, LANES), 1)
    qhalf = lane < GLA_DK
    qhalf_row = lax.broadcasted_iota(jnp.int32, (1, LANES), 1) < GLA_DK
    row = lax.broadcasted_iota(jnp.int32, (c_len, c_len), 0)
    col = lax.broadcasted_iota(jnp.int32, (c_len, c_len), 1)
    causal = col <= row

    group = GLA_GROUP
    pairs = [(h, c) for h in range(N_HEADS) for c in range(group)]

    def body(g, carries):
        prep = {}
        for h, c in pairs:
            rows = pl.ds(pl.multiple_of(g * (group * c_len), group * c_len) + c * c_len, c_len)
            qcb = x_ref[h // 2, rows, :].astype(F32)
            kcb = x_ref[2 + h // 2, rows, :].astype(F32)
            if h % 2 == 0:
                x = jnp.where(qhalf, qcb, pltpu.roll(kcb, GLA_DK, 1))
            else:
                x = jnp.where(qhalf, pltpu.roll(qcb, GLA_DK, 1), kcb)
            p = p_ref[h, rows, :]
            base = carries[h][1] if c == 0 else prep[h, c - 1]["p_last"]
            cum = p - base
            xt = x * jnp.exp(jnp.where(qhalf, cum, -cum))
            kr = pltpu.roll(xt, GLA_DK, 1)
            decay = jnp.where(qhalf_row, jnp.exp(cum[c_len - 1:c_len]), 0.0)
            prep[h, c] = dict(
                rows=rows, qz=jnp.where(qhalf, xt, 0.0).astype(BF16), kr=kr.astype(BF16),
                khat=(jnp.where(qhalf, kr, 0.0) * decay).astype(BF16),
                decay=decay, v=v_ref[h, rows, :], p_last=p[c_len - 1:c_len])
        scores, upd = {}, {}
        for h, c in pairs:
            d = prep[h, c]
            scores[h, c] = _nt_dot(d["qz"], d["kr"])
            upd[h, c] = lax.dot_general(d["v"], d["khat"], (((0,), (0,)), ((), ())),
                                        preferred_element_type=F32)
        states = {}
        for h in range(N_HEADS):
            states[h, 0] = carries[h][0]
            for c in range(group):
                states[h, c + 1] = states[h, c] * prep[h, c]["decay"] + upd[h, c]
        outs = {}
        for h, c in pairs:
            sc = jnp.where(causal, scores[h, c] * scale, 0.0).astype(BF16)
            outs[h, c] = (jnp.dot(sc, prep[h, c]["v"], preferred_element_type=F32),
                          _nt_dot(prep[h, c]["qz"], states[h, c].astype(BF16)))
        for h, c in pairs:
            o = outs[h, c][0] + outs[h, c][1] * scale
            y = o * lax.rsqrt(jnp.mean(o * o, axis=-1, keepdims=True) + RMS_EPS) * ng_ref[...]
            rows = prep[h, c]["rows"]
            o_ref[h, rows, :] = (y * _silu(g_ref[h, rows, :].astype(F32))).astype(BF16)
        return tuple((states[h, group], prep[h, group - 1]["p_last"]) for h in range(N_HEADS))

    init = tuple((jnp.zeros((HEAD_DIM, LANES), F32), jnp.zeros((1, LANES), F32))
                 for _ in range(N_HEADS))
    lax.fori_loop(0, s_len // (group * c_len), body, init)


def _gla(proj, p_cum, gla_norm_g, *, batch, s_len):
    t = proj.shape[1]
    spec = lambda slab: pl.BlockSpec((N_HEADS, s_len, LANES), lambda b: (slab, b, 0))
    return pl.pallas_call(
        _gla_kernel,
        grid=(batch,),
        in_specs=[
            spec(_SLAB["gla_qk"]), spec(_SLAB["gla_v"]), spec(_SLAB["gla_g"]),
            pl.BlockSpec((N_HEADS, s_len, LANES), lambda b: (0, b, 0)),
            pl.BlockSpec((1, HEAD_DIM), lambda b: (0, 0)),
        ],
        out_specs=pl.BlockSpec((N_HEADS, s_len, LANES), lambda b: (0, b, 0)),
        out_shape=jax.ShapeDtypeStruct((N_HEADS, t, LANES), BF16),
        compiler_params=pltpu.CompilerParams(dimension_semantics=("arbitrary",)),
        name="gla",
    )(proj, proj, proj, p_cum, gla_norm_g.reshape(1, HEAD_DIM))


def _outproj_kernel(final, x_ref, a_ref, b_ref, c_ref, d_ref, w_ref, fg_ref, o_ref):
    parts = [r[h] for r in (a_ref, b_ref, c_ref, d_ref) for h in range(N_HEADS)]
    mixed = jnp.concatenate(parts, axis=-1)
    y = x_ref[...] + jnp.dot(mixed, w_ref[...], preferred_element_type=F32)
    if final:
        y = y * lax.rsqrt(jnp.mean(y * y, axis=-1, keepdims=True) + RMS_EPS) * fg_ref[...]
    o_ref[...] = y


def _outproj(x2, mixers, w_out, layer, final_g, *, final, tm):
    t, d = x2.shape
    mspec = pl.BlockSpec((N_HEADS, tm, LANES), lambda i: (0, i, 0))
    return pl.pallas_call(
        functools.partial(_outproj_kernel, final),
        grid=(t // tm,),
        in_specs=[
            pl.BlockSpec((tm, d), lambda i: (i, 0)),
            mspec, mspec, mspec, mspec,
            pl.BlockSpec((None,) + w_out.shape[1:], lambda i: (layer, 0, 0)),
            pl.BlockSpec((1, d), lambda i: (0, 0)),
        ],
        out_specs=pl.BlockSpec((tm, d), lambda i: (i, 0)),
        out_shape=jax.ShapeDtypeStruct((t, d), F32),
        compiler_params=pltpu.CompilerParams(
            dimension_semantics=("arbitrary",), vmem_limit_bytes=56 * 1024 * 1024),
        name="outproj",
    )(x2, *mixers, w_out, final_g.reshape(1, d))


RELAYOUT_ROWS = 256
RELAYOUT_TAIL = 32


def _relayout_kernel(a_ref, b_ref, o_ref):
    r = pl.program_id(0)
    depth, rows, _ = o_ref.shape
    for l in range(depth):
        x = jnp.concatenate([a_ref[:, l, :], b_ref[:, l, :]], axis=0)
        out_lo = 0
        for (src_lo, src_hi) in _MAIN_RUNS:
            shift = src_lo - out_lo
            out_hi = out_lo + (src_hi - src_lo)
            assert shift <= RELAYOUT_TAIL and out_lo % rows == 0 and out_hi % rows == 0

            @pl.when((r >= out_lo // rows) & (r < out_hi // rows))
            def _(shift=shift, l=l, x=x):
                o_ref[l] = x[shift:shift + rows].astype(BF16)

            out_lo = out_hi


def _relayout_w_in(w_in):
    depth, d, d_in = w_in.shape
    wt = jnp.transpose(w_in, (2, 0, 1))
    rows, tail = RELAYOUT_ROWS, RELAYOUT_TAIL
    main = pl.pallas_call(
        _relayout_kernel,
        grid=(N_MAIN // rows,),
        in_specs=[pl.BlockSpec((rows, depth, d), lambda r: (r, 0, 0)),
                  pl.BlockSpec((tail, depth, d), lambda r: ((r + 1) * (rows // tail), 0, 0))],
        out_specs=pl.BlockSpec((depth, rows, d), lambda r: (0, r, 0)),
        out_shape=jax.ShapeDtypeStruct((depth, N_MAIN, d), BF16),
        compiler_params=pltpu.CompilerParams(dimension_semantics=("arbitrary",)),
        name="relayout",
    )(wt, wt)
    f_off, a_off = _SEG_OFF["fox_f"][0], _SEG_OFF["gla_a"][0]
    gate_cols = jnp.concatenate([w_in[..., f_off:f_off + N_HEADS],
                                 w_in[..., a_off:a_off + GLA_RANK]], axis=-1)
    small = jnp.pad(jnp.transpose(gate_cols, (0, 2, 1)),
                    ((0, 0), (0, LANES - N_HEADS - GLA_RANK), (0, 0))).astype(BF16)
    return main, small


def _rope_tables(s_len, comp_dim, rot_dim):
    half = rot_dim // 2
    inv_freq = ROPE_THETA ** (-jnp.arange(0, rot_dim, 2, dtype=F32) / rot_dim)
    ang = jnp.arange(s_len, dtype=F32)[:, None] * inv_freq[None, :]
    cos, sin = jnp.cos(ang), jnp.sin(ang)
    zeros = jnp.zeros((s_len, comp_dim - rot_dim), F32)
    ones = jnp.ones((s_len, comp_dim - rot_dim), F32)
    reps = LANES // comp_dim
    cos_t = jnp.tile(jnp.concatenate([cos, cos, ones], axis=-1), (1, reps))
    sin_up = jnp.tile(jnp.concatenate([-sin, jnp.zeros_like(sin), zeros], axis=-1), (1, reps))
    sin_dn = jnp.tile(jnp.concatenate([jnp.zeros_like(sin), sin, zeros], axis=-1), (1, reps))
    return cos_t, sin_up, sin_dn


def kernel(x, norm_g, w_in, fox_fb, diff_lam, diff_norm_g, gla_wa2, gla_ba, gla_norm_g, w_out,
           final_norm_g):
    batch, s_len, d_model = x.shape
    depth = w_in.shape[0]
    assert s_len % Q_TILE == 0 and Q_TILE == K_TILE == MOBA_BLOCK
    assert s_len // MOBA_BLOCK <= GATE_ROWS
    t = batch * s_len
    tm_in = min(1024, t)
    tn_in = 1536
    tm_out = min(512, t)

    w_main, w_small = _relayout_w_in(w_in)
    w_out_b = w_out.astype(BF16)
    fb_pad = jnp.pad(fox_fb, ((0, 0), (0, LANES - N_HEADS)))[:, None, :]
    wa_pad = jnp.pad(gla_wa2, ((0, 0), (N_HEADS, LANES - N_HEADS - GLA_RANK), (0, 0))).astype(BF16)
    ba_dup = gla_ba[:, None, :]
    rope_diff = _rope_tables(s_len, DIFF_QK_DIM, DIFF_QK_DIM // ROPE_FRACTION)
    rope_moba = _rope_tables(s_len, HEAD_DIM, HEAD_DIM // ROPE_FRACTION)

    x2 = x.reshape(t, d_model)
    for l in range(depth):
        proj, small = _inproj(x2, norm_g[l], w_main, w_small, l, tm=tm_in, tn=tn_in)
        fk_rep, fq_rows, p_cum = _gates(small, fb_pad[l], wa_pad[l], ba_dup[l],
                                        batch=batch, s_len=s_len)
        lam_init = 0.8 - 0.6 * math.exp(-0.3 * l)
        mixers = (
            _fox(proj, fk_rep, fq_rows, batch=batch, s_len=s_len),
            _diff(proj, diff_lam[l], diff_norm_g[l], rope_diff, lam_init, batch=batch, s_len=s_len),
            _moba(proj, rope_moba, batch=batch, s_len=s_len),
            _gla(proj, p_cum, gla_norm_g[l], batch=batch, s_len=s_len),
        )
        x2 = _outproj(x2, mixers, w_out_b, l, final_norm_g, final=(l == depth - 1), tm=tm_out)
    return x2.reshape(batch, s_len, d_model)
```

```python
import functools
import math

import jax
import jax.numpy as jnp
from jax import lax
from jax.experimental import pallas as pl
from jax.experimental.pallas import tpu as pltpu

F32 = jnp.float32
BF16 = jnp.bfloat16

LANES = 128
HEAD_DIM = 128
N_HEADS = 4
GROUP_W = N_HEADS * HEAD_DIM
DIFF_QK_DIM = HEAD_DIM // 2
MOBA_BLOCK = 256
MOBA_TOPK = 3
GLA_DK = HEAD_DIM // 2
GLA_RANK = 16
GLA_TAU = 16.0
GLA_CHUNK = 64
ROPE_THETA = 500000.0
ROPE_FRACTION = 4
RMS_EPS = 1e-6
MASKED_LOGIT = -1e30

_SEGMENTS = (
    ("fox_q", GROUP_W), ("fox_k", GROUP_W), ("fox_v", GROUP_W),
    ("fox_f", N_HEADS), ("fox_g", GROUP_W),
    ("diff_q", GROUP_W), ("diff_k", GROUP_W), ("diff_v", GROUP_W), ("diff_g", GROUP_W),
    ("moba_q", GROUP_W), ("moba_k", GROUP_W), ("moba_v", GROUP_W), ("moba_g", GROUP_W),
    ("gla_q", N_HEADS * GLA_DK), ("gla_k", N_HEADS * GLA_DK), ("gla_v", GROUP_W),
    ("gla_a", GLA_RANK), ("gla_g", GROUP_W),
)
_SEG_OFF = {}
_off = 0
for _name, _w in _SEGMENTS:
    _SEG_OFF[_name] = (_off, _w)
    _off += _w

_SLABS = ("fox_q", "fox_k", "fox_v", "fox_g", "diff_q", "diff_k", "diff_v", "diff_g",
          "moba_q", "moba_k", "moba_v", "moba_g", "gla_qk", "gla_v", "gla_g")
_SLAB = {name: i for i, name in enumerate(_SLABS)}
N_MAIN = len(_SLABS) * GROUP_W
_MAIN_RUNS = ((0, _SEG_OFF["fox_f"][0]),
              (_SEG_OFF["fox_g"][0], _SEG_OFF["gla_a"][0]),
              (_SEG_OFF["gla_g"][0], _off))
assert sum(b - a for a, b in _MAIN_RUNS) == N_MAIN

Q_TILE = 256
K_TILE = 256
CUM_BLOCK = 128
GATE_ROWS = 16
GLA_GROUP = 4
ONES_ROWS = 16
LOG2E = math.log2(math.e)


def _silu(x):
    return x * (1.0 / (1.0 + jnp.exp(-x)))


def _log_sigmoid(x):
    return jnp.minimum(x, 0.0) - jnp.log1p(jnp.exp(-jnp.abs(x)))


def _nt_dot(a, b):
    return lax.dot_general(a, b, (((1,), (1,)), ((), ())), preferred_element_type=F32)


def _inproj_kernel(x_ref, g_ref, w_ref, ws_ref, o_ref, small_ref, h_scr):
    @pl.when(pl.program_id(1) == 0)
    def _():
        x = x_ref[...]
        y = x * lax.rsqrt(jnp.mean(x * x, axis=-1, keepdims=True) + RMS_EPS) * g_ref[...]
        h = y.astype(BF16)
        h_scr[...] = h
        small_ref[...] = _nt_dot(h, ws_ref[...])

    acc = _nt_dot(h_scr[...], w_ref[...])
    for c in range(o_ref.shape[0]):
        o_ref[c] = acc[:, c * LANES:(c + 1) * LANES].astype(BF16)


def _inproj(x2, norm_g, w_main, w_small, layer, *, tm, tn):
    t, d = x2.shape
    n_main = w_main.shape[1]
    return pl.pallas_call(
        _inproj_kernel,
        grid=(t // tm, n_main // tn),
        in_specs=[
            pl.BlockSpec((tm, d), lambda i, j: (i, 0)),
            pl.BlockSpec((1, d), lambda i, j: (0, 0)),
            pl.BlockSpec((None, tn, d), lambda i, j: (layer, j, 0)),
            pl.BlockSpec((None, LANES, d), lambda i, j: (layer, 0, 0)),
        ],
        out_specs=[
            pl.BlockSpec((tn // LANES, tm, LANES), lambda i, j: (j, i, 0)),
            pl.BlockSpec((tm, LANES), lambda i, j: (i, 0)),
        ],
        out_shape=[
            jax.ShapeDtypeStruct((n_main // LANES, t, LANES), BF16),
            jax.ShapeDtypeStruct((t, LANES), F32),
        ],
        scratch_shapes=[pltpu.VMEM((tm, d), BF16)],
        compiler_params=pltpu.CompilerParams(
            dimension_semantics=("arbitrary", "arbitrary"),
            vmem_limit_bytes=56 * 1024 * 1024),
        name="inproj",
    )(x2, norm_g.reshape(1, d), w_main, w_small)


def _gates_kernel(small_ref, fb_ref, wa_ref, ba_ref, fk_ref, fq_ref, p_ref):
    s_len = small_ref.shape[0]
    small = small_ref[...]
    log_f = _log_sigmoid(small + fb_ref[...])
    a_logit = jnp.dot(small.astype(BF16), wa_ref[...], preferred_element_type=F32) + ba_ref[...]
    log_a = _log_sigmoid(a_logit) * (1.0 / GLA_TAU)
    both = jnp.concatenate([log_f, log_a], axis=-1)

    row = lax.broadcasted_iota(jnp.int32, (CUM_BLOCK, CUM_BLOCK), 0)
    col = lax.broadcasted_iota(jnp.int32, (CUM_BLOCK, CUM_BLOCK), 1)
    tri = (col <= row).astype(BF16)
    low_half = lax.broadcasted_iota(jnp.int32, (CUM_BLOCK, LANES), 1) < GLA_DK
    carry = jnp.zeros((1, both.shape[1]), F32)
    for c in range(s_len // CUM_BLOCK):
        blk = both[c * CUM_BLOCK:(c + 1) * CUM_BLOCK]
        hi = blk.astype(BF16)
        rest = blk - hi.astype(F32)
        mid = rest.astype(BF16)
        lo = (rest - mid.astype(F32)).astype(BF16)
        cum = carry + sum(jnp.dot(tri, part, preferred_element_type=F32) for part in (hi, mid, lo))
        carry = cum[CUM_BLOCK - 1:CUM_BLOCK]
        rows = slice(c * CUM_BLOCK, (c + 1) * CUM_BLOCK)
        f_cum = cum[:, :LANES] * LOG2E
        per_tile = Q_TILE // CUM_BLOCK
        fq_ref[c // per_tile, :, (c % per_tile) * CUM_BLOCK:(c % per_tile + 1) * CUM_BLOCK] = (
            f_cum.T[:GATE_ROWS])
        for h in range(N_HEADS):
            fk_ref[h, rows, :] = jnp.broadcast_to(f_cum[:, h:h + 1], (CUM_BLOCK, LANES))
            pair = cum[:, (1 + h // 2) * LANES:(2 + h // 2) * LANES]
            swapped = pltpu.roll(pair, GLA_DK, 1)
            p_ref[h, rows, :] = (jnp.where(low_half, pair, swapped) if h % 2 == 0
                                 else jnp.where(low_half, swapped, pair))


def _gates(small, fb_pad, wa_pad, ba_dup, *, batch, s_len):
    t = small.shape[0]
    return pl.pallas_call(
        _gates_kernel,
        grid=(batch,),
        in_specs=[
            pl.BlockSpec((s_len, LANES), lambda b: (b, 0)),
            pl.BlockSpec((1, LANES), lambda b: (0, 0)),
            pl.BlockSpec((LANES, N_HEADS * GLA_DK), lambda b: (0, 0)),
            pl.BlockSpec((1, N_HEADS * GLA_DK), lambda b: (0, 0)),
        ],
        out_specs=[
            pl.BlockSpec((N_HEADS, s_len, LANES), lambda b: (0, b, 0)),
            pl.BlockSpec((None, s_len // Q_TILE, GATE_ROWS, Q_TILE), lambda b: (b, 0, 0, 0)),
            pl.BlockSpec((N_HEADS, s_len, LANES), lambda b: (0, b, 0)),
        ],
        out_shape=[
            jax.ShapeDtypeStruct((N_HEADS, t, LANES), F32),
            jax.ShapeDtypeStruct((batch, s_len // Q_TILE, GATE_ROWS, Q_TILE), F32),
            jax.ShapeDtypeStruct((N_HEADS, t, LANES), F32),
        ],
        compiler_params=pltpu.CompilerParams(dimension_semantics=("arbitrary",)),
        name="gates",
    )(small, fb_pad, wa_pad, ba_dup)


def _causal_mask_t(tk, width, tq):
    key = lax.broadcasted_iota(jnp.int32, (tk, width), 0)
    qry = lax.broadcasted_iota(jnp.int32, (tk, width), 1)
    if width != tq:
        qry = qry & (tq - 1)
    return key <= qry


def _transpose_values(v_ref, vt_scr):
    dv = v_ref.shape[2]
    for h in range(v_ref.shape[0]):
        for j in range(v_ref.shape[1] // K_TILE):
            blk = v_ref[h, j * K_TILE:(j + 1) * K_TILE, :].astype(F32)
            vt_scr[h, j, :dv, :] = blk.T.astype(BF16)
            vt_scr[h, j, dv:, :] = jnp.ones((ONES_ROWS, K_TILE), BF16)


def _col_max(z):
    rows = z.shape[0]
    while rows > 8:
        rows //= 2
        z = jnp.maximum(z[:rows], z[rows:])
    return jnp.max(z, axis=0, keepdims=True)


def _causal_tiles_t(n_tiles, prepare_fn, logits_fn, vt_fn, finish_fn, tq, z_scr, acc_scr):
    n, _, width = acc_scr.shape

    def update(j, slot, ms, tile_max, masked):
        stats = []
        for s in range(n):
            z = z_scr[slot, s]
            if masked:
                z = jnp.where(_causal_mask_t(K_TILE, width, tq), z, -jnp.inf)
                m_new = jnp.maximum(ms[s], _col_max(z))
            else:
                m_new = jnp.maximum(ms[s], tile_max[s])
            stats.append((m_new, jnp.exp2(ms[s] - m_new), jnp.exp2(z - m_new).astype(BF16)))
        for s in range(n):
            pv = jnp.dot(vt_fn(s, j), stats[s][2], preferred_element_type=F32)
            acc_scr[s] = stats[s][1] * acc_scr[s] + pv
        return tuple(st[0] for st in stats)

    def first_logits(i):
        ctx = prepare_fn(i)
        return ctx, [logits_fn(ctx, s, jnp.int32(0)) for s in range(n)]

    def open_tile(zs):
        for s in range(n):
            z_scr[0, s] = zs[s]
            acc_scr[s] = jnp.zeros(acc_scr.shape[1:], F32)
        return (tuple(jnp.full((1, width), -jnp.inf, F32) for _ in range(n)),
                tuple(_col_max(z) for z in zs))

    def close_tile(i, ms):
        update(i, lax.rem(i, 2), ms, None, True)
        finish_fn(i)

    def off_diagonal(i, ctx, state):
        def body(j, state):
            ms, tile_max = state
            slot = lax.rem(j, 2)
            z_next = [logits_fn(ctx, s, j + 1) for s in range(n)]
            ms = update(j, slot, ms, tile_max, False)
            for s in range(n):
                z_scr[1 - slot, s] = z_next[s]
            return ms, tuple(_col_max(z) for z in z_next)

        return lax.fori_loop(0, i, body, state)[0]

    _, zs = first_logits(jnp.int32(0))
    ms, _ = open_tile(zs)

    def outer(i, ms_prev):
        ctx, zs = first_logits(i)
        close_tile(i - 1, ms_prev)
        return off_diagonal(i, ctx, open_tile(zs))

    ms = lax.fori_loop(1, n_tiles, outer, ms)
    close_tile(jnp.int32(n_tiles - 1), ms)


def _softmax_result_t(acc):
    dv = acc.shape[0] - ONES_ROWS
    return acc[:dv] / acc[dv:dv + 1]


def _rope(x, cos_t, sin_up, sin_dn, half):
    return (x * cos_t + pltpu.roll(x, LANES - half, 1) * sin_up
            + pltpu.roll(x, half, 1) * sin_dn)


def _key_rows(j):
    return pl.ds(pl.multiple_of(j * K_TILE, K_TILE), K_TILE)


def _query_rows(i):
    return pl.ds(pl.multiple_of(i * Q_TILE, Q_TILE), Q_TILE)


def _batch_spec(s_len):
    return lambda slab: pl.BlockSpec((N_HEADS, s_len, LANES), lambda b: (slab, b, 0))


def _fox_kernel(q_ref, k_ref, v_ref, g_ref, fk_ref, fq_ref, o_ref, vt_scr, z_scr, acc_scr):
    scale = HEAD_DIM ** -0.5 * LOG2E
    _transpose_values(v_ref, vt_scr)

    def prepare(i):
        fq_all = fq_ref[i]
        return ([q_ref[h, _query_rows(i), :] for h in range(N_HEADS)],
                [fq_all[h:h + 1, :] for h in range(N_HEADS)])

    def logits(ctx, h, j):
        qs, fqs = ctx
        fk = fk_ref[h, _key_rows(j), :]
        fk = jnp.concatenate([fk] * (Q_TILE // LANES), axis=-1)
        return _nt_dot(k_ref[h, _key_rows(j), :], qs[h]) * scale + (fqs[h] - fk)

    def finish(i):
        for h in range(N_HEADS):
            o = _softmax_result_t(acc_scr[h]).T
            gate = _silu(g_ref[h, _query_rows(i), :].astype(F32))
            o_ref[h, _query_rows(i), :] = (o * gate).astype(BF16)

    _causal_tiles_t(q_ref.shape[1] // Q_TILE, prepare, logits, lambda h, j: vt_scr[h, j], finish,
                    Q_TILE, z_scr, acc_scr)


def _fox(proj, fk_rep, fq_rows, *, batch, s_len):
    t = proj.shape[1]
    nq = s_len // Q_TILE
    nk = s_len // K_TILE
    spec = _batch_spec(s_len)
    return pl.pallas_call(
        _fox_kernel,
        grid=(batch,),
        in_specs=[
            spec(_SLAB["fox_q"]), spec(_SLAB["fox_k"]), spec(_SLAB["fox_v"]), spec(_SLAB["fox_g"]),
            spec(0),
            pl.BlockSpec((None, nq, GATE_ROWS, Q_TILE), lambda b: (b, 0, 0, 0)),
        ],
        out_specs=spec(0),
        out_shape=jax.ShapeDtypeStruct((N_HEADS, t, LANES), BF16),
        scratch_shapes=[pltpu.VMEM((N_HEADS, nk, HEAD_DIM + ONES_ROWS, K_TILE), BF16),
                        pltpu.VMEM((2, N_HEADS, K_TILE, Q_TILE), F32),
                        pltpu.VMEM((N_HEADS, HEAD_DIM + ONES_ROWS, Q_TILE), F32)],
        compiler_params=pltpu.CompilerParams(dimension_semantics=("arbitrary",)),
        name="fox",
    )(proj, proj, proj, proj, fk_rep, fq_rows)


def _diff_kernel(lam_init, q_ref, k_ref, v_ref, g_ref, lam_ref, ng_ref, c_ref, u_ref, d_ref, o_ref,
                 kz_scr, qr_scr, a1_scr, vt_scr, z_scr, acc_scr):
    tq = Q_TILE
    half = DIFF_QK_DIM // ROPE_FRACTION // 2
    scale = DIFF_QK_DIM ** -0.5 * LOG2E

    _transpose_values(v_ref, vt_scr)
    first = lax.broadcasted_iota(jnp.int32, (k_ref.shape[1], LANES), 1) < DIFF_QK_DIM
    for h in range(N_HEADS):
        kr = _rope(k_ref[h].astype(F32), c_ref[...], u_ref[...], d_ref[...], half)
        kz_scr[0, h] = jnp.where(first, kr, 0.0).astype(BF16)
        kz_scr[1, h] = jnp.where(first, 0.0, kr).astype(BF16)
        qr_scr[h] = _rope(q_ref[h].astype(F32), c_ref[...], u_ref[...], d_ref[...], half).astype(BF16)

    lf = lam_ref[...]
    lam = (jnp.exp(jnp.sum(lf[0:1] * lf[1:2], axis=-1, keepdims=True))
           - jnp.exp(jnp.sum(lf[2:3] * lf[3:4], axis=-1, keepdims=True)) + lam_init)

    def prepare(i):
        return [qr_scr[h, _query_rows(i), :] for h in range(N_HEADS)]

    def finish_first(i):
        for h in range(N_HEADS):
            a1_scr[h, _query_rows(i), :] = _softmax_result_t(acc_scr[h]).T

    def finish_second(i):
        for h in range(N_HEADS):
            o = a1_scr[h, _query_rows(i), :] - lam * _softmax_result_t(acc_scr[h]).T
            o = o * lax.rsqrt(jnp.mean(o * o, axis=-1, keepdims=True) + RMS_EPS) * ng_ref[...]
            o = o * (1.0 - lam_init)
            gate = _silu(g_ref[h, _query_rows(i), :].astype(F32))
            o_ref[h, _query_rows(i), :] = (o * gate).astype(BF16)

    for comp, finish in ((0, finish_first), (1, finish_second)):
        def logits(qs, h, j, comp=comp):
            return _nt_dot(kz_scr[comp, h, _key_rows(j), :], qs[h]) * scale

        _causal_tiles_t(q_ref.shape[1] // tq, prepare, logits, lambda h, j: vt_scr[h, j], finish,
                        tq, z_scr, acc_scr)


def _diff(proj, diff_lam, diff_norm_g, tables, lam_init, *, batch, s_len):
    t = proj.shape[1]
    nk = s_len // K_TILE
    spec = _batch_spec(s_len)
    table_spec = pl.BlockSpec((s_len, LANES), lambda b: (0, 0))
    return pl.pallas_call(
        functools.partial(_diff_kernel, lam_init),
        grid=(batch,),
        in_specs=[
            spec(_SLAB["diff_q"]), spec(_SLAB["diff_k"]), spec(_SLAB["diff_v"]), spec(_SLAB["diff_g"]),
            pl.BlockSpec((4, DIFF_QK_DIM), lambda b: (0, 0)),
            pl.BlockSpec((1, HEAD_DIM), lambda b: (0, 0)),
            table_spec, table_spec, table_spec,
        ],
        out_specs=spec(0),
        out_shape=jax.ShapeDtypeStruct((N_HEADS, t, LANES), BF16),
        scratch_shapes=[pltpu.VMEM((2, N_HEADS, s_len, LANES), BF16),
                        pltpu.VMEM((N_HEADS, s_len, LANES), BF16),
                        pltpu.VMEM((N_HEADS, s_len, LANES), F32),
                        pltpu.VMEM((N_HEADS, nk, HEAD_DIM + ONES_ROWS, K_TILE), BF16),
                        pltpu.VMEM((2, N_HEADS, K_TILE, Q_TILE), F32),
                        pltpu.VMEM((N_HEADS, HEAD_DIM + ONES_ROWS, Q_TILE), F32)],
        compiler_params=pltpu.CompilerParams(dimension_semantics=("arbitrary",)),
        name="diff",
    )(proj, proj, proj, proj, diff_lam, diff_norm_g.reshape(1, HEAD_DIM), *tables)


def _moba_kernel(q_ref, k_ref, v_ref, g_ref, c_ref, u_ref, d_ref,
                 o_ref, kr_scr, kmean_scr, vt_scr, z_scr, acc_scr):
    tq = Q_TILE
    half = HEAD_DIM // ROPE_FRACTION // 2
    scale = HEAD_DIM ** -0.5 * LOG2E
    n_blk = k_ref.shape[1] // MOBA_BLOCK

    _transpose_values(v_ref, vt_scr)
    kmean_scr[...] = jnp.zeros(kmean_scr.shape, F32)
    for h in range(N_HEADS):
        kr = _rope(k_ref[h].astype(F32), c_ref[...], u_ref[...], d_ref[...], half)
        kr_scr[h] = kr.astype(BF16)
        for n in range(n_blk):
            blk = kr[n * MOBA_BLOCK:(n + 1) * MOBA_BLOCK]
            kmean_scr[h, n:n + 1, :] = jnp.sum(blk, axis=0, keepdims=True) * (1.0 / MOBA_BLOCK)

    blk_id = lax.broadcasted_iota(jnp.int32, (GATE_ROWS, tq), 0).astype(F32)

    def prepare(i):
        rows = _query_rows(i)
        past = blk_id < lax.convert_element_type(i, F32)
        qs, biases = [], []
        for h in range(N_HEADS):
            q = _rope(q_ref[h, rows, :].astype(F32), c_ref[rows, :], u_ref[rows, :], d_ref[rows, :],
                      half).astype(BF16)
            gate = _nt_dot(kmean_scr[h].astype(BF16), q)
            gate = jnp.where(past, gate, -jnp.inf)
            open_ = past
            sel = jnp.zeros((GATE_ROWS, tq), jnp.bool_)
            for _ in range(MOBA_TOPK):
                top = jnp.max(gate, axis=0, keepdims=True)
                idx = jnp.min(jnp.where(gate == top, blk_id, float(GATE_ROWS)), axis=0, keepdims=True)
                pick = (blk_id == idx) & open_
                sel = sel | pick
                open_ = open_ & jnp.logical_not(pick)
                gate = jnp.where(pick, -jnp.inf, gate)
            qs.append(q)
            biases.append(jnp.where(sel, 0.0, MASKED_LOGIT))
        return i, qs, biases

    def logits(ctx, h, j):
        i, qs, biases = ctx
        here = blk_id == lax.convert_element_type(j, F32)
        bias = jnp.sum(jnp.where(here, biases[h], 0.0), axis=0, keepdims=True)
        bias = jnp.where(j == i, 0.0, bias)
        return _nt_dot(kr_scr[h, _key_rows(j), :], qs[h]) * scale + bias

    def finish(i):
        for h in range(N_HEADS):
            o = _softmax_result_t(acc_scr[h]).T
            gate = _silu(g_ref[h, _query_rows(i), :].astype(F32))
            o_ref[h, _query_rows(i), :] = (o * gate).astype(BF16)

    _causal_tiles_t(q_ref.shape[1] // tq, prepare, logits, lambda h, j: vt_scr[h, j], finish,
                    tq, z_scr, acc_scr)


def _moba(proj, tables, *, batch, s_len):
    t = proj.shape[1]
    nk = s_len // K_TILE
    spec = _batch_spec(s_len)
    table_spec = pl.BlockSpec((s_len, LANES), lambda b: (0, 0))
    return pl.pallas_call(
        _moba_kernel,
        grid=(batch,),
        in_specs=[
            spec(_SLAB["moba_q"]), spec(_SLAB["moba_k"]), spec(_SLAB["moba_v"]), spec(_SLAB["moba_g"]),
            table_spec, table_spec, table_spec,
        ],
        out_specs=spec(0),
        out_shape=jax.ShapeDtypeStruct((N_HEADS, t, LANES), BF16),
        scratch_shapes=[pltpu.VMEM((N_HEADS, s_len, LANES), BF16),
                        pltpu.VMEM((N_HEADS, GATE_ROWS, LANES), F32),
                        pltpu.VMEM((N_HEADS, nk, HEAD_DIM + ONES_ROWS, K_TILE), BF16),
                        pltpu.VMEM((2, N_HEADS, K_TILE, Q_TILE), F32),
                        pltpu.VMEM((N_HEADS, HEAD_DIM + ONES_ROWS, Q_TILE), F32)],
        compiler_params=pltpu.CompilerParams(dimension_semantics=("arbitrary",)),
        name="moba",
    )(proj, proj, proj, proj, *tables)


def _gla_kernel(x_ref, v_ref, g_ref, p_ref, ng_ref, o_ref):
    s_len = x_ref.shape[1]
    c_len = GLA_CHUNK
    scale = GLA_DK ** -0.5
    lane = lax.broadcasted_iota(jnp.int32, (c_len, LANES), 1)
    qhalf = lane < GLA_DK
    qhalf_row = lax.broadcasted_iota(jnp.int32, (1, LANES), 1) < GLA_DK
    row = lax.broadcasted_iota(jnp.int32, (c_len, c_len), 0)
    col = lax.broadcasted_iota(jnp.int32, (c_len, c_len), 1)
    causal = col <= row

    group = GLA_GROUP
    pairs = [(h, c) for h in range(N_HEADS) for c in range(group)]

    def body(g, carries):
        prep = {}
        for h, c in pairs:
            rows = pl.ds(pl.multiple_of(g * (group * c_len), group * c_len) + c * c_len, c_len)
            qcb = x_ref[h // 2, rows, :].astype(F32)
            kcb = x_ref[2 + h // 2, rows, :].astype(F32)
            if h % 2 == 0:
                x = jnp.where(qhalf, qcb, pltpu.roll(kcb, GLA_DK, 1))
            else:
                x = jnp.where(qhalf, pltpu.roll(qcb, GLA_DK, 1), kcb)
            p = p_ref[h, rows, :]
            base = carries[h][1] if c == 0 else prep[h, c - 1]["p_last"]
            cum = p - base
            xt = x * jnp.exp(jnp.where(qhalf, cum, -cum))
            kr = pltpu.roll(xt, GLA_DK, 1)
            decay = jnp.where(qhalf_row, jnp.exp(cum[c_len - 1:c_len]), 0.0)
            prep[h, c] = dict(
                rows=rows, qz=jnp.where(qhalf, xt, 0.0).astype(BF16), kr=kr.astype(BF16),
                khat=(jnp.where(qhalf, kr, 0.0) * decay).astype(BF16),
                decay=decay, v=v_ref[h, rows, :], p_last=p[c_len - 1:c_len])
        scores, upd = {}, {}
        for h, c in pairs:
            d = prep[h, c]
            scores[h, c] = _nt_dot(d["qz"], d["kr"])
            upd[h, c] = lax.dot_general(d["v"], d["khat"], (((0,), (0,)), ((), ())),
                                        preferred_element_type=F32)
        states = {}
        for h in range(N_HEADS):
            states[h, 0] = carries[h][0]
            for c in range(group):
                states[h, c + 1] = states[h, c] * prep[h, c]["decay"] + upd[h, c]
        outs = {}
        for h, c in pairs:
            sc = jnp.where(causal, scores[h, c] * scale, 0.0).astype(BF16)
            outs[h, c] = (jnp.dot(sc, prep[h, c]["v"], preferred_element_type=F32),
                          _nt_dot(prep[h, c]["qz"], states[h, c].astype(BF16)))
        for h, c in pairs:
            o = outs[h, c][0] + outs[h, c][1] * scale
            y = o * lax.rsqrt(jnp.mean(o * o, axis=-1, keepdims=True) + RMS_EPS) * ng_ref[...]
            rows = prep[h, c]["rows"]
            o_ref[h, rows, :] = (y * _silu(g_ref[h, rows, :].astype(F32))).astype(BF16)
        return tuple((states[h, group], prep[h, group - 1]["p_last"]) for h in range(N_HEADS))

    init = tuple((jnp.zeros((HEAD_DIM, LANES), F32), jnp.zeros((1, LANES), F32))
                 for _ in range(N_HEADS))
    lax.fori_loop(0, s_len // (group * c_len), body, init)


def _gla(proj, p_cum, gla_norm_g, *, batch, s_len):
    t = proj.shape[1]
    spec = lambda slab: pl.BlockSpec((N_HEADS, s_len, LANES), lambda b: (slab, b, 0))
    return pl.pallas_call(
        _gla_kernel,
        grid=(batch,),
        in_specs=[
            spec(_SLAB["gla_qk"]), spec(_SLAB["gla_v"]), spec(_SLAB["gla_g"]),
            pl.BlockSpec((N_HEADS, s_len, LANES), lambda b: (0, b, 0)),
            pl.BlockSpec((1, HEAD_DIM), lambda b: (0, 0)),
        ],
        out_specs=pl.BlockSpec((N_HEADS, s_len, LANES), lambda b: (0, b, 0)),
        out_shape=jax.ShapeDtypeStruct((N_HEADS, t, LANES), BF16),
        compiler_params=pltpu.CompilerParams(dimension_semantics=("arbitrary",)),
        name="gla",
    )(proj, proj, proj, p_cum, gla_norm_g.reshape(1, HEAD_DIM))


def _outproj_kernel(final, x_ref, a_ref, b_ref, c_ref, d_ref, w_ref, fg_ref, o_ref):
    parts = [r[h] for r in (a_ref, b_ref, c_ref, d_ref) for h in range(N_HEADS)]
    mixed = jnp.concatenate(parts, axis=-1)
    y = x_ref[...] + jnp.dot(mixed, w_ref[...], preferred_element_type=F32)
    if final:
        y = y * lax.rsqrt(jnp.mean(y * y, axis=-1, keepdims=True) + RMS_EPS) * fg_ref[...]
    o_ref[...] = y


def _outproj(x2, mixers, w_out, layer, final_g, *, final, tm):
    t, d = x2.shape
    mspec = pl.BlockSpec((N_HEADS, tm, LANES), lambda i: (0, i, 0))
    return pl.pallas_call(
        functools.partial(_outproj_kernel, final),
        grid=(t // tm,),
        in_specs=[
            pl.BlockSpec((tm, d), lambda i: (i, 0)),
            mspec, mspec, mspec, mspec,
            pl.BlockSpec((None,) + w_out.shape[1:], lambda i: (layer, 0, 0)),
            pl.BlockSpec((1, d), lambda i: (0, 0)),
        ],
        out_specs=pl.BlockSpec((tm, d), lambda i: (i, 0)),
        out_shape=jax.ShapeDtypeStruct((t, d), F32),
        compiler_params=pltpu.CompilerParams(
            dimension_semantics=("arbitrary",), vmem_limit_bytes=56 * 1024 * 1024),
        name="outproj",
    )(x2, *mixers, w_out, final_g.reshape(1, d))


RELAYOUT_ROWS = 256
RELAYOUT_TAIL = 32


def _relayout_kernel(a_ref, b_ref, o_ref):
    r = pl.program_id(0)
    depth, rows, _ = o_ref.shape
    for l in range(depth):
        x = jnp.concatenate([a_ref[:, l, :], b_ref[:, l, :]], axis=0)
        out_lo = 0
        for (src_lo, src_hi) in _MAIN_RUNS:
            shift = src_lo - out_lo
            out_hi = out_lo + (src_hi - src_lo)
            assert shift <= RELAYOUT_TAIL and out_lo % rows == 0 and out_hi % rows == 0

            @pl.when((r >= out_lo // rows) & (r < out_hi // rows))
            def _(shift=shift, l=l, x=x):
                o_ref[l] = x[shift:shift + rows].astype(BF16)

            out_lo = out_hi


def _relayout_w_in(w_in):
    depth, d, d_in = w_in.shape
    wt = jnp.transpose(w_in, (2, 0, 1))
    rows, tail = RELAYOUT_ROWS, RELAYOUT_TAIL
    main = pl.pallas_call(
        _relayout_kernel,
        grid=(N_MAIN // rows,),
        in_specs=[pl.BlockSpec((rows, depth, d), lambda r: (r, 0, 0)),
                  pl.BlockSpec((tail, depth, d), lambda r: ((r + 1) * (rows // tail), 0, 0))],
        out_specs=pl.BlockSpec((depth, rows, d), lambda r: (0, r, 0)),
        out_shape=jax.ShapeDtypeStruct((depth, N_MAIN, d), BF16),
        compiler_params=pltpu.CompilerParams(dimension_semantics=("arbitrary",)),
        name="relayout",
    )(wt, wt)
    f_off, a_off = _SEG_OFF["fox_f"][0], _SEG_OFF["gla_a"][0]
    gate_cols = jnp.concatenate([w_in[..., f_off:f_off + N_HEADS],
                                 w_in[..., a_off:a_off + GLA_RANK]], axis=-1)
    small = jnp.pad(jnp.transpose(gate_cols, (0, 2, 1)),
                    ((0, 0), (0, LANES - N_HEADS - GLA_RANK), (0, 0))).astype(BF16)
    return main, small


def _rope_tables(s_len, comp_dim, rot_dim):
    half = rot_dim // 2
    inv_freq = ROPE_THETA ** (-jnp.arange(0, rot_dim, 2, dtype=F32) / rot_dim)
    ang = jnp.arange(s_len, dtype=F32)[:, None] * inv_freq[None, :]
    cos, sin = jnp.cos(ang), jnp.sin(ang)
    zeros = jnp.zeros((s_len, comp_dim - rot_dim), F32)
    ones = jnp.ones((s_len, comp_dim - rot_dim), F32)
    reps = LANES // comp_dim
    cos_t = jnp.tile(jnp.concatenate([cos, cos, ones], axis=-1), (1, reps))
    sin_up = jnp.tile(jnp.concatenate([-sin, jnp.zeros_like(sin), zeros], axis=-1), (1, reps))
    sin_dn = jnp.tile(jnp.concatenate([jnp.zeros_like(sin), sin, zeros], axis=-1), (1, reps))
    return cos_t, sin_up, sin_dn


def kernel(x, norm_g, w_in, fox_fb, diff_lam, diff_norm_g, gla_wa2, gla_ba, gla_norm_g, w_out,
           final_norm_g):
    batch, s_len, d_model = x.shape
    depth = w_in.shape[0]
    assert s_len % Q_TILE == 0 and Q_TILE == K_TILE == MOBA_BLOCK
    assert s_len // MOBA_BLOCK <= GATE_ROWS
    t = batch * s_len
    tm_in = min(1024, t)
    tn_in = 1536
    tm_out = min(512, t)

    w_main, w_small = _relayout_w_in(w_in)
    w_out_b = w_out.astype(BF16)
    fb_pad = jnp.pad(fox_fb, ((0, 0), (0, LANES - N_HEADS)))[:, None, :]
    wa_pad = jnp.pad(gla_wa2, ((0, 0), (N_HEADS, LANES - N_HEADS - GLA_RANK), (0, 0))).astype(BF16)
    ba_dup = gla_ba[:, None, :]
    rope_diff = _rope_tables(s_len, DIFF_QK_DIM, DIFF_QK_DIM // ROPE_FRACTION)
    rope_moba = _rope_tables(s_len, HEAD_DIM, HEAD_DIM // ROPE_FRACTION)

    x2 = x.reshape(t, d_model)
    for l in range(depth):
        proj, small = _inproj(x2, norm_g[l], w_main, w_small, l, tm=tm_in, tn=tn_in)
        fk_rep, fq_rows, p_cum = _gates(small, fb_pad[l], wa_pad[l], ba_dup[l],
                                        batch=batch, s_len=s_len)
        lam_init = 0.8 - 0.6 * math.exp(-0.3 * l)
        mixers = (
            _fox(proj, fk_rep, fq_rows, batch=batch, s_len=s_len),
            _diff(proj, diff_lam[l], diff_norm_g[l], rope_diff, lam_init, batch=batch, s_len=s_len),
            _moba(proj, rope_moba, batch=batch, s_len=s_len),
            _gla(proj, p_cum, gla_norm_g[l], batch=batch, s_len=s_len),
        )
        x2 = _outproj(x2, mixers, w_out_b, l, final_norm_g, final=(l == depth - 1), tm=tm_out)
    return x2.reshape(batch, s_len, d_model)
```

```python
import functools
import math

import jax
import jax.numpy as jnp
from jax import lax
from jax.experimental import pallas as pl
from jax.experimental.pallas import tpu as pltpu

F32 = jnp.float32
BF16 = jnp.bfloat16

LANES = 128
HEAD_DIM = 128
N_HEADS = 4
GROUP_W = N_HEADS * HEAD_DIM
DIFF_QK_DIM = HEAD_DIM // 2
MOBA_BLOCK = 256
MOBA_TOPK = 3
GLA_DK = HEAD_DIM // 2
GLA_RANK = 16
GLA_TAU = 16.0
GLA_CHUNK = 64
ROPE_THETA = 500000.0
ROPE_FRACTION = 4
RMS_EPS = 1e-6
MASKED_LOGIT = -1e30

_SEGMENTS = (
    ("fox_q", GROUP_W), ("fox_k", GROUP_W), ("fox_v", GROUP_W),
    ("fox_f", N_HEADS), ("fox_g", GROUP_W),
    ("diff_q", GROUP_W), ("diff_k", GROUP_W), ("diff_v", GROUP_W), ("diff_g", GROUP_W),
    ("moba_q", GROUP_W), ("moba_k", GROUP_W), ("moba_v", GROUP_W), ("moba_g", GROUP_W),
    ("gla_q", N_HEADS * GLA_DK), ("gla_k", N_HEADS * GLA_DK), ("gla_v", GROUP_W),
    ("gla_a", GLA_RANK), ("gla_g", GROUP_W),
)
_SEG_OFF = {}
_off = 0
for _name, _w in _SEGMENTS:
    _SEG_OFF[_name] = (_off, _w)
    _off += _w

_SLABS = ("fox_q", "fox_k", "fox_v", "fox_g", "diff_q", "diff_k", "diff_v", "diff_g",
          "moba_q", "moba_k", "moba_v", "moba_g", "gla_qk", "gla_v", "gla_g")
_SLAB = {name: i for i, name in enumerate(_SLABS)}
N_MAIN = len(_SLABS) * GROUP_W
_MAIN_RUNS = ((0, _SEG_OFF["fox_f"][0]),
              (_SEG_OFF["fox_g"][0], _SEG_OFF["gla_a"][0]),
              (_SEG_OFF["gla_g"][0], _off))
assert sum(b - a for a, b in _MAIN_RUNS) == N_MAIN

Q_TILE = 256
K_TILE = 256
CUM_BLOCK = 128
GATE_ROWS = 16
GLA_GROUP = 4
ONES_ROWS = 16
LOG2E = math.log2(math.e)
_QUERY_GAIN = {"fox_q": HEAD_DIM ** -0.5 * LOG2E, "diff_q": DIFF_QK_DIM ** -0.5 * LOG2E,
               "moba_q": HEAD_DIM ** -0.5 * LOG2E}


def _silu(x):
    return x * (1.0 / (1.0 + jnp.exp(-x)))


def _log_sigmoid(x):
    return jnp.minimum(x, 0.0) - jnp.log1p(jnp.exp(-jnp.abs(x)))


def _nt_dot(a, b):
    return lax.dot_general(a, b, (((1,), (1,)), ((), ())), preferred_element_type=F32)


def _inproj_kernel(x_ref, g_ref, w_ref, ws_ref, o_ref, small_ref, h_scr):
    @pl.when(pl.program_id(1) == 0)
    def _():
        x = x_ref[...]
        y = x * lax.rsqrt(jnp.mean(x * x, axis=-1, keepdims=True) + RMS_EPS) * g_ref[...]
        h = y.astype(BF16)
        h_scr[...] = h
        small_ref[...] = _nt_dot(h, ws_ref[...])

    acc = _nt_dot(h_scr[...], w_ref[...])
    for c in range(o_ref.shape[0]):
        o_ref[c] = acc[:, c * LANES:(c + 1) * LANES].astype(BF16)


def _inproj(x2, norm_g, w_main, w_small, layer, *, tm, tn):
    t, d = x2.shape
    n_main = w_main.shape[1]
    return pl.pallas_call(
        _inproj_kernel,
        grid=(t // tm, n_main // tn),
        in_specs=[
            pl.BlockSpec((tm, d), lambda i, j: (i, 0)),
            pl.BlockSpec((1, d), lambda i, j: (0, 0)),
            pl.BlockSpec((None, tn, d), lambda i, j: (layer, j, 0)),
            pl.BlockSpec((None, LANES, d), lambda i, j: (layer, 0, 0)),
        ],
        out_specs=[
            pl.BlockSpec((tn // LANES, tm, LANES), lambda i, j: (j, i, 0)),
            pl.BlockSpec((tm, LANES), lambda i, j: (i, 0)),
        ],
        out_shape=[
            jax.ShapeDtypeStruct((n_main // LANES, t, LANES), BF16),
            jax.ShapeDtypeStruct((t, LANES), F32),
        ],
        scratch_shapes=[pltpu.VMEM((tm, d), BF16)],
        compiler_params=pltpu.CompilerParams(
            dimension_semantics=("arbitrary", "arbitrary"),
            vmem_limit_bytes=56 * 1024 * 1024),
        name="inproj",
    )(x2, norm_g.reshape(1, d), w_main, w_small)


def _gates_kernel(small_ref, fb_ref, wa_ref, ba_ref, fk_ref, fq_ref, p_ref):
    s_len = small_ref.shape[0]
    small = small_ref[...]
    log_f = _log_sigmoid(small + fb_ref[...])
    a_logit = jnp.dot(small.astype(BF16), wa_ref[...], preferred_element_type=F32) + ba_ref[...]
    log_a = _log_sigmoid(a_logit) * (1.0 / GLA_TAU)
    both = jnp.concatenate([log_f, log_a], axis=-1)

    row = lax.broadcasted_iota(jnp.int32, (CUM_BLOCK, CUM_BLOCK), 0)
    col = lax.broadcasted_iota(jnp.int32, (CUM_BLOCK, CUM_BLOCK), 1)
    tri = (col <= row).astype(BF16)
    low_half = lax.broadcasted_iota(jnp.int32, (CUM_BLOCK, LANES), 1) < GLA_DK
    carry = jnp.zeros((1, both.shape[1]), F32)
    for c in range(s_len // CUM_BLOCK):
        blk = both[c * CUM_BLOCK:(c + 1) * CUM_BLOCK]
        hi = blk.astype(BF16)
        rest = blk - hi.astype(F32)
        mid = rest.astype(BF16)
        lo = (rest - mid.astype(F32)).astype(BF16)
        cum = carry + sum(jnp.dot(tri, part, preferred_element_type=F32) for part in (hi, mid, lo))
        carry = cum[CUM_BLOCK - 1:CUM_BLOCK]
        rows = slice(c * CUM_BLOCK, (c + 1) * CUM_BLOCK)
        f_cum = cum[:, :LANES] * LOG2E
        per_tile = Q_TILE // CUM_BLOCK
        fq_ref[c // per_tile, :, (c % per_tile) * CUM_BLOCK:(c % per_tile + 1) * CUM_BLOCK] = (
            f_cum.T[:GATE_ROWS])
        for h in range(N_HEADS):
            fk_ref[h, rows, :] = jnp.broadcast_to(f_cum[:, h:h + 1], (CUM_BLOCK, LANES))
            pair = cum[:, (1 + h // 2) * LANES:(2 + h // 2) * LANES]
            swapped = pltpu.roll(pair, GLA_DK, 1)
            p_ref[h, rows, :] = (jnp.where(low_half, pair, swapped) if h % 2 == 0
                                 else jnp.where(low_half, swapped, pair))


def _gates(small, fb_pad, wa_pad, ba_dup, *, batch, s_len):
    t = small.shape[0]
    return pl.pallas_call(
        _gates_kernel,
        grid=(batch,),
        in_specs=[
            pl.BlockSpec((s_len, LANES), lambda b: (b, 0)),
            pl.BlockSpec((1, LANES), lambda b: (0, 0)),
            pl.BlockSpec((LANES, N_HEADS * GLA_DK), lambda b: (0, 0)),
            pl.BlockSpec((1, N_HEADS * GLA_DK), lambda b: (0, 0)),
        ],
        out_specs=[
            pl.BlockSpec((N_HEADS, s_len, LANES), lambda b: (0, b, 0)),
            pl.BlockSpec((None, s_len // Q_TILE, GATE_ROWS, Q_TILE), lambda b: (b, 0, 0, 0)),
            pl.BlockSpec((N_HEADS, s_len, LANES), lambda b: (0, b, 0)),
        ],
        out_shape=[
            jax.ShapeDtypeStruct((N_HEADS, t, LANES), F32),
            jax.ShapeDtypeStruct((batch, s_len // Q_TILE, GATE_ROWS, Q_TILE), F32),
            jax.ShapeDtypeStruct((N_HEADS, t, LANES), F32),
        ],
        compiler_params=pltpu.CompilerParams(dimension_semantics=("arbitrary",)),
        name="gates",
    )(small, fb_pad, wa_pad, ba_dup)


def _causal_mask_t(tk, width, tq):
    key = lax.broadcasted_iota(jnp.int32, (tk, width), 0)
    qry = lax.broadcasted_iota(jnp.int32, (tk, width), 1)
    if width != tq:
        qry = qry & (tq - 1)
    return key <= qry


def _transpose_values(v_ref, vt_scr):
    dv = v_ref.shape[2]
    for h in range(v_ref.shape[0]):
        for j in range(v_ref.shape[1] // K_TILE):
            blk = v_ref[h, j * K_TILE:(j + 1) * K_TILE, :].astype(F32)
            vt_scr[h, j, :dv, :] = blk.T.astype(BF16)
            vt_scr[h, j, dv:, :] = jnp.ones((ONES_ROWS, K_TILE), BF16)


def _col_max(z):
    rows = z.shape[0]
    while rows > 8:
        rows //= 2
        z = jnp.maximum(z[:rows], z[rows:])
    return jnp.max(z, axis=0, keepdims=True)


def _causal_tiles_t(n_tiles, prepare_fn, logits_fn, vt_fn, finish_fn, tq, z_scr, acc_scr,
                    offset_fn=None):
    n, _, width = acc_scr.shape

    def col_max(z, s, offs):
        m = _col_max(z)
        return m if offs is None else m + offs[s]

    def offsets(i):
        return None if offset_fn is None else offset_fn(i)

    def update(j, slot, ms, tile_max, offs, masked):
        stats = []
        for s in range(n):
            z = z_scr[slot, s]
            if masked:
                z = jnp.where(_causal_mask_t(K_TILE, width, tq), z, -jnp.inf)
                m_new = jnp.maximum(ms[s], col_max(z, s, offs))
            else:
                m_new = jnp.maximum(ms[s], tile_max[s])
            shift = m_new if offs is None else m_new - offs[s]
            stats.append((m_new, jnp.exp2(ms[s] - m_new), jnp.exp2(z - shift).astype(BF16)))
        for s in range(n):
            pv = jnp.dot(vt_fn(s, j), stats[s][2], preferred_element_type=F32)
            acc_scr[s] = stats[s][1] * acc_scr[s] + pv
        return tuple(st[0] for st in stats)

    def first_logits(i):
        ctx = prepare_fn(i)
        return ctx, [logits_fn(ctx, s, jnp.int32(0)) for s in range(n)]

    def open_tile(zs, offs):
        for s in range(n):
            z_scr[0, s] = zs[s]
            acc_scr[s] = jnp.zeros(acc_scr.shape[1:], F32)
        return (tuple(jnp.full((1, width), -jnp.inf, F32) for _ in range(n)),
                tuple(col_max(z, s, offs) for s, z in enumerate(zs)))

    def close_tile(i, ms):
        update(i, lax.rem(i, 2), ms, None, offsets(i), True)
        finish_fn(i)

    def off_diagonal(i, ctx, offs, state):
        def body(j, state):
            ms, tile_max = state
            slot = lax.rem(j, 2)
            z_next = [logits_fn(ctx, s, j + 1) for s in range(n)]
            ms = update(j, slot, ms, tile_max, offs, False)
            for s in range(n):
                z_scr[1 - slot, s] = z_next[s]
            return ms, tuple(col_max(z, s, offs) for s, z in enumerate(z_next))

        return lax.fori_loop(0, i, body, state)[0]

    _, zs = first_logits(jnp.int32(0))
    ms, _ = open_tile(zs, offsets(jnp.int32(0)))

    def outer(i, ms_prev):
        ctx, zs = first_logits(i)
        close_tile(i - 1, ms_prev)
        offs = offsets(i)
        return off_diagonal(i, ctx, offs, open_tile(zs, offs))

    ms = lax.fori_loop(1, n_tiles, outer, ms)
    close_tile(jnp.int32(n_tiles - 1), ms)


def _softmax_result_t(acc):
    dv = acc.shape[0] - ONES_ROWS
    return acc[:dv] / acc[dv:dv + 1]


def _rope(x, cos_t, sin_up, sin_dn, half):
    return (x * cos_t + pltpu.roll(x, LANES - half, 1) * sin_up
            + pltpu.roll(x, half, 1) * sin_dn)


def _key_rows(j):
    return pl.ds(pl.multiple_of(j * K_TILE, K_TILE), K_TILE)


def _query_rows(i):
    return pl.ds(pl.multiple_of(i * Q_TILE, Q_TILE), Q_TILE)


def _batch_spec(s_len):
    return lambda slab: pl.BlockSpec((N_HEADS, s_len, LANES), lambda b: (slab, b, 0))


def _fox_kernel(q_ref, k_ref, v_ref, g_ref, fk_ref, fq_ref, o_ref, vt_scr, z_scr, acc_scr):
    _transpose_values(v_ref, vt_scr)

    def prepare(i):
        return [q_ref[h, _query_rows(i), :] for h in range(N_HEADS)]

    def query_gate(i):
        fq_all = fq_ref[i]
        return [fq_all[h:h + 1, :] for h in range(N_HEADS)]

    def logits(qs, h, j):
        fk = fk_ref[h, _key_rows(j), :]
        fk = jnp.concatenate([fk] * (Q_TILE // LANES), axis=-1)
        return _nt_dot(k_ref[h, _key_rows(j), :], qs[h]) - fk

    def finish(i):
        for h in range(N_HEADS):
            o = _softmax_result_t(acc_scr[h]).T
            gate = _silu(g_ref[h, _query_rows(i), :].astype(F32))
            o_ref[h, _query_rows(i), :] = (o * gate).astype(BF16)

    _causal_tiles_t(q_ref.shape[1] // Q_TILE, prepare, logits, lambda h, j: vt_scr[h, j], finish,
                    Q_TILE, z_scr, acc_scr, offset_fn=query_gate)


def _fox(proj, fk_rep, fq_rows, *, batch, s_len):
    t = proj.shape[1]
    nq = s_len // Q_TILE
    nk = s_len // K_TILE
    spec = _batch_spec(s_len)
    return pl.pallas_call(
        _fox_kernel,
        grid=(batch,),
        in_specs=[
            spec(_SLAB["fox_q"]), spec(_SLAB["fox_k"]), spec(_SLAB["fox_v"]), spec(_SLAB["fox_g"]),
            spec(0),
            pl.BlockSpec((None, nq, GATE_ROWS, Q_TILE), lambda b: (b, 0, 0, 0)),
        ],
        out_specs=spec(0),
        out_shape=jax.ShapeDtypeStruct((N_HEADS, t, LANES), BF16),
        scratch_shapes=[pltpu.VMEM((N_HEADS, nk, HEAD_DIM + ONES_ROWS, K_TILE), BF16),
                        pltpu.VMEM((2, N_HEADS, K_TILE, Q_TILE), F32),
                        pltpu.VMEM((N_HEADS, HEAD_DIM + ONES_ROWS, Q_TILE), F32)],
        compiler_params=pltpu.CompilerParams(dimension_semantics=("arbitrary",)),
        name="fox",
    )(proj, proj, proj, proj, fk_rep, fq_rows)


def _diff_kernel(lam_init, q_ref, k_ref, v_ref, g_ref, lam_ref, ng_ref, c_ref, u_ref, d_ref, o_ref,
                 kz_scr, qr_scr, a1_scr, vt_scr, z_scr, acc_scr):
    tq = Q_TILE
    half = DIFF_QK_DIM // ROPE_FRACTION // 2

    _transpose_values(v_ref, vt_scr)
    first = lax.broadcasted_iota(jnp.int32, (k_ref.shape[1], LANES), 1) < DIFF_QK_DIM
    for h in range(N_HEADS):
        kr = _rope(k_ref[h].astype(F32), c_ref[...], u_ref[...], d_ref[...], half)
        kz_scr[0, h] = jnp.where(first, kr, 0.0).astype(BF16)
        kz_scr[1, h] = jnp.where(first, 0.0, kr).astype(BF16)
        qr_scr[h] = _rope(q_ref[h].astype(F32), c_ref[...], u_ref[...], d_ref[...], half).astype(BF16)

    lf = lam_ref[...]
    lam = (jnp.exp(jnp.sum(lf[0:1] * lf[1:2], axis=-1, keepdims=True))
           - jnp.exp(jnp.sum(lf[2:3] * lf[3:4], axis=-1, keepdims=True)) + lam_init)

    def prepare(i):
        return [qr_scr[h, _query_rows(i), :] for h in range(N_HEADS)]

    def finish_first(i):
        for h in range(N_HEADS):
            a1_scr[h, _query_rows(i), :] = _softmax_result_t(acc_scr[h]).T

    def finish_second(i):
        for h in range(N_HEADS):
            o = a1_scr[h, _query_rows(i), :] - lam * _softmax_result_t(acc_scr[h]).T
            o = o * lax.rsqrt(jnp.mean(o * o, axis=-1, keepdims=True) + RMS_EPS) * ng_ref[...]
            o = o * (1.0 - lam_init)
            gate = _silu(g_ref[h, _query_rows(i), :].astype(F32))
            o_ref[h, _query_rows(i), :] = (o * gate).astype(BF16)

    for comp, finish in ((0, finish_first), (1, finish_second)):
        def logits(qs, h, j, comp=comp):
            return _nt_dot(kz_scr[comp, h, _key_rows(j), :], qs[h])

        _causal_tiles_t(q_ref.shape[1] // tq, prepare, logits, lambda h, j: vt_scr[h, j], finish,
                        tq, z_scr, acc_scr)


def _diff(proj, diff_lam, diff_norm_g, tables, lam_init, *, batch, s_len):
    t = proj.shape[1]
    nk = s_len // K_TILE
    spec = _batch_spec(s_len)
    table_spec = pl.BlockSpec((s_len, LANES), lambda b: (0, 0))
    return pl.pallas_call(
        functools.partial(_diff_kernel, lam_init),
        grid=(batch,),
        in_specs=[
            spec(_SLAB["diff_q"]), spec(_SLAB["diff_k"]), spec(_SLAB["diff_v"]), spec(_SLAB["diff_g"]),
            pl.BlockSpec((4, DIFF_QK_DIM), lambda b: (0, 0)),
            pl.BlockSpec((1, HEAD_DIM), lambda b: (0, 0)),
            table_spec, table_spec, table_spec,
        ],
        out_specs=spec(0),
        out_shape=jax.ShapeDtypeStruct((N_HEADS, t, LANES), BF16),
        scratch_shapes=[pltpu.VMEM((2, N_HEADS, s_len, LANES), BF16),
                        pltpu.VMEM((N_HEADS, s_len, LANES), BF16),
                        pltpu.VMEM((N_HEADS, s_len, LANES), F32),
                        pltpu.VMEM((N_HEADS, nk, HEAD_DIM + ONES_ROWS, K_TILE), BF16),
                        pltpu.VMEM((2, N_HEADS, K_TILE, Q_TILE), F32),
                        pltpu.VMEM((N_HEADS, HEAD_DIM + ONES_ROWS, Q_TILE), F32)],
        compiler_params=pltpu.CompilerParams(dimension_semantics=("arbitrary",)),
        name="diff",
    )(proj, proj, proj, proj, diff_lam, diff_norm_g.reshape(1, HEAD_DIM), *tables)


def _moba_kernel(q_ref, k_ref, v_ref, g_ref, c_ref, u_ref, d_ref,
                 o_ref, kr_scr, kmean_scr, vt_scr, z_scr, acc_scr):
    tq = Q_TILE
    half = HEAD_DIM // ROPE_FRACTION // 2
    n_blk = k_ref.shape[1] // MOBA_BLOCK

    _transpose_values(v_ref, vt_scr)
    kmean_scr[...] = jnp.zeros(kmean_scr.shape, F32)
    for h in range(N_HEADS):
        kr = _rope(k_ref[h].astype(F32), c_ref[...], u_ref[...], d_ref[...], half)
        kr_scr[h] = kr.astype(BF16)
        for n in range(n_blk):
            blk = kr[n * MOBA_BLOCK:(n + 1) * MOBA_BLOCK]
            kmean_scr[h, n:n + 1, :] = jnp.sum(blk, axis=0, keepdims=True) * (1.0 / MOBA_BLOCK)

    blk_id = lax.broadcasted_iota(jnp.int32, (GATE_ROWS, tq), 0).astype(F32)

    def prepare(i):
        rows = _query_rows(i)
        past = blk_id < lax.convert_element_type(i, F32)
        qs, biases = [], []
        for h in range(N_HEADS):
            q = _rope(q_ref[h, rows, :].astype(F32), c_ref[rows, :], u_ref[rows, :], d_ref[rows, :],
                      half).astype(BF16)
            gate = _nt_dot(kmean_scr[h].astype(BF16), q)
            gate = jnp.where(past, gate, -jnp.inf)
            open_ = past
            sel = jnp.zeros((GATE_ROWS, tq), jnp.bool_)
            for _ in range(MOBA_TOPK):
                top = jnp.max(gate, axis=0, keepdims=True)
                idx = jnp.min(jnp.where(gate == top, blk_id, float(GATE_ROWS)), axis=0, keepdims=True)
                pick = (blk_id == idx) & open_
                sel = sel | pick
                open_ = open_ & jnp.logical_not(pick)
                gate = jnp.where(pick, -jnp.inf, gate)
            qs.append(q)
            biases.append(jnp.where(sel, 0.0, MASKED_LOGIT))
        return i, qs, biases

    def logits(ctx, h, j):
        i, qs, biases = ctx
        here = blk_id == lax.convert_element_type(j, F32)
        bias = jnp.sum(jnp.where(here, biases[h], 0.0), axis=0, keepdims=True)
        bias = jnp.where(j == i, 0.0, bias)
        return _nt_dot(kr_scr[h, _key_rows(j), :], qs[h]) + bias

    def finish(i):
        for h in range(N_HEADS):
            o = _softmax_result_t(acc_scr[h]).T
            gate = _silu(g_ref[h, _query_rows(i), :].astype(F32))
            o_ref[h, _query_rows(i), :] = (o * gate).astype(BF16)

    _causal_tiles_t(q_ref.shape[1] // tq, prepare, logits, lambda h, j: vt_scr[h, j], finish,
                    tq, z_scr, acc_scr)


def _moba(proj, tables, *, batch, s_len):
    t = proj.shape[1]
    nk = s_len // K_TILE
    spec = _batch_spec(s_len)
    table_spec = pl.BlockSpec((s_len, LANES), lambda b: (0, 0))
    return pl.pallas_call(
        _moba_kernel,
        grid=(batch,),
        in_specs=[
            spec(_SLAB["moba_q"]), spec(_SLAB["moba_k"]), spec(_SLAB["moba_v"]), spec(_SLAB["moba_g"]),
            table_spec, table_spec, table_spec,
        ],
        out_specs=spec(0),
        out_shape=jax.ShapeDtypeStruct((N_HEADS, t, LANES), BF16),
        scratch_shapes=[pltpu.VMEM((N_HEADS, s_len, LANES), BF16),
                        pltpu.VMEM((N_HEADS, GATE_ROWS, LANES), F32),
                        pltpu.VMEM((N_HEADS, nk, HEAD_DIM + ONES_ROWS, K_TILE), BF16),
                        pltpu.VMEM((2, N_HEADS, K_TILE, Q_TILE), F32),
                        pltpu.VMEM((N_HEADS, HEAD_DIM + ONES_ROWS, Q_TILE), F32)],
        compiler_params=pltpu.CompilerParams(dimension_semantics=("arbitrary",)),
        name="moba",
    )(proj, proj, proj, proj, *tables)


def _gla_kernel(x_ref, v_ref, g_ref, p_ref, ng_ref, o_ref):
    s_len = x_ref.shape[1]
    c_len = GLA_CHUNK
    scale = GLA_DK ** -0.5
    lane = lax.broadcasted_iota(jnp.int32, (c_len, LANES), 1)
    qhalf = lane < GLA_DK
    qhalf_row = lax.broadcasted_iota(jnp.int32, (1, LANES), 1) < GLA_DK
    row = lax.broadcasted_iota(jnp.int32, (c_len, c_len), 0)
    col = lax.broadcasted_iota(jnp.int32, (c_len, c_len), 1)
    causal = col <= row

    group = GLA_GROUP
    pairs = [(h, c) for h in range(N_HEADS) for c in range(group)]

    def body(g, carries):
        prep = {}
        for h, c in pairs:
            rows = pl.ds(pl.multiple_of(g * (group * c_len), group * c_len) + c * c_len, c_len)
            qcb = x_ref[h // 2, rows, :].astype(F32)
            kcb = x_ref[2 + h // 2, rows, :].astype(F32)
            if h % 2 == 0:
                x = jnp.where(qhalf, qcb, pltpu.roll(kcb, GLA_DK, 1))
            else:
                x = jnp.where(qhalf, pltpu.roll(qcb, GLA_DK, 1), kcb)
            p = p_ref[h, rows, :]
            base = carries[h][1] if c == 0 else prep[h, c - 1]["p_last"]
            cum = p - base
            xt = x * jnp.exp(jnp.where(qhalf, cum, -cum))
            kr = pltpu.roll(xt, GLA_DK, 1)
            decay = jnp.where(qhalf_row, jnp.exp(cum[c_len - 1:c_len]), 0.0)
            prep[h, c] = dict(
                rows=rows, qz=jnp.where(qhalf, xt, 0.0).astype(BF16), kr=kr.astype(BF16),
                khat=(jnp.where(qhalf, kr, 0.0) * decay).astype(BF16),
                decay=decay, v=v_ref[h, rows, :], p_last=p[c_len - 1:c_len])
        scores, upd = {}, {}
        for h, c in pairs:
            d = prep[h, c]
            scores[h, c] = _nt_dot(d["qz"], d["kr"])
            upd[h, c] = lax.dot_general(d["v"], d["khat"], (((0,), (0,)), ((), ())),
                                        preferred_element_type=F32)
        states = {}
        for h in range(N_HEADS):
            states[h, 0] = carries[h][0]
            for c in range(group):
                states[h, c + 1] = states[h, c] * prep[h, c]["decay"] + upd[h, c]
        outs = {}
        for h, c in pairs:
            sc = jnp.where(causal, scores[h, c] * scale, 0.0).astype(BF16)
            outs[h, c] = (jnp.dot(sc, prep[h, c]["v"], preferred_element_type=F32),
                          _nt_dot(prep[h, c]["qz"], states[h, c].astype(BF16)))
        for h, c in pairs:
            o = outs[h, c][0] + outs[h, c][1] * scale
            y = o * lax.rsqrt(jnp.mean(o * o, axis=-1, keepdims=True) + RMS_EPS) * ng_ref[...]
            rows = prep[h, c]["rows"]
            o_ref[h, rows, :] = (y * _silu(g_ref[h, rows, :].astype(F32))).astype(BF16)
        return tuple((states[h, group], prep[h, group - 1]["p_last"]) for h in range(N_HEADS))

    init = tuple((jnp.zeros((HEAD_DIM, LANES), F32), jnp.zeros((1, LANES), F32))
                 for _ in range(N_HEADS))
    lax.fori_loop(0, s_len // (group * c_len), body, init)


def _gla(proj, p_cum, gla_norm_g, *, batch, s_len):
    t = proj.shape[1]
    spec = lambda slab: pl.BlockSpec((N_HEADS, s_len, LANES), lambda b: (slab, b, 0))
    return pl.pallas_call(
        _gla_kernel,
        grid=(batch,),
        in_specs=[
            spec(_SLAB["gla_qk"]), spec(_SLAB["gla_v"]), spec(_SLAB["gla_g"]),
            pl.BlockSpec((N_HEADS, s_len, LANES), lambda b: (0, b, 0)),
            pl.BlockSpec((1, HEAD_DIM), lambda b: (0, 0)),
        ],
        out_specs=pl.BlockSpec((N_HEADS, s_len, LANES), lambda b: (0, b, 0)),
        out_shape=jax.ShapeDtypeStruct((N_HEADS, t, LANES), BF16),
        compiler_params=pltpu.CompilerParams(dimension_semantics=("arbitrary",)),
        name="gla",
    )(proj, proj, proj, p_cum, gla_norm_g.reshape(1, HEAD_DIM))


def _outproj_kernel(final, x_ref, a_ref, b_ref, c_ref, d_ref, w_ref, fg_ref, o_ref):
    parts = [r[h] for r in (a_ref, b_ref, c_ref, d_ref) for h in range(N_HEADS)]
    mixed = jnp.concatenate(parts, axis=-1)
    y = x_ref[...] + jnp.dot(mixed, w_ref[...], preferred_element_type=F32)
    if final:
        y = y * lax.rsqrt(jnp.mean(y * y, axis=-1, keepdims=True) + RMS_EPS) * fg_ref[...]
    o_ref[...] = y


def _outproj(x2, mixers, w_out, layer, final_g, *, final, tm):
    t, d = x2.shape
    mspec = pl.BlockSpec((N_HEADS, tm, LANES), lambda i: (0, i, 0))
    return pl.pallas_call(
        functools.partial(_outproj_kernel, final),
        grid=(t // tm,),
        in_specs=[
            pl.BlockSpec((tm, d), lambda i: (i, 0)),
            mspec, mspec, mspec, mspec,
            pl.BlockSpec((None,) + w_out.shape[1:], lambda i: (layer, 0, 0)),
            pl.BlockSpec((1, d), lambda i: (0, 0)),
        ],
        out_specs=pl.BlockSpec((tm, d), lambda i: (i, 0)),
        out_shape=jax.ShapeDtypeStruct((t, d), F32),
        compiler_params=pltpu.CompilerParams(
            dimension_semantics=("arbitrary",), vmem_limit_bytes=56 * 1024 * 1024),
        name="outproj",
    )(x2, *mixers, w_out, final_g.reshape(1, d))


RELAYOUT_ROWS = 256
RELAYOUT_TAIL = 32


def _relayout_kernel(a_ref, b_ref, o_ref):
    r = pl.program_id(0)
    depth, rows, _ = o_ref.shape
    gain = jnp.float32(1.0)
    for name, c in _QUERY_GAIN.items():
        lo, hi = _SLAB[name] * GROUP_W, (_SLAB[name] + 1) * GROUP_W
        assert lo % rows == 0 and hi % rows == 0
        gain = jnp.where((r >= lo // rows) & (r < hi // rows), jnp.float32(c), gain)
    for l in range(depth):
        x = jnp.concatenate([a_ref[:, l, :], b_ref[:, l, :]], axis=0) * gain
        out_lo = 0
        for (src_lo, src_hi) in _MAIN_RUNS:
            shift = src_lo - out_lo
            out_hi = out_lo + (src_hi - src_lo)
            assert shift <= RELAYOUT_TAIL and out_lo % rows == 0 and out_hi % rows == 0

            @pl.when((r >= out_lo // rows) & (r < out_hi // rows))
            def _(shift=shift, l=l, x=x):
                o_ref[l] = x[shift:shift + rows].astype(BF16)

            out_lo = out_hi


def _relayout_w_in(w_in):
    depth, d, d_in = w_in.shape
    wt = jnp.transpose(w_in, (2, 0, 1))
    rows, tail = RELAYOUT_ROWS, RELAYOUT_TAIL
    main = pl.pallas_call(
        _relayout_kernel,
        grid=(N_MAIN // rows,),
        in_specs=[pl.BlockSpec((rows, depth, d), lambda r: (r, 0, 0)),
                  pl.BlockSpec((tail, depth, d), lambda r: ((r + 1) * (rows // tail), 0, 0))],
        out_specs=pl.BlockSpec((depth, rows, d), lambda r: (0, r, 0)),
        out_shape=jax.ShapeDtypeStruct((depth, N_MAIN, d), BF16),
        compiler_params=pltpu.CompilerParams(dimension_semantics=("arbitrary",)),
        name="relayout",
    )(wt, wt)
    f_off, a_off = _SEG_OFF["fox_f"][0], _SEG_OFF["gla_a"][0]
    gate_cols = jnp.concatenate([w_in[..., f_off:f_off + N_HEADS],
                                 w_in[..., a_off:a_off + GLA_RANK]], axis=-1)
    small = jnp.pad(jnp.transpose(gate_cols, (0, 2, 1)),
                    ((0, 0), (0, LANES - N_HEADS - GLA_RANK), (0, 0))).astype(BF16)
    return main, small


def _rope_tables(s_len, comp_dim, rot_dim):
    half = rot_dim // 2
    inv_freq = ROPE_THETA ** (-jnp.arange(0, rot_dim, 2, dtype=F32) / rot_dim)
    ang = jnp.arange(s_len, dtype=F32)[:, None] * inv_freq[None, :]
    cos, sin = jnp.cos(ang), jnp.sin(ang)
    zeros = jnp.zeros((s_len, comp_dim - rot_dim), F32)
    ones = jnp.ones((s_len, comp_dim - rot_dim), F32)
    reps = LANES // comp_dim
    cos_t = jnp.tile(jnp.concatenate([cos, cos, ones], axis=-1), (1, reps))
    sin_up = jnp.tile(jnp.concatenate([-sin, jnp.zeros_like(sin), zeros], axis=-1), (1, reps))
    sin_dn = jnp.tile(jnp.concatenate([jnp.zeros_like(sin), sin, zeros], axis=-1), (1, reps))
    return cos_t, sin_up, sin_dn


def kernel(x, norm_g, w_in, fox_fb, diff_lam, diff_norm_g, gla_wa2, gla_ba, gla_norm_g, w_out,
           final_norm_g):
    batch, s_len, d_model = x.shape
    depth = w_in.shape[0]
    assert s_len % Q_TILE == 0 and Q_TILE == K_TILE == MOBA_BLOCK
    assert s_len // MOBA_BLOCK <= GATE_ROWS
    t = batch * s_len
    tm_in = min(1024, t)
    tn_in = 1536
    tm_out = min(512, t)

    w_main, w_small = _relayout_w_in(w_in)
    w_out_b = w_out.astype(BF16)
    fb_pad = jnp.pad(fox_fb, ((0, 0), (0, LANES - N_HEADS)))[:, None, :]
    wa_pad = jnp.pad(gla_wa2, ((0, 0), (N_HEADS, LANES - N_HEADS - GLA_RANK), (0, 0))).astype(BF16)
    ba_dup = gla_ba[:, None, :]
    rope_diff = _rope_tables(s_len, DIFF_QK_DIM, DIFF_QK_DIM // ROPE_FRACTION)
    rope_moba = _rope_tables(s_len, HEAD_DIM, HEAD_DIM // ROPE_FRACTION)

    x2 = x.reshape(t, d_model)
    for l in range(depth):
        proj, small = _inproj(x2, norm_g[l], w_main, w_small, l, tm=tm_in, tn=tn_in)
        fk_rep, fq_rows, p_cum = _gates(small, fb_pad[l], wa_pad[l], ba_dup[l],
                                        batch=batch, s_len=s_len)
        lam_init = 0.8 - 0.6 * math.exp(-0.3 * l)
        mixers = (
            _fox(proj, fk_rep, fq_rows, batch=batch, s_len=s_len),
            _diff(proj, diff_lam[l], diff_norm_g[l], rope_diff, lam_init, batch=batch, s_len=s_len),
            _moba(proj, rope_moba, batch=batch, s_len=s_len),
            _gla(proj, p_cum, gla_norm_g[l], batch=batch, s_len=s_len),
        )
        x2 = _outproj(x2, mixers, w_out_b, l, final_norm_g, final=(l == depth - 1), tm=tm_out)
    return x2.reshape(batch, s_len, d_model)
```

```python
import functools
import math

import jax
import jax.numpy as jnp
from jax import lax
from jax.experimental import pallas as pl
from jax.experimental.pallas import tpu as pltpu

F32 = jnp.float32
BF16 = jnp.bfloat16

LANES = 128
HEAD_DIM = 128
N_HEADS = 4
GROUP_W = N_HEADS * HEAD_DIM
DIFF_QK_DIM = HEAD_DIM // 2
MOBA_BLOCK = 256
MOBA_TOPK = 3
GLA_DK = HEAD_DIM // 2
GLA_RANK = 16
GLA_TAU = 16.0
GLA_CHUNK = 64
ROPE_THETA = 500000.0
ROPE_FRACTION = 4
RMS_EPS = 1e-6
MASKED_LOGIT = -1e30

_SEGMENTS = (
    ("fox_q", GROUP_W), ("fox_k", GROUP_W), ("fox_v", GROUP_W),
    ("fox_f", N_HEADS), ("fox_g", GROUP_W),
    ("diff_q", GROUP_W), ("diff_k", GROUP_W), ("diff_v", GROUP_W), ("diff_g", GROUP_W),
    ("moba_q", GROUP_W), ("moba_k", GROUP_W), ("moba_v", GROUP_W), ("moba_g", GROUP_W),
    ("gla_q", N_HEADS * GLA_DK), ("gla_k", N_HEADS * GLA_DK), ("gla_v", GROUP_W),
    ("gla_a", GLA_RANK), ("gla_g", GROUP_W),
)
_SEG_OFF = {}
_off = 0
for _name, _w in _SEGMENTS:
    _SEG_OFF[_name] = (_off, _w)
    _off += _w

_SLABS = ("fox_q", "fox_k", "fox_v", "fox_g", "diff_q", "diff_k", "diff_v", "diff_g",
          "moba_q", "moba_k", "moba_v", "moba_g", "gla_qk", "gla_v", "gla_g")
_SLAB = {name: i for i, name in enumerate(_SLABS)}
N_MAIN = len(_SLABS) * GROUP_W
_MAIN_RUNS = ((0, _SEG_OFF["fox_f"][0]),
              (_SEG_OFF["fox_g"][0], _SEG_OFF["gla_a"][0]),
              (_SEG_OFF["gla_g"][0], _off))
assert sum(b - a for a, b in _MAIN_RUNS) == N_MAIN

Q_TILE = 256
K_TILE = 256
CUM_BLOCK = 128
GATE_ROWS = 16
GLA_GROUP = 4
ONES_ROWS = 16
LOG2E = math.log2(math.e)


def _silu(x):
    return x * (1.0 / (1.0 + jnp.exp(-x)))


def _log_sigmoid(x):
    return jnp.minimum(x, 0.0) - jnp.log1p(jnp.exp(-jnp.abs(x)))


def _nt_dot(a, b):
    return lax.dot_general(a, b, (((1,), (1,)), ((), ())), preferred_element_type=F32)


def _rope(x, tables, half):
    cos_t, sin_up, sin_dn = tables
    return (x * cos_t + pltpu.roll(x, LANES - half, 1) * sin_up
            + pltpu.roll(x, half, 1) * sin_dn)


_ROPE_HALF = {"diff_q": DIFF_QK_DIM // ROPE_FRACTION // 2, "diff_k": DIFF_QK_DIM // ROPE_FRACTION // 2,
              "moba_q": HEAD_DIM // ROPE_FRACTION // 2, "moba_k": HEAD_DIM // ROPE_FRACTION // 2}


def _inproj_kernel(x_ref, g_ref, w_ref, ws_ref, dc_ref, du_ref, dd_ref, mc_ref, mu_ref, md_ref,
                   o_ref, small_ref, h_scr):
    j = pl.program_id(1)

    @pl.when(j == 0)
    def _():
        x = x_ref[...]
        y = x * lax.rsqrt(jnp.mean(x * x, axis=-1, keepdims=True) + RMS_EPS) * g_ref[...]
        h = y.astype(BF16)
        h_scr[...] = h
        small_ref[...] = _nt_dot(h, ws_ref[...])

    blocks = o_ref.shape[0]
    tables = {"diff": (dc_ref, du_ref, dd_ref), "moba": (mc_ref, mu_ref, md_ref)}
    for step in range(N_MAIN // LANES // blocks):
        @pl.when(j == step)
        def _(step=step):
            acc = _nt_dot(h_scr[...], w_ref[...])
            for c in range(blocks):
                tile = acc[:, c * LANES:(c + 1) * LANES]
                name = _SLABS[(step * blocks + c) // N_HEADS]
                if name in _ROPE_HALF:
                    refs = tables[name.split("_")[0]]
                    tile = _rope(tile, tuple(r[...] for r in refs), _ROPE_HALF[name])
                o_ref[c] = tile.astype(BF16)


def _inproj(x2, norm_g, w_main, w_small, layer, rope_diff, rope_moba, *, tm, tn):
    t, d = x2.shape
    n_main = w_main.shape[1]
    s_len = rope_diff[0].shape[0]
    assert s_len % tm == 0
    table_spec = pl.BlockSpec((tm, LANES), lambda i, j: (i % (s_len // tm), 0))
    return pl.pallas_call(
        _inproj_kernel,
        grid=(t // tm, n_main // tn),
        in_specs=[
            pl.BlockSpec((tm, d), lambda i, j: (i, 0)),
            pl.BlockSpec((1, d), lambda i, j: (0, 0)),
            pl.BlockSpec((None, tn, d), lambda i, j: (layer, j, 0)),
            pl.BlockSpec((None, LANES, d), lambda i, j: (layer, 0, 0)),
            table_spec, table_spec, table_spec, table_spec, table_spec, table_spec,
        ],
        out_specs=[
            pl.BlockSpec((tn // LANES, tm, LANES), lambda i, j: (j, i, 0)),
            pl.BlockSpec((tm, LANES), lambda i, j: (i, 0)),
        ],
        out_shape=[
            jax.ShapeDtypeStruct((n_main // LANES, t, LANES), BF16),
            jax.ShapeDtypeStruct((t, LANES), F32),
        ],
        scratch_shapes=[pltpu.VMEM((tm, d), BF16)],
        compiler_params=pltpu.CompilerParams(
            dimension_semantics=("arbitrary", "arbitrary"),
            vmem_limit_bytes=56 * 1024 * 1024),
        name="inproj",
    )(x2, norm_g.reshape(1, d), w_main, w_small, *rope_diff, *rope_moba)


def _gates_kernel(small_ref, fb_ref, wa_ref, ba_ref, fk_ref, fq_ref, p_ref):
    s_len = small_ref.shape[0]
    small = small_ref[...]
    log_f = _log_sigmoid(small + fb_ref[...])
    a_logit = jnp.dot(small.astype(BF16), wa_ref[...], preferred_element_type=F32) + ba_ref[...]
    log_a = _log_sigmoid(a_logit) * (1.0 / GLA_TAU)
    both = jnp.concatenate([log_f, log_a], axis=-1)

    row = lax.broadcasted_iota(jnp.int32, (CUM_BLOCK, CUM_BLOCK), 0)
    col = lax.broadcasted_iota(jnp.int32, (CUM_BLOCK, CUM_BLOCK), 1)
    tri = (col <= row).astype(BF16)
    low_half = lax.broadcasted_iota(jnp.int32, (CUM_BLOCK, LANES), 1) < GLA_DK
    carry = jnp.zeros((1, both.shape[1]), F32)
    for c in range(s_len // CUM_BLOCK):
        blk = both[c * CUM_BLOCK:(c + 1) * CUM_BLOCK]
        hi = blk.astype(BF16)
        rest = blk - hi.astype(F32)
        mid = rest.astype(BF16)
        lo = (rest - mid.astype(F32)).astype(BF16)
        cum = carry + sum(jnp.dot(tri, part, preferred_element_type=F32) for part in (hi, mid, lo))
        carry = cum[CUM_BLOCK - 1:CUM_BLOCK]
        rows = slice(c * CUM_BLOCK, (c + 1) * CUM_BLOCK)
        f_cum = cum[:, :LANES] * LOG2E
        per_tile = Q_TILE // CUM_BLOCK
        fq_ref[c // per_tile, :, (c % per_tile) * CUM_BLOCK:(c % per_tile + 1) * CUM_BLOCK] = (
            f_cum.T[:GATE_ROWS])
        for h in range(N_HEADS):
            fk_ref[h, rows, :] = jnp.broadcast_to(f_cum[:, h:h + 1], (CUM_BLOCK, LANES))
            pair = cum[:, (1 + h // 2) * LANES:(2 + h // 2) * LANES]
            swapped = pltpu.roll(pair, GLA_DK, 1)
            p_ref[h, rows, :] = (jnp.where(low_half, pair, swapped) if h % 2 == 0
                                 else jnp.where(low_half, swapped, pair))


def _gates(small, fb_pad, wa_pad, ba_dup, *, batch, s_len):
    t = small.shape[0]
    return pl.pallas_call(
        _gates_kernel,
        grid=(batch,),
        in_specs=[
            pl.BlockSpec((s_len, LANES), lambda b: (b, 0)),
            pl.BlockSpec((1, LANES), lambda b: (0, 0)),
            pl.BlockSpec((LANES, N_HEADS * GLA_DK), lambda b: (0, 0)),
            pl.BlockSpec((1, N_HEADS * GLA_DK), lambda b: (0, 0)),
        ],
        out_specs=[
            pl.BlockSpec((N_HEADS, s_len, LANES), lambda b: (0, b, 0)),
            pl.BlockSpec((None, s_len // Q_TILE, GATE_ROWS, Q_TILE), lambda b: (b, 0, 0, 0)),
            pl.BlockSpec((N_HEADS, s_len, LANES), lambda b: (0, b, 0)),
        ],
        out_shape=[
            jax.ShapeDtypeStruct((N_HEADS, t, LANES), F32),
            jax.ShapeDtypeStruct((batch, s_len // Q_TILE, GATE_ROWS, Q_TILE), F32),
            jax.ShapeDtypeStruct((N_HEADS, t, LANES), F32),
        ],
        compiler_params=pltpu.CompilerParams(dimension_semantics=("arbitrary",)),
        name="gates",
    )(small, fb_pad, wa_pad, ba_dup)


def _causal_mask_t(tk, width, tq):
    key = lax.broadcasted_iota(jnp.int32, (tk, width), 0)
    qry = lax.broadcasted_iota(jnp.int32, (tk, width), 1)
    if width != tq:
        qry = qry & (tq - 1)
    return key <= qry


def _transpose_values(v_ref, vt_scr):
    dv = v_ref.shape[2]
    for h in range(v_ref.shape[0]):
        for j in range(v_ref.shape[1] // K_TILE):
            blk = v_ref[h, j * K_TILE:(j + 1) * K_TILE, :].astype(F32)
            vt_scr[h, j, :dv, :] = blk.T.astype(BF16)
            vt_scr[h, j, dv:, :] = jnp.ones((ONES_ROWS, K_TILE), BF16)


def _col_max(z):
    rows = z.shape[0]
    while rows > 8:
        rows //= 2
        z = jnp.maximum(z[:rows], z[rows:])
    return jnp.max(z, axis=0, keepdims=True)


def _causal_tiles_t(n_tiles, prepare_fn, logits_fn, vt_fn, finish_fn, tq, z_scr, acc_scr):
    n, _, width = acc_scr.shape

    def update(j, slot, ms, tile_max, masked):
        stats = []
        for s in range(n):
            z = z_scr[slot, s]
            if masked:
                z = jnp.where(_causal_mask_t(K_TILE, width, tq), z, -jnp.inf)
                m_new = jnp.maximum(ms[s], _col_max(z))
            else:
                m_new = jnp.maximum(ms[s], tile_max[s])
            stats.append((m_new, jnp.exp2(ms[s] - m_new), jnp.exp2(z - m_new).astype(BF16)))
        for s in range(n):
            pv = jnp.dot(vt_fn(s, j), stats[s][2], preferred_element_type=F32)
            acc_scr[s] = stats[s][1] * acc_scr[s] + pv
        return tuple(st[0] for st in stats)

    def first_logits(i):
        ctx = prepare_fn(i)
        return ctx, [logits_fn(ctx, s, jnp.int32(0)) for s in range(n)]

    def open_tile(zs):
        for s in range(n):
            z_scr[0, s] = zs[s]
            acc_scr[s] = jnp.zeros(acc_scr.shape[1:], F32)
        return (tuple(jnp.full((1, width), -jnp.inf, F32) for _ in range(n)),
                tuple(_col_max(z) for z in zs))

    def close_tile(i, ms):
        update(i, lax.rem(i, 2), ms, None, True)
        finish_fn(i)

    def off_diagonal(i, ctx, state):
        def body(j, state):
            ms, tile_max = state
            slot = lax.rem(j, 2)
            z_next = [logits_fn(ctx, s, j + 1) for s in range(n)]
            ms = update(j, slot, ms, tile_max, False)
            for s in range(n):
                z_scr[1 - slot, s] = z_next[s]
            return ms, tuple(_col_max(z) for z in z_next)

        return lax.fori_loop(0, i, body, state)[0]

    _, zs = first_logits(jnp.int32(0))
    ms, _ = open_tile(zs)

    def outer(i, ms_prev):
        ctx, zs = first_logits(i)
        close_tile(i - 1, ms_prev)
        return off_diagonal(i, ctx, open_tile(zs))

    ms = lax.fori_loop(1, n_tiles, outer, ms)
    close_tile(jnp.int32(n_tiles - 1), ms)


def _softmax_result_t(acc):
    dv = acc.shape[0] - ONES_ROWS
    return acc[:dv] / acc[dv:dv + 1]


def _key_rows(j):
    return pl.ds(pl.multiple_of(j * K_TILE, K_TILE), K_TILE)


def _query_rows(i):
    return pl.ds(pl.multiple_of(i * Q_TILE, Q_TILE), Q_TILE)


def _batch_spec(s_len):
    return lambda slab: pl.BlockSpec((N_HEADS, s_len, LANES), lambda b: (slab, b, 0))


def _fox_kernel(q_ref, k_ref, v_ref, g_ref, fk_ref, fq_ref, o_ref, vt_scr, z_scr, acc_scr):
    scale = HEAD_DIM ** -0.5 * LOG2E
    _transpose_values(v_ref, vt_scr)

    def prepare(i):
        fq_all = fq_ref[i]
        return ([q_ref[h, _query_rows(i), :] for h in range(N_HEADS)],
                [fq_all[h:h + 1, :] for h in range(N_HEADS)])

    def logits(ctx, h, j):
        qs, fqs = ctx
        fk = fk_ref[h, _key_rows(j), :]
        fk = jnp.concatenate([fk] * (Q_TILE // LANES), axis=-1)
        return _nt_dot(k_ref[h, _key_rows(j), :], qs[h]) * scale + (fqs[h] - fk)

    def finish(i):
        for h in range(N_HEADS):
            o = _softmax_result_t(acc_scr[h]).T
            gate = _silu(g_ref[h, _query_rows(i), :].astype(F32))
            o_ref[h, _query_rows(i), :] = (o * gate).astype(BF16)

    _causal_tiles_t(q_ref.shape[1] // Q_TILE, prepare, logits, lambda h, j: vt_scr[h, j], finish,
                    Q_TILE, z_scr, acc_scr)


def _fox(proj, fk_rep, fq_rows, *, batch, s_len):
    t = proj.shape[1]
    nq = s_len // Q_TILE
    nk = s_len // K_TILE
    spec = _batch_spec(s_len)
    return pl.pallas_call(
        _fox_kernel,
        grid=(batch,),
        in_specs=[
            spec(_SLAB["fox_q"]), spec(_SLAB["fox_k"]), spec(_SLAB["fox_v"]), spec(_SLAB["fox_g"]),
            spec(0),
            pl.BlockSpec((None, nq, GATE_ROWS, Q_TILE), lambda b: (b, 0, 0, 0)),
        ],
        out_specs=spec(0),
        out_shape=jax.ShapeDtypeStruct((N_HEADS, t, LANES), BF16),
        scratch_shapes=[pltpu.VMEM((N_HEADS, nk, HEAD_DIM + ONES_ROWS, K_TILE), BF16),
                        pltpu.VMEM((2, N_HEADS, K_TILE, Q_TILE), F32),
                        pltpu.VMEM((N_HEADS, HEAD_DIM + ONES_ROWS, Q_TILE), F32)],
        compiler_params=pltpu.CompilerParams(dimension_semantics=("arbitrary",)),
        name="fox",
    )(proj, proj, proj, proj, fk_rep, fq_rows)


def _diff_kernel(lam_init, q_ref, k_ref, v_ref, g_ref, lam_ref, ng_ref, o_ref,
                 kz_scr, a1_scr, vt_scr, z_scr, acc_scr):
    tq = Q_TILE
    scale = DIFF_QK_DIM ** -0.5 * LOG2E

    _transpose_values(v_ref, vt_scr)
    first = lax.broadcasted_iota(jnp.int32, (k_ref.shape[1], LANES), 1) < DIFF_QK_DIM
    for h in range(N_HEADS):
        k = k_ref[h]
        kz_scr[0, h] = jnp.where(first, k, jnp.zeros((), BF16))
        kz_scr[1, h] = jnp.where(first, jnp.zeros((), BF16), k)

    lf = lam_ref[...]
    lam = (jnp.exp(jnp.sum(lf[0:1] * lf[1:2], axis=-1, keepdims=True))
           - jnp.exp(jnp.sum(lf[2:3] * lf[3:4], axis=-1, keepdims=True)) + lam_init)

    def prepare(i):
        return [q_ref[h, _query_rows(i), :] for h in range(N_HEADS)]

    def finish_first(i):
        for h in range(N_HEADS):
            a1_scr[h, _query_rows(i), :] = _softmax_result_t(acc_scr[h]).T

    def finish_second(i):
        for h in range(N_HEADS):
            o = a1_scr[h, _query_rows(i), :] - lam * _softmax_result_t(acc_scr[h]).T
            o = o * lax.rsqrt(jnp.mean(o * o, axis=-1, keepdims=True) + RMS_EPS) * ng_ref[...]
            o = o * (1.0 - lam_init)
            gate = _silu(g_ref[h, _query_rows(i), :].astype(F32))
            o_ref[h, _query_rows(i), :] = (o * gate).astype(BF16)

    for comp, finish in ((0, finish_first), (1, finish_second)):
        def logits(qs, h, j, comp=comp):
            return _nt_dot(kz_scr[comp, h, _key_rows(j), :], qs[h]) * scale

        _causal_tiles_t(q_ref.shape[1] // tq, prepare, logits, lambda h, j: vt_scr[h, j], finish,
                        tq, z_scr, acc_scr)


def _diff(proj, diff_lam, diff_norm_g, lam_init, *, batch, s_len):
    t = proj.shape[1]
    nk = s_len // K_TILE
    spec = _batch_spec(s_len)
    return pl.pallas_call(
        functools.partial(_diff_kernel, lam_init),
        grid=(batch,),
        in_specs=[
            spec(_SLAB["diff_q"]), spec(_SLAB["diff_k"]), spec(_SLAB["diff_v"]), spec(_SLAB["diff_g"]),
            pl.BlockSpec((4, DIFF_QK_DIM), lambda b: (0, 0)),
            pl.BlockSpec((1, HEAD_DIM), lambda b: (0, 0)),
        ],
        out_specs=spec(0),
        out_shape=jax.ShapeDtypeStruct((N_HEADS, t, LANES), BF16),
        scratch_shapes=[pltpu.VMEM((2, N_HEADS, s_len, LANES), BF16),
                        pltpu.VMEM((N_HEADS, s_len, LANES), F32),
                        pltpu.VMEM((N_HEADS, nk, HEAD_DIM + ONES_ROWS, K_TILE), BF16),
                        pltpu.VMEM((2, N_HEADS, K_TILE, Q_TILE), F32),
                        pltpu.VMEM((N_HEADS, HEAD_DIM + ONES_ROWS, Q_TILE), F32)],
        compiler_params=pltpu.CompilerParams(dimension_semantics=("arbitrary",)),
        name="diff",
    )(proj, proj, proj, proj, diff_lam, diff_norm_g.reshape(1, HEAD_DIM))


def _moba_kernel(q_ref, k_ref, v_ref, g_ref, o_ref, kmean_scr, vt_scr, z_scr, acc_scr):
    tq = Q_TILE
    scale = HEAD_DIM ** -0.5 * LOG2E
    n_blk = k_ref.shape[1] // MOBA_BLOCK

    _transpose_values(v_ref, vt_scr)
    kmean_scr[...] = jnp.zeros(kmean_scr.shape, F32)
    for h in range(N_HEADS):
        for n in range(n_blk):
            blk = k_ref[h, n * MOBA_BLOCK:(n + 1) * MOBA_BLOCK, :].astype(F32)
            kmean_scr[h, n:n + 1, :] = jnp.sum(blk, axis=0, keepdims=True) * (1.0 / MOBA_BLOCK)

    blk_id = lax.broadcasted_iota(jnp.int32, (GATE_ROWS, tq), 0).astype(F32)

    def prepare(i):
        past = blk_id < lax.convert_element_type(i, F32)
        qs, biases = [], []
        for h in range(N_HEADS):
            q = q_ref[h, _query_rows(i), :]
            gate = _nt_dot(kmean_scr[h].astype(BF16), q)
            gate = jnp.where(past, gate, -jnp.inf)
            open_ = past
            sel = jnp.zeros((GATE_ROWS, tq), jnp.bool_)
            for _ in range(MOBA_TOPK):
                top = jnp.max(gate, axis=0, keepdims=True)
                idx = jnp.min(jnp.where(gate == top, blk_id, float(GATE_ROWS)), axis=0, keepdims=True)
                pick = (blk_id == idx) & open_
                sel = sel | pick
                open_ = open_ & jnp.logical_not(pick)
                gate = jnp.where(pick, -jnp.inf, gate)
            qs.append(q)
            biases.append(jnp.where(sel, 0.0, MASKED_LOGIT))
        return i, qs, biases

    def logits(ctx, h, j):
        i, qs, biases = ctx
        here = blk_id == lax.convert_element_type(j, F32)
        bias = jnp.sum(jnp.where(here, biases[h], 0.0), axis=0, keepdims=True)
        bias = jnp.where(j == i, 0.0, bias)
        return _nt_dot(k_ref[h, _key_rows(j), :], qs[h]) * scale + bias

    def finish(i):
        for h in range(N_HEADS):
            o = _softmax_result_t(acc_scr[h]).T
            gate = _silu(g_ref[h, _query_rows(i), :].astype(F32))
            o_ref[h, _query_rows(i), :] = (o * gate).astype(BF16)

    _causal_tiles_t(q_ref.shape[1] // tq, prepare, logits, lambda h, j: vt_scr[h, j], finish,
                    tq, z_scr, acc_scr)


def _moba(proj, *, batch, s_len):
    t = proj.shape[1]
    nk = s_len // K_TILE
    spec = _batch_spec(s_len)
    return pl.pallas_call(
        _moba_kernel,
        grid=(batch,),
        in_specs=[
            spec(_SLAB["moba_q"]), spec(_SLAB["moba_k"]), spec(_SLAB["moba_v"]), spec(_SLAB["moba_g"]),
        ],
        out_specs=spec(0),
        out_shape=jax.ShapeDtypeStruct((N_HEADS, t, LANES), BF16),
        scratch_shapes=[pltpu.VMEM((N_HEADS, GATE_ROWS, LANES), F32),
                        pltpu.VMEM((N_HEADS, nk, HEAD_DIM + ONES_ROWS, K_TILE), BF16),
                        pltpu.VMEM((2, N_HEADS, K_TILE, Q_TILE), F32),
                        pltpu.VMEM((N_HEADS, HEAD_DIM + ONES_ROWS, Q_TILE), F32)],
        compiler_params=pltpu.CompilerParams(dimension_semantics=("arbitrary",)),
        name="moba",
    )(proj, proj, proj, proj)


def _gla_kernel(x_ref, v_ref, g_ref, p_ref, ng_ref, o_ref):
    s_len = x_ref.shape[1]
    c_len = GLA_CHUNK
    scale = GLA_DK ** -0.5
    lane = lax.broadcasted_iota(jnp.int32, (c_len, LANES), 1)
    qhalf = lane < GLA_DK
    qhalf_row = lax.broadcasted_iota(jnp.int32, (1, LANES), 1) < GLA_DK
    row = lax.broadcasted_iota(jnp.int32, (c_len, c_len), 0)
    col = lax.broadcasted_iota(jnp.int32, (c_len, c_len), 1)
    causal = col <= row

    group = GLA_GROUP
    pairs = [(h, c) for h in range(N_HEADS) for c in range(group)]

    def body(g, carries):
        prep = {}
        for h, c in pairs:
            rows = pl.ds(pl.multiple_of(g * (group * c_len), group * c_len) + c * c_len, c_len)
            qcb = x_ref[h // 2, rows, :].astype(F32)
            kcb = x_ref[2 + h // 2, rows, :].astype(F32)
            if h % 2 == 0:
                x = jnp.where(qhalf, qcb, pltpu.roll(kcb, GLA_DK, 1))
            else:
                x = jnp.where(qhalf, pltpu.roll(qcb, GLA_DK, 1), kcb)
            p = p_ref[h, rows, :]
            base = carries[h][1] if c == 0 else prep[h, c - 1]["p_last"]
            cum = p - base
            xt = x * jnp.exp(jnp.where(qhalf, cum, -cum))
            kr = pltpu.roll(xt, GLA_DK, 1)
            decay = jnp.where(qhalf_row, jnp.exp(cum[c_len - 1:c_len]), 0.0)
            prep[h, c] = dict(
                rows=rows, qz=jnp.where(qhalf, xt, 0.0).astype(BF16), kr=kr.astype(BF16),
                khat=(jnp.where(qhalf, kr, 0.0) * decay).astype(BF16),
                decay=decay, v=v_ref[h, rows, :], p_last=p[c_len - 1:c_len])
        scores, upd = {}, {}
        for h, c in pairs:
            d = prep[h, c]
            scores[h, c] = _nt_dot(d["qz"], d["kr"])
            upd[h, c] = lax.dot_general(d["v"], d["khat"], (((0,), (0,)), ((), ())),
                                        preferred_element_type=F32)
        states = {}
        for h in range(N_HEADS):
            states[h, 0] = carries[h][0]
            for c in range(group):
                states[h, c + 1] = states[h, c] * prep[h, c]["decay"] + upd[h, c]
        outs = {}
        for h, c in pairs:
            sc = jnp.where(causal, scores[h, c] * scale, 0.0).astype(BF16)
            outs[h, c] = (jnp.dot(sc, prep[h, c]["v"], preferred_element_type=F32),
                          _nt_dot(prep[h, c]["qz"], states[h, c].astype(BF16)))
        for h, c in pairs:
            o = outs[h, c][0] + outs[h, c][1] * scale
            y = o * lax.rsqrt(jnp.mean(o * o, axis=-1, keepdims=True) + RMS_EPS) * ng_ref[...]
            rows = prep[h, c]["rows"]
            o_ref[h, rows, :] = (y * _silu(g_ref[h, rows, :].astype(F32))).astype(BF16)
        return tuple((states[h, group], prep[h, group - 1]["p_last"]) for h in range(N_HEADS))

    init = tuple((jnp.zeros((HEAD_DIM, LANES), F32), jnp.zeros((1, LANES), F32))
                 for _ in range(N_HEADS))
    lax.fori_loop(0, s_len // (group * c_len), body, init)


def _gla(proj, p_cum, gla_norm_g, *, batch, s_len):
    t = proj.shape[1]
    spec = lambda slab: pl.BlockSpec((N_HEADS, s_len, LANES), lambda b: (slab, b, 0))
    return pl.pallas_call(
        _gla_kernel,
        grid=(batch,),
        in_specs=[
            spec(_SLAB["gla_qk"]), spec(_SLAB["gla_v"]), spec(_SLAB["gla_g"]),
            pl.BlockSpec((N_HEADS, s_len, LANES), lambda b: (0, b, 0)),
            pl.BlockSpec((1, HEAD_DIM), lambda b: (0, 0)),
        ],
        out_specs=pl.BlockSpec((N_HEADS, s_len, LANES), lambda b: (0, b, 0)),
        out_shape=jax.ShapeDtypeStruct((N_HEADS, t, LANES), BF16),
        compiler_params=pltpu.CompilerParams(dimension_semantics=("arbitrary",)),
        name="gla",
    )(proj, proj, proj, p_cum, gla_norm_g.reshape(1, HEAD_DIM))


def _outproj_kernel(final, x_ref, a_ref, b_ref, c_ref, d_ref, w_ref, fg_ref, o_ref):
    parts = [r[h] for r in (a_ref, b_ref, c_ref, d_ref) for h in range(N_HEADS)]
    mixed = jnp.concatenate(parts, axis=-1)
    y = x_ref[...] + jnp.dot(mixed, w_ref[...], preferred_element_type=F32)
    if final:
        y = y * lax.rsqrt(jnp.mean(y * y, axis=-1, keepdims=True) + RMS_EPS) * fg_ref[...]
    o_ref[...] = y


def _outproj(x2, mixers, w_out, layer, final_g, *, final, tm):
    t, d = x2.shape
    mspec = pl.BlockSpec((N_HEADS, tm, LANES), lambda i: (0, i, 0))
    return pl.pallas_call(
        functools.partial(_outproj_kernel, final),
        grid=(t // tm,),
        in_specs=[
            pl.BlockSpec((tm, d), lambda i: (i, 0)),
            mspec, mspec, mspec, mspec,
            pl.BlockSpec((None,) + w_out.shape[1:], lambda i: (layer, 0, 0)),
            pl.BlockSpec((1, d), lambda i: (0, 0)),
        ],
        out_specs=pl.BlockSpec((tm, d), lambda i: (i, 0)),
        out_shape=jax.ShapeDtypeStruct((t, d), F32),
        compiler_params=pltpu.CompilerParams(
            dimension_semantics=("arbitrary",), vmem_limit_bytes=56 * 1024 * 1024),
        name="outproj",
    )(x2, *mixers, w_out, final_g.reshape(1, d))


RELAYOUT_ROWS = 256
RELAYOUT_TAIL = 32


def _relayout_kernel(a_ref, b_ref, o_ref):
    r = pl.program_id(0)
    depth, rows, _ = o_ref.shape
    for l in range(depth):
        x = jnp.concatenate([a_ref[:, l, :], b_ref[:, l, :]], axis=0)
        out_lo = 0
        for (src_lo, src_hi) in _MAIN_RUNS:
            shift = src_lo - out_lo
            out_hi = out_lo + (src_hi - src_lo)
            assert shift <= RELAYOUT_TAIL and out_lo % rows == 0 and out_hi % rows == 0

            @pl.when((r >= out_lo // rows) & (r < out_hi // rows))
            def _(shift=shift, l=l, x=x):
                o_ref[l] = x[shift:shift + rows].astype(BF16)

            out_lo = out_hi


def _relayout_w_in(w_in):
    depth, d, d_in = w_in.shape
    wt = jnp.transpose(w_in, (2, 0, 1))
    rows, tail = RELAYOUT_ROWS, RELAYOUT_TAIL
    main = pl.pallas_call(
        _relayout_kernel,
        grid=(N_MAIN // rows,),
        in_specs=[pl.BlockSpec((rows, depth, d), lambda r: (r, 0, 0)),
                  pl.BlockSpec((tail, depth, d), lambda r: ((r + 1) * (rows // tail), 0, 0))],
        out_specs=pl.BlockSpec((depth, rows, d), lambda r: (0, r, 0)),
        out_shape=jax.ShapeDtypeStruct((depth, N_MAIN, d), BF16),
        compiler_params=pltpu.CompilerParams(dimension_semantics=("arbitrary",)),
        name="relayout",
    )(wt, wt)
    f_off, a_off = _SEG_OFF["fox_f"][0], _SEG_OFF["gla_a"][0]
    gate_cols = jnp.concatenate([w_in[..., f_off:f_off + N_HEADS],
                                 w_in[..., a_off:a_off + GLA_RANK]], axis=-1)
    small = jnp.pad(jnp.transpose(gate_cols, (0, 2, 1)),
                    ((0, 0), (0, LANES - N_HEADS - GLA_RANK), (0, 0))).astype(BF16)
    return main, small


def _rope_tables(s_len, comp_dim, rot_dim):
    half = rot_dim // 2
    inv_freq = ROPE_THETA ** (-jnp.arange(0, rot_dim, 2, dtype=F32) / rot_dim)
    ang = jnp.arange(s_len, dtype=F32)[:, None] * inv_freq[None, :]
    cos, sin = jnp.cos(ang), jnp.sin(ang)
    zeros = jnp.zeros((s_len, comp_dim - rot_dim), F32)
    ones = jnp.ones((s_len, comp_dim - rot_dim), F32)
    reps = LANES // comp_dim
    cos_t = jnp.tile(jnp.concatenate([cos, cos, ones], axis=-1), (1, reps))
    sin_up = jnp.tile(jnp.concatenate([-sin, jnp.zeros_like(sin), zeros], axis=-1), (1, reps))
    sin_dn = jnp.tile(jnp.concatenate([jnp.zeros_like(sin), sin, zeros], axis=-1), (1, reps))
    return cos_t, sin_up, sin_dn


def kernel(x, norm_g, w_in, fox_fb, diff_lam, diff_norm_g, gla_wa2, gla_ba, gla_norm_g, w_out,
           final_norm_g):
    batch, s_len, d_model = x.shape
    depth = w_in.shape[0]
    assert s_len % Q_TILE == 0 and Q_TILE == K_TILE == MOBA_BLOCK
    assert s_len // MOBA_BLOCK <= GATE_ROWS
    t = batch * s_len
    tm_in = min(1024, t)
    tn_in = 1536
    tm_out = min(512, t)

    w_main, w_small = _relayout_w_in(w_in)
    w_out_b = w_out.astype(BF16)
    fb_pad = jnp.pad(fox_fb, ((0, 0), (0, LANES - N_HEADS)))[:, None, :]
    wa_pad = jnp.pad(gla_wa2, ((0, 0), (N_HEADS, LANES - N_HEADS - GLA_RANK), (0, 0))).astype(BF16)
    ba_dup = gla_ba[:, None, :]
    rope_diff = _rope_tables(s_len, DIFF_QK_DIM, DIFF_QK_DIM // ROPE_FRACTION)
    rope_moba = _rope_tables(s_len, HEAD_DIM, HEAD_DIM // ROPE_FRACTION)

    x2 = x.reshape(t, d_model)
    for l in range(depth):
        proj, small = _inproj(x2, norm_g[l], w_main, w_small, l, rope_diff, rope_moba,
                              tm=tm_in, tn=tn_in)
        fk_rep, fq_rows, p_cum = _gates(small, fb_pad[l], wa_pad[l], ba_dup[l],
                                        batch=batch, s_len=s_len)
        lam_init = 0.8 - 0.6 * math.exp(-0.3 * l)
        mixers = (
            _fox(proj, fk_rep, fq_rows, batch=batch, s_len=s_len),
            _diff(proj, diff_lam[l], diff_norm_g[l], lam_init, batch=batch, s_len=s_len),
            _moba(proj, batch=batch, s_len=s_len),
            _gla(proj, p_cum, gla_norm_g[l], batch=batch, s_len=s_len),
        )
        x2 = _outproj(x2, mixers, w_out_b, l, final_norm_g, final=(l == depth - 1), tm=tm_out)
    return x2.reshape(batch, s_len, d_model)
```

```python
import functools
import math

import jax
import jax.numpy as jnp
from jax import lax
from jax.experimental import pallas as pl
from jax.experimental.pallas import tpu as pltpu

F32 = jnp.float32
BF16 = jnp.bfloat16

LANES = 128
VMEM_LIMIT_BYTES = 56 * 2 ** 20
HEAD_DIM = 128
N_HEADS = 4
GROUP_W = N_HEADS * HEAD_DIM
DIFF_QK_DIM = HEAD_DIM // 2
MOBA_BLOCK = 256
MOBA_TOPK = 3
GLA_DK = HEAD_DIM // 2
GLA_RANK = 16
GLA_TAU = 16.0
GLA_CHUNK = 64
ROPE_THETA = 500000.0
ROPE_FRACTION = 4
RMS_EPS = 1e-6
MASKED_LOGIT = -1e30

_SEGMENTS = (
    ("fox_q", GROUP_W), ("fox_k", GROUP_W), ("fox_v", GROUP_W),
    ("fox_f", N_HEADS), ("fox_g", GROUP_W),
    ("diff_q", GROUP_W), ("diff_k", GROUP_W), ("diff_v", GROUP_W), ("diff_g", GROUP_W),
    ("moba_q", GROUP_W), ("moba_k", GROUP_W), ("moba_v", GROUP_W), ("moba_g", GROUP_W),
    ("gla_q", N_HEADS * GLA_DK), ("gla_k", N_HEADS * GLA_DK), ("gla_v", GROUP_W),
    ("gla_a", GLA_RANK), ("gla_g", GROUP_W),
)
_SEG_OFF = {}
_off = 0
for _name, _w in _SEGMENTS:
    _SEG_OFF[_name] = (_off, _w)
    _off += _w

_SLABS = ("fox_q", "fox_k", "fox_v", "fox_g", "diff_q", "diff_k", "diff_v", "diff_g",
          "moba_q", "moba_k", "moba_v", "moba_g", "gla_qk", "gla_v", "gla_g")
_SLAB = {name: i for i, name in enumerate(_SLABS)}
N_MAIN = len(_SLABS) * GROUP_W
_MAIN_RUNS = ((0, _SEG_OFF["fox_f"][0]),
              (_SEG_OFF["fox_g"][0], _SEG_OFF["gla_a"][0]),
              (_SEG_OFF["gla_g"][0], _off))
assert sum(b - a for a, b in _MAIN_RUNS) == N_MAIN

Q_TILE = 256
K_TILE = 256
CUM_BLOCK = 128
GATE_ROWS = 16
GLA_GROUP = 4
ONES_ROWS = 16
LOG2E = math.log2(math.e)


def _silu(x):
    return x * (1.0 / (1.0 + jnp.exp(-x)))


def _log_sigmoid(x):
    return jnp.minimum(x, 0.0) - jnp.log1p(jnp.exp(-jnp.abs(x)))


def _nt_dot(a, b):
    return lax.dot_general(a, b, (((1,), (1,)), ((), ())), preferred_element_type=F32)


def _inproj_kernel(x_ref, g_ref, w_ref, ws_ref, o_ref, small_ref, h_scr):
    @pl.when(pl.program_id(1) == 0)
    def _():
        x = x_ref[...]
        y = x * lax.rsqrt(jnp.mean(x * x, axis=-1, keepdims=True) + RMS_EPS) * g_ref[...]
        h = y.astype(BF16)
        h_scr[...] = h
        small_ref[...] = _nt_dot(h, ws_ref[...])

    acc = _nt_dot(h_scr[...], w_ref[...])
    for c in range(o_ref.shape[0]):
        o_ref[c] = acc[:, c * LANES:(c + 1) * LANES].astype(BF16)


def _inproj(x2, norm_g, w_main, w_small, layer, *, tm, tn):
    t, d = x2.shape
    n_main = w_main.shape[1]
    return pl.pallas_call(
        _inproj_kernel,
        grid=(t // tm, n_main // tn),
        in_specs=[
            pl.BlockSpec((tm, d), lambda i, j: (i, 0)),
            pl.BlockSpec((1, d), lambda i, j: (0, 0)),
            pl.BlockSpec((None, tn, d), lambda i, j: (layer, j, 0)),
            pl.BlockSpec((None, LANES, d), lambda i, j: (layer, 0, 0)),
        ],
        out_specs=[
            pl.BlockSpec((tn // LANES, tm, LANES), lambda i, j: (j, i, 0)),
            pl.BlockSpec((tm, LANES), lambda i, j: (i, 0)),
        ],
        out_shape=[
            jax.ShapeDtypeStruct((n_main // LANES, t, LANES), BF16),
            jax.ShapeDtypeStruct((t, LANES), F32),
        ],
        scratch_shapes=[pltpu.VMEM((tm, d), BF16)],
        compiler_params=pltpu.CompilerParams(
            dimension_semantics=("arbitrary", "arbitrary"), vmem_limit_bytes=VMEM_LIMIT_BYTES),
        name="inproj",
    )(x2, norm_g.reshape(1, d), w_main, w_small)


def _gates_kernel(small_ref, fb_ref, wa_ref, ba_ref, fk_ref, fq_ref, p_ref):
    s_len = small_ref.shape[0]
    small = small_ref[...]
    log_f = _log_sigmoid(small + fb_ref[...])
    a_logit = jnp.dot(small.astype(BF16), wa_ref[...], preferred_element_type=F32) + ba_ref[...]
    log_a = _log_sigmoid(a_logit) * (1.0 / GLA_TAU)
    both = jnp.concatenate([log_f, log_a], axis=-1)

    row = lax.broadcasted_iota(jnp.int32, (CUM_BLOCK, CUM_BLOCK), 0)
    col = lax.broadcasted_iota(jnp.int32, (CUM_BLOCK, CUM_BLOCK), 1)
    tri = (col <= row).astype(BF16)
    low_half = lax.broadcasted_iota(jnp.int32, (CUM_BLOCK, LANES), 1) < GLA_DK
    carry = jnp.zeros((1, both.shape[1]), F32)
    for c in range(s_len // CUM_BLOCK):
        blk = both[c * CUM_BLOCK:(c + 1) * CUM_BLOCK]
        hi = blk.astype(BF16)
        rest = blk - hi.astype(F32)
        mid = rest.astype(BF16)
        lo = (rest - mid.astype(F32)).astype(BF16)
        cum = carry + sum(jnp.dot(tri, part, preferred_element_type=F32) for part in (hi, mid, lo))
        carry = cum[CUM_BLOCK - 1:CUM_BLOCK]
        rows = slice(c * CUM_BLOCK, (c + 1) * CUM_BLOCK)
        f_cum = cum[:, :LANES] * LOG2E
        per_tile = Q_TILE // CUM_BLOCK
        fq_ref[c // per_tile, :, (c % per_tile) * CUM_BLOCK:(c % per_tile + 1) * CUM_BLOCK] = (
            f_cum.T[:GATE_ROWS])
        for h in range(N_HEADS):
            fk_ref[h, rows, :] = jnp.broadcast_to(f_cum[:, h:h + 1], (CUM_BLOCK, LANES))
            pair = cum[:, (1 + h // 2) * LANES:(2 + h // 2) * LANES]
            swapped = pltpu.roll(pair, GLA_DK, 1)
            p_ref[h, rows, :] = (jnp.where(low_half, pair, swapped) if h % 2 == 0
                                 else jnp.where(low_half, swapped, pair))


def _gates(small, fb_pad, wa_pad, ba, *, batch, s_len):
    t = small.shape[0]
    return pl.pallas_call(
        _gates_kernel,
        grid=(batch,),
        in_specs=[
            pl.BlockSpec((s_len, LANES), lambda b: (b, 0)),
            pl.BlockSpec((1, LANES), lambda b: (0, 0)),
            pl.BlockSpec((LANES, N_HEADS * GLA_DK), lambda b: (0, 0)),
            pl.BlockSpec((1, N_HEADS * GLA_DK), lambda b: (0, 0)),
        ],
        out_specs=[
            pl.BlockSpec((N_HEADS, s_len, LANES), lambda b: (0, b, 0)),
            pl.BlockSpec((None, s_len // Q_TILE, GATE_ROWS, Q_TILE), lambda b: (b, 0, 0, 0)),
            pl.BlockSpec((N_HEADS, s_len, LANES), lambda b: (0, b, 0)),
        ],
        out_shape=[
            jax.ShapeDtypeStruct((N_HEADS, t, LANES), F32),
            jax.ShapeDtypeStruct((batch, s_len // Q_TILE, GATE_ROWS, Q_TILE), F32),
            jax.ShapeDtypeStruct((N_HEADS, t, LANES), F32),
        ],
        compiler_params=pltpu.CompilerParams(dimension_semantics=("arbitrary",)),
        name="gates",
    )(small, fb_pad, wa_pad, ba)


def _causal_mask_t(tk, tq):
    key = lax.broadcasted_iota(jnp.int32, (tk, tq), 0)
    qry = lax.broadcasted_iota(jnp.int32, (tk, tq), 1)
    return key <= qry


def _transpose_values(v_ref, vt_scr):
    dv = v_ref.shape[2]
    for h in range(v_ref.shape[0]):
        for j in range(v_ref.shape[1] // K_TILE):
            blk = v_ref[h, j * K_TILE:(j + 1) * K_TILE, :].astype(F32)
            vt_scr[h, j, :dv, :] = blk.T.astype(BF16)
            vt_scr[h, j, dv:, :] = jnp.ones((ONES_ROWS, K_TILE), BF16)


def _col_max(z):
    rows = z.shape[0]
    while rows > 8:
        rows //= 2
        z = jnp.maximum(z[:rows], z[rows:])
    return jnp.max(z, axis=0, keepdims=True)


def _causal_tiles_t(n_tiles, prepare_fn, logits_fn, vt_fn, finish_fn, z_scr, acc_scr):
    n, _, tq = acc_scr.shape

    def update(j, slot, ms, tile_max, masked):
        stats = []
        for s in range(n):
            z = z_scr[slot, s]
            if masked:
                z = jnp.where(_causal_mask_t(K_TILE, tq), z, -jnp.inf)
                m_new = jnp.maximum(ms[s], _col_max(z))
            else:
                m_new = jnp.maximum(ms[s], tile_max[s])
            stats.append((m_new, jnp.exp2(ms[s] - m_new), jnp.exp2(z - m_new).astype(BF16)))
        for s in range(n):
            pv = jnp.dot(vt_fn(s, j), stats[s][2], preferred_element_type=F32)
            acc_scr[s] = stats[s][1] * acc_scr[s] + pv
        return tuple(st[0] for st in stats)

    def first_logits(i):
        ctx = prepare_fn(i)
        return ctx, [logits_fn(ctx, s, jnp.int32(0)) for s in range(n)]

    def open_tile(zs):
        for s in range(n):
            z_scr[0, s] = zs[s]
            acc_scr[s] = jnp.zeros(acc_scr.shape[1:], F32)
        return (tuple(jnp.full((1, tq), -jnp.inf, F32) for _ in range(n)),
                tuple(_col_max(z) for z in zs))

    def close_tile(i, ms):
        update(i, lax.rem(i, 2), ms, None, True)
        finish_fn(i)

    def off_diagonal(i, ctx, state):
        def body(j, state):
            ms, tile_max = state
            slot = lax.rem(j, 2)
            z_next = [logits_fn(ctx, s, j + 1) for s in range(n)]
            ms = update(j, slot, ms, tile_max, False)
            for s in range(n):
                z_scr[1 - slot, s] = z_next[s]
            return ms, tuple(_col_max(z) for z in z_next)

        return lax.fori_loop(0, i, body, state)[0]

    _, zs = first_logits(jnp.int32(0))
    ms, _ = open_tile(zs)

    def outer(i, ms_prev):
        ctx, zs = first_logits(i)
        close_tile(i - 1, ms_prev)
        return off_diagonal(i, ctx, open_tile(zs))

    ms = lax.fori_loop(1, n_tiles, outer, ms)
    close_tile(jnp.int32(n_tiles - 1), ms)


def _softmax_result_t(acc):
    dv = acc.shape[0] - ONES_ROWS
    return acc[:dv] / acc[dv:dv + 1]


def _rope(x, cos_t, sin_up, sin_dn, half):
    return (x * cos_t + pltpu.roll(x, LANES - half, 1) * sin_up
            + pltpu.roll(x, half, 1) * sin_dn)


def _key_rows(j):
    return pl.ds(pl.multiple_of(j * K_TILE, K_TILE), K_TILE)


def _query_rows(i):
    return pl.ds(pl.multiple_of(i * Q_TILE, Q_TILE), Q_TILE)


def _batch_spec(s_len):
    return lambda slab: pl.BlockSpec((N_HEADS, s_len, LANES), lambda b: (slab, b, 0))


def _softmax_scratch(s_len):
    return [pltpu.VMEM((N_HEADS, s_len // K_TILE, HEAD_DIM + ONES_ROWS, K_TILE), BF16),
            pltpu.VMEM((2, N_HEADS, K_TILE, Q_TILE), F32),
            pltpu.VMEM((N_HEADS, HEAD_DIM + ONES_ROWS, Q_TILE), F32)]


def _fox_kernel(q_ref, k_ref, v_ref, g_ref, fk_ref, fq_ref, o_ref, vt_scr, z_scr, acc_scr):
    scale = HEAD_DIM ** -0.5 * LOG2E
    _transpose_values(v_ref, vt_scr)

    def prepare(i):
        fq_all = fq_ref[i]
        return ([q_ref[h, _query_rows(i), :] for h in range(N_HEADS)],
                [fq_all[h:h + 1, :] for h in range(N_HEADS)])

    def logits(ctx, h, j):
        qs, fqs = ctx
        fk = fk_ref[h, _key_rows(j), :]
        fk = jnp.concatenate([fk] * (Q_TILE // LANES), axis=-1)
        return _nt_dot(k_ref[h, _key_rows(j), :], qs[h]) * scale + (fqs[h] - fk)

    def finish(i):
        for h in range(N_HEADS):
            o = _softmax_result_t(acc_scr[h]).T
            gate = _silu(g_ref[h, _query_rows(i), :].astype(F32))
            o_ref[h, _query_rows(i), :] = (o * gate).astype(BF16)

    _causal_tiles_t(q_ref.shape[1] // Q_TILE, prepare, logits, lambda h, j: vt_scr[h, j], finish,
                    z_scr, acc_scr)


def _fox(proj, fk_rep, fq_rows, *, batch, s_len):
    t = proj.shape[1]
    spec = _batch_spec(s_len)
    return pl.pallas_call(
        _fox_kernel,
        grid=(batch,),
        in_specs=[
            spec(_SLAB["fox_q"]), spec(_SLAB["fox_k"]), spec(_SLAB["fox_v"]), spec(_SLAB["fox_g"]),
            spec(0),
            pl.BlockSpec((None, s_len // Q_TILE, GATE_ROWS, Q_TILE), lambda b: (b, 0, 0, 0)),
        ],
        out_specs=spec(0),
        out_shape=jax.ShapeDtypeStruct((N_HEADS, t, LANES), BF16),
        scratch_shapes=_softmax_scratch(s_len),
        compiler_params=pltpu.CompilerParams(dimension_semantics=("arbitrary",)),
        name="fox",
    )(proj, proj, proj, proj, fk_rep, fq_rows)


def _diff_kernel(lam_init, q_ref, k_ref, v_ref, g_ref, lam_ref, ng_ref, c_ref, u_ref, d_ref, o_ref,
                 kz_scr, qr_scr, a1_scr, vt_scr, z_scr, acc_scr):
    half = DIFF_QK_DIM // ROPE_FRACTION // 2
    scale = DIFF_QK_DIM ** -0.5 * LOG2E

    _transpose_values(v_ref, vt_scr)
    first = lax.broadcasted_iota(jnp.int32, (k_ref.shape[1], LANES), 1) < DIFF_QK_DIM
    for h in range(N_HEADS):
        kr = _rope(k_ref[h].astype(F32), c_ref[...], u_ref[...], d_ref[...], half)
        kz_scr[0, h] = jnp.where(first, kr, 0.0).astype(BF16)
        kz_scr[1, h] = jnp.where(first, 0.0, kr).astype(BF16)
        qr_scr[h] = _rope(q_ref[h].astype(F32), c_ref[...], u_ref[...], d_ref[...], half).astype(BF16)

    lf = lam_ref[...]
    lam = (jnp.exp(jnp.sum(lf[0:1] * lf[1:2], axis=-1, keepdims=True))
           - jnp.exp(jnp.sum(lf[2:3] * lf[3:4], axis=-1, keepdims=True)) + lam_init)

    def prepare(i):
        return [qr_scr[h, _query_rows(i), :] for h in range(N_HEADS)]

    def finish_first(i):
        for h in range(N_HEADS):
            a1_scr[h, _query_rows(i), :] = _softmax_result_t(acc_scr[h]).T

    def finish_second(i):
        for h in range(N_HEADS):
            o = a1_scr[h, _query_rows(i), :] - lam * _softmax_result_t(acc_scr[h]).T
            o = o * lax.rsqrt(jnp.mean(o * o, axis=-1, keepdims=True) + RMS_EPS) * ng_ref[...]
            o = o * (1.0 - lam_init)
            gate = _silu(g_ref[h, _query_rows(i), :].astype(F32))
            o_ref[h, _query_rows(i), :] = (o * gate).astype(BF16)

    for comp, finish in ((0, finish_first), (1, finish_second)):
        def logits(qs, h, j, comp=comp):
            return _nt_dot(kz_scr[comp, h, _key_rows(j), :], qs[h]) * scale

        _causal_tiles_t(q_ref.shape[1] // Q_TILE, prepare, logits, lambda h, j: vt_scr[h, j], finish,
                        z_scr, acc_scr)


def _diff(proj, diff_lam, diff_norm_g, tables, lam_init, *, batch, s_len):
    t = proj.shape[1]
    spec = _batch_spec(s_len)
    table_spec = pl.BlockSpec((s_len, LANES), lambda b: (0, 0))
    return pl.pallas_call(
        functools.partial(_diff_kernel, lam_init),
        grid=(batch,),
        in_specs=[
            spec(_SLAB["diff_q"]), spec(_SLAB["diff_k"]), spec(_SLAB["diff_v"]), spec(_SLAB["diff_g"]),
            pl.BlockSpec((4, DIFF_QK_DIM), lambda b: (0, 0)),
            pl.BlockSpec((1, HEAD_DIM), lambda b: (0, 0)),
            table_spec, table_spec, table_spec,
        ],
        out_specs=spec(0),
        out_shape=jax.ShapeDtypeStruct((N_HEADS, t, LANES), BF16),
        scratch_shapes=[pltpu.VMEM((2, N_HEADS, s_len, LANES), BF16),
                        pltpu.VMEM((N_HEADS, s_len, LANES), BF16),
                        pltpu.VMEM((N_HEADS, s_len, LANES), F32),
                        ] + _softmax_scratch(s_len),
        compiler_params=pltpu.CompilerParams(dimension_semantics=("arbitrary",)),
        name="diff",
    )(proj, proj, proj, proj, diff_lam, diff_norm_g.reshape(1, HEAD_DIM), *tables)


def _moba_kernel(q_ref, k_ref, v_ref, g_ref, c_ref, u_ref, d_ref,
                 o_ref, kr_scr, kmean_scr, vt_scr, z_scr, acc_scr):
    tq = Q_TILE
    half = HEAD_DIM // ROPE_FRACTION // 2
    scale = HEAD_DIM ** -0.5 * LOG2E
    n_blk = k_ref.shape[1] // MOBA_BLOCK

    _transpose_values(v_ref, vt_scr)
    kmean_scr[...] = jnp.zeros(kmean_scr.shape, F32)
    for h in range(N_HEADS):
        kr = _rope(k_ref[h].astype(F32), c_ref[...], u_ref[...], d_ref[...], half)
        kr_scr[h] = kr.astype(BF16)
        for n in range(n_blk):
            blk = kr[n * MOBA_BLOCK:(n + 1) * MOBA_BLOCK]
            kmean_scr[h, n:n + 1, :] = jnp.sum(blk, axis=0, keepdims=True) * (1.0 / MOBA_BLOCK)

    blk_id = lax.broadcasted_iota(jnp.int32, (GATE_ROWS, tq), 0).astype(F32)

    def prepare(i):
        rows = _query_rows(i)
        past = blk_id < lax.convert_element_type(i, F32)
        qs, biases = [], []
        for h in range(N_HEADS):
            q = _rope(q_ref[h, rows, :].astype(F32), c_ref[rows, :], u_ref[rows, :], d_ref[rows, :],
                      half).astype(BF16)
            gate = _nt_dot(kmean_scr[h].astype(BF16), q)
            gate = jnp.where(past, gate, -jnp.inf)
            open_ = past
            sel = jnp.zeros((GATE_ROWS, tq), jnp.bool_)
            for _ in range(MOBA_TOPK):
                top = jnp.max(gate, axis=0, keepdims=True)
                idx = jnp.min(jnp.where(gate == top, blk_id, float(GATE_ROWS)), axis=0, keepdims=True)
                pick = (blk_id == idx) & open_
                sel = sel | pick
                open_ = open_ & jnp.logical_not(pick)
                gate = jnp.where(pick, -jnp.inf, gate)
            qs.append(q)
            biases.append(jnp.where(sel, 0.0, MASKED_LOGIT))
        return i, qs, biases

    def logits(ctx, h, j):
        i, qs, biases = ctx
        here = blk_id == lax.convert_element_type(j, F32)
        bias = jnp.sum(jnp.where(here, biases[h], 0.0), axis=0, keepdims=True)
        bias = jnp.where(j == i, 0.0, bias)
        return _nt_dot(kr_scr[h, _key_rows(j), :], qs[h]) * scale + bias

    def finish(i):
        for h in range(N_HEADS):
            o = _softmax_result_t(acc_scr[h]).T
            gate = _silu(g_ref[h, _query_rows(i), :].astype(F32))
            o_ref[h, _query_rows(i), :] = (o * gate).astype(BF16)

    _causal_tiles_t(q_ref.shape[1] // tq, prepare, logits, lambda h, j: vt_scr[h, j], finish,
                    z_scr, acc_scr)


def _moba(proj, tables, *, batch, s_len):
    t = proj.shape[1]
    spec = _batch_spec(s_len)
    table_spec = pl.BlockSpec((s_len, LANES), lambda b: (0, 0))
    return pl.pallas_call(
        _moba_kernel,
        grid=(batch,),
        in_specs=[
            spec(_SLAB["moba_q"]), spec(_SLAB["moba_k"]), spec(_SLAB["moba_v"]), spec(_SLAB["moba_g"]),
            table_spec, table_spec, table_spec,
        ],
        out_specs=spec(0),
        out_shape=jax.ShapeDtypeStruct((N_HEADS, t, LANES), BF16),
        scratch_shapes=[pltpu.VMEM((N_HEADS, s_len, LANES), BF16),
                        pltpu.VMEM((N_HEADS, GATE_ROWS, LANES), F32),
                        ] + _softmax_scratch(s_len),
        compiler_params=pltpu.CompilerParams(dimension_semantics=("arbitrary",)),
        name="moba",
    )(proj, proj, proj, proj, *tables)


def _gla_kernel(x_ref, v_ref, g_ref, p_ref, ng_ref, o_ref):
    s_len = x_ref.shape[1]
    c_len = GLA_CHUNK
    scale = GLA_DK ** -0.5
    lane = lax.broadcasted_iota(jnp.int32, (c_len, LANES), 1)
    qhalf = lane < GLA_DK
    qhalf_row = lax.broadcasted_iota(jnp.int32, (1, LANES), 1) < GLA_DK
    row = lax.broadcasted_iota(jnp.int32, (c_len, c_len), 0)
    col = lax.broadcasted_iota(jnp.int32, (c_len, c_len), 1)
    causal = col <= row

    group = GLA_GROUP
    pairs = [(h, c) for h in range(N_HEADS) for c in range(group)]

    def body(g, carries):
        prep = {}
        for h, c in pairs:
            rows = pl.ds(pl.multiple_of(g * (group * c_len), group * c_len) + c * c_len, c_len)
            qcb = x_ref[h // 2, rows, :].astype(F32)
            kcb = x_ref[2 + h // 2, rows, :].astype(F32)
            if h % 2 == 0:
                x = jnp.where(qhalf, qcb, pltpu.roll(kcb, GLA_DK, 1))
            else:
                x = jnp.where(qhalf, pltpu.roll(qcb, GLA_DK, 1), kcb)
            p = p_ref[h, rows, :]
            base = carries[h][1] if c == 0 else prep[h, c - 1]["p_last"]
            cum = p - base
            xt = x * jnp.exp(jnp.where(qhalf, cum, -cum))
            kr = pltpu.roll(xt, GLA_DK, 1)
            decay = jnp.where(qhalf_row, jnp.exp(cum[c_len - 1:c_len]), 0.0)
            prep[h, c] = dict(
                rows=rows, qz=jnp.where(qhalf, xt, 0.0).astype(BF16), kr=kr.astype(BF16),
                khat=(jnp.where(qhalf, kr, 0.0) * decay).astype(BF16),
                decay=decay, v=v_ref[h, rows, :], p_last=p[c_len - 1:c_len])
        scores, upd = {}, {}
        for h, c in pairs:
            d = prep[h, c]
            scores[h, c] = _nt_dot(d["qz"], d["kr"])
            upd[h, c] = lax.dot_general(d["v"], d["khat"], (((0,), (0,)), ((), ())),
                                        preferred_element_type=F32)
        states = {}
        for h in range(N_HEADS):
            states[h, 0] = carries[h][0]
            for c in range(group):
                states[h, c + 1] = states[h, c] * prep[h, c]["decay"] + upd[h, c]
        outs = {}
        for h, c in pairs:
            sc = jnp.where(causal, scores[h, c] * scale, 0.0).astype(BF16)
            outs[h, c] = (jnp.dot(sc, prep[h, c]["v"], preferred_element_type=F32),
                          _nt_dot(prep[h, c]["qz"], states[h, c].astype(BF16)))
        for h, c in pairs:
            o = outs[h, c][0] + outs[h, c][1] * scale
            y = o * lax.rsqrt(jnp.mean(o * o, axis=-1, keepdims=True) + RMS_EPS) * ng_ref[...]
            rows = prep[h, c]["rows"]
            o_ref[h, rows, :] = (y * _silu(g_ref[h, rows, :].astype(F32))).astype(BF16)
        return tuple((states[h, group], prep[h, group - 1]["p_last"]) for h in range(N_HEADS))

    init = tuple((jnp.zeros((HEAD_DIM, LANES), F32), jnp.zeros((1, LANES), F32))
                 for _ in range(N_HEADS))
    lax.fori_loop(0, s_len // (group * c_len), body, init)


def _gla(proj, p_cum, gla_norm_g, *, batch, s_len):
    t = proj.shape[1]
    spec = _batch_spec(s_len)
    return pl.pallas_call(
        _gla_kernel,
        grid=(batch,),
        in_specs=[
            spec(_SLAB["gla_qk"]), spec(_SLAB["gla_v"]), spec(_SLAB["gla_g"]), spec(0),
            pl.BlockSpec((1, HEAD_DIM), lambda b: (0, 0)),
        ],
        out_specs=spec(0),
        out_shape=jax.ShapeDtypeStruct((N_HEADS, t, LANES), BF16),
        compiler_params=pltpu.CompilerParams(dimension_semantics=("arbitrary",)),
        name="gla",
    )(proj, proj, proj, p_cum, gla_norm_g.reshape(1, HEAD_DIM))


def _outproj_kernel(final, x_ref, a_ref, b_ref, c_ref, d_ref, w_ref, fg_ref, o_ref):
    parts = [r[h] for r in (a_ref, b_ref, c_ref, d_ref) for h in range(N_HEADS)]
    mixed = jnp.concatenate(parts, axis=-1)
    y = x_ref[...] + jnp.dot(mixed, w_ref[...], preferred_element_type=F32)
    if final:
        y = y * lax.rsqrt(jnp.mean(y * y, axis=-1, keepdims=True) + RMS_EPS) * fg_ref[...]
    o_ref[...] = y


def _outproj(x2, mixers, w_out, layer, final_g, *, final, tm):
    t, d = x2.shape
    mspec = pl.BlockSpec((N_HEADS, tm, LANES), lambda i: (0, i, 0))
    return pl.pallas_call(
        functools.partial(_outproj_kernel, final),
        grid=(t // tm,),
        in_specs=[
            pl.BlockSpec((tm, d), lambda i: (i, 0)),
            mspec, mspec, mspec, mspec,
            pl.BlockSpec((None,) + w_out.shape[1:], lambda i: (layer, 0, 0)),
            pl.BlockSpec((1, d), lambda i: (0, 0)),
        ],
        out_specs=pl.BlockSpec((tm, d), lambda i: (i, 0)),
        out_shape=jax.ShapeDtypeStruct((t, d), F32),
        compiler_params=pltpu.CompilerParams(
            dimension_semantics=("arbitrary",), vmem_limit_bytes=VMEM_LIMIT_BYTES),
        name="outproj",
    )(x2, *mixers, w_out, final_g.reshape(1, d))


RELAYOUT_ROWS = 256
RELAYOUT_TAIL = 32


def _relayout_kernel(a_ref, b_ref, o_ref):
    r = pl.program_id(0)
    depth, rows, _ = o_ref.shape
    for l in range(depth):
        x = jnp.concatenate([a_ref[:, l, :], b_ref[:, l, :]], axis=0)
        out_lo = 0
        for (src_lo, src_hi) in _MAIN_RUNS:
            shift = src_lo - out_lo
            out_hi = out_lo + (src_hi - src_lo)
            assert shift <= RELAYOUT_TAIL and out_lo % rows == 0 and out_hi % rows == 0

            @pl.when((r >= out_lo // rows) & (r < out_hi // rows))
            def _(shift=shift, l=l, x=x):
                o_ref[l] = x[shift:shift + rows].astype(BF16)

            out_lo = out_hi


def _relayout_w_in(w_in):
    depth, d, d_in = w_in.shape
    wt = jnp.transpose(w_in, (2, 0, 1))
    rows, tail = RELAYOUT_ROWS, RELAYOUT_TAIL
    main = pl.pallas_call(
        _relayout_kernel,
        grid=(N_MAIN // rows,),
        in_specs=[pl.BlockSpec((rows, depth, d), lambda r: (r, 0, 0)),
                  pl.BlockSpec((tail, depth, d), lambda r: ((r + 1) * (rows // tail), 0, 0))],
        out_specs=pl.BlockSpec((depth, rows, d), lambda r: (0, r, 0)),
        out_shape=jax.ShapeDtypeStruct((depth, N_MAIN, d), BF16),
        compiler_params=pltpu.CompilerParams(dimension_semantics=("arbitrary",)),
        name="relayout",
    )(wt, wt)
    f_off, a_off = _SEG_OFF["fox_f"][0], _SEG_OFF["gla_a"][0]
    gate_cols = jnp.concatenate([w_in[..., f_off:f_off + N_HEADS],
                                 w_in[..., a_off:a_off + GLA_RANK]], axis=-1)
    small = jnp.pad(jnp.transpose(gate_cols, (0, 2, 1)),
                    ((0, 0), (0, LANES - N_HEADS - GLA_RANK), (0, 0))).astype(BF16)
    return main, small


def _rope_tables(s_len, comp_dim, rot_dim):
    half = rot_dim // 2
    inv_freq = ROPE_THETA ** (-jnp.arange(0, rot_dim, 2, dtype=F32) / rot_dim)
    ang = jnp.arange(s_len, dtype=F32)[:, None] * inv_freq[None, :]
    cos, sin = jnp.cos(ang), jnp.sin(ang)
    zeros = jnp.zeros((s_len, comp_dim - rot_dim), F32)
    ones = jnp.ones((s_len, comp_dim - rot_dim), F32)
    reps = LANES // comp_dim
    cos_t = jnp.tile(jnp.concatenate([cos, cos, ones], axis=-1), (1, reps))
    sin_up = jnp.tile(jnp.concatenate([-sin, jnp.zeros_like(sin), zeros], axis=-1), (1, reps))
    sin_dn = jnp.tile(jnp.concatenate([jnp.zeros_like(sin), sin, zeros], axis=-1), (1, reps))
    return cos_t, sin_up, sin_dn


def kernel(x, norm_g, w_in, fox_fb, diff_lam, diff_norm_g, gla_wa2, gla_ba, gla_norm_g, w_out,
           final_norm_g):
    batch, s_len, d_model = x.shape
    depth = w_in.shape[0]
    assert s_len % Q_TILE == 0 and Q_TILE == K_TILE == MOBA_BLOCK
    assert s_len // MOBA_BLOCK <= GATE_ROWS
    t = batch * s_len
    tm_in = min(1024, t)
    tn_in = 1536
    tm_out = min(512, t)

    w_main, w_small = _relayout_w_in(w_in)
    w_out_b = w_out.astype(BF16)
    fb_pad = jnp.pad(fox_fb, ((0, 0), (0, LANES - N_HEADS)))[:, None, :]
    wa_pad = jnp.pad(gla_wa2, ((0, 0), (N_HEADS, LANES - N_HEADS - GLA_RANK), (0, 0))).astype(BF16)
    ba = gla_ba[:, None, :]
    rope_diff = _rope_tables(s_len, DIFF_QK_DIM, DIFF_QK_DIM // ROPE_FRACTION)
    rope_moba = _rope_tables(s_len, HEAD_DIM, HEAD_DIM // ROPE_FRACTION)

    x2 = x.reshape(t, d_model)
    for l in range(depth):
        proj, small = _inproj(x2, norm_g[l], w_main, w_small, l, tm=tm_in, tn=tn_in)
        fk_rep, fq_rows, p_cum = _gates(small, fb_pad[l], wa_pad[l], ba[l], batch=batch, s_len=s_len)
        lam_init = 0.8 - 0.6 * math.exp(-0.3 * l)
        mixers = (
            _fox(proj, fk_rep, fq_rows, batch=batch, s_len=s_len),
            _diff(proj, diff_lam[l], diff_norm_g[l], rope_diff, lam_init, batch=batch, s_len=s_len),
            _moba(proj, rope_moba, batch=batch, s_len=s_len),
            _gla(proj, p_cum, gla_norm_g[l], batch=batch, s_len=s_len),
        )
        x2 = _outproj(x2, mixers, w_out_b, l, final_norm_g, final=(l == depth - 1), tm=tm_out)
    return x2.reshape(batch, s_len, d_model)
```

```python
import functools
import math

import jax
import jax.numpy as jnp
from jax import lax
from jax.experimental import pallas as pl
from jax.experimental.pallas import tpu as pltpu

F32 = jnp.float32
BF16 = jnp.bfloat16

LANES = 128
VMEM_LIMIT_BYTES = 56 * 2 ** 20
HEAD_DIM = 128
N_HEADS = 4
GROUP_W = N_HEADS * HEAD_DIM
DIFF_QK_DIM = HEAD_DIM // 2
MOBA_BLOCK = 256
MOBA_TOPK = 3
GLA_DK = HEAD_DIM // 2
GLA_RANK = 16
GLA_TAU = 16.0
GLA_CHUNK = 64
ROPE_THETA = 500000.0
ROPE_FRACTION = 4
RMS_EPS = 1e-6
MASKED_LOGIT = -1e30

_SEGMENTS = (
    ("fox_q", GROUP_W), ("fox_k", GROUP_W), ("fox_v", GROUP_W),
    ("fox_f", N_HEADS), ("fox_g", GROUP_W),
    ("diff_q", GROUP_W), ("diff_k", GROUP_W), ("diff_v", GROUP_W), ("diff_g", GROUP_W),
    ("moba_q", GROUP_W), ("moba_k", GROUP_W), ("moba_v", GROUP_W), ("moba_g", GROUP_W),
    ("gla_q", N_HEADS * GLA_DK), ("gla_k", N_HEADS * GLA_DK), ("gla_v", GROUP_W),
    ("gla_a", GLA_RANK), ("gla_g", GROUP_W),
)
_SEG_OFF = {}
_off = 0
for _name, _w in _SEGMENTS:
    _SEG_OFF[_name] = (_off, _w)
    _off += _w

_SLABS = ("fox_q", "fox_k", "fox_v", "fox_g", "diff_q", "diff_k", "diff_v", "diff_g",
          "moba_q", "moba_k", "moba_v", "moba_g", "gla_qk", "gla_v", "gla_g")
_SLAB = {name: i for i, name in enumerate(_SLABS)}
N_MAIN = len(_SLABS) * GROUP_W
_MAIN_RUNS = ((0, _SEG_OFF["fox_f"][0]),
              (_SEG_OFF["fox_g"][0], _SEG_OFF["gla_a"][0]),
              (_SEG_OFF["gla_g"][0], _off))
assert sum(b - a for a, b in _MAIN_RUNS) == N_MAIN

Q_TILE = 256
K_TILE = 256
CUM_BLOCK = 128
GATE_ROWS = 16
GLA_GROUP = 4
ONES_ROWS = 16
LOG2E = math.log2(math.e)


def _silu(x):
    return x * (1.0 / (1.0 + jnp.exp(-x)))


def _log_sigmoid(x):
    return jnp.minimum(x, 0.0) - jnp.log1p(jnp.exp(-jnp.abs(x)))


def _nt_dot(a, b):
    return lax.dot_general(a, b, (((1,), (1,)), ((), ())), preferred_element_type=F32)


def _inproj_kernel(x_ref, g_ref, w_ref, ws_ref, o_ref, small_ref, h_scr):
    @pl.when(pl.program_id(1) == 0)
    def _():
        x = x_ref[...]
        y = x * lax.rsqrt(jnp.mean(x * x, axis=-1, keepdims=True) + RMS_EPS) * g_ref[...]
        h = y.astype(BF16)
        h_scr[...] = h
        small_ref[...] = _nt_dot(h, ws_ref[...])

    acc = _nt_dot(h_scr[...], w_ref[...])
    for c in range(o_ref.shape[0]):
        o_ref[c] = acc[:, c * LANES:(c + 1) * LANES].astype(BF16)


def _inproj(x2, norm_g, w_main, w_small, layer, *, tm, tn):
    t, d = x2.shape
    n_main = w_main.shape[1]
    return pl.pallas_call(
        _inproj_kernel,
        grid=(t // tm, n_main // tn),
        in_specs=[
            pl.BlockSpec((tm, d), lambda i, j: (i, 0)),
            pl.BlockSpec((1, d), lambda i, j: (0, 0)),
            pl.BlockSpec((None, tn, d), lambda i, j: (layer, j, 0)),
            pl.BlockSpec((None, LANES, d), lambda i, j: (layer, 0, 0)),
        ],
        out_specs=[
            pl.BlockSpec((tn // LANES, tm, LANES), lambda i, j: (j, i, 0)),
            pl.BlockSpec((tm, LANES), lambda i, j: (i, 0)),
        ],
        out_shape=[
            jax.ShapeDtypeStruct((n_main // LANES, t, LANES), BF16),
            jax.ShapeDtypeStruct((t, LANES), F32),
        ],
        scratch_shapes=[pltpu.VMEM((tm, d), BF16)],
        compiler_params=pltpu.CompilerParams(
            dimension_semantics=("arbitrary", "arbitrary"), vmem_limit_bytes=VMEM_LIMIT_BYTES),
        name="inproj",
    )(x2, norm_g.reshape(1, d), w_main, w_small)


def _gates_kernel(small_ref, fb_ref, wa_ref, ba_ref, fk_ref, fq_ref, p_ref):
    s_len = small_ref.shape[0]
    small = small_ref[...]
    log_f = _log_sigmoid(small + fb_ref[...])
    a_logit = jnp.dot(small.astype(BF16), wa_ref[...], preferred_element_type=F32) + ba_ref[...]
    log_a = _log_sigmoid(a_logit) * (1.0 / GLA_TAU)
    both = jnp.concatenate([log_f, log_a], axis=-1)

    row = lax.broadcasted_iota(jnp.int32, (CUM_BLOCK, CUM_BLOCK), 0)
    col = lax.broadcasted_iota(jnp.int32, (CUM_BLOCK, CUM_BLOCK), 1)
    tri = (col <= row).astype(BF16)
    low_half = lax.broadcasted_iota(jnp.int32, (CUM_BLOCK, LANES), 1) < GLA_DK
    carry = jnp.zeros((1, both.shape[1]), F32)
    for c in range(s_len // CUM_BLOCK):
        blk = both[c * CUM_BLOCK:(c + 1) * CUM_BLOCK]
        hi = blk.astype(BF16)
        rest = blk - hi.astype(F32)
        mid = rest.astype(BF16)
        lo = (rest - mid.astype(F32)).astype(BF16)
        cum = carry + sum(jnp.dot(tri, part, preferred_element_type=F32) for part in (hi, mid, lo))
        carry = cum[CUM_BLOCK - 1:CUM_BLOCK]
        rows = slice(c * CUM_BLOCK, (c + 1) * CUM_BLOCK)
        f_cum = cum[:, :LANES] * LOG2E
        per_tile = Q_TILE // CUM_BLOCK
        fq_ref[c // per_tile, :, (c % per_tile) * CUM_BLOCK:(c % per_tile + 1) * CUM_BLOCK] = (
            f_cum.T[:GATE_ROWS])
        for h in range(N_HEADS):
            fk_ref[h, rows, :] = jnp.broadcast_to(f_cum[:, h:h + 1], (CUM_BLOCK, LANES))
            pair = cum[:, (1 + h // 2) * LANES:(2 + h // 2) * LANES]
            swapped = pltpu.roll(pair, GLA_DK, 1)
            p_ref[h, rows, :] = (jnp.where(low_half, pair, swapped) if h % 2 == 0
                                 else jnp.where(low_half, swapped, pair))


def _gates(small, fb_pad, wa_pad, ba, *, batch, s_len):
    t = small.shape[0]
    return pl.pallas_call(
        _gates_kernel,
        grid=(batch,),
        in_specs=[
            pl.BlockSpec((s_len, LANES), lambda b: (b, 0)),
            pl.BlockSpec((1, LANES), lambda b: (0, 0)),
            pl.BlockSpec((LANES, N_HEADS * GLA_DK), lambda b: (0, 0)),
            pl.BlockSpec((1, N_HEADS * GLA_DK), lambda b: (0, 0)),
        ],
        out_specs=[
            pl.BlockSpec((N_HEADS, s_len, LANES), lambda b: (0, b, 0)),
            pl.BlockSpec((None, s_len // Q_TILE, GATE_ROWS, Q_TILE), lambda b: (b, 0, 0, 0)),
            pl.BlockSpec((N_HEADS, s_len, LANES), lambda b: (0, b, 0)),
        ],
        out_shape=[
            jax.ShapeDtypeStruct((N_HEADS, t, LANES), F32),
            jax.ShapeDtypeStruct((batch, s_len // Q_TILE, GATE_ROWS, Q_TILE), F32),
            jax.ShapeDtypeStruct((N_HEADS, t, LANES), F32),
        ],
        compiler_params=pltpu.CompilerParams(dimension_semantics=("arbitrary",)),
        name="gates",
    )(small, fb_pad, wa_pad, ba)


def _causal_mask_t(tk, tq):
    key = lax.broadcasted_iota(jnp.int32, (tk, tq), 0)
    qry = lax.broadcasted_iota(jnp.int32, (tk, tq), 1)
    return key <= qry


def _transpose_values(v_ref, vt_scr):
    dv = v_ref.shape[2]
    for h in range(v_ref.shape[0]):
        for j in range(v_ref.shape[1] // K_TILE):
            blk = v_ref[h, j * K_TILE:(j + 1) * K_TILE, :].astype(F32)
            vt_scr[h, j, :dv, :] = blk.T.astype(BF16)
            vt_scr[h, j, dv:, :] = jnp.ones((ONES_ROWS, K_TILE), BF16)


def _col_max(z):
    rows = z.shape[0]
    while rows > 8:
        rows //= 2
        z = jnp.maximum(z[:rows], z[rows:])
    return jnp.max(z, axis=0, keepdims=True)


def _causal_tiles_t(n_tiles, prepare_fn, logits_fn, vt_fn, finish_fn, z_scr, acc_scr):
    n, _, tq = acc_scr.shape

    def update(j, slot, ms, tile_max, masked):
        stats = []
        for s in range(n):
            z = z_scr[slot, s]
            if masked:
                z = jnp.where(_causal_mask_t(K_TILE, tq), z, -jnp.inf)
                m_new = jnp.maximum(ms[s], _col_max(z))
            else:
                m_new = jnp.maximum(ms[s], tile_max[s])
            stats.append((m_new, jnp.exp2(ms[s] - m_new), jnp.exp2(z - m_new).astype(BF16)))
        for s in range(n):
            pv = jnp.dot(vt_fn(s, j), stats[s][2], preferred_element_type=F32)
            acc_scr[s] = stats[s][1] * acc_scr[s] + pv
        return tuple(st[0] for st in stats)

    def first_logits(i):
        ctx = prepare_fn(i)
        return ctx, [logits_fn(ctx, s, jnp.int32(0)) for s in range(n)]

    def open_tile(zs):
        for s in range(n):
            z_scr[0, s] = zs[s]
            acc_scr[s] = jnp.zeros(acc_scr.shape[1:], F32)
        return (tuple(jnp.full((1, tq), -jnp.inf, F32) for _ in range(n)),
                tuple(_col_max(z) for z in zs))

    def close_tile(i, ms):
        update(i, lax.rem(i, 2), ms, None, True)
        finish_fn(i)

    def off_diagonal(i, ctx, state):
        def body(j, state):
            ms, tile_max = state
            slot = lax.rem(j, 2)
            z_next = [logits_fn(ctx, s, j + 1) for s in range(n)]
            ms = update(j, slot, ms, tile_max, False)
            for s in range(n):
                z_scr[1 - slot, s] = z_next[s]
            return ms, tuple(_col_max(z) for z in z_next)

        return lax.fori_loop(0, i, body, state)[0]

    _, zs = first_logits(jnp.int32(0))
    ms, _ = open_tile(zs)

    def outer(i, ms_prev):
        ctx, zs = first_logits(i)
        close_tile(i - 1, ms_prev)
        return off_diagonal(i, ctx, open_tile(zs))

    ms = lax.fori_loop(1, n_tiles, outer, ms)
    close_tile(jnp.int32(n_tiles - 1), ms)


def _softmax_result_t(acc):
    dv = acc.shape[0] - ONES_ROWS
    return acc[:dv] / acc[dv:dv + 1]


def _rope(x, cos_t, sin_up, sin_dn, half):
    return (x * cos_t + pltpu.roll(x, LANES - half, 1) * sin_up
            + pltpu.roll(x, half, 1) * sin_dn)


def _key_rows(j):
    return pl.ds(pl.multiple_of(j * K_TILE, K_TILE), K_TILE)


def _query_rows(i):
    return pl.ds(pl.multiple_of(i * Q_TILE, Q_TILE), Q_TILE)


def _batch_spec(s_len):
    return lambda slab: pl.BlockSpec((N_HEADS, s_len, LANES), lambda b: (slab, b, 0))


def _softmax_scratch(s_len, streams=N_HEADS):
    return [pltpu.VMEM((N_HEADS, s_len // K_TILE, HEAD_DIM + ONES_ROWS, K_TILE), BF16),
            pltpu.VMEM((2, streams, K_TILE, Q_TILE), F32),
            pltpu.VMEM((streams, HEAD_DIM + ONES_ROWS, Q_TILE), F32)]


def _fox_kernel(q_ref, k_ref, v_ref, g_ref, fk_ref, fq_ref, o_ref, vt_scr, z_scr, acc_scr):
    scale = HEAD_DIM ** -0.5 * LOG2E
    _transpose_values(v_ref, vt_scr)
    n = acc_scr.shape[0]

    for h0 in range(0, N_HEADS, n):
        def prepare(i, h0=h0):
            fq_all = fq_ref[i]
            return ([q_ref[h0 + s, _query_rows(i), :] for s in range(n)],
                    [fq_all[h0 + s:h0 + s + 1, :] for s in range(n)])

        def logits(ctx, s, j, h0=h0):
            qs, fqs = ctx
            fk = fk_ref[h0 + s, _key_rows(j), :]
            fk = jnp.concatenate([fk] * (Q_TILE // LANES), axis=-1)
            return _nt_dot(k_ref[h0 + s, _key_rows(j), :], qs[s]) * scale + (fqs[s] - fk)

        def finish(i, h0=h0):
            for s in range(n):
                o = _softmax_result_t(acc_scr[s]).T
                gate = _silu(g_ref[h0 + s, _query_rows(i), :].astype(F32))
                o_ref[h0 + s, _query_rows(i), :] = (o * gate).astype(BF16)

        _causal_tiles_t(q_ref.shape[1] // Q_TILE, prepare, logits,
                        lambda s, j, h0=h0: vt_scr[h0 + s, j], finish, z_scr, acc_scr)


def _fox(proj, fk_rep, fq_rows, *, batch, s_len):
    t = proj.shape[1]
    spec = _batch_spec(s_len)
    return pl.pallas_call(
        _fox_kernel,
        grid=(batch,),
        in_specs=[
            spec(_SLAB["fox_q"]), spec(_SLAB["fox_k"]), spec(_SLAB["fox_v"]), spec(_SLAB["fox_g"]),
            spec(0),
            pl.BlockSpec((None, s_len // Q_TILE, GATE_ROWS, Q_TILE), lambda b: (b, 0, 0, 0)),
        ],
        out_specs=spec(0),
        out_shape=jax.ShapeDtypeStruct((N_HEADS, t, LANES), BF16),
        scratch_shapes=_softmax_scratch(s_len, streams=2),
        compiler_params=pltpu.CompilerParams(dimension_semantics=("arbitrary",)),
        name="fox",
    )(proj, proj, proj, proj, fk_rep, fq_rows)


def _diff_kernel(lam_init, q_ref, k_ref, v_ref, g_ref, lam_ref, ng_ref, c_ref, u_ref, d_ref, o_ref,
                 kz_scr, qr_scr, a1_scr, vt_scr, z_scr, acc_scr):
    half = DIFF_QK_DIM // ROPE_FRACTION // 2
    scale = DIFF_QK_DIM ** -0.5 * LOG2E

    _transpose_values(v_ref, vt_scr)
    first = lax.broadcasted_iota(jnp.int32, (k_ref.shape[1], LANES), 1) < DIFF_QK_DIM
    for h in range(N_HEADS):
        kr = _rope(k_ref[h].astype(F32), c_ref[...], u_ref[...], d_ref[...], half)
        kz_scr[0, h] = jnp.where(first, kr, 0.0).astype(BF16)
        kz_scr[1, h] = jnp.where(first, 0.0, kr).astype(BF16)
        qr_scr[h] = _rope(q_ref[h].astype(F32), c_ref[...], u_ref[...], d_ref[...], half).astype(BF16)

    lf = lam_ref[...]
    lam = (jnp.exp(jnp.sum(lf[0:1] * lf[1:2], axis=-1, keepdims=True))
           - jnp.exp(jnp.sum(lf[2:3] * lf[3:4], axis=-1, keepdims=True)) + lam_init)

    def prepare(i):
        return [qr_scr[h, _query_rows(i), :] for h in range(N_HEADS)]

    def finish_first(i):
        for h in range(N_HEADS):
            a1_scr[h, _query_rows(i), :] = _softmax_result_t(acc_scr[h]).T

    def finish_second(i):
        for h in range(N_HEADS):
            o = a1_scr[h, _query_rows(i), :] - lam * _softmax_result_t(acc_scr[h]).T
            o = o * lax.rsqrt(jnp.mean(o * o, axis=-1, keepdims=True) + RMS_EPS) * ng_ref[...]
            o = o * (1.0 - lam_init)
            gate = _silu(g_ref[h, _query_rows(i), :].astype(F32))
            o_ref[h, _query_rows(i), :] = (o * gate).astype(BF16)

    for comp, finish in ((0, finish_first), (1, finish_second)):
        def logits(qs, h, j, comp=comp):
            return _nt_dot(kz_scr[comp, h, _key_rows(j), :], qs[h]) * scale

        _causal_tiles_t(q_ref.shape[1] // Q_TILE, prepare, logits, lambda h, j: vt_scr[h, j], finish,
                        z_scr, acc_scr)


def _diff(proj, diff_lam, diff_norm_g, tables, lam_init, *, batch, s_len):
    t = proj.shape[1]
    spec = _batch_spec(s_len)
    table_spec = pl.BlockSpec((s_len, LANES), lambda b: (0, 0))
    return pl.pallas_call(
        functools.partial(_diff_kernel, lam_init),
        grid=(batch,),
        in_specs=[
            spec(_SLAB["diff_q"]), spec(_SLAB["diff_k"]), spec(_SLAB["diff_v"]), spec(_SLAB["diff_g"]),
            pl.BlockSpec((4, DIFF_QK_DIM), lambda b: (0, 0)),
            pl.BlockSpec((1, HEAD_DIM), lambda b: (0, 0)),
            table_spec, table_spec, table_spec,
        ],
        out_specs=spec(0),
        out_shape=jax.ShapeDtypeStruct((N_HEADS, t, LANES), BF16),
        scratch_shapes=[pltpu.VMEM((2, N_HEADS, s_len, LANES), BF16),
                        pltpu.VMEM((N_HEADS, s_len, LANES), BF16),
                        pltpu.VMEM((N_HEADS, s_len, LANES), F32),
                        ] + _softmax_scratch(s_len),
        compiler_params=pltpu.CompilerParams(dimension_semantics=("arbitrary",)),
        name="diff",
    )(proj, proj, proj, proj, diff_lam, diff_norm_g.reshape(1, HEAD_DIM), *tables)


def _moba_kernel(q_ref, k_ref, v_ref, g_ref, c_ref, u_ref, d_ref,
                 o_ref, kr_scr, kmean_scr, vt_scr, z_scr, acc_scr):
    tq = Q_TILE
    half = HEAD_DIM // ROPE_FRACTION // 2
    scale = HEAD_DIM ** -0.5 * LOG2E
    n_blk = k_ref.shape[1] // MOBA_BLOCK

    _transpose_values(v_ref, vt_scr)
    kmean_scr[...] = jnp.zeros(kmean_scr.shape, F32)
    for h in range(N_HEADS):
        kr = _rope(k_ref[h].astype(F32), c_ref[...], u_ref[...], d_ref[...], half)
        kr_scr[h] = kr.astype(BF16)
        for n in range(n_blk):
            blk = kr[n * MOBA_BLOCK:(n + 1) * MOBA_BLOCK]
            kmean_scr[h, n:n + 1, :] = jnp.sum(blk, axis=0, keepdims=True) * (1.0 / MOBA_BLOCK)

    blk_id = lax.broadcasted_iota(jnp.int32, (GATE_ROWS, tq), 0).astype(F32)

    def prepare(i):
        rows = _query_rows(i)
        past = blk_id < lax.convert_element_type(i, F32)
        qs, biases = [], []
        for h in range(N_HEADS):
            q = _rope(q_ref[h, rows, :].astype(F32), c_ref[rows, :], u_ref[rows, :], d_ref[rows, :],
                      half).astype(BF16)
            gate = _nt_dot(kmean_scr[h].astype(BF16), q)
            gate = jnp.where(past, gate, -jnp.inf)
            open_ = past
            sel = jnp.zeros((GATE_ROWS, tq), jnp.bool_)
            for _ in range(MOBA_TOPK):
                top = jnp.max(gate, axis=0, keepdims=True)
                idx = jnp.min(jnp.where(gate == top, blk_id, float(GATE_ROWS)), axis=0, keepdims=True)
                pick = (blk_id == idx) & open_
                sel = sel | pick
                open_ = open_ & jnp.logical_not(pick)
                gate = jnp.where(pick, -jnp.inf, gate)
            qs.append(q)
            biases.append(jnp.where(sel, 0.0, MASKED_LOGIT))
        return i, qs, biases

    def logits(ctx, h, j):
        i, qs, biases = ctx
        here = blk_id == lax.convert_element_type(j, F32)
        bias = jnp.sum(jnp.where(here, biases[h], 0.0), axis=0, keepdims=True)
        bias = jnp.where(j == i, 0.0, bias)
        return _nt_dot(kr_scr[h, _key_rows(j), :], qs[h]) * scale + bias

    def finish(i):
        for h in range(N_HEADS):
            o = _softmax_result_t(acc_scr[h]).T
            gate = _silu(g_ref[h, _query_rows(i), :].astype(F32))
            o_ref[h, _query_rows(i), :] = (o * gate).astype(BF16)

    _causal_tiles_t(q_ref.shape[1] // tq, prepare, logits, lambda h, j: vt_scr[h, j], finish,
                    z_scr, acc_scr)


def _moba(proj, tables, *, batch, s_len):
    t = proj.shape[1]
    spec = _batch_spec(s_len)
    table_spec = pl.BlockSpec((s_len, LANES), lambda b: (0, 0))
    return pl.pallas_call(
        _moba_kernel,
        grid=(batch,),
        in_specs=[
            spec(_SLAB["moba_q"]), spec(_SLAB["moba_k"]), spec(_SLAB["moba_v"]), spec(_SLAB["moba_g"]),
            table_spec, table_spec, table_spec,
        ],
        out_specs=spec(0),
        out_shape=jax.ShapeDtypeStruct((N_HEADS, t, LANES), BF16),
        scratch_shapes=[pltpu.VMEM((N_HEADS, s_len, LANES), BF16),
                        pltpu.VMEM((N_HEADS, GATE_ROWS, LANES), F32),
                        ] + _softmax_scratch(s_len),
        compiler_params=pltpu.CompilerParams(dimension_semantics=("arbitrary",)),
        name="moba",
    )(proj, proj, proj, proj, *tables)


def _gla_kernel(x_ref, v_ref, g_ref, p_ref, ng_ref, o_ref):
    s_len = x_ref.shape[1]
    c_len = GLA_CHUNK
    scale = GLA_DK ** -0.5
    lane = lax.broadcasted_iota(jnp.int32, (c_len, LANES), 1)
    qhalf = lane < GLA_DK
    qhalf_row = lax.broadcasted_iota(jnp.int32, (1, LANES), 1) < GLA_DK
    row = lax.broadcasted_iota(jnp.int32, (c_len, c_len), 0)
    col = lax.broadcasted_iota(jnp.int32, (c_len, c_len), 1)
    causal = col <= row

    group = GLA_GROUP
    pairs = [(h, c) for h in range(N_HEADS) for c in range(group)]

    def body(g, carries):
        prep = {}
        for h, c in pairs:
            rows = pl.ds(pl.multiple_of(g * (group * c_len), group * c_len) + c * c_len, c_len)
            qcb = x_ref[h // 2, rows, :].astype(F32)
            kcb = x_ref[2 + h // 2, rows, :].astype(F32)
            if h % 2 == 0:
                x = jnp.where(qhalf, qcb, pltpu.roll(kcb, GLA_DK, 1))
            else:
                x = jnp.where(qhalf, pltpu.roll(qcb, GLA_DK, 1), kcb)
            p = p_ref[h, rows, :]
            base = carries[h][1] if c == 0 else prep[h, c - 1]["p_last"]
            cum = p - base
            xt = x * jnp.exp(jnp.where(qhalf, cum, -cum))
            kr = pltpu.roll(xt, GLA_DK, 1)
            decay = jnp.where(qhalf_row, jnp.exp(cum[c_len - 1:c_len]), 0.0)
            prep[h, c] = dict(
                rows=rows, qz=jnp.where(qhalf, xt, 0.0).astype(BF16), kr=kr.astype(BF16),
                khat=(jnp.where(qhalf, kr, 0.0) * decay).astype(BF16),
                decay=decay, v=v_ref[h, rows, :], p_last=p[c_len - 1:c_len])
        scores, upd = {}, {}
        for h, c in pairs:
            d = prep[h, c]
            scores[h, c] = _nt_dot(d["qz"], d["kr"])
            upd[h, c] = lax.dot_general(d["v"], d["khat"], (((0,), (0,)), ((), ())),
                                        preferred_element_type=F32)
        states = {}
        for h in range(N_HEADS):
            states[h, 0] = carries[h][0]
            for c in range(group):
                states[h, c + 1] = states[h, c] * prep[h, c]["decay"] + upd[h, c]
        outs = {}
        for h, c in pairs:
            sc = jnp.where(causal, scores[h, c] * scale, 0.0).astype(BF16)
            outs[h, c] = (jnp.dot(sc, prep[h, c]["v"], preferred_element_type=F32),
                          _nt_dot(prep[h, c]["qz"], states[h, c].astype(BF16)))
        for h, c in pairs:
            o = outs[h, c][0] + outs[h, c][1] * scale
            y = o * lax.rsqrt(jnp.mean(o * o, axis=-1, keepdims=True) + RMS_EPS) * ng_ref[...]
            rows = prep[h, c]["rows"]
            o_ref[h, rows, :] = (y * _silu(g_ref[h, rows, :].astype(F32))).astype(BF16)
        return tuple((states[h, group], prep[h, group - 1]["p_last"]) for h in range(N_HEADS))

    init = tuple((jnp.zeros((HEAD_DIM, LANES), F32), jnp.zeros((1, LANES), F32))
                 for _ in range(N_HEADS))
    lax.fori_loop(0, s_len // (group * c_len), body, init)


def _gla(proj, p_cum, gla_norm_g, *, batch, s_len):
    t = proj.shape[1]
    spec = _batch_spec(s_len)
    return pl.pallas_call(
        _gla_kernel,
        grid=(batch,),
        in_specs=[
            spec(_SLAB["gla_qk"]), spec(_SLAB["gla_v"]), spec(_SLAB["gla_g"]), spec(0),
            pl.BlockSpec((1, HEAD_DIM), lambda b: (0, 0)),
        ],
        out_specs=spec(0),
        out_shape=jax.ShapeDtypeStruct((N_HEADS, t, LANES), BF16),
        compiler_params=pltpu.CompilerParams(dimension_semantics=("arbitrary",)),
        name="gla",
    )(proj, proj, proj, p_cum, gla_norm_g.reshape(1, HEAD_DIM))


def _outproj_kernel(final, x_ref, a_ref, b_ref, c_ref, d_ref, w_ref, fg_ref, o_ref):
    parts = [r[h] for r in (a_ref, b_ref, c_ref, d_ref) for h in range(N_HEADS)]
    mixed = jnp.concatenate(parts, axis=-1)
    y = x_ref[...] + jnp.dot(mixed, w_ref[...], preferred_element_type=F32)
    if final:
        y = y * lax.rsqrt(jnp.mean(y * y, axis=-1, keepdims=True) + RMS_EPS) * fg_ref[...]
    o_ref[...] = y


def _outproj(x2, mixers, w_out, layer, final_g, *, final, tm):
    t, d = x2.shape
    mspec = pl.BlockSpec((N_HEADS, tm, LANES), lambda i: (0, i, 0))
    return pl.pallas_call(
        functools.partial(_outproj_kernel, final),
        grid=(t // tm,),
        in_specs=[
            pl.BlockSpec((tm, d), lambda i: (i, 0)),
            mspec, mspec, mspec, mspec,
            pl.BlockSpec((None,) + w_out.shape[1:], lambda i: (layer, 0, 0)),
            pl.BlockSpec((1, d), lambda i: (0, 0)),
        ],
        out_specs=pl.BlockSpec((tm, d), lambda i: (i, 0)),
        out_shape=jax.ShapeDtypeStruct((t, d), F32),
        compiler_params=pltpu.CompilerParams(
            dimension_semantics=("arbitrary",), vmem_limit_bytes=VMEM_LIMIT_BYTES),
        name="outproj",
    )(x2, *mixers, w_out, final_g.reshape(1, d))


RELAYOUT_ROWS = 256
RELAYOUT_TAIL = 32


def _relayout_kernel(a_ref, b_ref, o_ref):
    r = pl.program_id(0)
    depth, rows, _ = o_ref.shape
    for l in range(depth):
        x = jnp.concatenate([a_ref[:, l, :], b_ref[:, l, :]], axis=0)
        out_lo = 0
        for (src_lo, src_hi) in _MAIN_RUNS:
            shift = src_lo - out_lo
            out_hi = out_lo + (src_hi - src_lo)
            assert shift <= RELAYOUT_TAIL and out_lo % rows == 0 and out_hi % rows == 0

            @pl.when((r >= out_lo // rows) & (r < out_hi // rows))
            def _(shift=shift, l=l, x=x):
                o_ref[l] = x[shift:shift + rows].astype(BF16)

            out_lo = out_hi


def _relayout_w_in(w_in):
    depth, d, d_in = w_in.shape
    wt = jnp.transpose(w_in, (2, 0, 1))
    rows, tail = RELAYOUT_ROWS, RELAYOUT_TAIL
    main = pl.pallas_call(
        _relayout_kernel,
        grid=(N_MAIN // rows,),
        in_specs=[pl.BlockSpec((rows, depth, d), lambda r: (r, 0, 0)),
                  pl.BlockSpec((tail, depth, d), lambda r: ((r + 1) * (rows // tail), 0, 0))],
        out_specs=pl.BlockSpec((depth, rows, d), lambda r: (0, r, 0)),
        out_shape=jax.ShapeDtypeStruct((depth, N_MAIN, d), BF16),
        compiler_params=pltpu.CompilerParams(dimension_semantics=("arbitrary",)),
        name="relayout",
    )(wt, wt)
    f_off, a_off = _SEG_OFF["fox_f"][0], _SEG_OFF["gla_a"][0]
    gate_cols = jnp.concatenate([w_in[..., f_off:f_off + N_HEADS],
                                 w_in[..., a_off:a_off + GLA_RANK]], axis=-1)
    small = jnp.pad(jnp.transpose(gate_cols, (0, 2, 1)),
                    ((0, 0), (0, LANES - N_HEADS - GLA_RANK), (0, 0))).astype(BF16)
    return main, small


def _rope_tables(s_len, comp_dim, rot_dim):
    half = rot_dim // 2
    inv_freq = ROPE_THETA ** (-jnp.arange(0, rot_dim, 2, dtype=F32) / rot_dim)
    ang = jnp.arange(s_len, dtype=F32)[:, None] * inv_freq[None, :]
    cos, sin = jnp.cos(ang), jnp.sin(ang)
    zeros = jnp.zeros((s_len, comp_dim - rot_dim), F32)
    ones = jnp.ones((s_len, comp_dim - rot_dim), F32)
    reps = LANES // comp_dim
    cos_t = jnp.tile(jnp.concatenate([cos, cos, ones], axis=-1), (1, reps))
    sin_up = jnp.tile(jnp.concatenate([-sin, jnp.zeros_like(sin), zeros], axis=-1), (1, reps))
    sin_dn = jnp.tile(jnp.concatenate([jnp.zeros_like(sin), sin, zeros], axis=-1), (1, reps))
    return cos_t, sin_up, sin_dn


def kernel(x, norm_g, w_in, fox_fb, diff_lam, diff_norm_g, gla_wa2, gla_ba, gla_norm_g, w_out,
           final_norm_g):
    batch, s_len, d_model = x.shape
    depth = w_in.shape[0]
    assert s_len % Q_TILE == 0 and Q_TILE == K_TILE == MOBA_BLOCK
    assert s_len // MOBA_BLOCK <= GATE_ROWS
    t = batch * s_len
    tm_in = min(1024, t)
    tn_in = 1536
    tm_out = min(512, t)

    w_main, w_small = _relayout_w_in(w_in)
    w_out_b = w_out.astype(BF16)
    fb_pad = jnp.pad(fox_fb, ((0, 0), (0, LANES - N_HEADS)))[:, None, :]
    wa_pad = jnp.pad(gla_wa2, ((0, 0), (N_HEADS, LANES - N_HEADS - GLA_RANK), (0, 0))).astype(BF16)
    ba = gla_ba[:, None, :]
    rope_diff = _rope_tables(s_len, DIFF_QK_DIM, DIFF_QK_DIM // ROPE_FRACTION)
    rope_moba = _rope_tables(s_len, HEAD_DIM, HEAD_DIM // ROPE_FRACTION)

    x2 = x.reshape(t, d_model)
    for l in range(depth):
        proj, small = _inproj(x2, norm_g[l], w_main, w_small, l, tm=tm_in, tn=tn_in)
        fk_rep, fq_rows, p_cum = _gates(small, fb_pad[l], wa_pad[l], ba[l], batch=batch, s_len=s_len)
        lam_init = 0.8 - 0.6 * math.exp(-0.3 * l)
        mixers = (
            _fox(proj, fk_rep, fq_rows, batch=batch, s_len=s_len),
            _diff(proj, diff_lam[l], diff_norm_g[l], rope_diff, lam_init, batch=batch, s_len=s_len),
            _moba(proj, rope_moba, batch=batch, s_len=s_len),
            _gla(proj, p_cum, gla_norm_g[l], batch=batch, s_len=s_len),
        )
        x2 = _outproj(x2, mixers, w_out_b, l, final_norm_g, final=(l == depth - 1), tm=tm_out)
    return x2.reshape(batch, s_len, d_model)
```

```python
import functools
import math

import jax
import jax.numpy as jnp
from jax import lax
from jax.experimental import pallas as pl
from jax.experimental.pallas import tpu as pltpu

F32 = jnp.float32
BF16 = jnp.bfloat16

LANES = 128
VMEM_LIMIT_BYTES = 56 * 2 ** 20
HEAD_DIM = 128
N_HEADS = 4
GROUP_W = N_HEADS * HEAD_DIM
DIFF_QK_DIM = HEAD_DIM // 2
MOBA_BLOCK = 256
MOBA_TOPK = 3
GLA_DK = HEAD_DIM // 2
GLA_RANK = 16
GLA_TAU = 16.0
GLA_CHUNK = 64
ROPE_THETA = 500000.0
ROPE_FRACTION = 4
RMS_EPS = 1e-6
MASKED_LOGIT = -1e30

_SEGMENTS = (
    ("fox_q", GROUP_W), ("fox_k", GROUP_W), ("fox_v", GROUP_W),
    ("fox_f", N_HEADS), ("fox_g", GROUP_W),
    ("diff_q", GROUP_W), ("diff_k", GROUP_W), ("diff_v", GROUP_W), ("diff_g", GROUP_W),
    ("moba_q", GROUP_W), ("moba_k", GROUP_W), ("moba_v", GROUP_W), ("moba_g", GROUP_W),
    ("gla_q", N_HEADS * GLA_DK), ("gla_k", N_HEADS * GLA_DK), ("gla_v", GROUP_W),
    ("gla_a", GLA_RANK), ("gla_g", GROUP_W),
)
_SEG_OFF = {}
_off = 0
for _name, _w in _SEGMENTS:
    _SEG_OFF[_name] = (_off, _w)
    _off += _w

_SLABS = ("fox_q", "fox_k", "fox_v", "fox_g", "diff_q", "diff_k", "diff_v", "diff_g",
          "moba_q", "moba_k", "moba_v", "moba_g", "gla_qk", "gla_v", "gla_g")
_SLAB = {name: i for i, name in enumerate(_SLABS)}
N_MAIN = len(_SLABS) * GROUP_W
_MAIN_RUNS = ((0, _SEG_OFF["fox_f"][0]),
              (_SEG_OFF["fox_g"][0], _SEG_OFF["gla_a"][0]),
              (_SEG_OFF["gla_g"][0], _off))
assert sum(b - a for a, b in _MAIN_RUNS) == N_MAIN

Q_TILE = 256
K_TILE = 256
CUM_BLOCK = 128
GATE_ROWS = 16
GLA_GROUP = 4
ONES_ROWS = 16
LOG2E = math.log2(math.e)


def _silu(x):
    return x * (1.0 / (1.0 + jnp.exp(-x)))


def _log_sigmoid(x):
    return jnp.minimum(x, 0.0) - jnp.log1p(jnp.exp(-jnp.abs(x)))


def _nt_dot(a, b):
    return lax.dot_general(a, b, (((1,), (1,)), ((), ())), preferred_element_type=F32)


def _inproj_tile(h, w_ref, o_ref):
    acc = _nt_dot(h, w_ref[...])
    for c in range(o_ref.shape[0]):
        o_ref[c] = acc[:, c * LANES:(c + 1) * LANES].astype(BF16)


def _inproj_norm_kernel(x_ref, g_ref, w_ref, ws_ref, o_ref, small_ref, h_scr):
    @pl.when(pl.program_id(1) == 0)
    def _():
        x = x_ref[...]
        y = x * lax.rsqrt(jnp.mean(x * x, axis=-1, keepdims=True) + RMS_EPS) * g_ref[...]
        h = y.astype(BF16)
        h_scr[...] = h
        small_ref[...] = _nt_dot(h, ws_ref[...])

    _inproj_tile(h_scr[...], w_ref, o_ref)


def _inproj_kernel(h_ref, w_ref, ws_ref, o_ref, small_ref):
    @pl.when(pl.program_id(1) == 0)
    def _():
        small_ref[...] = _nt_dot(h_ref[...], ws_ref[...])

    _inproj_tile(h_ref[...], w_ref, o_ref)


def _inproj(rows, norm_g, w_main, w_small, layer, *, tm, tn):
    t, d = rows.shape
    n_main = w_main.shape[1]
    with_norm = norm_g is not None
    row_spec = pl.BlockSpec((tm, d), lambda i, j: (i, 0))
    weight_specs = [pl.BlockSpec((None, tn, d), lambda i, j: (layer, j, 0)),
                    pl.BlockSpec((None, LANES, d), lambda i, j: (layer, 0, 0))]
    if with_norm:
        body, in_specs = _inproj_norm_kernel, [row_spec, pl.BlockSpec((1, d), lambda i, j: (0, 0))]
        args, scratch = (rows, norm_g.reshape(1, d)), [pltpu.VMEM((tm, d), BF16)]
    else:
        body, in_specs, args, scratch = _inproj_kernel, [row_spec], (rows,), []
    return pl.pallas_call(
        body,
        grid=(t // tm, n_main // tn),
        in_specs=in_specs + weight_specs,
        out_specs=[
            pl.BlockSpec((tn // LANES, tm, LANES), lambda i, j: (j, i, 0)),
            pl.BlockSpec((tm, LANES), lambda i, j: (i, 0)),
        ],
        out_shape=[
            jax.ShapeDtypeStruct((n_main // LANES, t, LANES), BF16),
            jax.ShapeDtypeStruct((t, LANES), F32),
        ],
        scratch_shapes=scratch,
        compiler_params=pltpu.CompilerParams(
            dimension_semantics=("arbitrary", "arbitrary"), vmem_limit_bytes=VMEM_LIMIT_BYTES),
        name="inproj",
    )(*args, w_main, w_small)


def _gates_kernel(small_ref, fb_ref, wa_ref, ba_ref, fk_ref, fq_ref, p_ref):
    s_len = small_ref.shape[0]
    small = small_ref[...]
    log_f = _log_sigmoid(small + fb_ref[...])
    a_logit = jnp.dot(small.astype(BF16), wa_ref[...], preferred_element_type=F32) + ba_ref[...]
    log_a = _log_sigmoid(a_logit) * (1.0 / GLA_TAU)
    both = jnp.concatenate([log_f, log_a], axis=-1)

    row = lax.broadcasted_iota(jnp.int32, (CUM_BLOCK, CUM_BLOCK), 0)
    col = lax.broadcasted_iota(jnp.int32, (CUM_BLOCK, CUM_BLOCK), 1)
    tri = (col <= row).astype(BF16)
    low_half = lax.broadcasted_iota(jnp.int32, (CUM_BLOCK, LANES), 1) < GLA_DK
    carry = jnp.zeros((1, both.shape[1]), F32)
    for c in range(s_len // CUM_BLOCK):
        blk = both[c * CUM_BLOCK:(c + 1) * CUM_BLOCK]
        hi = blk.astype(BF16)
        rest = blk - hi.astype(F32)
        mid = rest.astype(BF16)
        lo = (rest - mid.astype(F32)).astype(BF16)
        cum = carry + sum(jnp.dot(tri, part, preferred_element_type=F32) for part in (hi, mid, lo))
        carry = cum[CUM_BLOCK - 1:CUM_BLOCK]
        rows = slice(c * CUM_BLOCK, (c + 1) * CUM_BLOCK)
        f_cum = cum[:, :LANES] * LOG2E
        per_tile = Q_TILE // CUM_BLOCK
        fq_ref[c // per_tile, :, (c % per_tile) * CUM_BLOCK:(c % per_tile + 1) * CUM_BLOCK] = (
            f_cum.T[:GATE_ROWS])
        for h in range(N_HEADS):
            fk_ref[h, rows, :] = jnp.broadcast_to(f_cum[:, h:h + 1], (CUM_BLOCK, LANES))
            pair = cum[:, (1 + h // 2) * LANES:(2 + h // 2) * LANES]
            swapped = pltpu.roll(pair, GLA_DK, 1)
            p_ref[h, rows, :] = (jnp.where(low_half, pair, swapped) if h % 2 == 0
                                 else jnp.where(low_half, swapped, pair))


def _gates(small, fb_pad, wa_pad, ba, *, batch, s_len):
    t = small.shape[0]
    return pl.pallas_call(
        _gates_kernel,
        grid=(batch,),
        in_specs=[
            pl.BlockSpec((s_len, LANES), lambda b: (b, 0)),
            pl.BlockSpec((1, LANES), lambda b: (0, 0)),
            pl.BlockSpec((LANES, N_HEADS * GLA_DK), lambda b: (0, 0)),
            pl.BlockSpec((1, N_HEADS * GLA_DK), lambda b: (0, 0)),
        ],
        out_specs=[
            pl.BlockSpec((N_HEADS, s_len, LANES), lambda b: (0, b, 0)),
            pl.BlockSpec((None, s_len // Q_TILE, GATE_ROWS, Q_TILE), lambda b: (b, 0, 0, 0)),
            pl.BlockSpec((N_HEADS, s_len, LANES), lambda b: (0, b, 0)),
        ],
        out_shape=[
            jax.ShapeDtypeStruct((N_HEADS, t, LANES), F32),
            jax.ShapeDtypeStruct((batch, s_len // Q_TILE, GATE_ROWS, Q_TILE), F32),
            jax.ShapeDtypeStruct((N_HEADS, t, LANES), F32),
        ],
        compiler_params=pltpu.CompilerParams(dimension_semantics=("arbitrary",)),
        name="gates",
    )(small, fb_pad, wa_pad, ba)


def _causal_mask_t(tk, tq):
    key = lax.broadcasted_iota(jnp.int32, (tk, tq), 0)
    qry = lax.broadcasted_iota(jnp.int32, (tk, tq), 1)
    return key <= qry


def _transpose_values(v_ref, vt_scr):
    dv = v_ref.shape[2]
    for h in range(v_ref.shape[0]):
        for j in range(v_ref.shape[1] // K_TILE):
            blk = v_ref[h, j * K_TILE:(j + 1) * K_TILE, :].astype(F32)
            vt_scr[h, j, :dv, :] = blk.T.astype(BF16)
            vt_scr[h, j, dv:, :] = jnp.ones((ONES_ROWS, K_TILE), BF16)


def _col_max(z):
    rows = z.shape[0]
    while rows > 8:
        rows //= 2
        z = jnp.maximum(z[:rows], z[rows:])
    return jnp.max(z, axis=0, keepdims=True)


def _causal_tiles_t(n_tiles, prepare_fn, logits_fn, vt_fn, finish_fn, z_scr, acc_scr):
    n, _, tq = acc_scr.shape

    def update(j, slot, ms, tile_max, masked):
        stats = []
        for s in range(n):
            z = z_scr[slot, s]
            if masked:
                z = jnp.where(_causal_mask_t(K_TILE, tq), z, -jnp.inf)
                m_new = jnp.maximum(ms[s], _col_max(z))
            else:
                m_new = jnp.maximum(ms[s], tile_max[s])
            stats.append((m_new, jnp.exp2(ms[s] - m_new), jnp.exp2(z - m_new).astype(BF16)))
        for s in range(n):
            pv = jnp.dot(vt_fn(s, j), stats[s][2], preferred_element_type=F32)
            acc_scr[s] = stats[s][1] * acc_scr[s] + pv
        return tuple(st[0] for st in stats)

    def first_logits(i):
        ctx = prepare_fn(i)
        return ctx, [logits_fn(ctx, s, jnp.int32(0)) for s in range(n)]

    def open_tile(zs):
        for s in range(n):
            z_scr[0, s] = zs[s]
            acc_scr[s] = jnp.zeros(acc_scr.shape[1:], F32)
        return (tuple(jnp.full((1, tq), -jnp.inf, F32) for _ in range(n)),
                tuple(_col_max(z) for z in zs))

    def close_tile(i, ms):
        update(i, lax.rem(i, 2), ms, None, True)
        finish_fn(i)

    def off_diagonal(i, ctx, state):
        def body(j, state):
            ms, tile_max = state
            slot = lax.rem(j, 2)
            z_next = [logits_fn(ctx, s, j + 1) for s in range(n)]
            ms = update(j, slot, ms, tile_max, False)
            for s in range(n):
                z_scr[1 - slot, s] = z_next[s]
            return ms, tuple(_col_max(z) for z in z_next)

        return lax.fori_loop(0, i, body, state)[0]

    _, zs = first_logits(jnp.int32(0))
    ms, _ = open_tile(zs)

    def outer(i, ms_prev):
        ctx, zs = first_logits(i)
        close_tile(i - 1, ms_prev)
        return off_diagonal(i, ctx, open_tile(zs))

    ms = lax.fori_loop(1, n_tiles, outer, ms)
    close_tile(jnp.int32(n_tiles - 1), ms)


def _softmax_result_t(acc):
    dv = acc.shape[0] - ONES_ROWS
    return acc[:dv] / acc[dv:dv + 1]


def _rope(x, cos_t, sin_up, sin_dn, half):
    return (x * cos_t + pltpu.roll(x, LANES - half, 1) * sin_up
            + pltpu.roll(x, half, 1) * sin_dn)


def _key_rows(j):
    return pl.ds(pl.multiple_of(j * K_TILE, K_TILE), K_TILE)


def _query_rows(i):
    return pl.ds(pl.multiple_of(i * Q_TILE, Q_TILE), Q_TILE)


def _batch_spec(s_len):
    return lambda slab: pl.BlockSpec((N_HEADS, s_len, LANES), lambda b: (slab, b, 0))


def _softmax_scratch(s_len):
    return [pltpu.VMEM((N_HEADS, s_len // K_TILE, HEAD_DIM + ONES_ROWS, K_TILE), BF16),
            pltpu.VMEM((2, N_HEADS, K_TILE, Q_TILE), F32),
            pltpu.VMEM((N_HEADS, HEAD_DIM + ONES_ROWS, Q_TILE), F32)]


def _fox_kernel(q_ref, k_ref, v_ref, g_ref, fk_ref, fq_ref, o_ref, vt_scr, z_scr, acc_scr):
    scale = HEAD_DIM ** -0.5 * LOG2E
    _transpose_values(v_ref, vt_scr)

    def prepare(i):
        fq_all = fq_ref[i]
        return ([q_ref[h, _query_rows(i), :] for h in range(N_HEADS)],
                [fq_all[h:h + 1, :] for h in range(N_HEADS)])

    def logits(ctx, h, j):
        qs, fqs = ctx
        fk = fk_ref[h, _key_rows(j), :]
        fk = jnp.concatenate([fk] * (Q_TILE // LANES), axis=-1)
        return _nt_dot(k_ref[h, _key_rows(j), :], qs[h]) * scale + (fqs[h] - fk)

    def finish(i):
        for h in range(N_HEADS):
            o = _softmax_result_t(acc_scr[h]).T
            gate = _silu(g_ref[h, _query_rows(i), :].astype(F32))
            o_ref[h, _query_rows(i), :] = (o * gate).astype(BF16)

    _causal_tiles_t(q_ref.shape[1] // Q_TILE, prepare, logits, lambda h, j: vt_scr[h, j], finish,
                    z_scr, acc_scr)


def _fox(proj, fk_rep, fq_rows, *, batch, s_len):
    t = proj.shape[1]
    spec = _batch_spec(s_len)
    return pl.pallas_call(
        _fox_kernel,
        grid=(batch,),
        in_specs=[
            spec(_SLAB["fox_q"]), spec(_SLAB["fox_k"]), spec(_SLAB["fox_v"]), spec(_SLAB["fox_g"]),
            spec(0),
            pl.BlockSpec((None, s_len // Q_TILE, GATE_ROWS, Q_TILE), lambda b: (b, 0, 0, 0)),
        ],
        out_specs=spec(0),
        out_shape=jax.ShapeDtypeStruct((N_HEADS, t, LANES), BF16),
        scratch_shapes=_softmax_scratch(s_len),
        compiler_params=pltpu.CompilerParams(dimension_semantics=("arbitrary",)),
        name="fox",
    )(proj, proj, proj, proj, fk_rep, fq_rows)


def _diff_kernel(lam_init, q_ref, k_ref, v_ref, g_ref, lam_ref, ng_ref, c_ref, u_ref, d_ref, o_ref,
                 kz_scr, qr_scr, a1_scr, vt_scr, z_scr, acc_scr):
    half = DIFF_QK_DIM // ROPE_FRACTION // 2
    scale = DIFF_QK_DIM ** -0.5 * LOG2E

    _transpose_values(v_ref, vt_scr)
    first = lax.broadcasted_iota(jnp.int32, (k_ref.shape[1], LANES), 1) < DIFF_QK_DIM
    for h in range(N_HEADS):
        kr = _rope(k_ref[h].astype(F32), c_ref[...], u_ref[...], d_ref[...], half)
        kz_scr[0, h] = jnp.where(first, kr, 0.0).astype(BF16)
        kz_scr[1, h] = jnp.where(first, 0.0, kr).astype(BF16)
        qr_scr[h] = _rope(q_ref[h].astype(F32), c_ref[...], u_ref[...], d_ref[...], half).astype(BF16)

    lf = lam_ref[...]
    lam = (jnp.exp(jnp.sum(lf[0:1] * lf[1:2], axis=-1, keepdims=True))
           - jnp.exp(jnp.sum(lf[2:3] * lf[3:4], axis=-1, keepdims=True)) + lam_init)

    def prepare(i):
        return [qr_scr[h, _query_rows(i), :] for h in range(N_HEADS)]

    def finish_first(i):
        for h in range(N_HEADS):
            a1_scr[h, _query_rows(i), :] = _softmax_result_t(acc_scr[h]).T

    def finish_second(i):
        for h in range(N_HEADS):
            o = a1_scr[h, _query_rows(i), :] - lam * _softmax_result_t(acc_scr[h]).T
            o = o * lax.rsqrt(jnp.mean(o * o, axis=-1, keepdims=True) + RMS_EPS) * ng_ref[...]
            o = o * (1.0 - lam_init)
            gate = _silu(g_ref[h, _query_rows(i), :].astype(F32))
            o_ref[h, _query_rows(i), :] = (o * gate).astype(BF16)

    for comp, finish in ((0, finish_first), (1, finish_second)):
        def logits(qs, h, j, comp=comp):
            return _nt_dot(kz_scr[comp, h, _key_rows(j), :], qs[h]) * scale

        _causal_tiles_t(q_ref.shape[1] // Q_TILE, prepare, logits, lambda h, j: vt_scr[h, j], finish,
                        z_scr, acc_scr)


def _diff(proj, diff_lam, diff_norm_g, tables, lam_init, *, batch, s_len):
    t = proj.shape[1]
    spec = _batch_spec(s_len)
    table_spec = pl.BlockSpec((s_len, LANES), lambda b: (0, 0))
    return pl.pallas_call(
        functools.partial(_diff_kernel, lam_init),
        grid=(batch,),
        in_specs=[
            spec(_SLAB["diff_q"]), spec(_SLAB["diff_k"]), spec(_SLAB["diff_v"]), spec(_SLAB["diff_g"]),
            pl.BlockSpec((4, DIFF_QK_DIM), lambda b: (0, 0)),
            pl.BlockSpec((1, HEAD_DIM), lambda b: (0, 0)),
            table_spec, table_spec, table_spec,
        ],
        out_specs=spec(0),
        out_shape=jax.ShapeDtypeStruct((N_HEADS, t, LANES), BF16),
        scratch_shapes=[pltpu.VMEM((2, N_HEADS, s_len, LANES), BF16),
                        pltpu.VMEM((N_HEADS, s_len, LANES), BF16),
                        pltpu.VMEM((N_HEADS, s_len, LANES), F32),
                        ] + _softmax_scratch(s_len),
        compiler_params=pltpu.CompilerParams(dimension_semantics=("arbitrary",)),
        name="diff",
    )(proj, proj, proj, proj, diff_lam, diff_norm_g.reshape(1, HEAD_DIM), *tables)


def _moba_kernel(q_ref, k_ref, v_ref, g_ref, c_ref, u_ref, d_ref,
                 o_ref, kr_scr, kmean_scr, vt_scr, z_scr, acc_scr):
    tq = Q_TILE
    half = HEAD_DIM // ROPE_FRACTION // 2
    scale = HEAD_DIM ** -0.5 * LOG2E
    n_blk = k_ref.shape[1] // MOBA_BLOCK

    _transpose_values(v_ref, vt_scr)
    kmean_scr[...] = jnp.zeros(kmean_scr.shape, F32)
    for h in range(N_HEADS):
        kr = _rope(k_ref[h].astype(F32), c_ref[...], u_ref[...], d_ref[...], half)
        kr_scr[h] = kr.astype(BF16)
        for n in range(n_blk):
            blk = kr[n * MOBA_BLOCK:(n + 1) * MOBA_BLOCK]
            kmean_scr[h, n:n + 1, :] = jnp.sum(blk, axis=0, keepdims=True) * (1.0 / MOBA_BLOCK)

    blk_id = lax.broadcasted_iota(jnp.int32, (GATE_ROWS, tq), 0).astype(F32)

    def prepare(i):
        rows = _query_rows(i)
        past = blk_id < lax.convert_element_type(i, F32)
        qs, biases = [], []
        for h in range(N_HEADS):
            q = _rope(q_ref[h, rows, :].astype(F32), c_ref[rows, :], u_ref[rows, :], d_ref[rows, :],
                      half).astype(BF16)
            gate = _nt_dot(kmean_scr[h].astype(BF16), q)
            gate = jnp.where(past, gate, -jnp.inf)
            open_ = past
            sel = jnp.zeros((GATE_ROWS, tq), jnp.bool_)
            for _ in range(MOBA_TOPK):
                top = jnp.max(gate, axis=0, keepdims=True)
                idx = jnp.min(jnp.where(gate == top, blk_id, float(GATE_ROWS)), axis=0, keepdims=True)
                pick = (blk_id == idx) & open_
                sel = sel | pick
                open_ = open_ & jnp.logical_not(pick)
                gate = jnp.where(pick, -jnp.inf, gate)
            qs.append(q)
            biases.append(jnp.where(sel, 0.0, MASKED_LOGIT))
        return i, qs, biases

    def logits(ctx, h, j):
        i, qs, biases = ctx
        here = blk_id == lax.convert_element_type(j, F32)
        bias = jnp.sum(jnp.where(here, biases[h], 0.0), axis=0, keepdims=True)
        bias = jnp.where(j == i, 0.0, bias)
        return _nt_dot(kr_scr[h, _key_rows(j), :], qs[h]) * scale + bias

    def finish(i):
        for h in range(N_HEADS):
            o = _softmax_result_t(acc_scr[h]).T
            gate = _silu(g_ref[h, _query_rows(i), :].astype(F32))
            o_ref[h, _query_rows(i), :] = (o * gate).astype(BF16)

    _causal_tiles_t(q_ref.shape[1] // tq, prepare, logits, lambda h, j: vt_scr[h, j], finish,
                    z_scr, acc_scr)


def _moba(proj, tables, *, batch, s_len):
    t = proj.shape[1]
    spec = _batch_spec(s_len)
    table_spec = pl.BlockSpec((s_len, LANES), lambda b: (0, 0))
    return pl.pallas_call(
        _moba_kernel,
        grid=(batch,),
        in_specs=[
            spec(_SLAB["moba_q"]), spec(_SLAB["moba_k"]), spec(_SLAB["moba_v"]), spec(_SLAB["moba_g"]),
            table_spec, table_spec, table_spec,
        ],
        out_specs=spec(0),
        out_shape=jax.ShapeDtypeStruct((N_HEADS, t, LANES), BF16),
        scratch_shapes=[pltpu.VMEM((N_HEADS, s_len, LANES), BF16),
                        pltpu.VMEM((N_HEADS, GATE_ROWS, LANES), F32),
                        ] + _softmax_scratch(s_len),
        compiler_params=pltpu.CompilerParams(dimension_semantics=("arbitrary",)),
        name="moba",
    )(proj, proj, proj, proj, *tables)


def _gla_kernel(x_ref, v_ref, g_ref, p_ref, ng_ref, o_ref):
    s_len = x_ref.shape[1]
    c_len = GLA_CHUNK
    scale = GLA_DK ** -0.5
    lane = lax.broadcasted_iota(jnp.int32, (c_len, LANES), 1)
    qhalf = lane < GLA_DK
    qhalf_row = lax.broadcasted_iota(jnp.int32, (1, LANES), 1) < GLA_DK
    row = lax.broadcasted_iota(jnp.int32, (c_len, c_len), 0)
    col = lax.broadcasted_iota(jnp.int32, (c_len, c_len), 1)
    causal = col <= row

    group = GLA_GROUP
    pairs = [(h, c) for h in range(N_HEADS) for c in range(group)]

    def body(g, carries):
        prep = {}
        for h, c in pairs:
            rows = pl.ds(pl.multiple_of(g * (group * c_len), group * c_len) + c * c_len, c_len)
            qcb = x_ref[h // 2, rows, :].astype(F32)
            kcb = x_ref[2 + h // 2, rows, :].astype(F32)
            if h % 2 == 0:
                x = jnp.where(qhalf, qcb, pltpu.roll(kcb, GLA_DK, 1))
            else:
                x = jnp.where(qhalf, pltpu.roll(qcb, GLA_DK, 1), kcb)
            p = p_ref[h, rows, :]
            base = carries[h][1] if c == 0 else prep[h, c - 1]["p_last"]
            cum = p - base
            xt = x * jnp.exp(jnp.where(qhalf, cum, -cum))
            kr = pltpu.roll(xt, GLA_DK, 1)
            decay = jnp.where(qhalf_row, jnp.exp(cum[c_len - 1:c_len]), 0.0)
            prep[h, c] = dict(
                rows=rows, qz=jnp.where(qhalf, xt, 0.0).astype(BF16), kr=kr.astype(BF16),
                khat=(jnp.where(qhalf, kr, 0.0) * decay).astype(BF16),
                decay=decay, v=v_ref[h, rows, :], p_last=p[c_len - 1:c_len])
        scores, upd = {}, {}
        for h, c in pairs:
            d = prep[h, c]
            scores[h, c] = _nt_dot(d["qz"], d["kr"])
            upd[h, c] = lax.dot_general(d["v"], d["khat"], (((0,), (0,)), ((), ())),
                                        preferred_element_type=F32)
        states = {}
        for h in range(N_HEADS):
            states[h, 0] = carries[h][0]
            for c in range(group):
                states[h, c + 1] = states[h, c] * prep[h, c]["decay"] + upd[h, c]
        outs = {}
        for h, c in pairs:
            sc = jnp.where(causal, scores[h, c] * scale, 0.0).astype(BF16)
            outs[h, c] = (jnp.dot(sc, prep[h, c]["v"], preferred_element_type=F32),
                          _nt_dot(prep[h, c]["qz"], states[h, c].astype(BF16)))
        for h, c in pairs:
            o = outs[h, c][0] + outs[h, c][1] * scale
            y = o * lax.rsqrt(jnp.mean(o * o, axis=-1, keepdims=True) + RMS_EPS) * ng_ref[...]
            rows = prep[h, c]["rows"]
            o_ref[h, rows, :] = (y * _silu(g_ref[h, rows, :].astype(F32))).astype(BF16)
        return tuple((states[h, group], prep[h, group - 1]["p_last"]) for h in range(N_HEADS))

    init = tuple((jnp.zeros((HEAD_DIM, LANES), F32), jnp.zeros((1, LANES), F32))
                 for _ in range(N_HEADS))
    lax.fori_loop(0, s_len // (group * c_len), body, init)


def _gla(proj, p_cum, gla_norm_g, *, batch, s_len):
    t = proj.shape[1]
    spec = _batch_spec(s_len)
    return pl.pallas_call(
        _gla_kernel,
        grid=(batch,),
        in_specs=[
            spec(_SLAB["gla_qk"]), spec(_SLAB["gla_v"]), spec(_SLAB["gla_g"]), spec(0),
            pl.BlockSpec((1, HEAD_DIM), lambda b: (0, 0)),
        ],
        out_specs=spec(0),
        out_shape=jax.ShapeDtypeStruct((N_HEADS, t, LANES), BF16),
        compiler_params=pltpu.CompilerParams(dimension_semantics=("arbitrary",)),
        name="gla",
    )(proj, proj, proj, p_cum, gla_norm_g.reshape(1, HEAD_DIM))


def _outproj_kernel(final, x_ref, a_ref, b_ref, c_ref, d_ref, w_ref, g_ref, *o_refs):
    parts = [r[h] for r in (a_ref, b_ref, c_ref, d_ref) for h in range(N_HEADS)]
    mixed = jnp.concatenate(parts, axis=-1)
    y = x_ref[...] + jnp.dot(mixed, w_ref[...], preferred_element_type=F32)
    normed = y * lax.rsqrt(jnp.mean(y * y, axis=-1, keepdims=True) + RMS_EPS) * g_ref[...]
    if final:
        o_refs[0][...] = normed
    else:
        o_refs[0][...] = y
        o_refs[1][...] = normed.astype(BF16)


def _outproj(x2, mixers, w_out, layer, norm_gain, *, final, tm):
    t, d = x2.shape
    mspec = pl.BlockSpec((N_HEADS, tm, LANES), lambda i: (0, i, 0))
    row_spec = pl.BlockSpec((tm, d), lambda i: (i, 0))
    out_specs, out_shape = [row_spec], [jax.ShapeDtypeStruct((t, d), F32)]
    if not final:
        out_specs.append(row_spec)
        out_shape.append(jax.ShapeDtypeStruct((t, d), BF16))
    return pl.pallas_call(
        functools.partial(_outproj_kernel, final),
        grid=(t // tm,),
        in_specs=[
            row_spec, mspec, mspec, mspec, mspec,
            pl.BlockSpec((None,) + w_out.shape[1:], lambda i: (layer, 0, 0)),
            pl.BlockSpec((1, d), lambda i: (0, 0)),
        ],
        out_specs=out_specs,
        out_shape=out_shape,
        compiler_params=pltpu.CompilerParams(
            dimension_semantics=("arbitrary",), vmem_limit_bytes=VMEM_LIMIT_BYTES),
        name="outproj",
    )(x2, *mixers, w_out, norm_gain.reshape(1, d))


RELAYOUT_ROWS = 256
RELAYOUT_TAIL = 32


def _relayout_kernel(a_ref, b_ref, o_ref):
    r = pl.program_id(0)
    depth, rows, _ = o_ref.shape
    for l in range(depth):
        x = jnp.concatenate([a_ref[:, l, :], b_ref[:, l, :]], axis=0)
        out_lo = 0
        for (src_lo, src_hi) in _MAIN_RUNS:
            shift = src_lo - out_lo
            out_hi = out_lo + (src_hi - src_lo)
            assert shift <= RELAYOUT_TAIL and out_lo % rows == 0 and out_hi % rows == 0

            @pl.when((r >= out_lo // rows) & (r < out_hi // rows))
            def _(shift=shift, l=l, x=x):
                o_ref[l] = x[shift:shift + rows].astype(BF16)

            out_lo = out_hi


def _relayout_w_in(w_in):
    depth, d, d_in = w_in.shape
    wt = jnp.transpose(w_in, (2, 0, 1))
    rows, tail = RELAYOUT_ROWS, RELAYOUT_TAIL
    main = pl.pallas_call(
        _relayout_kernel,
        grid=(N_MAIN // rows,),
        in_specs=[pl.BlockSpec((rows, depth, d), lambda r: (r, 0, 0)),
                  pl.BlockSpec((tail, depth, d), lambda r: ((r + 1) * (rows // tail), 0, 0))],
        out_specs=pl.BlockSpec((depth, rows, d), lambda r: (0, r, 0)),
        out_shape=jax.ShapeDtypeStruct((depth, N_MAIN, d), BF16),
        compiler_params=pltpu.CompilerParams(dimension_semantics=("arbitrary",)),
        name="relayout",
    )(wt, wt)
    f_off, a_off = _SEG_OFF["fox_f"][0], _SEG_OFF["gla_a"][0]
    gate_cols = jnp.concatenate([w_in[..., f_off:f_off + N_HEADS],
                                 w_in[..., a_off:a_off + GLA_RANK]], axis=-1)
    small = jnp.pad(jnp.transpose(gate_cols, (0, 2, 1)),
                    ((0, 0), (0, LANES - N_HEADS - GLA_RANK), (0, 0))).astype(BF16)
    return main, small


def _rope_tables(s_len, comp_dim, rot_dim):
    half = rot_dim // 2
    inv_freq = ROPE_THETA ** (-jnp.arange(0, rot_dim, 2, dtype=F32) / rot_dim)
    ang = jnp.arange(s_len, dtype=F32)[:, None] * inv_freq[None, :]
    cos, sin = jnp.cos(ang), jnp.sin(ang)
    zeros = jnp.zeros((s_len, comp_dim - rot_dim), F32)
    ones = jnp.ones((s_len, comp_dim - rot_dim), F32)
    reps = LANES // comp_dim
    cos_t = jnp.tile(jnp.concatenate([cos, cos, ones], axis=-1), (1, reps))
    sin_up = jnp.tile(jnp.concatenate([-sin, jnp.zeros_like(sin), zeros], axis=-1), (1, reps))
    sin_dn = jnp.tile(jnp.concatenate([jnp.zeros_like(sin), sin, zeros], axis=-1), (1, reps))
    return cos_t, sin_up, sin_dn


def kernel(x, norm_g, w_in, fox_fb, diff_lam, diff_norm_g, gla_wa2, gla_ba, gla_norm_g, w_out,
           final_norm_g):
    batch, s_len, d_model = x.shape
    depth = w_in.shape[0]
    assert s_len % Q_TILE == 0 and Q_TILE == K_TILE == MOBA_BLOCK
    assert s_len // MOBA_BLOCK <= GATE_ROWS
    t = batch * s_len
    tm_in = min(1024, t)
    tn_in = 1536
    tn_next = 2560
    tm_out = min(512, t)

    w_main, w_small = _relayout_w_in(w_in)
    w_out_b = w_out.astype(BF16)
    fb_pad = jnp.pad(fox_fb, ((0, 0), (0, LANES - N_HEADS)))[:, None, :]
    wa_pad = jnp.pad(gla_wa2, ((0, 0), (N_HEADS, LANES - N_HEADS - GLA_RANK), (0, 0))).astype(BF16)
    ba = gla_ba[:, None, :]
    rope_diff = _rope_tables(s_len, DIFF_QK_DIM, DIFF_QK_DIM // ROPE_FRACTION)
    rope_moba = _rope_tables(s_len, HEAD_DIM, HEAD_DIM // ROPE_FRACTION)

    x2 = x.reshape(t, d_model)
    h2 = None
    for l in range(depth):
        if h2 is None:
            proj, small = _inproj(x2, norm_g[l], w_main, w_small, l, tm=tm_in, tn=tn_in)
        else:
            proj, small = _inproj(h2, None, w_main, w_small, l, tm=tm_in, tn=tn_next)
        fk_rep, fq_rows, p_cum = _gates(small, fb_pad[l], wa_pad[l], ba[l], batch=batch, s_len=s_len)
        lam_init = 0.8 - 0.6 * math.exp(-0.3 * l)
        mixers = (
            _fox(proj, fk_rep, fq_rows, batch=batch, s_len=s_len),
            _diff(proj, diff_lam[l], diff_norm_g[l], rope_diff, lam_init, batch=batch, s_len=s_len),
            _moba(proj, rope_moba, batch=batch, s_len=s_len),
            _gla(proj, p_cum, gla_norm_g[l], batch=batch, s_len=s_len),
        )
        if l == depth - 1:
            (x2,) = _outproj(x2, mixers, w_out_b, l, final_norm_g, final=True, tm=tm_out)
        else:
            x2, h2 = _outproj(x2, mixers, w_out_b, l, norm_g[l + 1], final=False, tm=tm_out)
    return x2.reshape(batch, s_len, d_model)
```

```python
import functools
import math

import jax
import jax.numpy as jnp
from jax import lax
from jax.experimental import pallas as pl
from jax.experimental.pallas import tpu as pltpu

F32 = jnp.float32
BF16 = jnp.bfloat16

LANES = 128
VMEM_LIMIT_BYTES = 56 * 2 ** 20
HEAD_DIM = 128
N_HEADS = 4
GROUP_W = N_HEADS * HEAD_DIM
DIFF_QK_DIM = HEAD_DIM // 2
MOBA_BLOCK = 256
MOBA_TOPK = 3
GLA_DK = HEAD_DIM // 2
GLA_RANK = 16
GLA_TAU = 16.0
GLA_CHUNK = 64
ROPE_THETA = 500000.0
ROPE_FRACTION = 4
RMS_EPS = 1e-6
MASKED_LOGIT = -1e30

_SEGMENTS = (
    ("fox_q", GROUP_W), ("fox_k", GROUP_W), ("fox_v", GROUP_W),
    ("fox_f", N_HEADS), ("fox_g", GROUP_W),
    ("diff_q", GROUP_W), ("diff_k", GROUP_W), ("diff_v", GROUP_W), ("diff_g", GROUP_W),
    ("moba_q", GROUP_W), ("moba_k", GROUP_W), ("moba_v", GROUP_W), ("moba_g", GROUP_W),
    ("gla_q", N_HEADS * GLA_DK), ("gla_k", N_HEADS * GLA_DK), ("gla_v", GROUP_W),
    ("gla_a", GLA_RANK), ("gla_g", GROUP_W),
)
_SEG_OFF = {}
_off = 0
for _name, _w in _SEGMENTS:
    _SEG_OFF[_name] = (_off, _w)
    _off += _w

_SLABS = ("fox_q", "fox_k", "fox_v", "fox_g", "diff_q", "diff_k", "diff_v", "diff_g",
          "moba_q", "moba_k", "moba_v", "moba_g", "gla_qk", "gla_v", "gla_g")
_SLAB = {name: i for i, name in enumerate(_SLABS)}
N_MAIN = len(_SLABS) * GROUP_W
_MAIN_RUNS = ((0, _SEG_OFF["fox_f"][0]),
              (_SEG_OFF["fox_g"][0], _SEG_OFF["gla_a"][0]),
              (_SEG_OFF["gla_g"][0], _off))
assert sum(b - a for a, b in _MAIN_RUNS) == N_MAIN

Q_TILE = 256
K_TILE = 256
CUM_BLOCK = 128
GATE_ROWS = 16
GLA_GROUP = 4
GLA_SAFE_DECAY = 60.0
ONES_ROWS = 16
LOG2E = math.log2(math.e)


def _silu(x):
    return x * (1.0 / (1.0 + jnp.exp(-x)))


def _log_sigmoid(x):
    return jnp.minimum(x, 0.0) - jnp.log1p(jnp.exp(-jnp.abs(x)))


def _nt_dot(a, b):
    return lax.dot_general(a, b, (((1,), (1,)), ((), ())), preferred_element_type=F32)


def _inproj_tile(h, w_ref, o_ref):
    acc = _nt_dot(h, w_ref[...])
    for c in range(o_ref.shape[0]):
        o_ref[c] = acc[:, c * LANES:(c + 1) * LANES].astype(BF16)


def _inproj_norm_kernel(x_ref, g_ref, w_ref, ws_ref, o_ref, small_ref, h_scr):
    @pl.when(pl.program_id(1) == 0)
    def _():
        x = x_ref[...]
        y = x * lax.rsqrt(jnp.mean(x * x, axis=-1, keepdims=True) + RMS_EPS) * g_ref[...]
        h = y.astype(BF16)
        h_scr[...] = h
        small_ref[...] = _nt_dot(h, ws_ref[...])

    _inproj_tile(h_scr[...], w_ref, o_ref)


def _inproj_kernel(h_ref, w_ref, ws_ref, o_ref, small_ref):
    @pl.when(pl.program_id(1) == 0)
    def _():
        small_ref[...] = _nt_dot(h_ref[...], ws_ref[...])

    _inproj_tile(h_ref[...], w_ref, o_ref)


def _inproj(rows, norm_g, w_main, w_small, layer, *, tm, tn):
    t, d = rows.shape
    n_main = w_main.shape[1]
    with_norm = norm_g is not None
    row_spec = pl.BlockSpec((tm, d), lambda i, j: (i, 0))
    weight_specs = [pl.BlockSpec((None, tn, d), lambda i, j: (layer, j, 0)),
                    pl.BlockSpec((None, LANES, d), lambda i, j: (layer, 0, 0))]
    if with_norm:
        body, in_specs = _inproj_norm_kernel, [row_spec, pl.BlockSpec((1, d), lambda i, j: (0, 0))]
        args, scratch = (rows, norm_g.reshape(1, d)), [pltpu.VMEM((tm, d), BF16)]
    else:
        body, in_specs, args, scratch = _inproj_kernel, [row_spec], (rows,), []
    return pl.pallas_call(
        body,
        grid=(t // tm, n_main // tn),
        in_specs=in_specs + weight_specs,
        out_specs=[
            pl.BlockSpec((tn // LANES, tm, LANES), lambda i, j: (j, i, 0)),
            pl.BlockSpec((tm, LANES), lambda i, j: (i, 0)),
        ],
        out_shape=[
            jax.ShapeDtypeStruct((n_main // LANES, t, LANES), BF16),
            jax.ShapeDtypeStruct((t, LANES), F32),
        ],
        scratch_shapes=scratch,
        compiler_params=pltpu.CompilerParams(
            dimension_semantics=("arbitrary", "arbitrary"), vmem_limit_bytes=VMEM_LIMIT_BYTES),
        name="inproj",
    )(*args, w_main, w_small)


def _gates_kernel(small_ref, fb_ref, wa_ref, ba_ref, fk_ref, fq_ref, p_ref):
    s_len = small_ref.shape[0]
    small = small_ref[...]
    log_f = _log_sigmoid(small + fb_ref[...])
    a_logit = jnp.dot(small.astype(BF16), wa_ref[...], preferred_element_type=F32) + ba_ref[...]
    log_a = _log_sigmoid(a_logit) * (1.0 / GLA_TAU)
    both = jnp.concatenate([log_f, log_a], axis=-1)

    row = lax.broadcasted_iota(jnp.int32, (CUM_BLOCK, CUM_BLOCK), 0)
    col = lax.broadcasted_iota(jnp.int32, (CUM_BLOCK, CUM_BLOCK), 1)
    tri = (col <= row).astype(BF16)
    low_half = lax.broadcasted_iota(jnp.int32, (CUM_BLOCK, LANES), 1) < GLA_DK
    carry = jnp.zeros((1, both.shape[1]), F32)
    for c in range(s_len // CUM_BLOCK):
        blk = both[c * CUM_BLOCK:(c + 1) * CUM_BLOCK]
        hi = blk.astype(BF16)
        rest = blk - hi.astype(F32)
        mid = rest.astype(BF16)
        lo = (rest - mid.astype(F32)).astype(BF16)
        cum = carry + sum(jnp.dot(tri, part, preferred_element_type=F32) for part in (hi, mid, lo))
        carry = cum[CUM_BLOCK - 1:CUM_BLOCK]
        rows = slice(c * CUM_BLOCK, (c + 1) * CUM_BLOCK)
        f_cum = cum[:, :LANES] * LOG2E
        per_tile = Q_TILE // CUM_BLOCK
        fq_ref[c // per_tile, :, (c % per_tile) * CUM_BLOCK:(c % per_tile + 1) * CUM_BLOCK] = (
            f_cum.T[:GATE_ROWS])
        for h in range(N_HEADS):
            fk_ref[h, rows, :] = jnp.broadcast_to(f_cum[:, h:h + 1], (CUM_BLOCK, LANES))
            pair = cum[:, (1 + h // 2) * LANES:(2 + h // 2) * LANES]
            swapped = pltpu.roll(pair, GLA_DK, 1)
            p_ref[h, rows, :] = (jnp.where(low_half, pair, swapped) if h % 2 == 0
                                 else jnp.where(low_half, swapped, pair))


def _gates(small, fb_pad, wa_pad, ba, *, batch, s_len):
    t = small.shape[0]
    return pl.pallas_call(
        _gates_kernel,
        grid=(batch,),
        in_specs=[
            pl.BlockSpec((s_len, LANES), lambda b: (b, 0)),
            pl.BlockSpec((1, LANES), lambda b: (0, 0)),
            pl.BlockSpec((LANES, N_HEADS * GLA_DK), lambda b: (0, 0)),
            pl.BlockSpec((1, N_HEADS * GLA_DK), lambda b: (0, 0)),
        ],
        out_specs=[
            pl.BlockSpec((N_HEADS, s_len, LANES), lambda b: (0, b, 0)),
            pl.BlockSpec((None, s_len // Q_TILE, GATE_ROWS, Q_TILE), lambda b: (b, 0, 0, 0)),
            pl.BlockSpec((N_HEADS, s_len, LANES), lambda b: (0, b, 0)),
        ],
        out_shape=[
            jax.ShapeDtypeStruct((N_HEADS, t, LANES), F32),
            jax.ShapeDtypeStruct((batch, s_len // Q_TILE, GATE_ROWS, Q_TILE), F32),
            jax.ShapeDtypeStruct((N_HEADS, t, LANES), F32),
        ],
        compiler_params=pltpu.CompilerParams(dimension_semantics=("arbitrary",)),
        name="gates",
    )(small, fb_pad, wa_pad, ba)


def _causal_mask_t(tk, tq):
    key = lax.broadcasted_iota(jnp.int32, (tk, tq), 0)
    qry = lax.broadcasted_iota(jnp.int32, (tk, tq), 1)
    return key <= qry


def _transpose_values(v_ref, vt_scr):
    dv = v_ref.shape[2]
    for h in range(v_ref.shape[0]):
        for j in range(v_ref.shape[1] // K_TILE):
            blk = v_ref[h, j * K_TILE:(j + 1) * K_TILE, :].astype(F32)
            vt_scr[h, j, :dv, :] = blk.T.astype(BF16)
            vt_scr[h, j, dv:, :] = jnp.ones((ONES_ROWS, K_TILE), BF16)


def _col_max(z):
    rows = z.shape[0]
    while rows > 8:
        rows //= 2
        z = jnp.maximum(z[:rows], z[rows:])
    return jnp.max(z, axis=0, keepdims=True)


def _causal_tiles_t(n_tiles, prepare_fn, logits_fn, vt_fn, finish_fn, z_scr, acc_scr):
    n, _, tq = acc_scr.shape

    def update(j, slot, ms, tile_max, masked):
        stats = []
        for s in range(n):
            z = z_scr[slot, s]
            if masked:
                z = jnp.where(_causal_mask_t(K_TILE, tq), z, -jnp.inf)
                m_new = jnp.maximum(ms[s], _col_max(z))
            else:
                m_new = jnp.maximum(ms[s], tile_max[s])
            stats.append((m_new, jnp.exp2(ms[s] - m_new), jnp.exp2(z - m_new).astype(BF16)))
        for s in range(n):
            pv = jnp.dot(vt_fn(s, j), stats[s][2], preferred_element_type=F32)
            acc_scr[s] = stats[s][1] * acc_scr[s] + pv
        return tuple(st[0] for st in stats)

    def first_logits(i):
        ctx = prepare_fn(i)
        return ctx, [logits_fn(ctx, s, jnp.int32(0)) for s in range(n)]

    def open_tile(zs):
        for s in range(n):
            z_scr[0, s] = zs[s]
            acc_scr[s] = jnp.zeros(acc_scr.shape[1:], F32)
        return (tuple(jnp.full((1, tq), -jnp.inf, F32) for _ in range(n)),
                tuple(_col_max(z) for z in zs))

    def close_tile(i, ms):
        update(i, lax.rem(i, 2), ms, None, True)
        finish_fn(i)

    def off_diagonal(i, ctx, state):
        def body(j, state):
            ms, tile_max = state
            slot = lax.rem(j, 2)
            z_next = [logits_fn(ctx, s, j + 1) for s in range(n)]
            ms = update(j, slot, ms, tile_max, False)
            for s in range(n):
                z_scr[1 - slot, s] = z_next[s]
            return ms, tuple(_col_max(z) for z in z_next)

        return lax.fori_loop(0, i, body, state)[0]

    _, zs = first_logits(jnp.int32(0))
    ms, _ = open_tile(zs)

    def outer(i, ms_prev):
        ctx, zs = first_logits(i)
        close_tile(i - 1, ms_prev)
        return off_diagonal(i, ctx, open_tile(zs))

    ms = lax.fori_loop(1, n_tiles, outer, ms)
    close_tile(jnp.int32(n_tiles - 1), ms)


def _softmax_result_t(acc):
    dv = acc.shape[0] - ONES_ROWS
    return acc[:dv] / acc[dv:dv + 1]


def _rope(x, cos_t, sin_up, sin_dn, half):
    return (x * cos_t + pltpu.roll(x, LANES - half, 1) * sin_up
            + pltpu.roll(x, half, 1) * sin_dn)


def _key_rows(j):
    return pl.ds(pl.multiple_of(j * K_TILE, K_TILE), K_TILE)


def _query_rows(i):
    return pl.ds(pl.multiple_of(i * Q_TILE, Q_TILE), Q_TILE)


def _batch_spec(s_len):
    return lambda slab: pl.BlockSpec((N_HEADS, s_len, LANES), lambda b: (slab, b, 0))


def _softmax_scratch(s_len):
    return [pltpu.VMEM((N_HEADS, s_len // K_TILE, HEAD_DIM + ONES_ROWS, K_TILE), BF16),
            pltpu.VMEM((2, N_HEADS, K_TILE, Q_TILE), F32),
            pltpu.VMEM((N_HEADS, HEAD_DIM + ONES_ROWS, Q_TILE), F32)]


def _fox_kernel(q_ref, k_ref, v_ref, g_ref, fk_ref, fq_ref, o_ref, vt_scr, z_scr, acc_scr):
    scale = HEAD_DIM ** -0.5 * LOG2E
    _transpose_values(v_ref, vt_scr)

    def prepare(i):
        fq_all = fq_ref[i]
        return ([q_ref[h, _query_rows(i), :] for h in range(N_HEADS)],
                [fq_all[h:h + 1, :] for h in range(N_HEADS)])

    def logits(ctx, h, j):
        qs, fqs = ctx
        fk = fk_ref[h, _key_rows(j), :]
        fk = jnp.concatenate([fk] * (Q_TILE // LANES), axis=-1)
        return _nt_dot(k_ref[h, _key_rows(j), :], qs[h]) * scale + (fqs[h] - fk)

    def finish(i):
        for h in range(N_HEADS):
            o = _softmax_result_t(acc_scr[h]).T
            gate = _silu(g_ref[h, _query_rows(i), :].astype(F32))
            o_ref[h, _query_rows(i), :] = (o * gate).astype(BF16)

    _causal_tiles_t(q_ref.shape[1] // Q_TILE, prepare, logits, lambda h, j: vt_scr[h, j], finish,
                    z_scr, acc_scr)


def _fox(proj, fk_rep, fq_rows, *, batch, s_len):
    t = proj.shape[1]
    spec = _batch_spec(s_len)
    return pl.pallas_call(
        _fox_kernel,
        grid=(batch,),
        in_specs=[
            spec(_SLAB["fox_q"]), spec(_SLAB["fox_k"]), spec(_SLAB["fox_v"]), spec(_SLAB["fox_g"]),
            spec(0),
            pl.BlockSpec((None, s_len // Q_TILE, GATE_ROWS, Q_TILE), lambda b: (b, 0, 0, 0)),
        ],
        out_specs=spec(0),
        out_shape=jax.ShapeDtypeStruct((N_HEADS, t, LANES), BF16),
        scratch_shapes=_softmax_scratch(s_len),
        compiler_params=pltpu.CompilerParams(dimension_semantics=("arbitrary",)),
        name="fox",
    )(proj, proj, proj, proj, fk_rep, fq_rows)


def _diff_kernel(lam_init, q_ref, k_ref, v_ref, g_ref, lam_ref, ng_ref, c_ref, u_ref, d_ref, o_ref,
                 kz_scr, qr_scr, a1_scr, vt_scr, z_scr, acc_scr):
    half = DIFF_QK_DIM // ROPE_FRACTION // 2
    scale = DIFF_QK_DIM ** -0.5 * LOG2E

    _transpose_values(v_ref, vt_scr)
    first = lax.broadcasted_iota(jnp.int32, (k_ref.shape[1], LANES), 1) < DIFF_QK_DIM
    for h in range(N_HEADS):
        kr = _rope(k_ref[h].astype(F32), c_ref[...], u_ref[...], d_ref[...], half)
        kz_scr[0, h] = jnp.where(first, kr, 0.0).astype(BF16)
        kz_scr[1, h] = jnp.where(first, 0.0, kr).astype(BF16)
        qr_scr[h] = _rope(q_ref[h].astype(F32), c_ref[...], u_ref[...], d_ref[...], half).astype(BF16)

    lf = lam_ref[...]
    lam = (jnp.exp(jnp.sum(lf[0:1] * lf[1:2], axis=-1, keepdims=True))
           - jnp.exp(jnp.sum(lf[2:3] * lf[3:4], axis=-1, keepdims=True)) + lam_init)

    def prepare(i):
        return [qr_scr[h, _query_rows(i), :] for h in range(N_HEADS)]

    def finish_first(i):
        for h in range(N_HEADS):
            a1_scr[h, _query_rows(i), :] = _softmax_result_t(acc_scr[h]).T

    def finish_second(i):
        for h in range(N_HEADS):
            o = a1_scr[h, _query_rows(i), :] - lam * _softmax_result_t(acc_scr[h]).T
            o = o * lax.rsqrt(jnp.mean(o * o, axis=-1, keepdims=True) + RMS_EPS) * ng_ref[...]
            o = o * (1.0 - lam_init)
            gate = _silu(g_ref[h, _query_rows(i), :].astype(F32))
            o_ref[h, _query_rows(i), :] = (o * gate).astype(BF16)

    for comp, finish in ((0, finish_first), (1, finish_second)):
        def logits(qs, h, j, comp=comp):
            return _nt_dot(kz_scr[comp, h, _key_rows(j), :], qs[h]) * scale

        _causal_tiles_t(q_ref.shape[1] // Q_TILE, prepare, logits, lambda h, j: vt_scr[h, j], finish,
                        z_scr, acc_scr)


def _diff(proj, diff_lam, diff_norm_g, tables, lam_init, *, batch, s_len):
    t = proj.shape[1]
    spec = _batch_spec(s_len)
    table_spec = pl.BlockSpec((s_len, LANES), lambda b: (0, 0))
    return pl.pallas_call(
        functools.partial(_diff_kernel, lam_init),
        grid=(batch,),
        in_specs=[
            spec(_SLAB["diff_q"]), spec(_SLAB["diff_k"]), spec(_SLAB["diff_v"]), spec(_SLAB["diff_g"]),
            pl.BlockSpec((4, DIFF_QK_DIM), lambda b: (0, 0)),
            pl.BlockSpec((1, HEAD_DIM), lambda b: (0, 0)),
            table_spec, table_spec, table_spec,
        ],
        out_specs=spec(0),
        out_shape=jax.ShapeDtypeStruct((N_HEADS, t, LANES), BF16),
        scratch_shapes=[pltpu.VMEM((2, N_HEADS, s_len, LANES), BF16),
                        pltpu.VMEM((N_HEADS, s_len, LANES), BF16),
                        pltpu.VMEM((N_HEADS, s_len, LANES), F32),
                        ] + _softmax_scratch(s_len),
        compiler_params=pltpu.CompilerParams(dimension_semantics=("arbitrary",)),
        name="diff",
    )(proj, proj, proj, proj, diff_lam, diff_norm_g.reshape(1, HEAD_DIM), *tables)


def _moba_kernel(q_ref, k_ref, v_ref, g_ref, c_ref, u_ref, d_ref,
                 o_ref, kr_scr, kmean_scr, vt_scr, z_scr, acc_scr):
    tq = Q_TILE
    half = HEAD_DIM // ROPE_FRACTION // 2
    scale = HEAD_DIM ** -0.5 * LOG2E
    n_blk = k_ref.shape[1] // MOBA_BLOCK

    _transpose_values(v_ref, vt_scr)
    kmean_scr[...] = jnp.zeros(kmean_scr.shape, F32)
    for h in range(N_HEADS):
        kr = _rope(k_ref[h].astype(F32), c_ref[...], u_ref[...], d_ref[...], half)
        kr_scr[h] = kr.astype(BF16)
        for n in range(n_blk):
            blk = kr[n * MOBA_BLOCK:(n + 1) * MOBA_BLOCK]
            kmean_scr[h, n:n + 1, :] = jnp.sum(blk, axis=0, keepdims=True) * (1.0 / MOBA_BLOCK)

    blk_id = lax.broadcasted_iota(jnp.int32, (GATE_ROWS, tq), 0).astype(F32)

    def prepare(i):
        rows = _query_rows(i)
        past = blk_id < lax.convert_element_type(i, F32)
        qs, biases = [], []
        for h in range(N_HEADS):
            q = _rope(q_ref[h, rows, :].astype(F32), c_ref[rows, :], u_ref[rows, :], d_ref[rows, :],
                      half).astype(BF16)
            gate = _nt_dot(kmean_scr[h].astype(BF16), q)
            gate = jnp.where(past, gate, -jnp.inf)
            open_ = past
            sel = jnp.zeros((GATE_ROWS, tq), jnp.bool_)
            for _ in range(MOBA_TOPK):
                top = jnp.max(gate, axis=0, keepdims=True)
                idx = jnp.min(jnp.where(gate == top, blk_id, float(GATE_ROWS)), axis=0, keepdims=True)
                pick = (blk_id == idx) & open_
                sel = sel | pick
                open_ = open_ & jnp.logical_not(pick)
                gate = jnp.where(pick, -jnp.inf, gate)
            qs.append(q)
            biases.append(jnp.where(sel, 0.0, MASKED_LOGIT))
        return i, qs, biases

    def logits(ctx, h, j):
        i, qs, biases = ctx
        here = blk_id == lax.convert_element_type(j, F32)
        bias = jnp.sum(jnp.where(here, biases[h], 0.0), axis=0, keepdims=True)
        bias = jnp.where(j == i, 0.0, bias)
        return _nt_dot(kr_scr[h, _key_rows(j), :], qs[h]) * scale + bias

    def finish(i):
        for h in range(N_HEADS):
            o = _softmax_result_t(acc_scr[h]).T
            gate = _silu(g_ref[h, _query_rows(i), :].astype(F32))
            o_ref[h, _query_rows(i), :] = (o * gate).astype(BF16)

    _causal_tiles_t(q_ref.shape[1] // tq, prepare, logits, lambda h, j: vt_scr[h, j], finish,
                    z_scr, acc_scr)


def _moba(proj, tables, *, batch, s_len):
    t = proj.shape[1]
    spec = _batch_spec(s_len)
    table_spec = pl.BlockSpec((s_len, LANES), lambda b: (0, 0))
    return pl.pallas_call(
        _moba_kernel,
        grid=(batch,),
        in_specs=[
            spec(_SLAB["moba_q"]), spec(_SLAB["moba_k"]), spec(_SLAB["moba_v"]), spec(_SLAB["moba_g"]),
            table_spec, table_spec, table_spec,
        ],
        out_specs=spec(0),
        out_shape=jax.ShapeDtypeStruct((N_HEADS, t, LANES), BF16),
        scratch_shapes=[pltpu.VMEM((N_HEADS, s_len, LANES), BF16),
                        pltpu.VMEM((N_HEADS, GATE_ROWS, LANES), F32),
                        ] + _softmax_scratch(s_len),
        compiler_params=pltpu.CompilerParams(dimension_semantics=("arbitrary",)),
        name="moba",
    )(proj, proj, proj, proj, *tables)


def _gla_factorised(x_ref, v_ref, g_ref, p_ref, ng_ref, o_ref):
    s_len = x_ref.shape[1]
    c_len = GLA_CHUNK
    scale = GLA_DK ** -0.5
    lane = lax.broadcasted_iota(jnp.int32, (c_len, LANES), 1)
    qhalf = lane < GLA_DK
    qhalf_row = lax.broadcasted_iota(jnp.int32, (1, LANES), 1) < GLA_DK
    row = lax.broadcasted_iota(jnp.int32, (c_len, c_len), 0)
    col = lax.broadcasted_iota(jnp.int32, (c_len, c_len), 1)
    causal = col <= row

    group = GLA_GROUP
    pairs = [(h, c) for h in range(N_HEADS) for c in range(group)]

    def body(g, carries):
        prep = {}
        for h, c in pairs:
            rows = pl.ds(pl.multiple_of(g * (group * c_len), group * c_len) + c * c_len, c_len)
            qcb = x_ref[h // 2, rows, :].astype(F32)
            kcb = x_ref[2 + h // 2, rows, :].astype(F32)
            if h % 2 == 0:
                x = jnp.where(qhalf, qcb, pltpu.roll(kcb, GLA_DK, 1))
            else:
                x = jnp.where(qhalf, pltpu.roll(qcb, GLA_DK, 1), kcb)
            p = p_ref[h, rows, :]
            base = carries[h][1] if c == 0 else prep[h, c - 1]["p_last"]
            cum = p - base
            xt = x * jnp.exp(jnp.where(qhalf, cum, -cum))
            kr = pltpu.roll(xt, GLA_DK, 1)
            decay = jnp.where(qhalf_row, jnp.exp(cum[c_len - 1:c_len]), 0.0)
            prep[h, c] = dict(
                rows=rows, qz=jnp.where(qhalf, xt, 0.0).astype(BF16), kr=kr.astype(BF16),
                khat=(jnp.where(qhalf, kr, 0.0) * decay).astype(BF16),
                decay=decay, v=v_ref[h, rows, :], p_last=p[c_len - 1:c_len])
        scores, upd = {}, {}
        for h, c in pairs:
            d = prep[h, c]
            scores[h, c] = _nt_dot(d["qz"], d["kr"])
            upd[h, c] = lax.dot_general(d["v"], d["khat"], (((0,), (0,)), ((), ())),
                                        preferred_element_type=F32)
        states = {}
        for h in range(N_HEADS):
            states[h, 0] = carries[h][0]
            for c in range(group):
                states[h, c + 1] = states[h, c] * prep[h, c]["decay"] + upd[h, c]
        outs = {}
        for h, c in pairs:
            sc = jnp.where(causal, scores[h, c] * scale, 0.0).astype(BF16)
            outs[h, c] = (jnp.dot(sc, prep[h, c]["v"], preferred_element_type=F32),
                          _nt_dot(prep[h, c]["qz"], states[h, c].astype(BF16)))
        for h, c in pairs:
            o = outs[h, c][0] + outs[h, c][1] * scale
            y = o * lax.rsqrt(jnp.mean(o * o, axis=-1, keepdims=True) + RMS_EPS) * ng_ref[...]
            rows = prep[h, c]["rows"]
            o_ref[h, rows, :] = (y * _silu(g_ref[h, rows, :].astype(F32))).astype(BF16)
        return tuple((states[h, group], prep[h, group - 1]["p_last"]) for h in range(N_HEADS))

    init = tuple((jnp.zeros((HEAD_DIM, LANES), F32), jnp.zeros((1, LANES), F32))
                 for _ in range(N_HEADS))
    lax.fori_loop(0, s_len // (group * c_len), body, init)


def _gla_unfactorised(x_ref, v_ref, g_ref, p_ref, ng_ref, o_ref, k_scr, cum_scr):
    s_len = x_ref.shape[1]
    c_len = GLA_CHUNK
    scale = GLA_DK ** -0.5
    lane = lax.broadcasted_iota(jnp.int32, (c_len, LANES), 1)
    row = lax.broadcasted_iota(jnp.int32, (c_len, LANES), 0)
    qhalf = lane < GLA_DK
    qhalf_row = lax.broadcasted_iota(jnp.int32, (1, LANES), 1) < GLA_DK

    for h in range(N_HEADS):
        def chunk(c, carry, h=h):
            state_t, base = carry
            rows = pl.ds(pl.multiple_of(c * c_len, c_len), c_len)
            qcb = x_ref[h // 2, rows, :].astype(F32)
            kcb = x_ref[2 + h // 2, rows, :].astype(F32)
            if h % 2 == 0:
                q, k = qcb, kcb
            else:
                q, k = pltpu.roll(qcb, GLA_DK, 1), pltpu.roll(kcb, GLA_DK, 1)
            q = jnp.where(qhalf, q, 0.0)
            k = jnp.where(qhalf, k, 0.0)
            v = v_ref[h, rows, :]
            p = p_ref[h, rows, :]
            cum = p - base
            last = cum[c_len - 1:c_len]
            k_scr[...] = k
            cum_scr[...] = cum

            def key(s, scores):
                ks = k_scr[pl.ds(s, 1), :]
                cs = cum_scr[pl.ds(s, 1), :]
                col = jnp.sum(q * ks * jnp.exp(jnp.minimum(cum - cs, 0.0)), axis=-1, keepdims=True)
                return jnp.where(lane == s, col, scores)

            scores = lax.fori_loop(0, c_len, key, jnp.zeros((c_len, LANES), F32))
            scores = jnp.where(lane <= row, scores * scale, 0.0)[:, :c_len].astype(BF16)
            o = jnp.dot(scores, v, preferred_element_type=F32)
            o = o + _nt_dot((q * jnp.exp(cum)).astype(BF16), state_t.astype(BF16)) * scale
            khat = (k * jnp.exp(last - cum)).astype(BF16)
            upd = lax.dot_general(v, khat, (((0,), (0,)), ((), ())), preferred_element_type=F32)
            state_t = state_t * jnp.where(qhalf_row, jnp.exp(last), 0.0) + upd
            y = o * lax.rsqrt(jnp.mean(o * o, axis=-1, keepdims=True) + RMS_EPS) * ng_ref[...]
            o_ref[h, rows, :] = (y * _silu(g_ref[h, rows, :].astype(F32))).astype(BF16)
            return state_t, p[c_len - 1:c_len]

        lax.fori_loop(0, s_len // c_len, chunk,
                      (jnp.zeros((HEAD_DIM, LANES), F32), jnp.zeros((1, LANES), F32)))


def _gla_kernel(x_ref, v_ref, g_ref, p_ref, ng_ref, o_ref, k_scr, cum_scr):
    n_chunks = x_ref.shape[1] // GLA_CHUNK
    worst = jnp.zeros((1, 1), F32)
    for h in range(N_HEADS):
        ends = p_ref[h, pl.ds(GLA_CHUNK - 1, n_chunks, stride=GLA_CHUNK), :]
        starts = jnp.concatenate([jnp.zeros((1, LANES), F32), ends[:-1]], axis=0)
        worst = jnp.maximum(worst, jnp.max(starts - ends, keepdims=True))
    risky = worst[0, 0] > GLA_SAFE_DECAY

    @pl.when(risky)
    def _():
        _gla_unfactorised(x_ref, v_ref, g_ref, p_ref, ng_ref, o_ref, k_scr, cum_scr)

    @pl.when(jnp.logical_not(risky))
    def _():
        _gla_factorised(x_ref, v_ref, g_ref, p_ref, ng_ref, o_ref)


def _gla(proj, p_cum, gla_norm_g, *, batch, s_len):
    t = proj.shape[1]
    spec = _batch_spec(s_len)
    return pl.pallas_call(
        _gla_kernel,
        grid=(batch,),
        in_specs=[
            spec(_SLAB["gla_qk"]), spec(_SLAB["gla_v"]), spec(_SLAB["gla_g"]), spec(0),
            pl.BlockSpec((1, HEAD_DIM), lambda b: (0, 0)),
        ],
        out_specs=spec(0),
        out_shape=jax.ShapeDtypeStruct((N_HEADS, t, LANES), BF16),
        scratch_shapes=[pltpu.VMEM((GLA_CHUNK, LANES), F32), pltpu.VMEM((GLA_CHUNK, LANES), F32)],
        compiler_params=pltpu.CompilerParams(dimension_semantics=("arbitrary",)),
        name="gla",
    )(proj, proj, proj, p_cum, gla_norm_g.reshape(1, HEAD_DIM))


def _outproj_kernel(final, x_ref, a_ref, b_ref, c_ref, d_ref, w_ref, g_ref, *o_refs):
    parts = [r[h] for r in (a_ref, b_ref, c_ref, d_ref) for h in range(N_HEADS)]
    mixed = jnp.concatenate(parts, axis=-1)
    y = x_ref[...] + jnp.dot(mixed, w_ref[...], preferred_element_type=F32)
    normed = y * lax.rsqrt(jnp.mean(y * y, axis=-1, keepdims=True) + RMS_EPS) * g_ref[...]
    if final:
        o_refs[0][...] = normed
    else:
        o_refs[0][...] = y
        o_refs[1][...] = normed.astype(BF16)


def _outproj(x2, mixers, w_out, layer, norm_gain, *, final, tm):
    t, d = x2.shape
    mspec = pl.BlockSpec((N_HEADS, tm, LANES), lambda i: (0, i, 0))
    row_spec = pl.BlockSpec((tm, d), lambda i: (i, 0))
    out_specs, out_shape = [row_spec], [jax.ShapeDtypeStruct((t, d), F32)]
    if not final:
        out_specs.append(row_spec)
        out_shape.append(jax.ShapeDtypeStruct((t, d), BF16))
    return pl.pallas_call(
        functools.partial(_outproj_kernel, final),
        grid=(t // tm,),
        in_specs=[
            row_spec, mspec, mspec, mspec, mspec,
            pl.BlockSpec((None,) + w_out.shape[1:], lambda i: (layer, 0, 0)),
            pl.BlockSpec((1, d), lambda i: (0, 0)),
        ],
        out_specs=out_specs,
        out_shape=out_shape,
        compiler_params=pltpu.CompilerParams(
            dimension_semantics=("arbitrary",), vmem_limit_bytes=VMEM_LIMIT_BYTES),
        name="outproj",
    )(x2, *mixers, w_out, norm_gain.reshape(1, d))


RELAYOUT_ROWS = 256
RELAYOUT_TAIL = 32


def _relayout_kernel(a_ref, b_ref, o_ref):
    r = pl.program_id(0)
    depth, rows, _ = o_ref.shape
    for l in range(depth):
        x = jnp.concatenate([a_ref[:, l, :], b_ref[:, l, :]], axis=0)
        out_lo = 0
        for (src_lo, src_hi) in _MAIN_RUNS:
            shift = src_lo - out_lo
            out_hi = out_lo + (src_hi - src_lo)
            assert shift <= RELAYOUT_TAIL and out_lo % rows == 0 and out_hi % rows == 0

            @pl.when((r >= out_lo // rows) & (r < out_hi // rows))
            def _(shift=shift, l=l, x=x):
                o_ref[l] = x[shift:shift + rows].astype(BF16)

            out_lo = out_hi


def _relayout_w_in(w_in):
    depth, d, d_in = w_in.shape
    wt = jnp.transpose(w_in, (2, 0, 1))
    rows, tail = RELAYOUT_ROWS, RELAYOUT_TAIL
    main = pl.pallas_call(
        _relayout_kernel,
        grid=(N_MAIN // rows,),
        in_specs=[pl.BlockSpec((rows, depth, d), lambda r: (r, 0, 0)),
                  pl.BlockSpec((tail, depth, d), lambda r: ((r + 1) * (rows // tail), 0, 0))],
        out_specs=pl.BlockSpec((depth, rows, d), lambda r: (0, r, 0)),
        out_shape=jax.ShapeDtypeStruct((depth, N_MAIN, d), BF16),
        compiler_params=pltpu.CompilerParams(dimension_semantics=("arbitrary",)),
        name="relayout",
    )(wt, wt)
    f_off, a_off = _SEG_OFF["fox_f"][0], _SEG_OFF["gla_a"][0]
    gate_cols = jnp.concatenate([w_in[..., f_off:f_off + N_HEADS],
                                 w_in[..., a_off:a_off + GLA_RANK]], axis=-1)
    small = jnp.pad(jnp.transpose(gate_cols, (0, 2, 1)),
                    ((0, 0), (0, LANES - N_HEADS - GLA_RANK), (0, 0))).astype(BF16)
    return main, small


def _rope_tables(s_len, comp_dim, rot_dim):
    half = rot_dim // 2
    inv_freq = ROPE_THETA ** (-jnp.arange(0, rot_dim, 2, dtype=F32) / rot_dim)
    ang = jnp.arange(s_len, dtype=F32)[:, None] * inv_freq[None, :]
    cos, sin = jnp.cos(ang), jnp.sin(ang)
    zeros = jnp.zeros((s_len, comp_dim - rot_dim), F32)
    ones = jnp.ones((s_len, comp_dim - rot_dim), F32)
    reps = LANES // comp_dim
    cos_t = jnp.tile(jnp.concatenate([cos, cos, ones], axis=-1), (1, reps))
    sin_up = jnp.tile(jnp.concatenate([-sin, jnp.zeros_like(sin), zeros], axis=-1), (1, reps))
    sin_dn = jnp.tile(jnp.concatenate([jnp.zeros_like(sin), sin, zeros], axis=-1), (1, reps))
    return cos_t, sin_up, sin_dn


def kernel(x, norm_g, w_in, fox_fb, diff_lam, diff_norm_g, gla_wa2, gla_ba, gla_norm_g, w_out,
           final_norm_g):
    batch, s_len, d_model = x.shape
    depth = w_in.shape[0]
    assert s_len % Q_TILE == 0 and Q_TILE == K_TILE == MOBA_BLOCK
    assert s_len // MOBA_BLOCK <= GATE_ROWS
    t = batch * s_len
    tm_in = min(1024, t)
    tn_in = 1536
    tn_next = 2560
    tm_out = min(512, t)

    w_main, w_small = _relayout_w_in(w_in)
    w_out_b = w_out.astype(BF16)
    fb_pad = jnp.pad(fox_fb, ((0, 0), (0, LANES - N_HEADS)))[:, None, :]
    wa_pad = jnp.pad(gla_wa2, ((0, 0), (N_HEADS, LANES - N_HEADS - GLA_RANK), (0, 0))).astype(BF16)
    ba = gla_ba[:, None, :]
    rope_diff = _rope_tables(s_len, DIFF_QK_DIM, DIFF_QK_DIM // ROPE_FRACTION)
    rope_moba = _rope_tables(s_len, HEAD_DIM, HEAD_DIM // ROPE_FRACTION)

    x2 = x.reshape(t, d_model)
    h2 = None
    for l in range(depth):
        if h2 is None:
            proj, small = _inproj(x2, norm_g[l], w_main, w_small, l, tm=tm_in, tn=tn_in)
        else:
            proj, small = _inproj(h2, None, w_main, w_small, l, tm=tm_in, tn=tn_next)
        fk_rep, fq_rows, p_cum = _gates(small, fb_pad[l], wa_pad[l], ba[l], batch=batch, s_len=s_len)
        lam_init = 0.8 - 0.6 * math.exp(-0.3 * l)
        mixers = (
            _fox(proj, fk_rep, fq_rows, batch=batch, s_len=s_len),
            _diff(proj, diff_lam[l], diff_norm_g[l], rope_diff, lam_init, batch=batch, s_len=s_len),
            _moba(proj, rope_moba, batch=batch, s_len=s_len),
            _gla(proj, p_cum, gla_norm_g[l], batch=batch, s_len=s_len),
        )
        if l == depth - 1:
            (x2,) = _outproj(x2, mixers, w_out_b, l, final_norm_g, final=True, tm=tm_out)
        else:
            x2, h2 = _outproj(x2, mixers, w_out_b, l, norm_g[l + 1], final=False, tm=tm_out)
    return x2.reshape(batch, s_len, d_model)
```

```python
import functools
import math

import jax
import jax.numpy as jnp
from jax import lax
from jax.experimental import pallas as pl
from jax.experimental.pallas import tpu as pltpu

F32 = jnp.float32
BF16 = jnp.bfloat16

LANES = 128
VMEM_LIMIT_BYTES = 56 * 2 ** 20
HEAD_DIM = 128
N_HEADS = 4
GROUP_W = N_HEADS * HEAD_DIM
DIFF_QK_DIM = HEAD_DIM // 2
MOBA_BLOCK = 256
MOBA_TOPK = 3
GLA_DK = HEAD_DIM // 2
GLA_RANK = 16
GLA_TAU = 16.0
GLA_CHUNK = 64
ROPE_THETA = 500000.0
ROPE_FRACTION = 4
RMS_EPS = 1e-6
MASKED_LOGIT = -1e30

_SEGMENTS = (
    ("fox_q", GROUP_W), ("fox_k", GROUP_W), ("fox_v", GROUP_W),
    ("fox_f", N_HEADS), ("fox_g", GROUP_W),
    ("diff_q", GROUP_W), ("diff_k", GROUP_W), ("diff_v", GROUP_W), ("diff_g", GROUP_W),
    ("moba_q", GROUP_W), ("moba_k", GROUP_W), ("moba_v", GROUP_W), ("moba_g", GROUP_W),
    ("gla_q", N_HEADS * GLA_DK), ("gla_k", N_HEADS * GLA_DK), ("gla_v", GROUP_W),
    ("gla_a", GLA_RANK), ("gla_g", GROUP_W),
)
_SEG_OFF = {}
_off = 0
for _name, _w in _SEGMENTS:
    _SEG_OFF[_name] = (_off, _w)
    _off += _w

_SLABS = ("fox_q", "fox_k", "fox_v", "fox_g", "diff_q", "diff_k", "diff_v", "diff_g",
          "moba_q", "moba_k", "moba_v", "moba_g", "gla_qk", "gla_v", "gla_g")
_SLAB = {name: i for i, name in enumerate(_SLABS)}
N_MAIN = len(_SLABS) * GROUP_W
_MAIN_RUNS = ((0, _SEG_OFF["fox_f"][0]),
              (_SEG_OFF["fox_g"][0], _SEG_OFF["gla_a"][0]),
              (_SEG_OFF["gla_g"][0], _off))
assert sum(b - a for a, b in _MAIN_RUNS) == N_MAIN

Q_TILE = 256
K_TILE = 256
CUM_BLOCK = 128
GATE_ROWS = 16
GLA_GROUP = 4
GLA_SAFE_DECAY = 60.0
ONES_ROWS = 16
LOG2E = math.log2(math.e)


def _silu(x):
    return x * (1.0 / (1.0 + jnp.exp(-x)))


def _log_sigmoid(x):
    return jnp.minimum(x, 0.0) - jnp.log1p(jnp.exp(-jnp.abs(x)))


def _nt_dot(a, b):
    return lax.dot_general(a, b, (((1,), (1,)), ((), ())), preferred_element_type=F32)


def _inproj_tile(h, w_ref, o_ref):
    acc = _nt_dot(h, w_ref[...])
    for c in range(o_ref.shape[0]):
        o_ref[c] = acc[:, c * LANES:(c + 1) * LANES].astype(BF16)


def _inproj_norm_kernel(x_ref, g_ref, w_ref, ws_ref, o_ref, small_ref, h_scr):
    @pl.when(pl.program_id(1) == 0)
    def _():
        x = x_ref[...]
        y = x * lax.rsqrt(jnp.mean(x * x, axis=-1, keepdims=True) + RMS_EPS) * g_ref[...]
        h = y.astype(BF16)
        h_scr[...] = h
        small_ref[...] = _nt_dot(h, ws_ref[...])

    _inproj_tile(h_scr[...], w_ref, o_ref)


def _inproj_kernel(h_ref, w_ref, ws_ref, o_ref, small_ref):
    @pl.when(pl.program_id(1) == 0)
    def _():
        small_ref[...] = _nt_dot(h_ref[...], ws_ref[...])

    _inproj_tile(h_ref[...], w_ref, o_ref)


def _inproj(rows, norm_g, w_main, w_small, layer, *, tm, tn):
    t, d = rows.shape
    n_main = w_main.shape[1]
    with_norm = norm_g is not None
    row_spec = pl.BlockSpec((tm, d), lambda i, j: (i, 0))
    weight_specs = [pl.BlockSpec((None, tn, d), lambda i, j: (layer, j, 0)),
                    pl.BlockSpec((None, LANES, d), lambda i, j: (layer, 0, 0))]
    if with_norm:
        body, in_specs = _inproj_norm_kernel, [row_spec, pl.BlockSpec((1, d), lambda i, j: (0, 0))]
        args, scratch = (rows, norm_g.reshape(1, d)), [pltpu.VMEM((tm, d), BF16)]
    else:
        body, in_specs, args, scratch = _inproj_kernel, [row_spec], (rows,), []
    return pl.pallas_call(
        body,
        grid=(t // tm, n_main // tn),
        in_specs=in_specs + weight_specs,
        out_specs=[
            pl.BlockSpec((tn // LANES, tm, LANES), lambda i, j: (j, i, 0)),
            pl.BlockSpec((tm, LANES), lambda i, j: (i, 0)),
        ],
        out_shape=[
            jax.ShapeDtypeStruct((n_main // LANES, t, LANES), BF16),
            jax.ShapeDtypeStruct((t, LANES), F32),
        ],
        scratch_shapes=scratch,
        compiler_params=pltpu.CompilerParams(
            dimension_semantics=("arbitrary", "arbitrary"), vmem_limit_bytes=VMEM_LIMIT_BYTES),
        name="inproj",
    )(*args, w_main, w_small)


def _gates_kernel(small_ref, fb_ref, wa_ref, ba_ref, fk_ref, fq_ref, p_ref):
    s_len = small_ref.shape[0]
    small = small_ref[...]
    log_f = _log_sigmoid(small + fb_ref[...])
    a_logit = jnp.dot(small.astype(BF16), wa_ref[...], preferred_element_type=F32) + ba_ref[...]
    log_a = _log_sigmoid(a_logit) * (1.0 / GLA_TAU)
    both = jnp.concatenate([log_f, log_a], axis=-1)

    row = lax.broadcasted_iota(jnp.int32, (CUM_BLOCK, CUM_BLOCK), 0)
    col = lax.broadcasted_iota(jnp.int32, (CUM_BLOCK, CUM_BLOCK), 1)
    tri = (col <= row).astype(BF16)
    carry = jnp.zeros((1, both.shape[1]), F32)
    for c in range(s_len // CUM_BLOCK):
        blk = both[c * CUM_BLOCK:(c + 1) * CUM_BLOCK]
        hi = blk.astype(BF16)
        rest = blk - hi.astype(F32)
        mid = rest.astype(BF16)
        lo = (rest - mid.astype(F32)).astype(BF16)
        cum = carry + sum(jnp.dot(tri, part, preferred_element_type=F32) for part in (hi, mid, lo))
        carry = cum[CUM_BLOCK - 1:CUM_BLOCK]
        rows = slice(c * CUM_BLOCK, (c + 1) * CUM_BLOCK)
        f_cum = cum[:, :LANES] * LOG2E
        per_tile = Q_TILE // CUM_BLOCK
        fq_ref[c // per_tile, :, (c % per_tile) * CUM_BLOCK:(c % per_tile + 1) * CUM_BLOCK] = (
            f_cum.T[:GATE_ROWS])
        fk_ref[rows, :] = f_cum
        for pair in range(N_HEADS // 2):
            p_ref[pair, rows, :] = cum[:, (1 + pair) * LANES:(2 + pair) * LANES]


def _gates(small, fb_pad, wa_pad, ba, *, batch, s_len):
    t = small.shape[0]
    return pl.pallas_call(
        _gates_kernel,
        grid=(batch,),
        in_specs=[
            pl.BlockSpec((s_len, LANES), lambda b: (b, 0)),
            pl.BlockSpec((1, LANES), lambda b: (0, 0)),
            pl.BlockSpec((LANES, N_HEADS * GLA_DK), lambda b: (0, 0)),
            pl.BlockSpec((1, N_HEADS * GLA_DK), lambda b: (0, 0)),
        ],
        out_specs=[
            pl.BlockSpec((s_len, LANES), lambda b: (b, 0)),
            pl.BlockSpec((None, s_len // Q_TILE, GATE_ROWS, Q_TILE), lambda b: (b, 0, 0, 0)),
            pl.BlockSpec((N_HEADS // 2, s_len, LANES), lambda b: (0, b, 0)),
        ],
        out_shape=[
            jax.ShapeDtypeStruct((t, LANES), F32),
            jax.ShapeDtypeStruct((batch, s_len // Q_TILE, GATE_ROWS, Q_TILE), F32),
            jax.ShapeDtypeStruct((N_HEADS // 2, t, LANES), F32),
        ],
        compiler_params=pltpu.CompilerParams(dimension_semantics=("arbitrary",)),
        name="gates",
    )(small, fb_pad, wa_pad, ba)


def _causal_mask_t(tk, tq):
    key = lax.broadcasted_iota(jnp.int32, (tk, tq), 0)
    qry = lax.broadcasted_iota(jnp.int32, (tk, tq), 1)
    return key <= qry


def _transpose_values(v_ref, vt_scr):
    dv = v_ref.shape[2]
    for h in range(v_ref.shape[0]):
        for j in range(v_ref.shape[1] // K_TILE):
            blk = v_ref[h, j * K_TILE:(j + 1) * K_TILE, :].astype(F32)
            vt_scr[h, j, :dv, :] = blk.T.astype(BF16)
            vt_scr[h, j, dv:, :] = jnp.ones((ONES_ROWS, K_TILE), BF16)


def _col_max(z):
    rows = z.shape[0]
    while rows > 8:
        rows //= 2
        z = jnp.maximum(z[:rows], z[rows:])
    return jnp.max(z, axis=0, keepdims=True)


def _causal_tiles_t(n_tiles, prepare_fn, logits_fn, vt_fn, finish_fn, z_scr, acc_scr):
    n, _, tq = acc_scr.shape

    def update(j, slot, ms, tile_max, masked):
        stats = []
        for s in range(n):
            z = z_scr[slot, s]
            if masked:
                z = jnp.where(_causal_mask_t(K_TILE, tq), z, -jnp.inf)
                m_new = jnp.maximum(ms[s], _col_max(z))
            else:
                m_new = jnp.maximum(ms[s], tile_max[s])
            stats.append((m_new, jnp.exp2(ms[s] - m_new), jnp.exp2(z - m_new).astype(BF16)))
        for s in range(n):
            pv = jnp.dot(vt_fn(s, j), stats[s][2], preferred_element_type=F32)
            acc_scr[s] = stats[s][1] * acc_scr[s] + pv
        return tuple(st[0] for st in stats)

    def first_logits(i):
        ctx = prepare_fn(i)
        return ctx, [logits_fn(ctx, s, jnp.int32(0)) for s in range(n)]

    def open_tile(zs):
        for s in range(n):
            z_scr[0, s] = zs[s]
            acc_scr[s] = jnp.zeros(acc_scr.shape[1:], F32)
        return (tuple(jnp.full((1, tq), -jnp.inf, F32) for _ in range(n)),
                tuple(_col_max(z) for z in zs))

    def close_tile(i, ms):
        update(i, lax.rem(i, 2), ms, None, True)
        finish_fn(i)

    def off_diagonal(i, ctx, state):
        def body(j, state):
            ms, tile_max = state
            slot = lax.rem(j, 2)
            z_next = [logits_fn(ctx, s, j + 1) for s in range(n)]
            ms = update(j, slot, ms, tile_max, False)
            for s in range(n):
                z_scr[1 - slot, s] = z_next[s]
            return ms, tuple(_col_max(z) for z in z_next)

        return lax.fori_loop(0, i, body, state)[0]

    _, zs = first_logits(jnp.int32(0))
    ms, _ = open_tile(zs)

    def outer(i, ms_prev):
        ctx, zs = first_logits(i)
        close_tile(i - 1, ms_prev)
        return off_diagonal(i, ctx, open_tile(zs))

    ms = lax.fori_loop(1, n_tiles, outer, ms)
    close_tile(jnp.int32(n_tiles - 1), ms)


def _softmax_result_t(acc):
    dv = acc.shape[0] - ONES_ROWS
    return acc[:dv] / acc[dv:dv + 1]


def _rope(x, cos_t, sin_up, sin_dn, half):
    return (x * cos_t + pltpu.roll(x, LANES - half, 1) * sin_up
            + pltpu.roll(x, half, 1) * sin_dn)


def _key_rows(j):
    return pl.ds(pl.multiple_of(j * K_TILE, K_TILE), K_TILE)


def _query_rows(i):
    return pl.ds(pl.multiple_of(i * Q_TILE, Q_TILE), Q_TILE)


def _batch_spec(s_len):
    return lambda slab: pl.BlockSpec((N_HEADS, s_len, LANES), lambda b: (slab, b, 0))


def _softmax_scratch(s_len):
    return [pltpu.VMEM((N_HEADS, s_len // K_TILE, HEAD_DIM + ONES_ROWS, K_TILE), BF16),
            pltpu.VMEM((2, N_HEADS, K_TILE, Q_TILE), F32),
            pltpu.VMEM((N_HEADS, HEAD_DIM + ONES_ROWS, Q_TILE), F32)]


def _fox_kernel(q_ref, k_ref, v_ref, g_ref, fc_ref, fq_ref, o_ref, fk_ref, vt_scr, z_scr, acc_scr):
    scale = HEAD_DIM ** -0.5 * LOG2E
    _transpose_values(v_ref, vt_scr)
    for c in range(fc_ref.shape[0] // CUM_BLOCK):
        rows = slice(c * CUM_BLOCK, (c + 1) * CUM_BLOCK)
        blk = fc_ref[rows, :]
        for h in range(N_HEADS):
            fk_ref[h, rows, :] = jnp.broadcast_to(blk[:, h:h + 1], (CUM_BLOCK, LANES))

    def prepare(i):
        fq_all = fq_ref[i]
        return ([q_ref[h, _query_rows(i), :] for h in range(N_HEADS)],
                [fq_all[h:h + 1, :] for h in range(N_HEADS)])

    def logits(ctx, h, j):
        qs, fqs = ctx
        fk = fk_ref[h, _key_rows(j), :]
        fk = jnp.concatenate([fk] * (Q_TILE // LANES), axis=-1)
        return _nt_dot(k_ref[h, _key_rows(j), :], qs[h]) * scale + (fqs[h] - fk)

    def finish(i):
        for h in range(N_HEADS):
            o = _softmax_result_t(acc_scr[h]).T
            gate = _silu(g_ref[h, _query_rows(i), :].astype(F32))
            o_ref[h, _query_rows(i), :] = (o * gate).astype(BF16)

    _causal_tiles_t(q_ref.shape[1] // Q_TILE, prepare, logits, lambda h, j: vt_scr[h, j], finish,
                    z_scr, acc_scr)


def _fox(proj, fk_cols, fq_rows, *, batch, s_len):
    t = proj.shape[1]
    spec = _batch_spec(s_len)
    return pl.pallas_call(
        _fox_kernel,
        grid=(batch,),
        in_specs=[
            spec(_SLAB["fox_q"]), spec(_SLAB["fox_k"]), spec(_SLAB["fox_v"]), spec(_SLAB["fox_g"]),
            pl.BlockSpec((s_len, LANES), lambda b: (b, 0)),
            pl.BlockSpec((None, s_len // Q_TILE, GATE_ROWS, Q_TILE), lambda b: (b, 0, 0, 0)),
        ],
        out_specs=spec(0),
        out_shape=jax.ShapeDtypeStruct((N_HEADS, t, LANES), BF16),
        scratch_shapes=[pltpu.VMEM((N_HEADS, s_len, LANES), F32)] + _softmax_scratch(s_len),
        compiler_params=pltpu.CompilerParams(dimension_semantics=("arbitrary",)),
        name="fox",
    )(proj, proj, proj, proj, fk_cols, fq_rows)


def _diff_kernel(lam_init, q_ref, k_ref, v_ref, g_ref, lam_ref, ng_ref, c_ref, u_ref, d_ref, o_ref,
                 kz_scr, qr_scr, a1_scr, vt_scr, z_scr, acc_scr):
    half = DIFF_QK_DIM // ROPE_FRACTION // 2
    scale = DIFF_QK_DIM ** -0.5 * LOG2E

    _transpose_values(v_ref, vt_scr)
    first = lax.broadcasted_iota(jnp.int32, (k_ref.shape[1], LANES), 1) < DIFF_QK_DIM
    for h in range(N_HEADS):
        kr = _rope(k_ref[h].astype(F32), c_ref[...], u_ref[...], d_ref[...], half)
        kz_scr[0, h] = jnp.where(first, kr, 0.0).astype(BF16)
        kz_scr[1, h] = jnp.where(first, 0.0, kr).astype(BF16)
        qr_scr[h] = _rope(q_ref[h].astype(F32), c_ref[...], u_ref[...], d_ref[...], half).astype(BF16)

    lf = lam_ref[...]
    lam = (jnp.exp(jnp.sum(lf[0:1] * lf[1:2], axis=-1, keepdims=True))
           - jnp.exp(jnp.sum(lf[2:3] * lf[3:4], axis=-1, keepdims=True)) + lam_init)

    def prepare(i):
        return [qr_scr[h, _query_rows(i), :] for h in range(N_HEADS)]

    def finish_first(i):
        for h in range(N_HEADS):
            a1_scr[h, _query_rows(i), :] = _softmax_result_t(acc_scr[h]).T

    def finish_second(i):
        for h in range(N_HEADS):
            o = a1_scr[h, _query_rows(i), :] - lam * _softmax_result_t(acc_scr[h]).T
            o = o * lax.rsqrt(jnp.mean(o * o, axis=-1, keepdims=True) + RMS_EPS) * ng_ref[...]
            o = o * (1.0 - lam_init)
            gate = _silu(g_ref[h, _query_rows(i), :].astype(F32))
            o_ref[h, _query_rows(i), :] = (o * gate).astype(BF16)

    for comp, finish in ((0, finish_first), (1, finish_second)):
        def logits(qs, h, j, comp=comp):
            return _nt_dot(kz_scr[comp, h, _key_rows(j), :], qs[h]) * scale

        _causal_tiles_t(q_ref.shape[1] // Q_TILE, prepare, logits, lambda h, j: vt_scr[h, j], finish,
                        z_scr, acc_scr)


def _diff(proj, diff_lam, diff_norm_g, tables, lam_init, *, batch, s_len):
    t = proj.shape[1]
    spec = _batch_spec(s_len)
    table_spec = pl.BlockSpec((s_len, LANES), lambda b: (0, 0))
    return pl.pallas_call(
        functools.partial(_diff_kernel, lam_init),
        grid=(batch,),
        in_specs=[
            spec(_SLAB["diff_q"]), spec(_SLAB["diff_k"]), spec(_SLAB["diff_v"]), spec(_SLAB["diff_g"]),
            pl.BlockSpec((4, DIFF_QK_DIM), lambda b: (0, 0)),
            pl.BlockSpec((1, HEAD_DIM), lambda b: (0, 0)),
            table_spec, table_spec, table_spec,
        ],
        out_specs=spec(0),
        out_shape=jax.ShapeDtypeStruct((N_HEADS, t, LANES), BF16),
        scratch_shapes=[pltpu.VMEM((2, N_HEADS, s_len, LANES), BF16),
                        pltpu.VMEM((N_HEADS, s_len, LANES), BF16),
                        pltpu.VMEM((N_HEADS, s_len, LANES), F32),
                        ] + _softmax_scratch(s_len),
        compiler_params=pltpu.CompilerParams(dimension_semantics=("arbitrary",)),
        name="diff",
    )(proj, proj, proj, proj, diff_lam, diff_norm_g.reshape(1, HEAD_DIM), *tables)


def _moba_kernel(q_ref, k_ref, v_ref, g_ref, c_ref, u_ref, d_ref,
                 o_ref, kr_scr, kmean_scr, vt_scr, z_scr, acc_scr):
    tq = Q_TILE
    half = HEAD_DIM // ROPE_FRACTION // 2
    scale = HEAD_DIM ** -0.5 * LOG2E
    n_blk = k_ref.shape[1] // MOBA_BLOCK

    _transpose_values(v_ref, vt_scr)
    kmean_scr[...] = jnp.zeros(kmean_scr.shape, F32)
    for h in range(N_HEADS):
        kr = _rope(k_ref[h].astype(F32), c_ref[...], u_ref[...], d_ref[...], half)
        kr_scr[h] = kr.astype(BF16)
        for n in range(n_blk):
            blk = kr[n * MOBA_BLOCK:(n + 1) * MOBA_BLOCK]
            kmean_scr[h, n:n + 1, :] = jnp.sum(blk, axis=0, keepdims=True) * (1.0 / MOBA_BLOCK)

    blk_id = lax.broadcasted_iota(jnp.int32, (GATE_ROWS, tq), 0).astype(F32)

    def prepare(i):
        rows = _query_rows(i)
        past = blk_id < lax.convert_element_type(i, F32)
        qs, biases = [], []
        for h in range(N_HEADS):
            q = _rope(q_ref[h, rows, :].astype(F32), c_ref[rows, :], u_ref[rows, :], d_ref[rows, :],
                      half).astype(BF16)
            gate = _nt_dot(kmean_scr[h].astype(BF16), q)
            gate = jnp.where(past, gate, -jnp.inf)
            open_ = past
            sel = jnp.zeros((GATE_ROWS, tq), jnp.bool_)
            for _ in range(MOBA_TOPK):
                top = jnp.max(gate, axis=0, keepdims=True)
                idx = jnp.min(jnp.where(gate == top, blk_id, float(GATE_ROWS)), axis=0, keepdims=True)
                pick = (blk_id == idx) & open_
                sel = sel | pick
                open_ = open_ & jnp.logical_not(pick)
                gate = jnp.where(pick, -jnp.inf, gate)
            qs.append(q)
            biases.append(jnp.where(sel, 0.0, MASKED_LOGIT))
        return i, qs, biases

    def logits(ctx, h, j):
        i, qs, biases = ctx
        here = blk_id == lax.convert_element_type(j, F32)
        bias = jnp.sum(jnp.where(here, biases[h], 0.0), axis=0, keepdims=True)
        bias = jnp.where(j == i, 0.0, bias)
        return _nt_dot(kr_scr[h, _key_rows(j), :], qs[h]) * scale + bias

    def finish(i):
        for h in range(N_HEADS):
            o = _softmax_result_t(acc_scr[h]).T
            gate = _silu(g_ref[h, _query_rows(i), :].astype(F32))
            o_ref[h, _query_rows(i), :] = (o * gate).astype(BF16)

    _causal_tiles_t(q_ref.shape[1] // tq, prepare, logits, lambda h, j: vt_scr[h, j], finish,
                    z_scr, acc_scr)


def _moba(proj, tables, *, batch, s_len):
    t = proj.shape[1]
    spec = _batch_spec(s_len)
    table_spec = pl.BlockSpec((s_len, LANES), lambda b: (0, 0))
    return pl.pallas_call(
        _moba_kernel,
        grid=(batch,),
        in_specs=[
            spec(_SLAB["moba_q"]), spec(_SLAB["moba_k"]), spec(_SLAB["moba_v"]), spec(_SLAB["moba_g"]),
            table_spec, table_spec, table_spec,
        ],
        out_specs=spec(0),
        out_shape=jax.ShapeDtypeStruct((N_HEADS, t, LANES), BF16),
        scratch_shapes=[pltpu.VMEM((N_HEADS, s_len, LANES), BF16),
                        pltpu.VMEM((N_HEADS, GATE_ROWS, LANES), F32),
                        ] + _softmax_scratch(s_len),
        compiler_params=pltpu.CompilerParams(dimension_semantics=("arbitrary",)),
        name="moba",
    )(proj, proj, proj, proj, *tables)


def _head_decay(p_ref, h, rows, qhalf):
    pair = p_ref[h // 2, rows, :]
    swapped = pltpu.roll(pair, GLA_DK, 1)
    return jnp.where(qhalf, pair, swapped) if h % 2 == 0 else jnp.where(qhalf, swapped, pair)


def _gla_factorised(x_ref, v_ref, g_ref, p_ref, ng_ref, o_ref):
    s_len = x_ref.shape[1]
    c_len = GLA_CHUNK
    scale = GLA_DK ** -0.5
    lane = lax.broadcasted_iota(jnp.int32, (c_len, LANES), 1)
    qhalf = lane < GLA_DK
    qhalf_row = lax.broadcasted_iota(jnp.int32, (1, LANES), 1) < GLA_DK
    row = lax.broadcasted_iota(jnp.int32, (c_len, c_len), 0)
    col = lax.broadcasted_iota(jnp.int32, (c_len, c_len), 1)
    causal = col <= row

    group = GLA_GROUP
    pairs = [(h, c) for h in range(N_HEADS) for c in range(group)]

    def body(g, carries):
        prep = {}
        for h, c in pairs:
            rows = pl.ds(pl.multiple_of(g * (group * c_len), group * c_len) + c * c_len, c_len)
            qcb = x_ref[h // 2, rows, :].astype(F32)
            kcb = x_ref[2 + h // 2, rows, :].astype(F32)
            if h % 2 == 0:
                x = jnp.where(qhalf, qcb, pltpu.roll(kcb, GLA_DK, 1))
            else:
                x = jnp.where(qhalf, pltpu.roll(qcb, GLA_DK, 1), kcb)
            p = _head_decay(p_ref, h, rows, qhalf)
            base = carries[h][1] if c == 0 else prep[h, c - 1]["p_last"]
            cum = p - base
            xt = x * jnp.exp(jnp.where(qhalf, cum, -cum))
            kr = pltpu.roll(xt, GLA_DK, 1)
            decay = jnp.where(qhalf_row, jnp.exp(cum[c_len - 1:c_len]), 0.0)
            prep[h, c] = dict(
                rows=rows, qz=jnp.where(qhalf, xt, 0.0).astype(BF16), kr=kr.astype(BF16),
                khat=(jnp.where(qhalf, kr, 0.0) * decay).astype(BF16),
                decay=decay, v=v_ref[h, rows, :], p_last=p[c_len - 1:c_len])
        scores, upd = {}, {}
        for h, c in pairs:
            d = prep[h, c]
            scores[h, c] = _nt_dot(d["qz"], d["kr"])
            upd[h, c] = lax.dot_general(d["v"], d["khat"], (((0,), (0,)), ((), ())),
                                        preferred_element_type=F32)
        states = {}
        for h in range(N_HEADS):
            states[h, 0] = carries[h][0]
            for c in range(group):
                states[h, c + 1] = states[h, c] * prep[h, c]["decay"] + upd[h, c]
        outs = {}
        for h, c in pairs:
            sc = jnp.where(causal, scores[h, c] * scale, 0.0).astype(BF16)
            outs[h, c] = (jnp.dot(sc, prep[h, c]["v"], preferred_element_type=F32),
                          _nt_dot(prep[h, c]["qz"], states[h, c].astype(BF16)))
        for h, c in pairs:
            o = outs[h, c][0] + outs[h, c][1] * scale
            y = o * lax.rsqrt(jnp.mean(o * o, axis=-1, keepdims=True) + RMS_EPS) * ng_ref[...]
            rows = prep[h, c]["rows"]
            o_ref[h, rows, :] = (y * _silu(g_ref[h, rows, :].astype(F32))).astype(BF16)
        return tuple((states[h, group], prep[h, group - 1]["p_last"]) for h in range(N_HEADS))

    init = tuple((jnp.zeros((HEAD_DIM, LANES), F32), jnp.zeros((1, LANES), F32))
                 for _ in range(N_HEADS))
    lax.fori_loop(0, s_len // (group * c_len), body, init)


def _gla_unfactorised(x_ref, v_ref, g_ref, p_ref, ng_ref, o_ref, k_scr, cum_scr):
    s_len = x_ref.shape[1]
    c_len = GLA_CHUNK
    scale = GLA_DK ** -0.5
    lane = lax.broadcasted_iota(jnp.int32, (c_len, LANES), 1)
    row = lax.broadcasted_iota(jnp.int32, (c_len, LANES), 0)
    qhalf = lane < GLA_DK
    qhalf_row = lax.broadcasted_iota(jnp.int32, (1, LANES), 1) < GLA_DK

    for h in range(N_HEADS):
        def chunk(c, carry, h=h):
            state_t, base = carry
            rows = pl.ds(pl.multiple_of(c * c_len, c_len), c_len)
            qcb = x_ref[h // 2, rows, :].astype(F32)
            kcb = x_ref[2 + h // 2, rows, :].astype(F32)
            if h % 2 == 0:
                q, k = qcb, kcb
            else:
                q, k = pltpu.roll(qcb, GLA_DK, 1), pltpu.roll(kcb, GLA_DK, 1)
            q = jnp.where(qhalf, q, 0.0)
            k = jnp.where(qhalf, k, 0.0)
            v = v_ref[h, rows, :]
            p = _head_decay(p_ref, h, rows, qhalf)
            cum = p - base
            last = cum[c_len - 1:c_len]
            k_scr[...] = k
            cum_scr[...] = cum

            def key(s, scores):
                ks = k_scr[pl.ds(s, 1), :]
                cs = cum_scr[pl.ds(s, 1), :]
                col = jnp.sum(q * ks * jnp.exp(jnp.minimum(cum - cs, 0.0)), axis=-1, keepdims=True)
                return jnp.where(lane == s, col, scores)

            scores = lax.fori_loop(0, c_len, key, jnp.zeros((c_len, LANES), F32))
            scores = jnp.where(lane <= row, scores * scale, 0.0)[:, :c_len].astype(BF16)
            o = jnp.dot(scores, v, preferred_element_type=F32)
            o = o + _nt_dot((q * jnp.exp(cum)).astype(BF16), state_t.astype(BF16)) * scale
            khat = (k * jnp.exp(last - cum)).astype(BF16)
            upd = lax.dot_general(v, khat, (((0,), (0,)), ((), ())), preferred_element_type=F32)
            state_t = state_t * jnp.where(qhalf_row, jnp.exp(last), 0.0) + upd
            y = o * lax.rsqrt(jnp.mean(o * o, axis=-1, keepdims=True) + RMS_EPS) * ng_ref[...]
            o_ref[h, rows, :] = (y * _silu(g_ref[h, rows, :].astype(F32))).astype(BF16)
            return state_t, p[c_len - 1:c_len]

        lax.fori_loop(0, s_len // c_len, chunk,
                      (jnp.zeros((HEAD_DIM, LANES), F32), jnp.zeros((1, LANES), F32)))


def _gla_kernel(x_ref, v_ref, g_ref, p_ref, ng_ref, o_ref, k_scr, cum_scr):
    n_chunks = x_ref.shape[1] // GLA_CHUNK
    worst = jnp.zeros((1, 1), F32)
    for pair in range(p_ref.shape[0]):
        ends = p_ref[pair, pl.ds(GLA_CHUNK - 1, n_chunks, stride=GLA_CHUNK), :]
        starts = jnp.concatenate([jnp.zeros((1, LANES), F32), ends[:-1]], axis=0)
        worst = jnp.maximum(worst, jnp.max(starts - ends, keepdims=True))
    risky = worst[0, 0] > GLA_SAFE_DECAY

    @pl.when(risky)
    def _():
        _gla_unfactorised(x_ref, v_ref, g_ref, p_ref, ng_ref, o_ref, k_scr, cum_scr)

    @pl.when(jnp.logical_not(risky))
    def _():
        _gla_factorised(x_ref, v_ref, g_ref, p_ref, ng_ref, o_ref)


def _gla(proj, p_cum, gla_norm_g, *, batch, s_len):
    t = proj.shape[1]
    spec = _batch_spec(s_len)
    return pl.pallas_call(
        _gla_kernel,
        grid=(batch,),
        in_specs=[
            spec(_SLAB["gla_qk"]), spec(_SLAB["gla_v"]), spec(_SLAB["gla_g"]),
            pl.BlockSpec((N_HEADS // 2, s_len, LANES), lambda b: (0, b, 0)),
            pl.BlockSpec((1, HEAD_DIM), lambda b: (0, 0)),
        ],
        out_specs=spec(0),
        out_shape=jax.ShapeDtypeStruct((N_HEADS, t, LANES), BF16),
        scratch_shapes=[pltpu.VMEM((GLA_CHUNK, LANES), F32), pltpu.VMEM((GLA_CHUNK, LANES), F32)],
        compiler_params=pltpu.CompilerParams(dimension_semantics=("arbitrary",)),
        name="gla",
    )(proj, proj, proj, p_cum, gla_norm_g.reshape(1, HEAD_DIM))


def _outproj_kernel(final, x_ref, a_ref, b_ref, c_ref, d_ref, w_ref, g_ref, *o_refs):
    parts = [r[h] for r in (a_ref, b_ref, c_ref, d_ref) for h in range(N_HEADS)]
    mixed = jnp.concatenate(parts, axis=-1)
    y = x_ref[...] + jnp.dot(mixed, w_ref[...], preferred_element_type=F32)
    normed = y * lax.rsqrt(jnp.mean(y * y, axis=-1, keepdims=True) + RMS_EPS) * g_ref[...]
    if final:
        o_refs[0][...] = normed
    else:
        o_refs[0][...] = y
        o_refs[1][...] = normed.astype(BF16)


def _outproj(x2, mixers, w_out, layer, norm_gain, *, final, tm):
    t, d = x2.shape
    mspec = pl.BlockSpec((N_HEADS, tm, LANES), lambda i: (0, i, 0))
    row_spec = pl.BlockSpec((tm, d), lambda i: (i, 0))
    out_specs, out_shape = [row_spec], [jax.ShapeDtypeStruct((t, d), F32)]
    if not final:
        out_specs.append(row_spec)
        out_shape.append(jax.ShapeDtypeStruct((t, d), BF16))
    return pl.pallas_call(
        functools.partial(_outproj_kernel, final),
        grid=(t // tm,),
        in_specs=[
            row_spec, mspec, mspec, mspec, mspec,
            pl.BlockSpec((None,) + w_out.shape[1:], lambda i: (layer, 0, 0)),
            pl.BlockSpec((1, d), lambda i: (0, 0)),
        ],
        out_specs=out_specs,
        out_shape=out_shape,
        compiler_params=pltpu.CompilerParams(
            dimension_semantics=("arbitrary",), vmem_limit_bytes=VMEM_LIMIT_BYTES),
        name="outproj",
    )(x2, *mixers, w_out, norm_gain.reshape(1, d))


RELAYOUT_ROWS = 256
RELAYOUT_TAIL = 32


def _relayout_kernel(a_ref, b_ref, o_ref):
    r = pl.program_id(0)
    depth, rows, _ = o_ref.shape
    for l in range(depth):
        x = jnp.concatenate([a_ref[:, l, :], b_ref[:, l, :]], axis=0)
        out_lo = 0
        for (src_lo, src_hi) in _MAIN_RUNS:
            shift = src_lo - out_lo
            out_hi = out_lo + (src_hi - src_lo)
            assert shift <= RELAYOUT_TAIL and out_lo % rows == 0 and out_hi % rows == 0

            @pl.when((r >= out_lo // rows) & (r < out_hi // rows))
            def _(shift=shift, l=l, x=x):
                o_ref[l] = x[shift:shift + rows].astype(BF16)

            out_lo = out_hi


def _relayout_w_in(w_in):
    depth, d, d_in = w_in.shape
    wt = jnp.transpose(w_in, (2, 0, 1))
    rows, tail = RELAYOUT_ROWS, RELAYOUT_TAIL
    main = pl.pallas_call(
        _relayout_kernel,
        grid=(N_MAIN // rows,),
        in_specs=[pl.BlockSpec((rows, depth, d), lambda r: (r, 0, 0)),
                  pl.BlockSpec((tail, depth, d), lambda r: ((r + 1) * (rows // tail), 0, 0))],
        out_specs=pl.BlockSpec((depth, rows, d), lambda r: (0, r, 0)),
        out_shape=jax.ShapeDtypeStruct((depth, N_MAIN, d), BF16),
        compiler_params=pltpu.CompilerParams(dimension_semantics=("arbitrary",)),
        name="relayout",
    )(wt, wt)
    f_off, a_off = _SEG_OFF["fox_f"][0], _SEG_OFF["gla_a"][0]
    gate_cols = jnp.concatenate([w_in[..., f_off:f_off + N_HEADS],
                                 w_in[..., a_off:a_off + GLA_RANK]], axis=-1)
    small = jnp.pad(jnp.transpose(gate_cols, (0, 2, 1)),
                    ((0, 0), (0, LANES - N_HEADS - GLA_RANK), (0, 0))).astype(BF16)
    return main, small


def _rope_tables(s_len, comp_dim, rot_dim):
    half = rot_dim // 2
    inv_freq = ROPE_THETA ** (-jnp.arange(0, rot_dim, 2, dtype=F32) / rot_dim)
    ang = jnp.arange(s_len, dtype=F32)[:, None] * inv_freq[None, :]
    cos, sin = jnp.cos(ang), jnp.sin(ang)
    zeros = jnp.zeros((s_len, comp_dim - rot_dim), F32)
    ones = jnp.ones((s_len, comp_dim - rot_dim), F32)
    reps = LANES // comp_dim
    cos_t = jnp.tile(jnp.concatenate([cos, cos, ones], axis=-1), (1, reps))
    sin_up = jnp.tile(jnp.concatenate([-sin, jnp.zeros_like(sin), zeros], axis=-1), (1, reps))
    sin_dn = jnp.tile(jnp.concatenate([jnp.zeros_like(sin), sin, zeros], axis=-1), (1, reps))
    return cos_t, sin_up, sin_dn


def kernel(x, norm_g, w_in, fox_fb, diff_lam, diff_norm_g, gla_wa2, gla_ba, gla_norm_g, w_out,
           final_norm_g):
    batch, s_len, d_model = x.shape
    depth = w_in.shape[0]
    assert s_len % Q_TILE == 0 and Q_TILE == K_TILE == MOBA_BLOCK
    assert s_len // MOBA_BLOCK <= GATE_ROWS
    t = batch * s_len
    tm_in = min(1024, t)
    tn_in = 1536
    tn_next = 2560
    tm_out = min(512, t)

    w_main, w_small = _relayout_w_in(w_in)
    w_out_b = w_out.astype(BF16)
    fb_pad = jnp.pad(fox_fb, ((0, 0), (0, LANES - N_HEADS)))[:, None, :]
    wa_pad = jnp.pad(gla_wa2, ((0, 0), (N_HEADS, LANES - N_HEADS - GLA_RANK), (0, 0))).astype(BF16)
    ba = gla_ba[:, None, :]
    rope_diff = _rope_tables(s_len, DIFF_QK_DIM, DIFF_QK_DIM // ROPE_FRACTION)
    rope_moba = _rope_tables(s_len, HEAD_DIM, HEAD_DIM // ROPE_FRACTION)

    x2 = x.reshape(t, d_model)
    h2 = None
    for l in range(depth):
        if h2 is None:
            proj, small = _inproj(x2, norm_g[l], w_main, w_small, l, tm=tm_in, tn=tn_in)
        else:
            proj, small = _inproj(h2, None, w_main, w_small, l, tm=tm_in, tn=tn_next)
        fk_cols, fq_rows, p_cum = _gates(small, fb_pad[l], wa_pad[l], ba[l], batch=batch, s_len=s_len)
        lam_init = 0.8 - 0.6 * math.exp(-0.3 * l)
        mixers = (
            _fox(proj, fk_cols, fq_rows, batch=batch, s_len=s_len),
            _diff(proj, diff_lam[l], diff_norm_g[l], rope_diff, lam_init, batch=batch, s_len=s_len),
            _moba(proj, rope_moba, batch=batch, s_len=s_len),
            _gla(proj, p_cum, gla_norm_g[l], batch=batch, s_len=s_len),
        )
        if l == depth - 1:
            (x2,) = _outproj(x2, mixers, w_out_b, l, final_norm_g, final=True, tm=tm_out)
        else:
            x2, h2 = _outproj(x2, mixers, w_out_b, l, norm_g[l + 1], final=False, tm=tm_out)
    return x2.reshape(batch, s_len, d_model)
```

```python
import functools
import math

import jax
import jax.numpy as jnp
from jax import lax
from jax.experimental import pallas as pl
from jax.experimental.pallas import tpu as pltpu

F32 = jnp.float32
BF16 = jnp.bfloat16

LANES = 128
VMEM_LIMIT_BYTES = 56 * 2 ** 20
HEAD_DIM = 128
N_HEADS = 4
GROUP_W = N_HEADS * HEAD_DIM
DIFF_QK_DIM = HEAD_DIM // 2
MOBA_BLOCK = 256
MOBA_TOPK = 3
GLA_DK = HEAD_DIM // 2
GLA_RANK = 16
GLA_TAU = 16.0
GLA_CHUNK = 64
ROPE_THETA = 500000.0
ROPE_FRACTION = 4
RMS_EPS = 1e-6
MASKED_LOGIT = -1e30

_SEGMENTS = (
    ("fox_q", GROUP_W), ("fox_k", GROUP_W), ("fox_v", GROUP_W),
    ("fox_f", N_HEADS), ("fox_g", GROUP_W),
    ("diff_q", GROUP_W), ("diff_k", GROUP_W), ("diff_v", GROUP_W), ("diff_g", GROUP_W),
    ("moba_q", GROUP_W), ("moba_k", GROUP_W), ("moba_v", GROUP_W), ("moba_g", GROUP_W),
    ("gla_q", N_HEADS * GLA_DK), ("gla_k", N_HEADS * GLA_DK), ("gla_v", GROUP_W),
    ("gla_a", GLA_RANK), ("gla_g", GROUP_W),
)
_SEG_OFF = {}
_off = 0
for _name, _w in _SEGMENTS:
    _SEG_OFF[_name] = (_off, _w)
    _off += _w

_SLABS = ("fox_q", "fox_k", "fox_v", "fox_g", "diff_q", "diff_k", "diff_v", "diff_g",
          "moba_q", "moba_k", "moba_v", "moba_g", "gla_qk", "gla_v", "gla_g")
_SLAB = {name: i for i, name in enumerate(_SLABS)}
N_MAIN = len(_SLABS) * GROUP_W
_MAIN_RUNS = ((0, _SEG_OFF["fox_f"][0]),
              (_SEG_OFF["fox_g"][0], _SEG_OFF["gla_a"][0]),
              (_SEG_OFF["gla_g"][0], _off))
assert sum(b - a for a, b in _MAIN_RUNS) == N_MAIN

Q_TILE = 256
K_TILE = 256
CUM_BLOCK = 128
GATE_ROWS = 16
GLA_GROUP = 4
GLA_SAFE_DECAY = 60.0
ONES_ROWS = 16
LOG2E = math.log2(math.e)


def _silu(x):
    return x * (1.0 / (1.0 + jnp.exp(-x)))


def _log_sigmoid(x):
    return jnp.minimum(x, 0.0) - jnp.log1p(jnp.exp(-jnp.abs(x)))


def _nt_dot(a, b):
    return lax.dot_general(a, b, (((1,), (1,)), ((), ())), preferred_element_type=F32)


def _inproj_tile(h, w_ref, o_ref):
    acc = _nt_dot(h, w_ref[...])
    for c in range(o_ref.shape[0]):
        o_ref[c] = acc[:, c * LANES:(c + 1) * LANES].astype(BF16)


def _inproj_norm_kernel(cast_blocks, x_ref, g_ref, w_ref, ws_ref, wo_ref, o_ref, small_ref, wob_ref,
                        h_scr):
    step = pl.program_id(0) * pl.num_programs(1) + pl.program_id(1)

    @pl.when(step < cast_blocks)
    def _():
        wob_ref[...] = wo_ref[...].astype(BF16)

    @pl.when(pl.program_id(1) == 0)
    def _():
        x = x_ref[...]
        y = x * lax.rsqrt(jnp.mean(x * x, axis=-1, keepdims=True) + RMS_EPS) * g_ref[...]
        h = y.astype(BF16)
        h_scr[...] = h
        small_ref[...] = _nt_dot(h, ws_ref[...])

    _inproj_tile(h_scr[...], w_ref, o_ref)


def _inproj_kernel(h_ref, w_ref, ws_ref, o_ref, small_ref):
    @pl.when(pl.program_id(1) == 0)
    def _():
        small_ref[...] = _nt_dot(h_ref[...], ws_ref[...])

    _inproj_tile(h_ref[...], w_ref, o_ref)


def _inproj(rows, norm_g, w_main, w_small, layer, w_out=None, *, tm, tn):
    t, d = rows.shape
    n_main = w_main.shape[1]
    with_norm = norm_g is not None
    grid = (t // tm, n_main // tn)
    row_spec = pl.BlockSpec((tm, d), lambda i, j: (i, 0))
    weight_specs = [pl.BlockSpec((None, tn, d), lambda i, j: (layer, j, 0)),
                    pl.BlockSpec((None, LANES, d), lambda i, j: (layer, 0, 0))]
    out_specs = [pl.BlockSpec((tn // LANES, tm, LANES), lambda i, j: (j, i, 0)),
                 pl.BlockSpec((tm, LANES), lambda i, j: (i, 0))]
    out_shape = [jax.ShapeDtypeStruct((n_main // LANES, t, LANES), BF16),
                 jax.ShapeDtypeStruct((t, LANES), F32)]
    if with_norm:
        wo_rows = w_out.reshape(-1, w_out.shape[-1])
        cast_blocks = 1 << ((grid[0] * grid[1]).bit_length() - 1)
        cast_rows = wo_rows.shape[0] // cast_blocks
        assert cast_rows * cast_blocks == wo_rows.shape[0] and cast_rows % 16 == 0
        cast_spec = pl.BlockSpec(
            (cast_rows, wo_rows.shape[1]),
            lambda i, j: (jnp.minimum(i * grid[1] + j, cast_blocks - 1), 0))
        body = functools.partial(_inproj_norm_kernel, cast_blocks)
        in_specs = [row_spec, pl.BlockSpec((1, d), lambda i, j: (0, 0))] + weight_specs + [cast_spec]
        args = (rows, norm_g.reshape(1, d), w_main, w_small, wo_rows)
        out_specs, out_shape = out_specs + [cast_spec], out_shape + [
            jax.ShapeDtypeStruct(wo_rows.shape, BF16)]
        scratch = [pltpu.VMEM((tm, d), BF16)]
    else:
        body, in_specs, scratch = _inproj_kernel, [row_spec] + weight_specs, []
        args = (rows, w_main, w_small)
    return pl.pallas_call(
        body,
        grid=grid,
        in_specs=in_specs,
        out_specs=out_specs,
        out_shape=out_shape,
        scratch_shapes=scratch,
        compiler_params=pltpu.CompilerParams(
            dimension_semantics=("arbitrary", "arbitrary"), vmem_limit_bytes=VMEM_LIMIT_BYTES),
        name="inproj",
    )(*args)


def _gates_kernel(small_ref, fb_ref, wa_ref, ba_ref, fk_ref, fq_ref, p_ref):
    s_len = small_ref.shape[0]
    small = small_ref[...]
    log_f = _log_sigmoid(small + fb_ref[...])
    a_logit = jnp.dot(small.astype(BF16), wa_ref[...], preferred_element_type=F32) + ba_ref[...]
    log_a = _log_sigmoid(a_logit) * (1.0 / GLA_TAU)
    both = jnp.concatenate([log_f, log_a], axis=-1)

    row = lax.broadcasted_iota(jnp.int32, (CUM_BLOCK, CUM_BLOCK), 0)
    col = lax.broadcasted_iota(jnp.int32, (CUM_BLOCK, CUM_BLOCK), 1)
    tri = (col <= row).astype(BF16)
    carry = jnp.zeros((1, both.shape[1]), F32)
    for c in range(s_len // CUM_BLOCK):
        blk = both[c * CUM_BLOCK:(c + 1) * CUM_BLOCK]
        hi = blk.astype(BF16)
        rest = blk - hi.astype(F32)
        mid = rest.astype(BF16)
        lo = (rest - mid.astype(F32)).astype(BF16)
        cum = carry + sum(jnp.dot(tri, part, preferred_element_type=F32) for part in (hi, mid, lo))
        carry = cum[CUM_BLOCK - 1:CUM_BLOCK]
        rows = slice(c * CUM_BLOCK, (c + 1) * CUM_BLOCK)
        f_cum = cum[:, :LANES] * LOG2E
        per_tile = Q_TILE // CUM_BLOCK
        fq_ref[c // per_tile, :, (c % per_tile) * CUM_BLOCK:(c % per_tile + 1) * CUM_BLOCK] = (
            f_cum.T[:GATE_ROWS])
        fk_ref[rows, :] = f_cum
        for pair in range(N_HEADS // 2):
            p_ref[pair, rows, :] = cum[:, (1 + pair) * LANES:(2 + pair) * LANES]


def _gates(small, fb_pad, wa_pad, ba, *, batch, s_len):
    t = small.shape[0]
    return pl.pallas_call(
        _gates_kernel,
        grid=(batch,),
        in_specs=[
            pl.BlockSpec((s_len, LANES), lambda b: (b, 0)),
            pl.BlockSpec((1, LANES), lambda b: (0, 0)),
            pl.BlockSpec((LANES, N_HEADS * GLA_DK), lambda b: (0, 0)),
            pl.BlockSpec((1, N_HEADS * GLA_DK), lambda b: (0, 0)),
        ],
        out_specs=[
            pl.BlockSpec((s_len, LANES), lambda b: (b, 0)),
            pl.BlockSpec((None, s_len // Q_TILE, GATE_ROWS, Q_TILE), lambda b: (b, 0, 0, 0)),
            pl.BlockSpec((N_HEADS // 2, s_len, LANES), lambda b: (0, b, 0)),
        ],
        out_shape=[
            jax.ShapeDtypeStruct((t, LANES), F32),
            jax.ShapeDtypeStruct((batch, s_len // Q_TILE, GATE_ROWS, Q_TILE), F32),
            jax.ShapeDtypeStruct((N_HEADS // 2, t, LANES), F32),
        ],
        compiler_params=pltpu.CompilerParams(dimension_semantics=("arbitrary",)),
        name="gates",
    )(small, fb_pad, wa_pad, ba)


def _causal_mask_t(tk, tq):
    key = lax.broadcasted_iota(jnp.int32, (tk, tq), 0)
    qry = lax.broadcasted_iota(jnp.int32, (tk, tq), 1)
    return key <= qry


def _transpose_values(v_ref, vt_scr):
    dv = v_ref.shape[2]
    for h in range(v_ref.shape[0]):
        for j in range(v_ref.shape[1] // K_TILE):
            blk = v_ref[h, j * K_TILE:(j + 1) * K_TILE, :].astype(F32)
            vt_scr[h, j, :dv, :] = blk.T.astype(BF16)
            vt_scr[h, j, dv:, :] = jnp.ones((ONES_ROWS, K_TILE), BF16)


def _col_max(z):
    rows = z.shape[0]
    while rows > 8:
        rows //= 2
        z = jnp.maximum(z[:rows], z[rows:])
    return jnp.max(z, axis=0, keepdims=True)


def _causal_tiles_t(n_tiles, prepare_fn, logits_fn, vt_fn, finish_fn, z_scr, acc_scr):
    n, _, tq = acc_scr.shape

    def update(j, slot, ms, tile_max, masked):
        stats = []
        for s in range(n):
            z = z_scr[slot, s]
            if masked:
                z = jnp.where(_causal_mask_t(K_TILE, tq), z, -jnp.inf)
                m_new = jnp.maximum(ms[s], _col_max(z))
            else:
                m_new = jnp.maximum(ms[s], tile_max[s])
            stats.append((m_new, jnp.exp2(ms[s] - m_new), jnp.exp2(z - m_new).astype(BF16)))
        for s in range(n):
            pv = jnp.dot(vt_fn(s, j), stats[s][2], preferred_element_type=F32)
            acc_scr[s] = stats[s][1] * acc_scr[s] + pv
        return tuple(st[0] for st in stats)

    def first_logits(i):
        ctx = prepare_fn(i)
        return ctx, [logits_fn(ctx, s, jnp.int32(0)) for s in range(n)]

    def open_tile(zs):
        for s in range(n):
            z_scr[0, s] = zs[s]
            acc_scr[s] = jnp.zeros(acc_scr.shape[1:], F32)
        return (tuple(jnp.full((1, tq), -jnp.inf, F32) for _ in range(n)),
                tuple(_col_max(z) for z in zs))

    def close_tile(i, ms):
        update(i, lax.rem(i, 2), ms, None, True)
        finish_fn(i)

    def off_diagonal(i, ctx, state):
        def body(j, state):
            ms, tile_max = state
            slot = lax.rem(j, 2)
            z_next = [logits_fn(ctx, s, j + 1) for s in range(n)]
            ms = update(j, slot, ms, tile_max, False)
            for s in range(n):
                z_scr[1 - slot, s] = z_next[s]
            return ms, tuple(_col_max(z) for z in z_next)

        return lax.fori_loop(0, i, body, state)[0]

    _, zs = first_logits(jnp.int32(0))
    ms, _ = open_tile(zs)

    def outer(i, ms_prev):
        ctx, zs = first_logits(i)
        close_tile(i - 1, ms_prev)
        return off_diagonal(i, ctx, open_tile(zs))

    ms = lax.fori_loop(1, n_tiles, outer, ms)
    close_tile(jnp.int32(n_tiles - 1), ms)


def _softmax_result_t(acc):
    dv = acc.shape[0] - ONES_ROWS
    return acc[:dv] / acc[dv:dv + 1]


def _rope(x, cos_t, sin_up, sin_dn, half):
    return (x * cos_t + pltpu.roll(x, LANES - half, 1) * sin_up
            + pltpu.roll(x, half, 1) * sin_dn)


def _key_rows(j):
    return pl.ds(pl.multiple_of(j * K_TILE, K_TILE), K_TILE)


def _query_rows(i):
    return pl.ds(pl.multiple_of(i * Q_TILE, Q_TILE), Q_TILE)


def _batch_spec(s_len):
    return lambda slab: pl.BlockSpec((N_HEADS, s_len, LANES), lambda b: (slab, b, 0))


def _softmax_scratch(s_len):
    return [pltpu.VMEM((N_HEADS, s_len // K_TILE, HEAD_DIM + ONES_ROWS, K_TILE), BF16),
            pltpu.VMEM((2, N_HEADS, K_TILE, Q_TILE), F32),
            pltpu.VMEM((N_HEADS, HEAD_DIM + ONES_ROWS, Q_TILE), F32)]


def _fox_kernel(q_ref, k_ref, v_ref, g_ref, fc_ref, fq_ref, o_ref, fk_ref, vt_scr, z_scr, acc_scr):
    scale = HEAD_DIM ** -0.5 * LOG2E
    _transpose_values(v_ref, vt_scr)
    for c in range(fc_ref.shape[0] // CUM_BLOCK):
        rows = slice(c * CUM_BLOCK, (c + 1) * CUM_BLOCK)
        blk = fc_ref[rows, :]
        for h in range(N_HEADS):
            fk_ref[h, rows, :] = jnp.broadcast_to(blk[:, h:h + 1], (CUM_BLOCK, LANES))

    def prepare(i):
        fq_all = fq_ref[i]
        return ([q_ref[h, _query_rows(i), :] for h in range(N_HEADS)],
                [fq_all[h:h + 1, :] for h in range(N_HEADS)])

    def logits(ctx, h, j):
        qs, fqs = ctx
        fk = fk_ref[h, _key_rows(j), :]
        fk = jnp.concatenate([fk] * (Q_TILE // LANES), axis=-1)
        return _nt_dot(k_ref[h, _key_rows(j), :], qs[h]) * scale + (fqs[h] - fk)

    def finish(i):
        for h in range(N_HEADS):
            o = _softmax_result_t(acc_scr[h]).T
            gate = _silu(g_ref[h, _query_rows(i), :].astype(F32))
            o_ref[h, _query_rows(i), :] = (o * gate).astype(BF16)

    _causal_tiles_t(q_ref.shape[1] // Q_TILE, prepare, logits, lambda h, j: vt_scr[h, j], finish,
                    z_scr, acc_scr)


def _fox(proj, fk_cols, fq_rows, *, batch, s_len):
    t = proj.shape[1]
    spec = _batch_spec(s_len)
    return pl.pallas_call(
        _fox_kernel,
        grid=(batch,),
        in_specs=[
            spec(_SLAB["fox_q"]), spec(_SLAB["fox_k"]), spec(_SLAB["fox_v"]), spec(_SLAB["fox_g"]),
            pl.BlockSpec((s_len, LANES), lambda b: (b, 0)),
            pl.BlockSpec((None, s_len // Q_TILE, GATE_ROWS, Q_TILE), lambda b: (b, 0, 0, 0)),
        ],
        out_specs=spec(0),
        out_shape=jax.ShapeDtypeStruct((N_HEADS, t, LANES), BF16),
        scratch_shapes=[pltpu.VMEM((N_HEADS, s_len, LANES), F32)] + _softmax_scratch(s_len),
        compiler_params=pltpu.CompilerParams(dimension_semantics=("arbitrary",)),
        name="fox",
    )(proj, proj, proj, proj, fk_cols, fq_rows)


def _diff_kernel(lam_init, q_ref, k_ref, v_ref, g_ref, lam_ref, ng_ref, c_ref, u_ref, d_ref, o_ref,
                 kz_scr, qr_scr, a1_scr, vt_scr, z_scr, acc_scr):
    half = DIFF_QK_DIM // ROPE_FRACTION // 2
    scale = DIFF_QK_DIM ** -0.5 * LOG2E

    _transpose_values(v_ref, vt_scr)
    first = lax.broadcasted_iota(jnp.int32, (k_ref.shape[1], LANES), 1) < DIFF_QK_DIM
    for h in range(N_HEADS):
        kr = _rope(k_ref[h].astype(F32), c_ref[...], u_ref[...], d_ref[...], half)
        kz_scr[0, h] = jnp.where(first, kr, 0.0).astype(BF16)
        kz_scr[1, h] = jnp.where(first, 0.0, kr).astype(BF16)
        qr_scr[h] = _rope(q_ref[h].astype(F32), c_ref[...], u_ref[...], d_ref[...], half).astype(BF16)

    lf = lam_ref[...]
    lam = (jnp.exp(jnp.sum(lf[0:1] * lf[1:2], axis=-1, keepdims=True))
           - jnp.exp(jnp.sum(lf[2:3] * lf[3:4], axis=-1, keepdims=True)) + lam_init)

    def prepare(i):
        return [qr_scr[h, _query_rows(i), :] for h in range(N_HEADS)]

    def finish_first(i):
        for h in range(N_HEADS):
            a1_scr[h, _query_rows(i), :] = _softmax_result_t(acc_scr[h]).T

    def finish_second(i):
        for h in range(N_HEADS):
            o = a1_scr[h, _query_rows(i), :] - lam * _softmax_result_t(acc_scr[h]).T
            o = o * lax.rsqrt(jnp.mean(o * o, axis=-1, keepdims=True) + RMS_EPS) * ng_ref[...]
            o = o * (1.0 - lam_init)
            gate = _silu(g_ref[h, _query_rows(i), :].astype(F32))
            o_ref[h, _query_rows(i), :] = (o * gate).astype(BF16)

    for comp, finish in ((0, finish_first), (1, finish_second)):
        def logits(qs, h, j, comp=comp):
            return _nt_dot(kz_scr[comp, h, _key_rows(j), :], qs[h]) * scale

        _causal_tiles_t(q_ref.shape[1] // Q_TILE, prepare, logits, lambda h, j: vt_scr[h, j], finish,
                        z_scr, acc_scr)


def _diff(proj, diff_lam, diff_norm_g, tables, lam_init, *, batch, s_len):
    t = proj.shape[1]
    spec = _batch_spec(s_len)
    table_spec = pl.BlockSpec((s_len, LANES), lambda b: (0, 0))
    return pl.pallas_call(
        functools.partial(_diff_kernel, lam_init),
        grid=(batch,),
        in_specs=[
            spec(_SLAB["diff_q"]), spec(_SLAB["diff_k"]), spec(_SLAB["diff_v"]), spec(_SLAB["diff_g"]),
            pl.BlockSpec((4, DIFF_QK_DIM), lambda b: (0, 0)),
            pl.BlockSpec((1, HEAD_DIM), lambda b: (0, 0)),
            table_spec, table_spec, table_spec,
        ],
        out_specs=spec(0),
        out_shape=jax.ShapeDtypeStruct((N_HEADS, t, LANES), BF16),
        scratch_shapes=[pltpu.VMEM((2, N_HEADS, s_len, LANES), BF16),
                        pltpu.VMEM((N_HEADS, s_len, LANES), BF16),
                        pltpu.VMEM((N_HEADS, s_len, LANES), F32),
                        ] + _softmax_scratch(s_len),
        compiler_params=pltpu.CompilerParams(dimension_semantics=("arbitrary",)),
        name="diff",
    )(proj, proj, proj, proj, diff_lam, diff_norm_g.reshape(1, HEAD_DIM), *tables)


def _moba_kernel(q_ref, k_ref, v_ref, g_ref, c_ref, u_ref, d_ref,
                 o_ref, kr_scr, kmean_scr, vt_scr, z_scr, acc_scr):
    tq = Q_TILE
    half = HEAD_DIM // ROPE_FRACTION // 2
    scale = HEAD_DIM ** -0.5 * LOG2E
    n_blk = k_ref.shape[1] // MOBA_BLOCK

    _transpose_values(v_ref, vt_scr)
    kmean_scr[...] = jnp.zeros(kmean_scr.shape, F32)
    for h in range(N_HEADS):
        kr = _rope(k_ref[h].astype(F32), c_ref[...], u_ref[...], d_ref[...], half)
        kr_scr[h] = kr.astype(BF16)
        for n in range(n_blk):
            blk = kr[n * MOBA_BLOCK:(n + 1) * MOBA_BLOCK]
            kmean_scr[h, n:n + 1, :] = jnp.sum(blk, axis=0, keepdims=True) * (1.0 / MOBA_BLOCK)

    blk_id = lax.broadcasted_iota(jnp.int32, (GATE_ROWS, tq), 0).astype(F32)

    def prepare(i):
        rows = _query_rows(i)
        past = blk_id < lax.convert_element_type(i, F32)
        qs, biases = [], []
        for h in range(N_HEADS):
            q = _rope(q_ref[h, rows, :].astype(F32), c_ref[rows, :], u_ref[rows, :], d_ref[rows, :],
                      half).astype(BF16)
            gate = _nt_dot(kmean_scr[h].astype(BF16), q)
            gate = jnp.where(past, gate, -jnp.inf)
            open_ = past
            sel = jnp.zeros((GATE_ROWS, tq), jnp.bool_)
            for _ in range(MOBA_TOPK):
                top = jnp.max(gate, axis=0, keepdims=True)
                idx = jnp.min(jnp.where(gate == top, blk_id, float(GATE_ROWS)), axis=0, keepdims=True)
                pick = (blk_id == idx) & open_
                sel = sel | pick
                open_ = open_ & jnp.logical_not(pick)
                gate = jnp.where(pick, -jnp.inf, gate)
            qs.append(q)
            biases.append(jnp.where(sel, 0.0, MASKED_LOGIT))
        return i, qs, biases

    def logits(ctx, h, j):
        i, qs, biases = ctx
        here = blk_id == lax.convert_element_type(j, F32)
        bias = jnp.sum(jnp.where(here, biases[h], 0.0), axis=0, keepdims=True)
        bias = jnp.where(j == i, 0.0, bias)
        return _nt_dot(kr_scr[h, _key_rows(j), :], qs[h]) * scale + bias

    def finish(i):
        for h in range(N_HEADS):
            o = _softmax_result_t(acc_scr[h]).T
            gate = _silu(g_ref[h, _query_rows(i), :].astype(F32))
            o_ref[h, _query_rows(i), :] = (o * gate).astype(BF16)

    _causal_tiles_t(q_ref.shape[1] // tq, prepare, logits, lambda h, j: vt_scr[h, j], finish,
                    z_scr, acc_scr)


def _moba(proj, tables, *, batch, s_len):
    t = proj.shape[1]
    spec = _batch_spec(s_len)
    table_spec = pl.BlockSpec((s_len, LANES), lambda b: (0, 0))
    return pl.pallas_call(
        _moba_kernel,
        grid=(batch,),
        in_specs=[
            spec(_SLAB["moba_q"]), spec(_SLAB["moba_k"]), spec(_SLAB["moba_v"]), spec(_SLAB["moba_g"]),
            table_spec, table_spec, table_spec,
        ],
        out_specs=spec(0),
        out_shape=jax.ShapeDtypeStruct((N_HEADS, t, LANES), BF16),
        scratch_shapes=[pltpu.VMEM((N_HEADS, s_len, LANES), BF16),
                        pltpu.VMEM((N_HEADS, GATE_ROWS, LANES), F32),
                        ] + _softmax_scratch(s_len),
        compiler_params=pltpu.CompilerParams(dimension_semantics=("arbitrary",)),
        name="moba",
    )(proj, proj, proj, proj, *tables)


def _head_decay(p_ref, h, rows, qhalf):
    pair = p_ref[h // 2, rows, :]
    swapped = pltpu.roll(pair, GLA_DK, 1)
    return jnp.where(qhalf, pair, swapped) if h % 2 == 0 else jnp.where(qhalf, swapped, pair)


def _gla_factorised(x_ref, v_ref, g_ref, p_ref, ng_ref, o_ref):
    s_len = x_ref.shape[1]
    c_len = GLA_CHUNK
    scale = GLA_DK ** -0.5
    lane = lax.broadcasted_iota(jnp.int32, (c_len, LANES), 1)
    qhalf = lane < GLA_DK
    qhalf_row = lax.broadcasted_iota(jnp.int32, (1, LANES), 1) < GLA_DK
    row = lax.broadcasted_iota(jnp.int32, (c_len, c_len), 0)
    col = lax.broadcasted_iota(jnp.int32, (c_len, c_len), 1)
    causal = col <= row

    group = GLA_GROUP
    pairs = [(h, c) for h in range(N_HEADS) for c in range(group)]

    def body(g, carries):
        prep = {}
        for h, c in pairs:
            rows = pl.ds(pl.multiple_of(g * (group * c_len), group * c_len) + c * c_len, c_len)
            qcb = x_ref[h // 2, rows, :].astype(F32)
            kcb = x_ref[2 + h // 2, rows, :].astype(F32)
            if h % 2 == 0:
                x = jnp.where(qhalf, qcb, pltpu.roll(kcb, GLA_DK, 1))
            else:
                x = jnp.where(qhalf, pltpu.roll(qcb, GLA_DK, 1), kcb)
            p = _head_decay(p_ref, h, rows, qhalf)
            base = carries[h][1] if c == 0 else prep[h, c - 1]["p_last"]
            cum = p - base
            xt = x * jnp.exp(jnp.where(qhalf, cum, -cum))
            kr = pltpu.roll(xt, GLA_DK, 1)
            decay = jnp.where(qhalf_row, jnp.exp(cum[c_len - 1:c_len]), 0.0)
            prep[h, c] = dict(
                rows=rows, qz=jnp.where(qhalf, xt, 0.0).astype(BF16), kr=kr.astype(BF16),
                khat=(jnp.where(qhalf, kr, 0.0) * decay).astype(BF16),
                decay=decay, v=v_ref[h, rows, :], p_last=p[c_len - 1:c_len])
        scores, upd = {}, {}
        for h, c in pairs:
            d = prep[h, c]
            scores[h, c] = _nt_dot(d["qz"], d["kr"])
            upd[h, c] = lax.dot_general(d["v"], d["khat"], (((0,), (0,)), ((), ())),
                                        preferred_element_type=F32)
        states = {}
        for h in range(N_HEADS):
            states[h, 0] = carries[h][0]
            for c in range(group):
                states[h, c + 1] = states[h, c] * prep[h, c]["decay"] + upd[h, c]
        outs = {}
        for h, c in pairs:
            sc = jnp.where(causal, scores[h, c] * scale, 0.0).astype(BF16)
            outs[h, c] = (jnp.dot(sc, prep[h, c]["v"], preferred_element_type=F32),
                          _nt_dot(prep[h, c]["qz"], states[h, c].astype(BF16)))
        for h, c in pairs:
            o = outs[h, c][0] + outs[h, c][1] * scale
            y = o * lax.rsqrt(jnp.mean(o * o, axis=-1, keepdims=True) + RMS_EPS) * ng_ref[...]
            rows = prep[h, c]["rows"]
            o_ref[h, rows, :] = (y * _silu(g_ref[h, rows, :].astype(F32))).astype(BF16)
        return tuple((states[h, group], prep[h, group - 1]["p_last"]) for h in range(N_HEADS))

    init = tuple((jnp.zeros((HEAD_DIM, LANES), F32), jnp.zeros((1, LANES), F32))
                 for _ in range(N_HEADS))
    lax.fori_loop(0, s_len // (group * c_len), body, init)


def _gla_unfactorised(x_ref, v_ref, g_ref, p_ref, ng_ref, o_ref, k_scr, cum_scr):
    s_len = x_ref.shape[1]
    c_len = GLA_CHUNK
    scale = GLA_DK ** -0.5
    lane = lax.broadcasted_iota(jnp.int32, (c_len, LANES), 1)
    row = lax.broadcasted_iota(jnp.int32, (c_len, LANES), 0)
    qhalf = lane < GLA_DK
    qhalf_row = lax.broadcasted_iota(jnp.int32, (1, LANES), 1) < GLA_DK

    for h in range(N_HEADS):
        def chunk(c, carry, h=h):
            state_t, base = carry
            rows = pl.ds(pl.multiple_of(c * c_len, c_len), c_len)
            qcb = x_ref[h // 2, rows, :].astype(F32)
            kcb = x_ref[2 + h // 2, rows, :].astype(F32)
            if h % 2 == 0:
                q, k = qcb, kcb
            else:
                q, k = pltpu.roll(qcb, GLA_DK, 1), pltpu.roll(kcb, GLA_DK, 1)
            q = jnp.where(qhalf, q, 0.0)
            k = jnp.where(qhalf, k, 0.0)
            v = v_ref[h, rows, :]
            p = _head_decay(p_ref, h, rows, qhalf)
            cum = p - base
            last = cum[c_len - 1:c_len]
            k_scr[...] = k
            cum_scr[...] = cum

            def key(s, scores):
                ks = k_scr[pl.ds(s, 1), :]
                cs = cum_scr[pl.ds(s, 1), :]
                col = jnp.sum(q * ks * jnp.exp(jnp.minimum(cum - cs, 0.0)), axis=-1, keepdims=True)
                return jnp.where(lane == s, col, scores)

            scores = lax.fori_loop(0, c_len, key, jnp.zeros((c_len, LANES), F32))
            scores = jnp.where(lane <= row, scores * scale, 0.0)[:, :c_len].astype(BF16)
            o = jnp.dot(scores, v, preferred_element_type=F32)
            o = o + _nt_dot((q * jnp.exp(cum)).astype(BF16), state_t.astype(BF16)) * scale
            khat = (k * jnp.exp(last - cum)).astype(BF16)
            upd = lax.dot_general(v, khat, (((0,), (0,)), ((), ())), preferred_element_type=F32)
            state_t = state_t * jnp.where(qhalf_row, jnp.exp(last), 0.0) + upd
            y = o * lax.rsqrt(jnp.mean(o * o, axis=-1, keepdims=True) + RMS_EPS) * ng_ref[...]
            o_ref[h, rows, :] = (y * _silu(g_ref[h, rows, :].astype(F32))).astype(BF16)
            return state_t, p[c_len - 1:c_len]

        lax.fori_loop(0, s_len // c_len, chunk,
                      (jnp.zeros((HEAD_DIM, LANES), F32), jnp.zeros((1, LANES), F32)))


def _gla_kernel(x_ref, v_ref, g_ref, p_ref, ng_ref, o_ref, k_scr, cum_scr):
    n_chunks = x_ref.shape[1] // GLA_CHUNK
    worst = jnp.zeros((1, 1), F32)
    for pair in range(p_ref.shape[0]):
        ends = p_ref[pair, pl.ds(GLA_CHUNK - 1, n_chunks, stride=GLA_CHUNK), :]
        starts = jnp.concatenate([jnp.zeros((1, LANES), F32), ends[:-1]], axis=0)
        worst = jnp.maximum(worst, jnp.max(starts - ends, keepdims=True))
    risky = worst[0, 0] > GLA_SAFE_DECAY

    @pl.when(risky)
    def _():
        _gla_unfactorised(x_ref, v_ref, g_ref, p_ref, ng_ref, o_ref, k_scr, cum_scr)

    @pl.when(jnp.logical_not(risky))
    def _():
        _gla_factorised(x_ref, v_ref, g_ref, p_ref, ng_ref, o_ref)


def _gla(proj, p_cum, gla_norm_g, *, batch, s_len):
    t = proj.shape[1]
    spec = _batch_spec(s_len)
    return pl.pallas_call(
        _gla_kernel,
        grid=(batch,),
        in_specs=[
            spec(_SLAB["gla_qk"]), spec(_SLAB["gla_v"]), spec(_SLAB["gla_g"]),
            pl.BlockSpec((N_HEADS // 2, s_len, LANES), lambda b: (0, b, 0)),
            pl.BlockSpec((1, HEAD_DIM), lambda b: (0, 0)),
        ],
        out_specs=spec(0),
        out_shape=jax.ShapeDtypeStruct((N_HEADS, t, LANES), BF16),
        scratch_shapes=[pltpu.VMEM((GLA_CHUNK, LANES), F32), pltpu.VMEM((GLA_CHUNK, LANES), F32)],
        compiler_params=pltpu.CompilerParams(dimension_semantics=("arbitrary",)),
        name="gla",
    )(proj, proj, proj, p_cum, gla_norm_g.reshape(1, HEAD_DIM))


def _outproj_kernel(final, x_ref, a_ref, b_ref, c_ref, d_ref, w_ref, g_ref, *o_refs):
    parts = [r[h] for r in (a_ref, b_ref, c_ref, d_ref) for h in range(N_HEADS)]
    mixed = jnp.concatenate(parts, axis=-1)
    y = x_ref[...] + jnp.dot(mixed, w_ref[...], preferred_element_type=F32)
    normed = y * lax.rsqrt(jnp.mean(y * y, axis=-1, keepdims=True) + RMS_EPS) * g_ref[...]
    if final:
        o_refs[0][...] = normed
    else:
        o_refs[0][...] = y
        o_refs[1][...] = normed.astype(BF16)


def _outproj(x2, mixers, w_out, layer, norm_gain, *, final, tm):
    t, d = x2.shape
    mspec = pl.BlockSpec((N_HEADS, tm, LANES), lambda i: (0, i, 0))
    row_spec = pl.BlockSpec((tm, d), lambda i: (i, 0))
    out_specs, out_shape = [row_spec], [jax.ShapeDtypeStruct((t, d), F32)]
    if not final:
        out_specs.append(row_spec)
        out_shape.append(jax.ShapeDtypeStruct((t, d), BF16))
    return pl.pallas_call(
        functools.partial(_outproj_kernel, final),
        grid=(t // tm,),
        in_specs=[
            row_spec, mspec, mspec, mspec, mspec,
            pl.BlockSpec((None,) + w_out.shape[1:], lambda i: (layer, 0, 0)),
            pl.BlockSpec((1, d), lambda i: (0, 0)),
        ],
        out_specs=out_specs,
        out_shape=out_shape,
        compiler_params=pltpu.CompilerParams(
            dimension_semantics=("arbitrary",), vmem_limit_bytes=VMEM_LIMIT_BYTES),
        name="outproj",
    )(x2, *mixers, w_out, norm_gain.reshape(1, d))


RELAYOUT_ROWS = 256
RELAYOUT_TAIL = 32


def _relayout_kernel(a_ref, b_ref, o_ref):
    r = pl.program_id(0)
    depth, rows, _ = o_ref.shape
    for l in range(depth):
        x = jnp.concatenate([a_ref[:, l, :], b_ref[:, l, :]], axis=0)
        out_lo = 0
        for (src_lo, src_hi) in _MAIN_RUNS:
            shift = src_lo - out_lo
            out_hi = out_lo + (src_hi - src_lo)
            assert shift <= RELAYOUT_TAIL and out_lo % rows == 0 and out_hi % rows == 0

            @pl.when((r >= out_lo // rows) & (r < out_hi // rows))
            def _(shift=shift, l=l, x=x):
                o_ref[l] = x[shift:shift + rows].astype(BF16)

            out_lo = out_hi


def _relayout_w_in(w_in):
    depth, d, d_in = w_in.shape
    wt = jnp.transpose(w_in, (2, 0, 1))
    rows, tail = RELAYOUT_ROWS, RELAYOUT_TAIL
    main = pl.pallas_call(
        _relayout_kernel,
        grid=(N_MAIN // rows,),
        in_specs=[pl.BlockSpec((rows, depth, d), lambda r: (r, 0, 0)),
                  pl.BlockSpec((tail, depth, d), lambda r: ((r + 1) * (rows // tail), 0, 0))],
        out_specs=pl.BlockSpec((depth, rows, d), lambda r: (0, r, 0)),
        out_shape=jax.ShapeDtypeStruct((depth, N_MAIN, d), BF16),
        compiler_params=pltpu.CompilerParams(dimension_semantics=("arbitrary",)),
        name="relayout",
    )(wt, wt)
    f_off, a_off = _SEG_OFF["fox_f"][0], _SEG_OFF["gla_a"][0]
    gate_cols = jnp.concatenate([w_in[..., f_off:f_off + N_HEADS],
                                 w_in[..., a_off:a_off + GLA_RANK]], axis=-1)
    small = jnp.pad(jnp.transpose(gate_cols, (0, 2, 1)),
                    ((0, 0), (0, LANES - N_HEADS - GLA_RANK), (0, 0))).astype(BF16)
    return main, small


def _rope_tables(s_len, comp_dim, rot_dim):
    half = rot_dim // 2
    inv_freq = ROPE_THETA ** (-jnp.arange(0, rot_dim, 2, dtype=F32) / rot_dim)
    ang = jnp.arange(s_len, dtype=F32)[:, None] * inv_freq[None, :]
    cos, sin = jnp.cos(ang), jnp.sin(ang)
    zeros = jnp.zeros((s_len, comp_dim - rot_dim), F32)
    ones = jnp.ones((s_len, comp_dim - rot_dim), F32)
    reps = LANES // comp_dim
    cos_t = jnp.tile(jnp.concatenate([cos, cos, ones], axis=-1), (1, reps))
    sin_up = jnp.tile(jnp.concatenate([-sin, jnp.zeros_like(sin), zeros], axis=-1), (1, reps))
    sin_dn = jnp.tile(jnp.concatenate([jnp.zeros_like(sin), sin, zeros], axis=-1), (1, reps))
    return cos_t, sin_up, sin_dn


def kernel(x, norm_g, w_in, fox_fb, diff_lam, diff_norm_g, gla_wa2, gla_ba, gla_norm_g, w_out,
           final_norm_g):
    batch, s_len, d_model = x.shape
    depth = w_in.shape[0]
    assert s_len % Q_TILE == 0 and Q_TILE == K_TILE == MOBA_BLOCK
    assert s_len // MOBA_BLOCK <= GATE_ROWS
    t = batch * s_len
    tm_in = min(1024, t)
    tn_in = 1536
    tn_next = 2560
    tm_out = min(512, t)

    w_main, w_small = _relayout_w_in(w_in)
    fb_pad = jnp.pad(fox_fb, ((0, 0), (0, LANES - N_HEADS)))[:, None, :]
    wa_pad = jnp.pad(gla_wa2, ((0, 0), (N_HEADS, LANES - N_HEADS - GLA_RANK), (0, 0))).astype(BF16)
    ba = gla_ba[:, None, :]
    rope_diff = _rope_tables(s_len, DIFF_QK_DIM, DIFF_QK_DIM // ROPE_FRACTION)
    rope_moba = _rope_tables(s_len, HEAD_DIM, HEAD_DIM // ROPE_FRACTION)

    x2 = x.reshape(t, d_model)
    h2 = None
    for l in range(depth):
        if h2 is None:
            proj, small, w_out_rows = _inproj(x2, norm_g[l], w_main, w_small, l, w_out,
                                              tm=tm_in, tn=tn_in)
            w_out_b = w_out_rows.reshape(w_out.shape)
        else:
            proj, small = _inproj(h2, None, w_main, w_small, l, tm=tm_in, tn=tn_next)
        fk_cols, fq_rows, p_cum = _gates(small, fb_pad[l], wa_pad[l], ba[l], batch=batch, s_len=s_len)
        lam_init = 0.8 - 0.6 * math.exp(-0.3 * l)
        mixers = (
            _fox(proj, fk_cols, fq_rows, batch=batch, s_len=s_len),
            _diff(proj, diff_lam[l], diff_norm_g[l], rope_diff, lam_init, batch=batch, s_len=s_len),
            _moba(proj, rope_moba, batch=batch, s_len=s_len),
            _gla(proj, p_cum, gla_norm_g[l], batch=batch, s_len=s_len),
        )
        if l == depth - 1:
            (x2,) = _outproj(x2, mixers, w_out_b, l, final_norm_g, final=True, tm=tm_out)
        else:
            x2, h2 = _outproj(x2, mixers, w_out_b, l, norm_g[l + 1], final=False, tm=tm_out)
    return x2.reshape(batch, s_len, d_model)
```

```python
import functools
import math

import jax
import jax.numpy as jnp
from jax import lax
from jax.experimental import pallas as pl
from jax.experimental.pallas import tpu as pltpu

F32 = jnp.float32
BF16 = jnp.bfloat16

LANES = 128
VMEM_LIMIT_BYTES = 56 * 2 ** 20
HEAD_DIM = 128
N_HEADS = 4
GROUP_W = N_HEADS * HEAD_DIM
DIFF_QK_DIM = HEAD_DIM // 2
MOBA_BLOCK = 256
MOBA_TOPK = 3
GLA_DK = HEAD_DIM // 2
GLA_RANK = 16
GLA_TAU = 16.0
GLA_CHUNK = 64
ROPE_THETA = 500000.0
ROPE_FRACTION = 4
RMS_EPS = 1e-6
MASKED_LOGIT = -1e30

_SEGMENTS = (
    ("fox_q", GROUP_W), ("fox_k", GROUP_W), ("fox_v", GROUP_W),
    ("fox_f", N_HEADS), ("fox_g", GROUP_W),
    ("diff_q", GROUP_W), ("diff_k", GROUP_W), ("diff_v", GROUP_W), ("diff_g", GROUP_W),
    ("moba_q", GROUP_W), ("moba_k", GROUP_W), ("moba_v", GROUP_W), ("moba_g", GROUP_W),
    ("gla_q", N_HEADS * GLA_DK), ("gla_k", N_HEADS * GLA_DK), ("gla_v", GROUP_W),
    ("gla_a", GLA_RANK), ("gla_g", GROUP_W),
)
_SEG_OFF = {}
_off = 0
for _name, _w in _SEGMENTS:
    _SEG_OFF[_name] = (_off, _w)
    _off += _w

_SLABS = ("fox_q", "fox_k", "fox_v", "fox_g", "diff_q", "diff_k", "diff_v", "diff_g",
          "moba_q", "moba_k", "moba_v", "moba_g", "gla_qk", "gla_v", "gla_g")
_SLAB = {name: i for i, name in enumerate(_SLABS)}
N_MAIN = len(_SLABS) * GROUP_W
_MAIN_RUNS = ((0, _SEG_OFF["fox_f"][0]),
              (_SEG_OFF["fox_g"][0], _SEG_OFF["gla_a"][0]),
              (_SEG_OFF["gla_g"][0], _off))
assert sum(b - a for a, b in _MAIN_RUNS) == N_MAIN

Q_TILE = 256
K_TILE = 256
CUM_BLOCK = 128
GATE_ROWS = 16
GLA_GROUP = 4
GLA_SAFE_DECAY = 60.0
ONES_ROWS = 16
LOG2E = math.log2(math.e)


def _silu(x):
    return x * (1.0 / (1.0 + jnp.exp(-x)))


def _log_sigmoid(x):
    return jnp.minimum(x, 0.0) - jnp.log1p(jnp.exp(-jnp.abs(x)))


def _nt_dot(a, b):
    return lax.dot_general(a, b, (((1,), (1,)), ((), ())), preferred_element_type=F32)


def _inproj_tile(h, w_ref, o_ref):
    acc = _nt_dot(h, w_ref[...])
    for c in range(o_ref.shape[0]):
        o_ref[c] = acc[:, c * LANES:(c + 1) * LANES].astype(BF16)


def _inproj_norm_kernel(cast_blocks, x_ref, g_ref, w_ref, ws_ref, wo_ref, o_ref, small_ref, wob_ref,
                        h_scr):
    step = pl.program_id(0) * pl.num_programs(1) + pl.program_id(1)

    @pl.when(step < cast_blocks)
    def _():
        wob_ref[...] = wo_ref[...].astype(BF16)

    @pl.when(pl.program_id(1) == 0)
    def _():
        x = x_ref[...]
        y = x * lax.rsqrt(jnp.mean(x * x, axis=-1, keepdims=True) + RMS_EPS) * g_ref[...]
        h = y.astype(BF16)
        h_scr[...] = h
        small_ref[...] = _nt_dot(h, ws_ref[...])

    _inproj_tile(h_scr[...], w_ref, o_ref)


def _inproj_kernel(h_ref, w_ref, ws_ref, o_ref, small_ref):
    @pl.when(pl.program_id(1) == 0)
    def _():
        small_ref[...] = _nt_dot(h_ref[...], ws_ref[...])

    _inproj_tile(h_ref[...], w_ref, o_ref)


def _inproj(rows, norm_g, w_main, w_small, layer, w_out=None, *, tm, tn):
    t, d = rows.shape
    n_main = w_main.shape[1]
    with_norm = norm_g is not None
    grid = (t // tm, n_main // tn)
    row_spec = pl.BlockSpec((tm, d), lambda i, j: (i, 0))
    weight_specs = [pl.BlockSpec((None, tn, d), lambda i, j: (layer, j, 0)),
                    pl.BlockSpec((None, LANES, d), lambda i, j: (layer, 0, 0))]
    out_specs = [pl.BlockSpec((tn // LANES, tm, LANES), lambda i, j: (j, i, 0)),
                 pl.BlockSpec((tm, LANES), lambda i, j: (i, 0))]
    out_shape = [jax.ShapeDtypeStruct((n_main // LANES, t, LANES), BF16),
                 jax.ShapeDtypeStruct((t, LANES), F32)]
    if with_norm:
        wo_rows = w_out.reshape(-1, w_out.shape[-1])
        cast_blocks = 1 << ((grid[0] * grid[1]).bit_length() - 1)
        cast_rows = wo_rows.shape[0] // cast_blocks
        assert cast_rows * cast_blocks == wo_rows.shape[0] and cast_rows % 16 == 0
        cast_spec = pl.BlockSpec(
            (cast_rows, wo_rows.shape[1]),
            lambda i, j: (jnp.minimum(i * grid[1] + j, cast_blocks - 1), 0))
        body = functools.partial(_inproj_norm_kernel, cast_blocks)
        in_specs = [row_spec, pl.BlockSpec((1, d), lambda i, j: (0, 0))] + weight_specs + [cast_spec]
        args = (rows, norm_g.reshape(1, d), w_main, w_small, wo_rows)
        out_specs, out_shape = out_specs + [cast_spec], out_shape + [
            jax.ShapeDtypeStruct(wo_rows.shape, BF16)]
        scratch = [pltpu.VMEM((tm, d), BF16)]
    else:
        body, in_specs, scratch = _inproj_kernel, [row_spec] + weight_specs, []
        args = (rows, w_main, w_small)
    return pl.pallas_call(
        body,
        grid=grid,
        in_specs=in_specs,
        out_specs=out_specs,
        out_shape=out_shape,
        scratch_shapes=scratch,
        compiler_params=pltpu.CompilerParams(
            dimension_semantics=("arbitrary", "arbitrary"), vmem_limit_bytes=VMEM_LIMIT_BYTES),
        name="inproj",
    )(*args)


def _gates_kernel(small_ref, fb_ref, wa_ref, ba_ref, fk_ref, fq_ref, p_ref):
    s_len = small_ref.shape[0]
    small = small_ref[...]
    log_f = _log_sigmoid(small + fb_ref[...])
    a_logit = jnp.dot(small.astype(BF16), wa_ref[...], preferred_element_type=F32) + ba_ref[...]
    log_a = _log_sigmoid(a_logit) * (1.0 / GLA_TAU)
    both = jnp.concatenate([log_f, log_a], axis=-1)

    row = lax.broadcasted_iota(jnp.int32, (CUM_BLOCK, CUM_BLOCK), 0)
    col = lax.broadcasted_iota(jnp.int32, (CUM_BLOCK, CUM_BLOCK), 1)
    tri = (col <= row).astype(BF16)
    carry = jnp.zeros((1, both.shape[1]), F32)
    for c in range(s_len // CUM_BLOCK):
        blk = both[c * CUM_BLOCK:(c + 1) * CUM_BLOCK]
        hi = blk.astype(BF16)
        rest = blk - hi.astype(F32)
        mid = rest.astype(BF16)
        lo = (rest - mid.astype(F32)).astype(BF16)
        cum = carry + sum(jnp.dot(tri, part, preferred_element_type=F32) for part in (hi, mid, lo))
        carry = cum[CUM_BLOCK - 1:CUM_BLOCK]
        rows = slice(c * CUM_BLOCK, (c + 1) * CUM_BLOCK)
        f_cum = cum[:, :LANES] * LOG2E
        per_tile = Q_TILE // CUM_BLOCK
        fq_ref[c // per_tile, :, (c % per_tile) * CUM_BLOCK:(c % per_tile + 1) * CUM_BLOCK] = (
            f_cum.T[:GATE_ROWS])
        fk_ref[rows, :] = f_cum
        for pair in range(N_HEADS // 2):
            p_ref[pair, rows, :] = cum[:, (1 + pair) * LANES:(2 + pair) * LANES]


def _gates(small, fb_pad, wa_pad, ba, *, batch, s_len):
    t = small.shape[0]
    return pl.pallas_call(
        _gates_kernel,
        grid=(batch,),
        in_specs=[
            pl.BlockSpec((s_len, LANES), lambda b: (b, 0)),
            pl.BlockSpec((1, LANES), lambda b: (0, 0)),
            pl.BlockSpec((LANES, N_HEADS * GLA_DK), lambda b: (0, 0)),
            pl.BlockSpec((1, N_HEADS * GLA_DK), lambda b: (0, 0)),
        ],
        out_specs=[
            pl.BlockSpec((s_len, LANES), lambda b: (b, 0)),
            pl.BlockSpec((None, s_len // Q_TILE, GATE_ROWS, Q_TILE), lambda b: (b, 0, 0, 0)),
            pl.BlockSpec((N_HEADS // 2, s_len, LANES), lambda b: (0, b, 0)),
        ],
        out_shape=[
            jax.ShapeDtypeStruct((t, LANES), F32),
            jax.ShapeDtypeStruct((batch, s_len // Q_TILE, GATE_ROWS, Q_TILE), F32),
            jax.ShapeDtypeStruct((N_HEADS // 2, t, LANES), F32),
        ],
        compiler_params=pltpu.CompilerParams(dimension_semantics=("arbitrary",)),
        name="gates",
    )(small, fb_pad, wa_pad, ba)


def _causal_mask_t(tk, tq):
    key = lax.broadcasted_iota(jnp.int32, (tk, tq), 0)
    qry = lax.broadcasted_iota(jnp.int32, (tk, tq), 1)
    return key <= qry


def _transpose_values(v_ref, vt_scr):
    dv = v_ref.shape[2]
    for h in range(v_ref.shape[0]):
        for j in range(v_ref.shape[1] // K_TILE):
            blk = v_ref[h, j * K_TILE:(j + 1) * K_TILE, :].astype(F32)
            vt_scr[h, j, :dv, :] = blk.T.astype(BF16)
            vt_scr[h, j, dv:, :] = jnp.ones((ONES_ROWS, K_TILE), BF16)


def _col_max(z):
    rows = z.shape[0]
    while rows > 8:
        rows //= 2
        z = jnp.maximum(z[:rows], z[rows:])
    return jnp.max(z, axis=0, keepdims=True)


def _causal_tiles_t(n_tiles, prepare_fn, logits_fn, vt_fn, finish_fn, z_scr, acc_scr):
    n, _, tq = acc_scr.shape

    def update(j, slot, ms, tile_max, masked):
        stats = []
        for s in range(n):
            z = z_scr[slot, s]
            if masked:
                z = jnp.where(_causal_mask_t(K_TILE, tq), z, -jnp.inf)
                m_new = jnp.maximum(ms[s], _col_max(z))
            else:
                m_new = jnp.maximum(ms[s], tile_max[s])
            stats.append((m_new, jnp.exp2(ms[s] - m_new), jnp.exp2(z - m_new).astype(BF16)))
        for s in range(n):
            pv = jnp.dot(vt_fn(s, j), stats[s][2], preferred_element_type=F32)
            acc_scr[s] = stats[s][1] * acc_scr[s] + pv
        return tuple(st[0] for st in stats)

    def first_logits(i):
        ctx = prepare_fn(i)
        return ctx, [logits_fn(ctx, s, jnp.int32(0)) for s in range(n)]

    def open_tile(zs):
        for s in range(n):
            z_scr[0, s] = zs[s]
            acc_scr[s] = jnp.zeros(acc_scr.shape[1:], F32)
        return (tuple(jnp.full((1, tq), -jnp.inf, F32) for _ in range(n)),
                tuple(_col_max(z) for z in zs))

    def close_tile(i, ms):
        update(i, lax.rem(i, 2), ms, None, True)
        finish_fn(i)

    def off_diagonal(i, ctx, state):
        def body(j, state):
            ms, tile_max = state
            slot = lax.rem(j, 2)
            z_next = [logits_fn(ctx, s, j + 1) for s in range(n)]
            ms = update(j, slot, ms, tile_max, False)
            for s in range(n):
                z_scr[1 - slot, s] = z_next[s]
            return ms, tuple(_col_max(z) for z in z_next)

        return lax.fori_loop(0, i, body, state)[0]

    _, zs = first_logits(jnp.int32(0))
    ms, _ = open_tile(zs)

    def outer(i, ms_prev):
        ctx, zs = first_logits(i)
        close_tile(i - 1, ms_prev)
        return off_diagonal(i, ctx, open_tile(zs))

    ms = lax.fori_loop(1, n_tiles, outer, ms)
    close_tile(jnp.int32(n_tiles - 1), ms)


def _softmax_result_t(acc):
    dv = acc.shape[0] - ONES_ROWS
    return acc[:dv] / acc[dv:dv + 1]


def _rope(x, cos_t, sin_up, sin_dn, half):
    return (x * cos_t + pltpu.roll(x, LANES - half, 1) * sin_up
            + pltpu.roll(x, half, 1) * sin_dn)


def _key_rows(j):
    return pl.ds(pl.multiple_of(j * K_TILE, K_TILE), K_TILE)


def _query_rows(i):
    return pl.ds(pl.multiple_of(i * Q_TILE, Q_TILE), Q_TILE)


def _batch_spec(s_len):
    return lambda slab: pl.BlockSpec((N_HEADS, s_len, LANES), lambda b: (slab, b, 0))


def _softmax_scratch(s_len):
    return [pltpu.VMEM((N_HEADS, s_len // K_TILE, HEAD_DIM + ONES_ROWS, K_TILE), BF16),
            pltpu.VMEM((2, N_HEADS, K_TILE, Q_TILE), F32),
            pltpu.VMEM((N_HEADS, HEAD_DIM + ONES_ROWS, Q_TILE), F32)]


def _fox_kernel(q_ref, k_ref, v_ref, g_ref, fc_ref, fq_ref, o_ref, fk_ref, vt_scr, z_scr, acc_scr):
    scale = HEAD_DIM ** -0.5 * LOG2E
    _transpose_values(v_ref, vt_scr)
    for c in range(fc_ref.shape[0] // CUM_BLOCK):
        rows = slice(c * CUM_BLOCK, (c + 1) * CUM_BLOCK)
        blk = fc_ref[rows, :]
        for h in range(N_HEADS):
            fk_ref[h, rows, :] = jnp.broadcast_to(blk[:, h:h + 1], (CUM_BLOCK, LANES))

    def prepare(i):
        fq_all = fq_ref[i]
        return ([q_ref[h, _query_rows(i), :] for h in range(N_HEADS)],
                [fq_all[h:h + 1, :] for h in range(N_HEADS)])

    def logits(ctx, h, j):
        qs, fqs = ctx
        fk = fk_ref[h, _key_rows(j), :]
        fk = jnp.concatenate([fk] * (Q_TILE // LANES), axis=-1)
        return _nt_dot(k_ref[h, _key_rows(j), :], qs[h]) * scale + (fqs[h] - fk)

    def finish(i):
        for h in range(N_HEADS):
            o = _softmax_result_t(acc_scr[h]).T
            gate = _silu(g_ref[h, _query_rows(i), :].astype(F32))
            o_ref[h, _query_rows(i), :] = (o * gate).astype(BF16)

    _causal_tiles_t(q_ref.shape[1] // Q_TILE, prepare, logits, lambda h, j: vt_scr[h, j], finish,
                    z_scr, acc_scr)


def _fox(proj, fk_cols, fq_rows, *, batch, s_len):
    t = proj.shape[1]
    spec = _batch_spec(s_len)
    return pl.pallas_call(
        _fox_kernel,
        grid=(batch,),
        in_specs=[
            spec(_SLAB["fox_q"]), spec(_SLAB["fox_k"]), spec(_SLAB["fox_v"]), spec(_SLAB["fox_g"]),
            pl.BlockSpec((s_len, LANES), lambda b: (b, 0)),
            pl.BlockSpec((None, s_len // Q_TILE, GATE_ROWS, Q_TILE), lambda b: (b, 0, 0, 0)),
        ],
        out_specs=spec(0),
        out_shape=jax.ShapeDtypeStruct((N_HEADS, t, LANES), BF16),
        scratch_shapes=[pltpu.VMEM((N_HEADS, s_len, LANES), F32)] + _softmax_scratch(s_len),
        compiler_params=pltpu.CompilerParams(dimension_semantics=("arbitrary",)),
        name="fox",
    )(proj, proj, proj, proj, fk_cols, fq_rows)


def _diff_kernel(lam_init, q_ref, k_ref, v_ref, g_ref, lam_ref, ng_ref, c_ref, u_ref, d_ref, o_ref,
                 kz_scr, qr_scr, a1_scr, vt_scr, z_scr, acc_scr):
    half = DIFF_QK_DIM // ROPE_FRACTION // 2
    scale = DIFF_QK_DIM ** -0.5 * LOG2E

    _transpose_values(v_ref, vt_scr)
    first = lax.broadcasted_iota(jnp.int32, (k_ref.shape[1], LANES), 1) < DIFF_QK_DIM
    for h in range(N_HEADS):
        kr = _rope(k_ref[h].astype(F32), c_ref[...], u_ref[...], d_ref[...], half)
        kz_scr[0, h] = jnp.where(first, kr, 0.0).astype(BF16)
        kz_scr[1, h] = jnp.where(first, 0.0, kr).astype(BF16)
        qr_scr[h] = _rope(q_ref[h].astype(F32), c_ref[...], u_ref[...], d_ref[...], half).astype(BF16)

    lf = lam_ref[...]
    lam = (jnp.exp(jnp.sum(lf[0:1] * lf[1:2], axis=-1, keepdims=True))
           - jnp.exp(jnp.sum(lf[2:3] * lf[3:4], axis=-1, keepdims=True)) + lam_init)

    def prepare(i):
        return [qr_scr[h, _query_rows(i), :] for h in range(N_HEADS)]

    def finish_first(i):
        for h in range(N_HEADS):
            a1_scr[h, _query_rows(i), :] = _softmax_result_t(acc_scr[h]).T

    def finish_second(i):
        for h in range(N_HEADS):
            o = a1_scr[h, _query_rows(i), :] - lam * _softmax_result_t(acc_scr[h]).T
            o = o * lax.rsqrt(jnp.mean(o * o, axis=-1, keepdims=True) + RMS_EPS) * ng_ref[...]
            o = o * (1.0 - lam_init)
            gate = _silu(g_ref[h, _query_rows(i), :].astype(F32))
            o_ref[h, _query_rows(i), :] = (o * gate).astype(BF16)

    for comp, finish in ((0, finish_first), (1, finish_second)):
        def logits(qs, h, j, comp=comp):
            return _nt_dot(kz_scr[comp, h, _key_rows(j), :], qs[h]) * scale

        _causal_tiles_t(q_ref.shape[1] // Q_TILE, prepare, logits, lambda h, j: vt_scr[h, j], finish,
                        z_scr, acc_scr)


def _diff(proj, diff_lam, diff_norm_g, tables, lam_init, *, batch, s_len):
    t = proj.shape[1]
    spec = _batch_spec(s_len)
    table_spec = pl.BlockSpec((s_len, LANES), lambda b: (0, 0))
    return pl.pallas_call(
        functools.partial(_diff_kernel, lam_init),
        grid=(batch,),
        in_specs=[
            spec(_SLAB["diff_q"]), spec(_SLAB["diff_k"]), spec(_SLAB["diff_v"]), spec(_SLAB["diff_g"]),
            pl.BlockSpec((4, DIFF_QK_DIM), lambda b: (0, 0)),
            pl.BlockSpec((1, HEAD_DIM), lambda b: (0, 0)),
            table_spec, table_spec, table_spec,
        ],
        out_specs=spec(0),
        out_shape=jax.ShapeDtypeStruct((N_HEADS, t, LANES), BF16),
        scratch_shapes=[pltpu.VMEM((2, N_HEADS, s_len, LANES), BF16),
                        pltpu.VMEM((N_HEADS, s_len, LANES), BF16),
                        pltpu.VMEM((N_HEADS, s_len, LANES), F32),
                        ] + _softmax_scratch(s_len),
        compiler_params=pltpu.CompilerParams(dimension_semantics=("arbitrary",)),
        name="diff",
    )(proj, proj, proj, proj, diff_lam, diff_norm_g.reshape(1, HEAD_DIM), *tables)


def _moba_kernel(q_ref, k_ref, v_ref, g_ref, c_ref, u_ref, d_ref,
                 o_ref, kr_scr, kmean_scr, vt_scr, z_scr, acc_scr):
    tq = Q_TILE
    half = HEAD_DIM // ROPE_FRACTION // 2
    scale = HEAD_DIM ** -0.5 * LOG2E
    n_blk = k_ref.shape[1] // MOBA_BLOCK

    _transpose_values(v_ref, vt_scr)
    kmean_scr[...] = jnp.zeros(kmean_scr.shape, F32)
    for h in range(N_HEADS):
        kr = _rope(k_ref[h].astype(F32), c_ref[...], u_ref[...], d_ref[...], half)
        kr_scr[h] = kr.astype(BF16)
        for n in range(n_blk):
            blk = kr[n * MOBA_BLOCK:(n + 1) * MOBA_BLOCK]
            kmean_scr[h, n:n + 1, :] = jnp.sum(blk, axis=0, keepdims=True) * (1.0 / MOBA_BLOCK)

    blk_id = lax.broadcasted_iota(jnp.int32, (GATE_ROWS, tq), 0).astype(F32)

    def prepare(i):
        rows = _query_rows(i)
        past = blk_id < lax.convert_element_type(i, F32)
        qs, biases = [], []
        for h in range(N_HEADS):
            q = _rope(q_ref[h, rows, :].astype(F32), c_ref[rows, :], u_ref[rows, :], d_ref[rows, :],
                      half).astype(BF16)
            gate = _nt_dot(kmean_scr[h].astype(BF16), q)
            gate = jnp.where(past, gate, -jnp.inf)
            open_ = past
            sel = jnp.zeros((GATE_ROWS, tq), jnp.bool_)
            for _ in range(MOBA_TOPK):
                top = jnp.max(gate, axis=0, keepdims=True)
                idx = jnp.min(jnp.where(gate == top, blk_id, float(GATE_ROWS)), axis=0, keepdims=True)
                pick = (blk_id == idx) & open_
                sel = sel | pick
                open_ = open_ & jnp.logical_not(pick)
                gate = jnp.where(pick, -jnp.inf, gate)
            qs.append(q)
            biases.append(jnp.where(sel, 0.0, MASKED_LOGIT))
        return i, qs, biases

    def logits(ctx, h, j):
        i, qs, biases = ctx
        here = blk_id == lax.convert_element_type(j, F32)
        bias = jnp.sum(jnp.where(here, biases[h], 0.0), axis=0, keepdims=True)
        bias = jnp.where(j == i, 0.0, bias)
        return _nt_dot(kr_scr[h, _key_rows(j), :], qs[h]) * scale + bias

    def finish(i):
        for h in range(N_HEADS):
            o = _softmax_result_t(acc_scr[h]).T
            gate = _silu(g_ref[h, _query_rows(i), :].astype(F32))
            o_ref[h, _query_rows(i), :] = (o * gate).astype(BF16)

    _causal_tiles_t(q_ref.shape[1] // tq, prepare, logits, lambda h, j: vt_scr[h, j], finish,
                    z_scr, acc_scr)


def _moba(proj, tables, *, batch, s_len):
    t = proj.shape[1]
    spec = _batch_spec(s_len)
    table_spec = pl.BlockSpec((s_len, LANES), lambda b: (0, 0))
    return pl.pallas_call(
        _moba_kernel,
        grid=(batch,),
        in_specs=[
            spec(_SLAB["moba_q"]), spec(_SLAB["moba_k"]), spec(_SLAB["moba_v"]), spec(_SLAB["moba_g"]),
            table_spec, table_spec, table_spec,
        ],
        out_specs=spec(0),
        out_shape=jax.ShapeDtypeStruct((N_HEADS, t, LANES), BF16),
        scratch_shapes=[pltpu.VMEM((N_HEADS, s_len, LANES), BF16),
                        pltpu.VMEM((N_HEADS, GATE_ROWS, LANES), F32),
                        ] + _softmax_scratch(s_len),
        compiler_params=pltpu.CompilerParams(dimension_semantics=("arbitrary",)),
        name="moba",
    )(proj, proj, proj, proj, *tables)


def _head_decay(p_ref, h, rows, qhalf):
    pair = p_ref[h // 2, rows, :]
    swapped = pltpu.roll(pair, GLA_DK, 1)
    return jnp.where(qhalf, pair, swapped) if h % 2 == 0 else jnp.where(qhalf, swapped, pair)


def _gla_factorised(x_ref, v_ref, g_ref, p_ref, ng_ref, o_ref):
    s_len = x_ref.shape[1]
    c_len = GLA_CHUNK
    scale = GLA_DK ** -0.5
    lane = lax.broadcasted_iota(jnp.int32, (c_len, LANES), 1)
    qhalf = lane < GLA_DK
    qhalf_row = lax.broadcasted_iota(jnp.int32, (1, LANES), 1) < GLA_DK
    row = lax.broadcasted_iota(jnp.int32, (c_len, c_len), 0)
    col = lax.broadcasted_iota(jnp.int32, (c_len, c_len), 1)
    causal = col <= row

    group = GLA_GROUP
    pairs = [(h, c) for h in range(N_HEADS) for c in range(group)]

    def body(g, carries):
        prep = {}
        for h, c in pairs:
            rows = pl.ds(pl.multiple_of(g * (group * c_len), group * c_len) + c * c_len, c_len)
            qcb = x_ref[h // 2, rows, :].astype(F32)
            kcb = x_ref[2 + h // 2, rows, :].astype(F32)
            if h % 2 == 0:
                x = jnp.where(qhalf, qcb, pltpu.roll(kcb, GLA_DK, 1))
            else:
                x = jnp.where(qhalf, pltpu.roll(qcb, GLA_DK, 1), kcb)
            p = _head_decay(p_ref, h, rows, qhalf)
            base = carries[h][1] if c == 0 else prep[h, c - 1]["p_last"]
            cum = p - base
            xt = x * jnp.exp(jnp.where(qhalf, cum, -cum))
            kr = pltpu.roll(xt, GLA_DK, 1)
            decay = jnp.where(qhalf_row, jnp.exp(cum[c_len - 1:c_len]), 0.0)
            prep[h, c] = dict(
                rows=rows, qz=jnp.where(qhalf, xt, 0.0).astype(BF16), kr=kr.astype(BF16),
                khat=(jnp.where(qhalf, kr, 0.0) * decay).astype(BF16),
                decay=decay, v=v_ref[h, rows, :], p_last=p[c_len - 1:c_len])
        scores, upd = {}, {}
        for h, c in pairs:
            d = prep[h, c]
            scores[h, c] = _nt_dot(d["qz"], d["kr"])
            upd[h, c] = lax.dot_general(d["v"], d["khat"], (((0,), (0,)), ((), ())),
                                        preferred_element_type=F32)
        states = {}
        for h in range(N_HEADS):
            states[h, 0] = carries[h][0]
            for c in range(group):
                states[h, c + 1] = states[h, c] * prep[h, c]["decay"] + upd[h, c]
        outs = {}
        for h, c in pairs:
            sc = jnp.where(causal, scores[h, c] * scale, 0.0).astype(BF16)
            outs[h, c] = (jnp.dot(sc, prep[h, c]["v"], preferred_element_type=F32),
                          _nt_dot(prep[h, c]["qz"], states[h, c].astype(BF16)))
        for h, c in pairs:
            o = outs[h, c][0] + outs[h, c][1] * scale
            y = o * lax.rsqrt(jnp.mean(o * o, axis=-1, keepdims=True) + RMS_EPS) * ng_ref[...]
            rows = prep[h, c]["rows"]
            o_ref[h, rows, :] = (y * _silu(g_ref[h, rows, :].astype(F32))).astype(BF16)
        return tuple((states[h, group], prep[h, group - 1]["p_last"]) for h in range(N_HEADS))

    init = tuple((jnp.zeros((HEAD_DIM, LANES), F32), jnp.zeros((1, LANES), F32))
                 for _ in range(N_HEADS))
    lax.fori_loop(0, s_len // (group * c_len), body, init)


def _gla_unfactorised(x_ref, v_ref, g_ref, p_ref, ng_ref, o_ref, k_scr, cum_scr):
    s_len = x_ref.shape[1]
    c_len = GLA_CHUNK
    scale = GLA_DK ** -0.5
    lane = lax.broadcasted_iota(jnp.int32, (c_len, LANES), 1)
    row = lax.broadcasted_iota(jnp.int32, (c_len, LANES), 0)
    qhalf = lane < GLA_DK
    qhalf_row = lax.broadcasted_iota(jnp.int32, (1, LANES), 1) < GLA_DK

    for h in range(N_HEADS):
        def chunk(c, carry, h=h):
            state_t, base = carry
            rows = pl.ds(pl.multiple_of(c * c_len, c_len), c_len)
            qcb = x_ref[h // 2, rows, :].astype(F32)
            kcb = x_ref[2 + h // 2, rows, :].astype(F32)
            if h % 2 == 0:
                q, k = qcb, kcb
            else:
                q, k = pltpu.roll(qcb, GLA_DK, 1), pltpu.roll(kcb, GLA_DK, 1)
            q = jnp.where(qhalf, q, 0.0)
            k = jnp.where(qhalf, k, 0.0)
            v = v_ref[h, rows, :]
            p = _head_decay(p_ref, h, rows, qhalf)
            cum = p - base
            last = cum[c_len - 1:c_len]
            k_scr[...] = k
            cum_scr[...] = cum

            def key(s, scores):
                ks = k_scr[pl.ds(s, 1), :]
                cs = cum_scr[pl.ds(s, 1), :]
                col = jnp.sum(q * ks * jnp.exp(jnp.minimum(cum - cs, 0.0)), axis=-1, keepdims=True)
                return jnp.where(lane == s, col, scores)

            scores = lax.fori_loop(0, c_len, key, jnp.zeros((c_len, LANES), F32))
            scores = jnp.where(lane <= row, scores * scale, 0.0)[:, :c_len].astype(BF16)
            o = jnp.dot(scores, v, preferred_element_type=F32)
            o = o + _nt_dot((q * jnp.exp(cum)).astype(BF16), state_t.astype(BF16)) * scale
            khat = (k * jnp.exp(last - cum)).astype(BF16)
            upd = lax.dot_general(v, khat, (((0,), (0,)), ((), ())), preferred_element_type=F32)
            state_t = state_t * jnp.where(qhalf_row, jnp.exp(last), 0.0) + upd
            y = o * lax.rsqrt(jnp.mean(o * o, axis=-1, keepdims=True) + RMS_EPS) * ng_ref[...]
            o_ref[h, rows, :] = (y * _silu(g_ref[h, rows, :].astype(F32))).astype(BF16)
            return state_t, p[c_len - 1:c_len]

        lax.fori_loop(0, s_len // c_len, chunk,
                      (jnp.zeros((HEAD_DIM, LANES), F32), jnp.zeros((1, LANES), F32)))


def _gla_kernel(x_ref, v_ref, g_ref, p_ref, ng_ref, o_ref, k_scr, cum_scr):
    n_chunks = x_ref.shape[1] // GLA_CHUNK
    worst = jnp.zeros((1, 1), F32)
    for pair in range(p_ref.shape[0]):
        ends = p_ref[pair, pl.ds(GLA_CHUNK - 1, n_chunks, stride=GLA_CHUNK), :]
        starts = jnp.concatenate([jnp.zeros((1, LANES), F32), ends[:-1]], axis=0)
        worst = jnp.maximum(worst, jnp.max(starts - ends, keepdims=True))
    risky = worst[0, 0] > GLA_SAFE_DECAY

    @pl.when(risky)
    def _():
        _gla_unfactorised(x_ref, v_ref, g_ref, p_ref, ng_ref, o_ref, k_scr, cum_scr)

    @pl.when(jnp.logical_not(risky))
    def _():
        _gla_factorised(x_ref, v_ref, g_ref, p_ref, ng_ref, o_ref)


def _gla(proj, p_cum, gla_norm_g, *, batch, s_len):
    t = proj.shape[1]
    spec = _batch_spec(s_len)
    return pl.pallas_call(
        _gla_kernel,
        grid=(batch,),
        in_specs=[
            spec(_SLAB["gla_qk"]), spec(_SLAB["gla_v"]), spec(_SLAB["gla_g"]),
            pl.BlockSpec((N_HEADS // 2, s_len, LANES), lambda b: (0, b, 0)),
            pl.BlockSpec((1, HEAD_DIM), lambda b: (0, 0)),
        ],
        out_specs=spec(0),
        out_shape=jax.ShapeDtypeStruct((N_HEADS, t, LANES), BF16),
        scratch_shapes=[pltpu.VMEM((GLA_CHUNK, LANES), F32), pltpu.VMEM((GLA_CHUNK, LANES), F32)],
        compiler_params=pltpu.CompilerParams(dimension_semantics=("arbitrary",)),
        name="gla",
    )(proj, proj, proj, p_cum, gla_norm_g.reshape(1, HEAD_DIM))


def _outproj_kernel(final, x_ref, a_ref, b_ref, c_ref, d_ref, w_ref, g_ref, *o_refs):
    parts = [r[h] for r in (a_ref, b_ref, c_ref, d_ref) for h in range(N_HEADS)]
    mixed = jnp.concatenate(parts, axis=-1)
    y = x_ref[...] + jnp.dot(mixed, w_ref[...], preferred_element_type=F32)
    normed = y * lax.rsqrt(jnp.mean(y * y, axis=-1, keepdims=True) + RMS_EPS) * g_ref[...]
    if final:
        o_refs[0][...] = normed
    else:
        o_refs[0][...] = y
        o_refs[1][...] = normed.astype(BF16)


def _outproj(x2, mixers, w_out, layer, norm_gain, *, final, tm):
    t, d = x2.shape
    mspec = pl.BlockSpec((N_HEADS, tm, LANES), lambda i: (0, i, 0))
    row_spec = pl.BlockSpec((tm, d), lambda i: (i, 0))
    out_specs, out_shape = [row_spec], [jax.ShapeDtypeStruct((t, d), F32)]
    if not final:
        out_specs.append(row_spec)
        out_shape.append(jax.ShapeDtypeStruct((t, d), BF16))
    return pl.pallas_call(
        functools.partial(_outproj_kernel, final),
        grid=(t // tm,),
        in_specs=[
            row_spec, mspec, mspec, mspec, mspec,
            pl.BlockSpec((None,) + w_out.shape[1:], lambda i: (layer, 0, 0)),
            pl.BlockSpec((1, d), lambda i: (0, 0)),
        ],
        out_specs=out_specs,
        out_shape=out_shape,
        compiler_params=pltpu.CompilerParams(
            dimension_semantics=("arbitrary",), vmem_limit_bytes=VMEM_LIMIT_BYTES),
        name="outproj",
    )(x2, *mixers, w_out, norm_gain.reshape(1, d))


RELAYOUT_ROWS = 512
RELAYOUT_TAIL = 32


def _relayout_kernel(a_ref, b_ref, o_ref):
    r = pl.program_id(0)
    depth, rows, _ = o_ref.shape
    for l in range(depth):
        x = jnp.concatenate([a_ref[:, l, :], b_ref[:, l, :]], axis=0)
        out_lo = 0
        for (src_lo, src_hi) in _MAIN_RUNS:
            shift = src_lo - out_lo
            out_hi = out_lo + (src_hi - src_lo)
            assert shift <= RELAYOUT_TAIL and out_lo % rows == 0 and out_hi % rows == 0

            @pl.when((r >= out_lo // rows) & (r < out_hi // rows))
            def _(shift=shift, l=l, x=x):
                o_ref[l] = x[shift:shift + rows].astype(BF16)

            out_lo = out_hi


def _relayout_w_in(w_in):
    depth, d, d_in = w_in.shape
    wt = jnp.transpose(w_in, (2, 0, 1))
    rows, tail = RELAYOUT_ROWS, RELAYOUT_TAIL
    main = pl.pallas_call(
        _relayout_kernel,
        grid=(N_MAIN // rows,),
        in_specs=[pl.BlockSpec((rows, depth, d), lambda r: (r, 0, 0)),
                  pl.BlockSpec((tail, depth, d), lambda r: ((r + 1) * (rows // tail), 0, 0))],
        out_specs=pl.BlockSpec((depth, rows, d), lambda r: (0, r, 0)),
        out_shape=jax.ShapeDtypeStruct((depth, N_MAIN, d), BF16),
        compiler_params=pltpu.CompilerParams(dimension_semantics=("arbitrary",)),
        name="relayout",
    )(wt, wt)
    f_off, a_off = _SEG_OFF["fox_f"][0], _SEG_OFF["gla_a"][0]
    gate_cols = jnp.concatenate([w_in[..., f_off:f_off + N_HEADS],
                                 w_in[..., a_off:a_off + GLA_RANK]], axis=-1)
    small = jnp.pad(jnp.transpose(gate_cols, (0, 2, 1)),
                    ((0, 0), (0, LANES - N_HEADS - GLA_RANK), (0, 0))).astype(BF16)
    return main, small


def _rope_tables(s_len, comp_dim, rot_dim):
    half = rot_dim // 2
    inv_freq = ROPE_THETA ** (-jnp.arange(0, rot_dim, 2, dtype=F32) / rot_dim)
    ang = jnp.arange(s_len, dtype=F32)[:, None] * inv_freq[None, :]
    cos, sin = jnp.cos(ang), jnp.sin(ang)
    zeros = jnp.zeros((s_len, comp_dim - rot_dim), F32)
    ones = jnp.ones((s_len, comp_dim - rot_dim), F32)
    reps = LANES // comp_dim
    cos_t = jnp.tile(jnp.concatenate([cos, cos, ones], axis=-1), (1, reps))
    sin_up = jnp.tile(jnp.concatenate([-sin, jnp.zeros_like(sin), zeros], axis=-1), (1, reps))
    sin_dn = jnp.tile(jnp.concatenate([jnp.zeros_like(sin), sin, zeros], axis=-1), (1, reps))
    return cos_t, sin_up, sin_dn


def kernel(x, norm_g, w_in, fox_fb, diff_lam, diff_norm_g, gla_wa2, gla_ba, gla_norm_g, w_out,
           final_norm_g):
    batch, s_len, d_model = x.shape
    depth = w_in.shape[0]
    assert s_len % Q_TILE == 0 and Q_TILE == K_TILE == MOBA_BLOCK
    assert s_len // MOBA_BLOCK <= GATE_ROWS
    t = batch * s_len
    tm_in = min(1024, t)
    tn_in = 1536
    tn_next = 2560
    tm_out = min(512, t)

    w_main, w_small = _relayout_w_in(w_in)
    fb_pad = jnp.pad(fox_fb, ((0, 0), (0, LANES - N_HEADS)))[:, None, :]
    wa_pad = jnp.pad(gla_wa2, ((0, 0), (N_HEADS, LANES - N_HEADS - GLA_RANK), (0, 0))).astype(BF16)
    ba = gla_ba[:, None, :]
    rope_diff = _rope_tables(s_len, DIFF_QK_DIM, DIFF_QK_DIM // ROPE_FRACTION)
    rope_moba = _rope_tables(s_len, HEAD_DIM, HEAD_DIM // ROPE_FRACTION)

    x2 = x.reshape(t, d_model)
    h2 = None
    for l in range(depth):
        if h2 is None:
            proj, small, w_out_rows = _inproj(x2, norm_g[l], w_main, w_small, l, w_out,
                                              tm=tm_in, tn=tn_in)
            w_out_b = w_out_rows.reshape(w_out.shape)
        else:
            proj, small = _inproj(h2, None, w_main, w_small, l, tm=tm_in, tn=tn_next)
        fk_cols, fq_rows, p_cum = _gates(small, fb_pad[l], wa_pad[l], ba[l], batch=batch, s_len=s_len)
        lam_init = 0.8 - 0.6 * math.exp(-0.3 * l)
        mixers = (
            _fox(proj, fk_cols, fq_rows, batch=batch, s_len=s_len),
            _diff(proj, diff_lam[l], diff_norm_g[l], rope_diff, lam_init, batch=batch, s_len=s_len),
            _moba(proj, rope_moba, batch=batch, s_len=s_len),
            _gla(proj, p_cum, gla_norm_g[l], batch=batch, s_len=s_len),
        )
        if l == depth - 1:
            (x2,) = _outproj(x2, mixers, w_out_b, l, final_norm_g, final=True, tm=tm_out)
        else:
            x2, h2 = _outproj(x2, mixers, w_out_b, l, norm_g[l + 1], final=False, tm=tm_out)
    return x2.reshape(batch, s_len, d_model)
```
